```python
import math
import jax, jax.numpy as jnp
from jax import lax
import numpy as np

D_MODEL = 1024
BATCH = 16
SEQ = 2048
DEPTH = 4

F32 = jnp.float32
RMS_EPS = 1e-6
MIX_W = 512
N_BRANCH = 4

SSD_HEAD_DIM = 64
SSD_HEADS = MIX_W // SSD_HEAD_DIM
SSD_GROUPS = 2
SSD_STATE = 128
SSD_CONV = 4
SSD_CHUNK = 128
SSD_XBC = MIX_W + 2 * SSD_GROUPS * SSD_STATE
SSD_IN = MIX_W + SSD_XBC + SSD_HEADS

SC_CONV = 3
SC_IN = 3 * MIX_W

SG_GROUPS = 4
SG_CHUNK = 128
SG_IN = 2 * MIX_W

NSA_HEADS = 8
NSA_KV_HEADS = 2
NSA_REP = NSA_HEADS // NSA_KV_HEADS
NSA_HEAD_DIM = 64
CMP_BLOCK = 32
CMP_STRIDE = 16
SLC_BLOCK = 64
SLC_TOPN = 8
SLC_QBLOCK = 32
WIN = 256
WIN_QBLOCK = 128
FORCE_SCORE = 1e9
NSA_KV_W = NSA_KV_HEADS * NSA_HEAD_DIM
NSA_IN = NSA_HEADS * NSA_HEAD_DIM + 6 * NSA_KV_W + 3 * NSA_HEADS

D_IN = SSD_IN + SC_IN + SG_IN + NSA_IN
D_FF = ((8 * D_MODEL // 3 + 255) // 256) * 256

kernel_name = 'hybrid_gated_parallel_mixer_trunk'


def rms_norm(x, g):
    xf = x.astype(F32)
    y = xf * lax.rsqrt(jnp.mean(xf * xf, axis=-1, keepdims=True) + RMS_EPS)
    return (y * g.astype(F32)).astype(x.dtype)


def causal_dwconv(x, w):
    k, ch = w.shape
    return lax.conv_general_dilated(x, w[:, None, :].astype(x.dtype), window_strides=(1,),
                                    padding=[(k - 1, 0)], dimension_numbers=('NWC', 'WIO', 'NWC'),
                                    feature_group_count=ch)


def masked_softmax(s, mask):
    s = jnp.where(mask, s, -jnp.inf)
    m = jnp.max(s, axis=-1, keepdims=True)
    m = jnp.where(jnp.isfinite(m), m, 0.0)
    p = jnp.exp(s - m)
    return p / jnp.maximum(jnp.sum(p, axis=-1, keepdims=True), 1e-30)


def alibi_slopes():
    return 2.0 ** (-(8.0 / NSA_HEADS) * jnp.arange(1, NSA_HEADS + 1, dtype=F32))


def segsum(a):
    L = a.shape[-1]
    x = jnp.broadcast_to(a[..., :, None], a.shape + (L,))
    x = jnp.where(jnp.tril(jnp.ones((L, L), bool), -1), x, 0.0)
    s = jnp.cumsum(x, axis=-2)
    return jnp.where(jnp.tril(jnp.ones((L, L), bool)), s, -jnp.inf)


def ssd_mixer(u, conv_w, conv_b, dt_bias, a_log, d_skip, norm_g):
    Bsz, T, _ = u.shape
    H, P, G, N, Q = SSD_HEADS, SSD_HEAD_DIM, SSD_GROUPS, SSD_STATE, SSD_CHUNK
    R = H // G
    nc = T // Q
    z = u[..., :MIX_W]
    xbc = jax.nn.silu(causal_dwconv(u[..., MIX_W:MIX_W + SSD_XBC], conv_w) + conv_b)
    dt = jax.nn.softplus(u[..., MIX_W + SSD_XBC:].astype(F32) + dt_bias.astype(F32))
    xs = xbc[..., :MIX_W].astype(F32).reshape(Bsz, T, H, P)
    bm = xbc[..., MIX_W:MIX_W + G * N].astype(F32).reshape(Bsz, nc, Q, G, N)
    cm = xbc[..., MIX_W + G * N:].astype(F32).reshape(Bsz, nc, Q, G, N)
    xdt = (xs * dt[..., None]).reshape(Bsz, nc, Q, G, R, P)
    dta = (dt * -jnp.exp(a_log.astype(F32))).reshape(Bsz, nc, Q, G, R).transpose(0, 3, 4, 1, 2)
    a_cs = jnp.cumsum(dta, axis=-1)
    l_mat = jnp.exp(segsum(dta))
    cb = jnp.einsum('bclgn,bcsgn->bgcls', cm, bm)
    y_diag = jnp.einsum('bgrcls,bcsgrp->bclgrp', cb[:, :, None] * l_mat, xdt)
    decay_states = jnp.exp(a_cs[..., -1:] - a_cs)
    chunk_states = jnp.einsum('bclgn,bgrcl,bclgrp->cbgrpn', bm, decay_states, xdt)
    chunk_decay = jnp.moveaxis(jnp.exp(a_cs[..., -1]), -1, 0)

    def carry_state(h, inp):
        s_c, d_c = inp
        return h * d_c[..., None, None] + s_c, h

    _, h_in = lax.scan(carry_state, jnp.zeros((Bsz, G, R, P, N), F32), (chunk_states, chunk_decay))
    y_off = jnp.einsum('bclgn,cbgrpn,bgrcl->bclgrp', cm, h_in, jnp.exp(a_cs))
    y = (y_diag + y_off).reshape(Bsz, T, H, P) + xs * d_skip.astype(F32)[:, None]
    y = y.reshape(Bsz, T, MIX_W) * jax.nn.silu(z.astype(F32))
    y = rms_norm(y.reshape(Bsz, T, G, MIX_W // G), norm_g.reshape(G, MIX_W // G))
    return y.reshape(Bsz, T, MIX_W).astype(u.dtype)


def short_conv_mixer(u, conv_w):
    b_gate, c_gate, hx = jnp.split(u, 3, axis=-1)
    return b_gate * causal_dwconv(c_gate * hx, conv_w)


def spatial_gating_mixer(u, norm_g, w_s, b_s):
    Bsz, T, _ = u.shape
    uv = jax.nn.gelu(u)
    gate_u = uv[..., :MIX_W]
    v = rms_norm(uv[..., MIX_W:], norm_g)
    nc = T // SG_CHUNK
    v = v.reshape(Bsz, nc, SG_CHUNK, SG_GROUPS, MIX_W // SG_GROUPS)
    w = jnp.where(jnp.tril(jnp.ones((SG_CHUNK, SG_CHUNK), bool)), w_s, 0.0).astype(v.dtype)
    v = jnp.einsum('gts,bcsgd->bctgd', w, v) + b_s.T[:, :, None]
    return gate_u * v.reshape(Bsz, T, MIX_W)


def nsa_mixer(u, q_norm_g, k_norm_g, cmp_pe, cmp_w1, cmp_w2):
    Bsz, T, _ = u.shape
    H, G, R, dh = NSA_HEADS, NSA_KV_HEADS, NSA_REP, NSA_HEAD_DIM
    q_width = H * dh
    scale = dh ** -0.5
    q = rms_norm(u[..., :q_width].reshape(Bsz, T, H, dh), q_norm_g)
    q = q.reshape(Bsz, T, G, R, dh).transpose(0, 2, 3, 1, 4)
    kv = u[..., q_width:q_width + 6 * NSA_KV_W].reshape(Bsz, T, 6, G, dh)
    k_cmp, v_cmp, k_slc, v_slc, k_win, v_win = (kv[:, :, i] for i in range(6))
    gates = jax.nn.sigmoid(u[..., q_width + 6 * NSA_KV_W:].astype(F32))
    gates = gates.reshape(Bsz, T, G, R, 3).transpose(0, 2, 3, 1, 4)
    slopes = alibi_slopes().reshape(G, R)[None, :, :, None, None]
    t_pos = jnp.arange(T)

    n_cmp = (T - CMP_BLOCK) // CMP_STRIDE + 1
    cmp_start = jnp.arange(n_cmp) * CMP_STRIDE
    cmp_idx = cmp_start[:, None] + jnp.arange(CMP_BLOCK)[None, :]

    def compress(t, pe, w1, w2):
        blk = t[:, cmp_idx] + pe[:, None, :]
        blk = blk.transpose(0, 1, 3, 2, 4).reshape(Bsz, n_cmp, G, CMP_BLOCK * dh)
        return jax.nn.gelu(blk @ w1) @ w2

    kc = rms_norm(compress(k_cmp, cmp_pe[0], cmp_w1[0], cmp_w2[0]), k_norm_g[0])
    vc = compress(v_cmp, cmp_pe[1], cmp_w1[1], cmp_w2[1])
    dist_c = t_pos[:, None] - (cmp_start + CMP_BLOCK - 1)[None, :]
    s_c = jnp.einsum('bgrtd,bngd->bgrtn', q, kc, preferred_element_type=F32) * scale - slopes * dist_c.astype(F32)
    p_cmp = masked_softmax(s_c, dist_c >= 0)
    o_cmp = jnp.einsum('bgrtn,bngd->bgrtd', p_cmp.astype(vc.dtype), vc)

    n_slc = T // SLC_BLOCK
    top_n = min(SLC_TOPN, n_slc)
    slc_start = jnp.arange(n_slc) * SLC_BLOCK
    overlap = ((cmp_start[:, None] < slc_start[None, :] + SLC_BLOCK)
               & (cmp_start[:, None] + CMP_BLOCK > slc_start[None, :])).astype(F32)
    importance = jnp.einsum('bgrtn,nj->bgtj', p_cmp, overlap)
    cur = t_pos // SLC_BLOCK
    jb = jnp.arange(n_slc)
    forced = (jb[None, :] == 0) | (jb[None, :] == cur[:, None]) | (jb[None, :] == cur[:, None] - 1)
    future = jb[None, :] > cur[:, None]
    score = jnp.where(future, -jnp.inf, jnp.where(forced, FORCE_SCORE, importance))
    top_val, top_idx = lax.top_k(score, top_n)
    top_ok = top_val > -jnp.inf

    ks = rms_norm(k_slc, k_norm_g[1]).reshape(Bsz, n_slc, SLC_BLOCK, G, dh).transpose(0, 3, 1, 2, 4)
    vs = v_slc.reshape(Bsz, n_slc, SLC_BLOCK, G, dh).transpose(0, 3, 1, 2, 4)
    gather = jax.vmap(jax.vmap(lambda blocks, ids: blocks[ids]))
    nqb = T // SLC_QBLOCK
    n_keys = top_n * SLC_BLOCK
    q_sb = jnp.moveaxis(q.reshape(Bsz, G, R, nqb, SLC_QBLOCK, dh), 3, 0)
    idx_sb = jnp.moveaxis(top_idx.reshape(Bsz, G, nqb, SLC_QBLOCK, top_n), 2, 0)
    ok_sb = jnp.moveaxis(top_ok.reshape(Bsz, G, nqb, SLC_QBLOCK, top_n), 2, 0)

    def slc_block(args):
        qb, ib, okb, t0 = args
        kg = gather(ks, ib).reshape(Bsz, G, SLC_QBLOCK, n_keys, dh)
        vg = gather(vs, ib).reshape(Bsz, G, SLC_QBLOCK, n_keys, dh)
        tq = t0 + jnp.arange(SLC_QBLOCK)
        kpos = (ib[..., None] * SLC_BLOCK + jnp.arange(SLC_BLOCK)).reshape(Bsz, G, SLC_QBLOCK, n_keys)
        dist = tq[:, None] - kpos
        ok = jnp.repeat(okb, SLC_BLOCK, axis=-1) & (dist >= 0)
        s = jnp.einsum('bgrqd,bgqkd->bgrqk', qb, kg, preferred_element_type=F32) * scale \
            - slopes * dist[:, :, None].astype(F32)
        p = masked_softmax(s, ok[:, :, None])
        return jnp.einsum('bgrqk,bgqkd->bgrqd', p.astype(vg.dtype), vg)

    o_slc = lax.map(slc_block, (q_sb, idx_sb, ok_sb, jnp.arange(nqb) * SLC_QBLOCK))
    o_slc = jnp.moveaxis(o_slc, 0, 3).reshape(Bsz, G, R, T, dh)

    nwb = T // WIN_QBLOCK
    nback = WIN // WIN_QBLOCK

    def band(t):
        tb = t.reshape(Bsz, nwb, WIN_QBLOCK, G, dh)
        tb = jnp.pad(tb, ((0, 0), (nback, 0), (0, 0), (0, 0), (0, 0)))
        tb = jnp.concatenate([tb[:, i:i + nwb] for i in range(nback + 1)], axis=2)
        return tb.transpose(1, 0, 3, 2, 4)

    kw = band(rms_norm(k_win, k_norm_g[2]))
    vw = band(v_win)
    q_wb = jnp.moveaxis(q.reshape(Bsz, G, R, nwb, WIN_QBLOCK, dh), 3, 0)

    def win_block(args):
        qb, kb, vb, t0 = args
        tq = t0 + jnp.arange(WIN_QBLOCK)
        kpos = t0 - nback * WIN_QBLOCK + jnp.arange((nback + 1) * WIN_QBLOCK)
        dist = tq[:, None] - kpos[None, :]
        ok = (dist >= 0) & (dist < WIN) & (kpos[None, :] >= 0)
        s = jnp.einsum('bgrqd,bgkd->bgrqk', qb, kb, preferred_element_type=F32) * scale - slopes * dist.astype(F32)
        p = masked_softmax(s, ok)
        return jnp.einsum('bgrqk,bgkd->bgrqd', p.astype(vb.dtype), vb)

    o_win = lax.map(win_block, (q_wb, kw, vw, jnp.arange(nwb) * WIN_QBLOCK))
    o_win = jnp.moveaxis(o_win, 0, 3).reshape(Bsz, G, R, T, dh)

    o = gates[..., 0:1] * o_cmp + gates[..., 1:2] * o_slc + gates[..., 2:3] * o_win
    return o.transpose(0, 3, 1, 2, 4).reshape(Bsz, T, q_width).astype(u.dtype)


def hybrid_layer(x, c, ada_w, ada_b, norm_mix_g, norm_ffn_g, w_in,
                 ssd_conv_w, ssd_conv_b, ssd_dt_bias, ssd_a_log, ssd_d, ssd_norm_g,
                 sc_conv_w, sg_norm_g, sg_w, sg_b,
                 nsa_q_norm_g, nsa_k_norm_g, nsa_cmp_pe, nsa_cmp_w1, nsa_cmp_w2,
                 w_branch, w_branch_gate, w_out, w_ffn_in, w_ffn_out):
    mod = jax.nn.silu(c) @ ada_w + ada_b
    shift1, scale1, gate1, shift2, scale2, gate2 = jnp.split(mod[:, None, :], 6, axis=-1)
    h = rms_norm(x, norm_mix_g) * (1.0 + scale1) + shift1
    u = h @ w_in
    o1 = SSD_IN
    o2 = o1 + SC_IN
    o3 = o2 + SG_IN
    branches = (
        ssd_mixer(u[..., :o1], ssd_conv_w, ssd_conv_b, ssd_dt_bias, ssd_a_log, ssd_d, ssd_norm_g),
        short_conv_mixer(u[..., o1:o2], sc_conv_w),
        spatial_gating_mixer(u[..., o2:o3], sg_norm_g, sg_w, sg_b),
        nsa_mixer(u[..., o3:], nsa_q_norm_g, nsa_k_norm_g, nsa_cmp_pe, nsa_cmp_w1, nsa_cmp_w2),
    )
    merged = jnp.zeros_like(h)
    for i in range(N_BRANCH):
        merged = merged + jax.nn.sigmoid(h @ w_branch_gate[i]) * (branches[i] @ w_branch[i])
    x = x + gate1 * (merged @ w_out)
    h2 = rms_norm(x, norm_ffn_g) * (1.0 + scale2) + shift2
    a, b = jnp.split(h2 @ w_ffn_in, 2, axis=-1)
    return x + gate2 * ((jax.nn.silu(a) * b) @ w_ffn_out)


def setup_inputs(seed: int = 0) -> dict:
    key = jax.random.key(seed)
    ks = iter(jax.random.split(key, 32))
    L, D = DEPTH, D_MODEL

    def nrm(shape, s):
        return s * jax.random.normal(next(ks), shape, F32)

    dt = jnp.exp(jax.random.uniform(next(ks), (L, SSD_HEADS), F32, math.log(1e-3), math.log(1e-1)))
    a_init = jax.random.uniform(next(ks), (L, SSD_HEADS), F32, 1.0, 16.0)
    return {
        'x': nrm((BATCH, SEQ, D), 1.0),
        'c': nrm((BATCH, D), 1.0),
        'ada_w': nrm((L, D, 6 * D), 0.5 * D ** -0.5),
        'ada_b': nrm((L, 6 * D), 0.01),
        'norm_mix_g': 1.0 + nrm((L, D), 0.05),
        'norm_ffn_g': 1.0 + nrm((L, D), 0.05),
        'w_in': nrm((L, D, D_IN), D ** -0.5),
        'ssd_conv_w': nrm((L, SSD_CONV, SSD_XBC), SSD_CONV ** -0.5),
        'ssd_conv_b': nrm((L, SSD_XBC), 0.02),
        'ssd_dt_bias': dt + jnp.log(-jnp.expm1(-dt)),
        'ssd_a_log': jnp.log(a_init),
        'ssd_d': 1.0 + nrm((L, SSD_HEADS), 0.1),
        'ssd_norm_g': 1.0 + nrm((L, MIX_W), 0.05),
        'sc_conv_w': nrm((L, SC_CONV, MIX_W), SC_CONV ** -0.5),
        'sg_norm_g': 1.0 + nrm((L, MIX_W), 0.05),
        'sg_w': nrm((L, SG_GROUPS, SG_CHUNK, SG_CHUNK), 0.5 * SG_CHUNK ** -0.5),
        'sg_b': 1.0 + nrm((L, SG_GROUPS, SG_CHUNK), 0.1),
        'nsa_q_norm_g': 1.0 + nrm((L, NSA_HEAD_DIM), 0.05),
        'nsa_k_norm_g': 1.0 + nrm((L, 3, NSA_HEAD_DIM), 0.05),
        'nsa_cmp_pe': nrm((L, 2, CMP_BLOCK, NSA_HEAD_DIM), 0.1),
        'nsa_cmp_w1': nrm((L, 2, CMP_BLOCK * NSA_HEAD_DIM, NSA_HEAD_DIM), (CMP_BLOCK * NSA_HEAD_DIM) ** -0.5),
        'nsa_cmp_w2': nrm((L, 2, NSA_HEAD_DIM, NSA_HEAD_DIM), NSA_HEAD_DIM ** -0.5),
        'w_branch': nrm((L, N_BRANCH, MIX_W, D), MIX_W ** -0.5),
        'w_branch_gate': nrm((L, N_BRANCH, D, D), D ** -0.5),
        'w_out': nrm((L, D, D), D ** -0.5),
        'w_ffn_in': nrm((L, D, 2 * D_FF), D ** -0.5),
        'w_ffn_out': nrm((L, D_FF, D), D_FF ** -0.5),
    }


def reference(x, c, ada_w, ada_b, norm_mix_g, norm_ffn_g, w_in,
              ssd_conv_w, ssd_conv_b, ssd_dt_bias, ssd_a_log, ssd_d, ssd_norm_g,
              sc_conv_w, sg_norm_g, sg_w, sg_b,
              nsa_q_norm_g, nsa_k_norm_g, nsa_cmp_pe, nsa_cmp_w1, nsa_cmp_w2,
              w_branch, w_branch_gate, w_out, w_ffn_in, w_ffn_out):
    for l in range(DEPTH):
        x = hybrid_layer(x, c, ada_w[l], ada_b[l], norm_mix_g[l], norm_ffn_g[l], w_in[l],
                         ssd_conv_w[l], ssd_conv_b[l], ssd_dt_bias[l], ssd_a_log[l], ssd_d[l], ssd_norm_g[l],
                         sc_conv_w[l], sg_norm_g[l], sg_w[l], sg_b[l],
                         nsa_q_norm_g[l], nsa_k_norm_g[l], nsa_cmp_pe[l], nsa_cmp_w1[l], nsa_cmp_w2[l],
                         w_branch[l], w_branch_gate[l], w_out[l], w_ffn_in[l], w_ffn_out[l])
    return x
```

```python
import functools
import math

import jax
import jax.numpy as jnp
from jax import lax
from jax.experimental import pallas as pl
from jax.experimental.pallas import tpu as pltpu

F32 = jnp.float32
BF16 = jnp.bfloat16
RMS_EPS = 1e-6
NEG_INF = float("-inf")

MIX_W = 512
N_BRANCH = 4

SSD_HEAD_DIM = 64
SSD_HEADS = 8
SSD_GROUPS = 2
SSD_STATE = 128
SSD_CONV = 4
SSD_CHUNK = 128
SSD_XBC = MIX_W + 2 * SSD_GROUPS * SSD_STATE
SSD_IN = MIX_W + SSD_XBC + SSD_HEADS
SC_CONV = 3
SC_IN = 3 * MIX_W
SG_GROUPS = 4
SG_CHUNK = 128
SG_IN = 2 * MIX_W
NSA_HEADS = 8
NSA_KV_HEADS = 2
NSA_REP = NSA_HEADS // NSA_KV_HEADS
NSA_HEAD_DIM = 64
CMP_BLOCK = 32
CMP_STRIDE = 16
SLC_BLOCK = 64
SLC_TOPN = 8
WIN = 256
FORCE_SCORE = 1e9
NSA_KV_W = NSA_KV_HEADS * NSA_HEAD_DIM
NSA_Q_W = NSA_HEADS * NSA_HEAD_DIM
NSA_IN = NSA_Q_W + 6 * NSA_KV_W + 3 * NSA_HEADS

LANES = 128
SUBLANES = 8
VMEM_LIMIT_BYTES = 56 * 1024 * 1024

COL_Z = 0
COL_XBC = COL_Z + MIX_W
COL_SC = COL_XBC + SSD_XBC
COL_SG = COL_SC + SC_IN
COL_Q = COL_SG + SG_IN
COL_KV = COL_Q + NSA_Q_W
COL_MISC = COL_KV + 6 * NSA_KV_W
MISC_GATE0 = SSD_HEADS
U_TILE_N = 512
U_WIDTH = ((COL_MISC + LANES + U_TILE_N - 1) // U_TILE_N) * U_TILE_N


def _cparams(*sem):
    return pltpu.CompilerParams(dimension_semantics=sem, vmem_limit_bytes=VMEM_LIMIT_BYTES)


def _bdot(a, b):
    return jnp.dot(a.astype(BF16), b.astype(BF16), preferred_element_type=F32)


def _bdot_nt(a, b):
    return lax.dot_general(a.astype(BF16), b.astype(BF16), (((1,), (1,)), ((), ())),
                           preferred_element_type=F32)


def _split3(a):
    hi = a.astype(BF16)
    r1 = a - hi.astype(F32)
    mid = r1.astype(BF16)
    lo = (r1 - mid.astype(F32)).astype(BF16)
    return hi, mid, lo


def _dot_sel_rhs(a, sel):
    hi, mid, lo = _split3(a)
    return (jnp.dot(hi, sel, preferred_element_type=F32) + jnp.dot(mid, sel, preferred_element_type=F32)
            + jnp.dot(lo, sel, preferred_element_type=F32))


def _dot_sel_lhs(sel, a):
    hi, mid, lo = _split3(a)
    return (jnp.dot(sel, hi, preferred_element_type=F32) + jnp.dot(sel, mid, preferred_element_type=F32)
            + jnp.dot(sel, lo, preferred_element_type=F32))


def _sigmoid(x):
    return 1.0 / (1.0 + jnp.exp(-x))


def _silu(x):
    return x * _sigmoid(x)


def _gelu_tanh(x):
    c = math.sqrt(2.0 / math.pi)
    return 0.5 * x * (1.0 + jnp.tanh(c * (x + 0.044715 * (x * x * x))))


def _softplus(x):
    return jnp.maximum(x, 0.0) + jnp.log1p(jnp.exp(-jnp.abs(x)))


def _shift_rows(x, tail, k, row8):
    sh = pltpu.roll(x, k, 0)
    tl = pltpu.roll(tail, k, 0)
    top = jnp.where(row8 < k, tl, sh[0:SUBLANES])
    return jnp.concatenate([top, sh[SUBLANES:]], axis=0)


def _mod_kernel(c_ref, w_ref, b_ref, o_ref):
    o_ref[0] = _bdot(_silu(c_ref[...]), w_ref[0]) + b_ref[0]


def _modulation(c, ada_w, ada_b):
    L, D, D6 = ada_w.shape
    B = c.shape[0]
    tn = D6 // 4
    return pl.pallas_call(
        _mod_kernel,
        grid=(L, D6 // tn),
        in_specs=[pl.BlockSpec((B, D), lambda l, n: (0, 0)),
                  pl.BlockSpec((1, D, tn), lambda l, n: (l, 0, n)),
                  pl.BlockSpec((1, 1, tn), lambda l, n: (l, 0, n))],
        out_specs=pl.BlockSpec((1, B, tn), lambda l, n: (l, 0, n)),
        out_shape=jax.ShapeDtypeStruct((L, B, D6), F32),
        compiler_params=_cparams("parallel", "parallel"),
        name="adaln_modulation",
    )(c, ada_w, ada_b.reshape(L, 1, D6))


def _in_kernel(x_ref, mod_ref, g_ref, w_ref, u_ref, h_ref):
    @pl.when(pl.program_id(1) == 0)
    def _():
        x = x_ref[...]
        y = x * lax.rsqrt(jnp.mean(x * x, axis=-1, keepdims=True) + RMS_EPS) * g_ref[...]
        h = y * (1.0 + mod_ref[0, 1:2, :]) + mod_ref[0, 0:1, :]
        h_ref[...] = h.astype(BF16)

    u_ref[...] = jnp.dot(h_ref[...], w_ref[...], preferred_element_type=F32)


def _in_proj(x2d, mod_l, norm_g, w_cat, T):
    M, D = x2d.shape
    tm = min(1024, T)
    per_b = T // tm
    return pl.pallas_call(
        _in_kernel,
        grid=(M // tm, U_WIDTH // U_TILE_N),
        in_specs=[pl.BlockSpec((tm, D), lambda m, n: (m, 0)),
                  pl.BlockSpec((1, 6, D), lambda m, n: (m // per_b, 0, 0)),
                  pl.BlockSpec((1, D), lambda m, n: (0, 0)),
                  pl.BlockSpec((D, U_TILE_N), lambda m, n: (0, n))],
        out_specs=[pl.BlockSpec((tm, U_TILE_N), lambda m, n: (m, n)),
                   pl.BlockSpec((tm, D), lambda m, n: (m, 0))],
        out_shape=[jax.ShapeDtypeStruct((M, U_WIDTH), F32), jax.ShapeDtypeStruct((M, D), BF16)],
        compiler_params=_cparams("parallel", "arbitrary"),
        name="norm_in_proj",
    )(x2d, mod_l, norm_g.reshape(1, D), w_cat)


def _ssd_kernel(z_ref, xa_ref, xb_ref, misc_ref, cw_ref, cb_ref, dtb_ref, alog_ref, dsk_ref, ng_ref,
                o_ref, tail_ref, st_ref):
    Q, P, N, H, G = SSD_CHUNK, SSD_HEAD_DIM, SSD_STATE, SSD_HEADS, SSD_GROUPS
    R = H // G

    @pl.when(pl.program_id(1) == 0)
    def _():
        tail_ref[...] = jnp.zeros_like(tail_ref)
        st_ref[...] = jnp.zeros_like(st_ref)

    xin = jnp.concatenate([xa_ref[...], xb_ref[...]], axis=1)
    tail = tail_ref[...]
    row8 = lax.broadcasted_iota(jnp.int32, (SUBLANES, SSD_XBC), 0)
    acc = xin * cw_ref[SSD_CONV - 1:SSD_CONV, :] + cb_ref[...]
    for k in range(1, SSD_CONV):
        acc = acc + _shift_rows(xin, tail, k, row8) * cw_ref[SSD_CONV - 1 - k:SSD_CONV - k, :]
    tail_ref[...] = xin[Q - SUBLANES:Q, :]
    xbc = _silu(acc)
    xs = xbc[:, :MIX_W]
    bm = xbc[:, MIX_W:MIX_W + G * N]
    cm = xbc[:, MIX_W + G * N:]

    lane = lax.broadcasted_iota(jnp.int32, (Q, LANES), 1)
    rowi = lax.broadcasted_iota(jnp.int32, (Q, LANES), 0)
    is_head = lane < H
    dt = jnp.where(is_head, _softplus(misc_ref[...] + dtb_ref[...]), 0.0)
    a = dt * (-jnp.exp(alog_ref[...]))
    tri = (lane <= rowi).astype(BF16)
    a_cs = _dot_sel_lhs(tri, a)
    a_cs_t = _dot_sel_rhs(a.T, (rowi <= lane).astype(BF16))
    a_last = a_cs[Q - 1:Q, :]
    ea = jnp.exp(a_cs)
    dec = jnp.exp(a_last - a_cs)
    e_row = lax.broadcasted_iota(jnp.int32, (LANES, MIX_W), 0)
    e_col = lax.broadcasted_iota(jnp.int32, (LANES, MIX_W), 1)
    expand = (jnp.right_shift(e_col, 6) == e_row).astype(BF16)
    xdt = xs * _dot_sel_rhs(dt, expand)
    ea_e = _dot_sel_rhs(ea, expand)
    xdec = xdt * _dot_sel_rhs(dec, expand)
    causal = lane <= rowi

    ys = []
    for g in range(G):
        bg = bm[:, g * N:(g + 1) * N]
        cg = cm[:, g * N:(g + 1) * N].astype(BF16)
        cb = _bdot_nt(cg, bg)
        bg_t = bg.T.astype(BF16)
        for r in range(R):
            h = g * R + r
            seg = jnp.where(causal, a_cs[:, h:h + 1] - a_cs_t[h:h + 1, :], NEG_INF)
            y_diag = _bdot(cb * jnp.exp(seg), xdt[:, h * P:(h + 1) * P])
            state = st_ref[h]
            y_off = _bdot(cg, state) * ea_e[:, h * P:(h + 1) * P]
            st_ref[h] = state * jnp.exp(a_last[:, h:h + 1]) + _bdot(bg_t, xdec[:, h * P:(h + 1) * P])
            ys.append(y_diag + y_off)
    y = jnp.concatenate(ys, axis=1) + xs * dsk_ref[...]
    y = y * _silu(z_ref[...])
    gw = MIX_W // G
    outs = []
    for g in range(G):
        yg = y[:, g * gw:(g + 1) * gw]
        outs.append(yg * lax.rsqrt(jnp.mean(yg * yg, axis=-1, keepdims=True) + RMS_EPS))
    o_ref[...] = (jnp.concatenate(outs, axis=1) * ng_ref[...]).astype(BF16)


def _ssd_mixer(u, B, T, conv_w, conv_b, dt_bias, a_log, d_skip, norm_g):
    nc = T // SSD_CHUNK
    Q = SSD_CHUNK

    def pad_lane(v):
        return jnp.pad(v, (0, LANES - v.shape[0])).reshape(1, LANES)

    row = lambda b, c: b * nc + c
    full = lambda shape: pl.BlockSpec(shape, lambda b, c: (0,) * len(shape))
    return pl.pallas_call(
        _ssd_kernel,
        grid=(B, nc),
        in_specs=[pl.BlockSpec((Q, MIX_W), lambda b, c: (row(b, c), COL_Z // MIX_W)),
                  pl.BlockSpec((Q, MIX_W), lambda b, c: (row(b, c), COL_XBC // MIX_W)),
                  pl.BlockSpec((Q, MIX_W), lambda b, c: (row(b, c), COL_XBC // MIX_W + 1)),
                  pl.BlockSpec((Q, LANES), lambda b, c: (row(b, c), COL_MISC // LANES)),
                  full((SSD_CONV, SSD_XBC)), full((1, SSD_XBC)), full((1, LANES)), full((1, LANES)),
                  full((1, MIX_W)), full((1, MIX_W))],
        out_specs=pl.BlockSpec((Q, MIX_W), lambda b, c: (row(b, c), 0)),
        out_shape=jax.ShapeDtypeStruct((B * T, MIX_W), BF16),
        scratch_shapes=[pltpu.VMEM((SUBLANES, SSD_XBC), F32),
                        pltpu.VMEM((SSD_HEADS, SSD_STATE, SSD_HEAD_DIM), F32)],
        compiler_params=_cparams("parallel", "arbitrary"),
        name="ssd_mixer",
    )(u, u, u, u, conv_w, conv_b.reshape(1, SSD_XBC), pad_lane(dt_bias), pad_lane(a_log),
      jnp.repeat(d_skip, SSD_HEAD_DIM).reshape(1, MIX_W), norm_g.reshape(1, MIX_W))


def _sc_kernel(b_ref, c_ref, h_ref, w_ref, o_ref, tail_ref):
    @pl.when(pl.program_id(1) == 0)
    def _():
        tail_ref[...] = jnp.zeros_like(tail_ref)

    cx = c_ref[...] * h_ref[...]
    tt = cx.shape[0]
    tail = tail_ref[...]
    row8 = lax.broadcasted_iota(jnp.int32, (SUBLANES, MIX_W), 0)
    acc = cx * w_ref[SC_CONV - 1:SC_CONV, :]
    for k in range(1, SC_CONV):
        acc = acc + _shift_rows(cx, tail, k, row8) * w_ref[SC_CONV - 1 - k:SC_CONV - k, :]
    tail_ref[...] = cx[tt - SUBLANES:tt, :]
    o_ref[...] = (b_ref[...] * acc).astype(BF16)


def _short_conv_mixer(u, B, T, conv_w):
    tt = min(512, T)
    nt = T // tt
    c0 = COL_SC // MIX_W
    spec = lambda j: pl.BlockSpec((tt, MIX_W), lambda b, i: (b * nt + i, c0 + j))
    return pl.pallas_call(
        _sc_kernel,
        grid=(B, nt),
        in_specs=[spec(0), spec(1), spec(2), pl.BlockSpec((SC_CONV, MIX_W), lambda b, i: (0, 0))],
        out_specs=pl.BlockSpec((tt, MIX_W), lambda b, i: (b * nt + i, 0)),
        out_shape=jax.ShapeDtypeStruct((B * T, MIX_W), BF16),
        scratch_shapes=[pltpu.VMEM((SUBLANES, MIX_W), F32)],
        compiler_params=_cparams("parallel", "arbitrary"),
        name="short_conv_mixer",
    )(u, u, u, conv_w)


def _sg_kernel(u_ref, v_ref, ng_ref, w_ref, bias_ref, o_ref):
    gate_u = _gelu_tanh(u_ref[...])
    v = _gelu_tanh(v_ref[...])
    v = v * lax.rsqrt(jnp.mean(v * v, axis=-1, keepdims=True) + RMS_EPS) * ng_ref[...]
    Q = SG_CHUNK
    rowi = lax.broadcasted_iota(jnp.int32, (Q, Q), 0)
    coli = lax.broadcasted_iota(jnp.int32, (Q, Q), 1)
    gd = MIX_W // SG_GROUPS
    outs = []
    for g in range(SG_GROUPS):
        w = jnp.where(coli <= rowi, w_ref[g], 0.0)
        outs.append(_bdot(w, v[:, g * gd:(g + 1) * gd]))
    o_ref[...] = (gate_u * (jnp.concatenate(outs, axis=1) + bias_ref[...])).astype(BF16)


def _spatial_gating_mixer(u, B, T, norm_g, w_s, b_s):
    Q = SG_CHUNK
    nc = T // Q
    c0 = COL_SG // MIX_W
    bias = jnp.repeat(b_s.T, MIX_W // SG_GROUPS, axis=1)
    return pl.pallas_call(
        _sg_kernel,
        grid=(B, nc),
        in_specs=[pl.BlockSpec((Q, MIX_W), lambda b, c: (b * nc + c, c0)),
                  pl.BlockSpec((Q, MIX_W), lambda b, c: (b * nc + c, c0 + 1)),
                  pl.BlockSpec((1, MIX_W), lambda b, c: (0, 0)),
                  pl.BlockSpec((SG_GROUPS, Q, Q), lambda b, c: (0, 0, 0)),
                  pl.BlockSpec((Q, MIX_W), lambda b, c: (0, 0))],
        out_specs=pl.BlockSpec((Q, MIX_W), lambda b, c: (b * nc + c, 0)),
        out_shape=jax.ShapeDtypeStruct((B * T, MIX_W), BF16),
        compiler_params=_cparams("parallel", "parallel"),
        name="spatial_gating_mixer",
    )(u, u, norm_g.reshape(1, MIX_W), w_s, bias)


def _group_mean_sq(x, width):
    n = x.shape[1]
    r = lax.broadcasted_iota(jnp.int32, (n, n), 0)
    c = lax.broadcasted_iota(jnp.int32, (n, n), 1)
    sh = width.bit_length() - 1
    same = (jnp.right_shift(r, sh) == jnp.right_shift(c, sh)).astype(BF16)
    return _dot_sel_rhs(x * x, same) * (1.0 / width)


def _nsa_prep_kernel(q_ref, kc_ref, vc_ref, ks_ref, vs_ref, kw_ref, vw_ref, misc_ref, qg_ref, ksg_ref, kwg_ref,
                     qn_ref, kvc_ref, ksn_ref, vsb_ref, kwn_ref, vwb_ref, gate_ref):
    dh = NSA_HEAD_DIM
    q = q_ref[...]
    qn = q * lax.rsqrt(_group_mean_sq(q, dh) + RMS_EPS) * qg_ref[...]
    qn_ref[...] = (qn * (dh ** -0.5)).astype(BF16)
    kvc_ref[...] = jnp.concatenate([kc_ref[...], vc_ref[...]], axis=1).astype(BF16)
    ks = ks_ref[...]
    ksn = (ks * lax.rsqrt(_group_mean_sq(ks, dh) + RMS_EPS) * ksg_ref[...]).astype(BF16)
    kw = kw_ref[...]
    kwn = (kw * lax.rsqrt(_group_mean_sq(kw, dh) + RMS_EPS) * kwg_ref[...]).astype(BF16)
    vs = vs_ref[...].astype(BF16)
    vw = vw_ref[...].astype(BF16)
    for g in range(NSA_KV_HEADS):
        sl = slice(g * dh, (g + 1) * dh)
        ksn_ref[0, g] = ksn[:, sl]
        vsb_ref[0, g] = vs[:, sl]
        kwn_ref[0, g] = kwn[:, sl]
        vwb_ref[0, g] = vw[:, sl]
    gate_ref[...] = _sigmoid(misc_ref[...])


def _nsa_prep(u, B, T, q_norm_g, k_norm_g):
    tt = min(512, T)
    nt = T // tt
    G, dh = NSA_KV_HEADS, NSA_HEAD_DIM
    kv0 = COL_KV // LANES
    kvspec = lambda j: pl.BlockSpec((tt, LANES), lambda b, i: (b * nt + i, kv0 + j))
    vec = lambda n: pl.BlockSpec((1, n), lambda b, i: (0, 0))
    hspec = pl.BlockSpec((1, G, tt, dh), lambda b, i: (b, 0, i, 0))
    hshape = jax.ShapeDtypeStruct((B, G, T, dh), BF16)
    return pl.pallas_call(
        _nsa_prep_kernel,
        grid=(B, nt),
        in_specs=[pl.BlockSpec((tt, NSA_Q_W), lambda b, i: (b * nt + i, COL_Q // NSA_Q_W)),
                  kvspec(0), kvspec(1), kvspec(2), kvspec(3), kvspec(4), kvspec(5),
                  pl.BlockSpec((tt, LANES), lambda b, i: (b * nt + i, COL_MISC // LANES)),
                  vec(NSA_Q_W), vec(LANES), vec(LANES)],
        out_specs=[pl.BlockSpec((tt, NSA_Q_W), lambda b, i: (b * nt + i, 0)),
                   pl.BlockSpec((tt, 2 * LANES), lambda b, i: (b * nt + i, 0)),
                   hspec, hspec, hspec, hspec,
                   pl.BlockSpec((tt, LANES), lambda b, i: (b * nt + i, 0))],
        out_shape=[jax.ShapeDtypeStruct((B * T, NSA_Q_W), BF16),
                   jax.ShapeDtypeStruct((B * T, 2 * LANES), BF16),
                   hshape, hshape, hshape, hshape,
                   jax.ShapeDtypeStruct((B * T, LANES), F32)],
        compiler_params=_cparams("parallel", "parallel"),
        name="nsa_prep",
    )(u, u, u, u, u, u, u, u, jnp.tile(q_norm_g, NSA_HEADS).reshape(1, NSA_Q_W),
      jnp.tile(k_norm_g[1], G).reshape(1, LANES), jnp.tile(k_norm_g[2], G).reshape(1, LANES))


def _nsa_cmp_kernel(x_ref, w1_ref, pe_ref, w2_ref, kg_ref, kc_ref, vc_ref):
    G = NSA_KV_HEADS
    half = CMP_STRIDE * NSA_HEAD_DIM
    for j in range(2):
        w1 = w1_ref[j]
        pe_term = jnp.dot(pe_ref[j], w1, preferred_element_type=F32)[0:1, :]
        for g in range(G):
            x = x_ref[0, j * G + g]
            lo = jnp.dot(x, w1[:half], preferred_element_type=F32)
            hi = jnp.dot(x, w1[half:], preferred_element_type=F32)
            nseg = x.shape[0]
            y = lo + pltpu.roll(hi, nseg - 1, 0) + pe_term
            r = _bdot(_gelu_tanh(y), w2_ref[j])
            if j == 0:
                r = r * lax.rsqrt(jnp.mean(r * r, axis=-1, keepdims=True) + RMS_EPS) * kg_ref[...]
                kc_ref[0, g] = r.astype(BF16)
            else:
                vc_ref[0, g] = r.astype(BF16)


def _nsa_compress(xseg, cmp_pe, cmp_w1, cmp_w2, k_norm_g0):
    B, _, nseg, width = xseg.shape
    G, dh = NSA_KV_HEADS, NSA_HEAD_DIM
    pe = jnp.broadcast_to(cmp_pe.reshape(2, 1, CMP_BLOCK * dh), (2, SUBLANES, CMP_BLOCK * dh)).astype(BF16)
    full = lambda shape: pl.BlockSpec(shape, lambda b: (0,) * len(shape))
    oshape = jax.ShapeDtypeStruct((B, G, nseg, dh), BF16)
    ospec = pl.BlockSpec((1, G, nseg, dh), lambda b: (b, 0, 0, 0))
    return pl.pallas_call(
        _nsa_cmp_kernel,
        grid=(B,),
        in_specs=[pl.BlockSpec((1, 2 * G, nseg, width), lambda b: (b, 0, 0, 0)),
                  full((2, CMP_BLOCK * dh, dh)), full((2, SUBLANES, CMP_BLOCK * dh)), full((2, dh, dh)),
                  full((1, dh))],
        out_specs=[ospec, ospec],
        out_shape=[oshape, oshape],
        compiler_params=_cparams("parallel"),
        name="nsa_compress",
    )(xseg, cmp_w1.astype(BF16), pe, cmp_w2.astype(BF16), k_norm_g0.reshape(1, dh))


def _masked_softmax(s, ok):
    s = jnp.where(ok, s, NEG_INF)
    m = jnp.max(s, axis=-1, keepdims=True)
    m = jnp.where(m > NEG_INF, m, 0.0)
    p = jnp.exp(s - m)
    return p / jnp.maximum(jnp.sum(p, axis=-1, keepdims=True), 1e-30)


def _nsa_attn_kernel(q_ref, gate_ref, kc_ref, vc_ref, ks_ref, vs_ref, kw_ref, vw_ref, o_ref,
                     m_ref, l_ref, acc_ref, *, seq_len):
    G, R, dh = NSA_KV_HEADS, NSA_REP, NSA_HEAD_DIM
    TQ = LANES
    i = pl.program_id(1)
    t0 = i * TQ
    rowi = lax.broadcasted_iota(jnp.int32, (TQ, LANES), 0)
    lane = lax.broadcasted_iota(jnp.int32, (TQ, LANES), 1)
    tq = t0 + rowi
    gates = gate_ref[...]
    q = q_ref[...]

    dist_c = tq - (lane * CMP_STRIDE + (CMP_BLOCK - 1))
    ok_c = dist_c >= 0
    dist_cf = dist_c.astype(F32)
    overlap = ((rowi * CMP_STRIDE < lane * SLC_BLOCK + SLC_BLOCK)
               & (rowi * CMP_STRIDE + CMP_BLOCK > lane * SLC_BLOCK)
               & (lane < seq_len // SLC_BLOCK)).astype(BF16)
    cur = jnp.right_shift(tq, 6)
    lane_f = lane.astype(F32)
    future = lane > cur
    forced = (lane == 0) | (lane == cur) | (lane == cur - 1)

    wstart = jnp.maximum(i - (WIN // TQ), 0) * TQ
    wlen = (WIN // TQ + 1) * TQ
    wrow = lax.broadcasted_iota(jnp.int32, (TQ, wlen), 0)
    wlane = lax.broadcasted_iota(jnp.int32, (TQ, wlen), 1)
    dist_w = (t0 + wrow) - (wstart + wlane)
    ok_w = (dist_w >= 0) & (dist_w < WIN)
    dist_wf = dist_w.astype(F32)

    for g in range(G):
        qs = jnp.concatenate([q[:, (g * R + r) * dh:(g * R + r + 1) * dh] for r in range(R)], axis=0)
        slopes = [2.0 ** -(g * R + r + 1) for r in range(R)]

        s_c = _bdot_nt(qs, kc_ref[0, g])
        p_c = [_masked_softmax(s_c[r * TQ:(r + 1) * TQ] - slopes[r] * dist_cf, ok_c) for r in range(R)]
        o_cmp = _bdot(jnp.concatenate(p_c, axis=0), vc_ref[0, g])

        importance = _dot_sel_rhs(p_c[0] + p_c[1] + p_c[2] + p_c[3], overlap)
        score = jnp.where(future, NEG_INF, jnp.where(forced, FORCE_SCORE, importance))
        work = score
        picked = lane < 0
        for _ in range(SLC_TOPN):
            mx = jnp.max(work, axis=-1, keepdims=True)
            idx = jnp.min(jnp.where(work == mx, lane_f, float(LANES)), axis=-1, keepdims=True)
            hit = lane_f == idx
            picked = picked | hit
            work = jnp.where(hit, NEG_INF, work)
        sel = (picked & (score > NEG_INF)).astype(BF16)

        m_ref[...] = jnp.full(m_ref.shape, NEG_INF, F32)
        l_ref[...] = jnp.zeros(l_ref.shape, F32)
        acc_ref[...] = jnp.zeros(acc_ref.shape, F32)

        def slc_step(j, carry):
            k = ks_ref[0, g, pl.ds(pl.multiple_of(j * TQ, TQ), TQ), :]
            v = vs_ref[0, g, pl.ds(pl.multiple_of(j * TQ, TQ), TQ), :]
            s = _bdot_nt(qs, k)
            expand = (rowi == j * (TQ // SLC_BLOCK) + jnp.right_shift(lane, 6)).astype(BF16)
            dist = tq - (j * TQ + lane)
            ok = (jnp.dot(sel, expand, preferred_element_type=F32) > 0.5) & (dist >= 0)
            distf = dist.astype(F32)
            ps = []
            for r in range(R):
                rs = slice(r * TQ, (r + 1) * TQ)
                sr = jnp.where(ok, s[rs] - slopes[r] * distf, NEG_INF)
                m_old = m_ref[rs]
                m_new = jnp.maximum(m_old, jnp.max(sr, axis=-1, keepdims=True))
                m_safe = jnp.where(m_new > NEG_INF, m_new, 0.0)
                alpha = jnp.exp(m_old - m_safe)
                p = jnp.exp(sr - m_safe)
                l_ref[rs] = alpha * l_ref[rs] + jnp.sum(p, axis=-1, keepdims=True)
                acc_ref[rs] = alpha * acc_ref[rs]
                m_ref[rs] = m_new
                ps.append(p)
            acc_ref[...] += _bdot(jnp.concatenate(ps, axis=0), v)
            return carry

        lax.fori_loop(0, i + 1, slc_step, 0)
        o_slc = acc_ref[...] / jnp.maximum(l_ref[...], 1e-30)

        kw = kw_ref[0, g, pl.ds(pl.multiple_of(wstart, TQ), wlen), :]
        vw = vw_ref[0, g, pl.ds(pl.multiple_of(wstart, TQ), wlen), :]
        s_w = _bdot_nt(qs, kw)
        p_w = [_masked_softmax(s_w[r * TQ:(r + 1) * TQ] - slopes[r] * dist_wf, ok_w) for r in range(R)]
        o_win = _bdot(jnp.concatenate(p_w, axis=0), vw)

        for r in range(R):
            h = g * R + r
            rs = slice(r * TQ, (r + 1) * TQ)
            c = MISC_GATE0 + 3 * h
            o = (gates[:, c:c + 1] * o_cmp[rs] + gates[:, c + 1:c + 2] * o_slc[rs]
                 + gates[:, c + 2:c + 3] * o_win[rs])
            o_ref[:, h * dh:(h + 1) * dh] = o.astype(BF16)


def _nsa_attention(qn, gates, kc, vc, ksn, vsb, kwn, vwb, B, T):
    G, R, dh = NSA_KV_HEADS, NSA_REP, NSA_HEAD_DIM
    TQ = LANES
    nq = T // TQ
    nseg = kc.shape[2]
    cspec = pl.BlockSpec((1, G, nseg, dh), lambda b, i: (b, 0, 0, 0))
    tspec = pl.BlockSpec((1, G, T, dh), lambda b, i: (b, 0, 0, 0))
    return pl.pallas_call(
        functools.partial(_nsa_attn_kernel, seq_len=T),
        grid=(B, nq),
        in_specs=[pl.BlockSpec((TQ, NSA_Q_W), lambda b, i: (b * nq + i, 0)),
                  pl.BlockSpec((TQ, LANES), lambda b, i: (b * nq + i, 0)),
                  cspec, cspec, tspec, tspec, tspec, tspec],
        out_specs=pl.BlockSpec((TQ, NSA_Q_W), lambda b, i: (b * nq + i, 0)),
        out_shape=jax.ShapeDtypeStruct((B * T, NSA_Q_W), BF16),
        scratch_shapes=[pltpu.VMEM((R * TQ, 1), F32), pltpu.VMEM((R * TQ, 1), F32),
                        pltpu.VMEM((R * TQ, dh), F32)],
        compiler_params=_cparams("parallel", "arbitrary"),
        name="nsa_attention",
    )(qn, gates, kc, vc, ksn, vsb, kwn, vwb)


def _nsa_mixer(u, B, T, q_norm_g, k_norm_g, cmp_pe, cmp_w1, cmp_w2):
    G, dh = NSA_KV_HEADS, NSA_HEAD_DIM
    assert T % LANES == 0 and T >= (WIN // LANES + 1) * LANES and T // CMP_STRIDE == LANES
    qn, kvc, ksn, vsb, kwn, vwb, gates = _nsa_prep(u, B, T, q_norm_g, k_norm_g)
    nseg = T // CMP_STRIDE
    xseg = kvc.reshape(B, nseg, CMP_STRIDE, 2 * G, dh).transpose(0, 3, 1, 2, 4).reshape(
        B, 2 * G, nseg, CMP_STRIDE * dh)
    kc, vc = _nsa_compress(xseg, cmp_pe, cmp_w1, cmp_w2, k_norm_g[0])
    return _nsa_attention(qn, gates, kc, vc, ksn, vsb, kwn, vwb, B, T)


def _merge_kernel(x_ref, h_ref, oa_ref, ob_ref, oc_ref, od_ref, mod_ref, wg_ref, wb_ref, wo_ref, y_ref):
    h = h_ref[...]
    merged = None
    for i, o_ref in enumerate((oa_ref, ob_ref, oc_ref, od_ref)):
        gate = _sigmoid(jnp.dot(h, wg_ref[i], preferred_element_type=F32))
        term = gate * jnp.dot(o_ref[...], wb_ref[i], preferred_element_type=F32)
        merged = term if merged is None else merged + term
    y_ref[...] = x_ref[...] + mod_ref[0, 2:3, :] * _bdot(merged, wo_ref[...])


def _merge(x2d, h, outs, mod_l, wg, wb, wo, T):
    M, D = x2d.shape
    tm = min(512, T)
    per_b = T // tm
    row = lambda w: pl.BlockSpec((tm, w), lambda m: (m, 0))
    const = lambda shape: pl.BlockSpec(shape, lambda m: (0,) * len(shape), pipeline_mode=pl.Buffered(1))
    return pl.pallas_call(
        _merge_kernel,
        grid=(M // tm,),
        in_specs=[row(D), row(D), row(MIX_W), row(MIX_W), row(MIX_W), row(MIX_W),
                  pl.BlockSpec((1, 6, D), lambda m: (m // per_b, 0, 0)),
                  const((N_BRANCH, D, D)), const((N_BRANCH, MIX_W, D)), const((D, D))],
        out_specs=row(D),
        out_shape=jax.ShapeDtypeStruct((M, D), F32),
        compiler_params=_cparams("parallel"),
        name="gated_merge_out_proj",
    )(x2d, h, *outs, mod_l, wg, wb, wo)


def _ffn_kernel(x_ref, mod_ref, g_ref, wa_ref, wb_ref, wo_ref, y_ref, h_ref, acc_ref):
    f = pl.program_id(1)

    @pl.when(f == 0)
    def _():
        x = x_ref[...]
        y = x * lax.rsqrt(jnp.mean(x * x, axis=-1, keepdims=True) + RMS_EPS) * g_ref[...]
        h_ref[...] = (y * (1.0 + mod_ref[0, 4:5, :]) + mod_ref[0, 3:4, :]).astype(BF16)
        acc_ref[...] = jnp.zeros_like(acc_ref)

    h = h_ref[...]
    a = jnp.dot(h, wa_ref[...], preferred_element_type=F32)
    b = jnp.dot(h, wb_ref[...], preferred_element_type=F32)
    acc_ref[...] += _bdot(_silu(a) * b, wo_ref[...])

    @pl.when(f == pl.num_programs(1) - 1)
    def _():
        y_ref[...] = x_ref[...] + mod_ref[0, 5:6, :] * acc_ref[...]


def _ffn(x2d, mod_l, norm_g, w_in, w_out, T):
    M, D = x2d.shape
    d_ff = w_out.shape[0]
    tm = min(512, T)
    per_b = T // tm
    nf = 2
    tf = d_ff // nf
    return pl.pallas_call(
        _ffn_kernel,
        grid=(M // tm, nf),
        in_specs=[pl.BlockSpec((tm, D), lambda m, f: (m, 0)),
                  pl.BlockSpec((1, 6, D), lambda m, f: (m // per_b, 0, 0)),
                  pl.BlockSpec((1, D), lambda m, f: (0, 0)),
                  pl.BlockSpec((D, tf), lambda m, f: (0, f)),
                  pl.BlockSpec((D, tf), lambda m, f: (0, f + nf)),
                  pl.BlockSpec((tf, D), lambda m, f: (f, 0))],
        out_specs=pl.BlockSpec((tm, D), lambda m, f: (m, 0)),
        out_shape=jax.ShapeDtypeStruct((M, D), F32),
        scratch_shapes=[pltpu.VMEM((tm, D), BF16), pltpu.VMEM((tm, D), F32)],
        compiler_params=_cparams("parallel", "arbitrary"),
        name="swiglu_ffn",
    )(x2d, mod_l, norm_g.reshape(1, D), w_in, w_in, w_out)


def _pack_w_in(w_in):
    D = w_in.shape[0]
    o1 = SSD_IN
    o2 = o1 + SC_IN
    o3 = o2 + SG_IN
    q0 = o3
    kv0 = q0 + NSA_Q_W
    gt0 = kv0 + 6 * NSA_KV_W
    parts = [w_in[:, :MIX_W + SSD_XBC],
             w_in[:, o1:o3],
             w_in[:, q0:gt0],
             w_in[:, MIX_W + SSD_XBC:o1],
             w_in[:, gt0:gt0 + 3 * NSA_HEADS]]
    w = jnp.concatenate(parts, axis=1)
    return jnp.pad(w, ((0, 0), (0, U_WIDTH - w.shape[1]))).astype(BF16)


def kernel(x, c, ada_w, ada_b, norm_mix_g, norm_ffn_g, w_in, ssd_conv_w, ssd_conv_b, ssd_dt_bias, ssd_a_log, ssd_d,
           ssd_norm_g, sc_conv_w, sg_norm_g, sg_w, sg_b, nsa_q_norm_g, nsa_k_norm_g, nsa_cmp_pe, nsa_cmp_w1,
           nsa_cmp_w2, w_branch, w_branch_gate, w_out, w_ffn_in, w_ffn_out):
    B, T, D = x.shape
    L = w_in.shape[0]
    mod = _modulation(c, ada_w, ada_b).reshape(L, B, 6, D)
    x2d = x.reshape(B * T, D)
    for l in range(L):
        u, h = _in_proj(x2d, mod[l], norm_mix_g[l], _pack_w_in(w_in[l]), T)
        outs = (
            _ssd_mixer(u, B, T, ssd_conv_w[l], ssd_conv_b[l], ssd_dt_bias[l], ssd_a_log[l], ssd_d[l],
                       ssd_norm_g[l]),
            _short_conv_mixer(u, B, T, sc_conv_w[l]),
            _spatial_gating_mixer(u, B, T, sg_norm_g[l], sg_w[l], sg_b[l]),
            _nsa_mixer(u, B, T, nsa_q_norm_g[l], nsa_k_norm_g[l], nsa_cmp_pe[l], nsa_cmp_w1[l], nsa_cmp_w2[l]),
        )
        x2d = _merge(x2d, h, outs, mod[l], w_branch_gate[l].astype(BF16), w_branch[l].astype(BF16),
                     w_out[l].astype(BF16), T)
        x2d = _ffn(x2d, mod[l], norm_ffn_g[l], w_ffn_in[l].astype(BF16), w_ffn_out[l].astype(BF16), T)
    return x2d.reshape(B, T, D)
```

```python
import functools
import math

import jax
import jax.numpy as jnp
from jax import lax
from jax.experimental import pallas as pl
from jax.experimental.pallas import tpu as pltpu

F32 = jnp.float32
BF16 = jnp.bfloat16
RMS_EPS = 1e-6
NEG_INF = float("-inf")

MIX_W = 512
N_BRANCH = 4

SSD_HEAD_DIM = 64
SSD_HEADS = 8
SSD_GROUPS = 2
SSD_STATE = 128
SSD_CONV = 4
SSD_CHUNK = 128
SSD_XBC = MIX_W + 2 * SSD_GROUPS * SSD_STATE
SSD_IN = MIX_W + SSD_XBC + SSD_HEADS
SC_CONV = 3
SC_IN = 3 * MIX_W
SG_GROUPS = 4
SG_CHUNK = 128
SG_IN = 2 * MIX_W
NSA_HEADS = 8
NSA_KV_HEADS = 2
NSA_REP = NSA_HEADS // NSA_KV_HEADS
NSA_HEAD_DIM = 64
CMP_BLOCK = 32
CMP_STRIDE = 16
SLC_BLOCK = 64
SLC_TOPN = 8
WIN = 256
FORCE_SCORE = 1e9
NSA_KV_W = NSA_KV_HEADS * NSA_HEAD_DIM
NSA_Q_W = NSA_HEADS * NSA_HEAD_DIM
NSA_IN = NSA_Q_W + 6 * NSA_KV_W + 3 * NSA_HEADS

LANES = 128
SUBLANES = 8
VMEM_LIMIT_BYTES = 56 * 1024 * 1024

COL_Z = 0
COL_XBC = COL_Z + MIX_W
COL_SC = COL_XBC + SSD_XBC
COL_SG = COL_SC + SC_IN
COL_Q = COL_SG + SG_IN
COL_KV = COL_Q + NSA_Q_W
COL_MISC = COL_KV + 6 * NSA_KV_W
MISC_GATE0 = SSD_HEADS
GATE_ROWS = 32
SLC_SHIFT = 6
SLC_TILE = 256
U_TILE_N = 512
U_WIDTH = ((COL_MISC + LANES + U_TILE_N - 1) // U_TILE_N) * U_TILE_N


def _cparams(*sem):
    return pltpu.CompilerParams(dimension_semantics=sem, vmem_limit_bytes=VMEM_LIMIT_BYTES)


def _bdot(a, b):
    return jnp.dot(a.astype(BF16), b.astype(BF16), preferred_element_type=F32)


def _bdot_nt(a, b):
    return lax.dot_general(a.astype(BF16), b.astype(BF16), (((1,), (1,)), ((), ())),
                           preferred_element_type=F32)


def _split3(a):
    hi = a.astype(BF16)
    r1 = a - hi.astype(F32)
    mid = r1.astype(BF16)
    lo = (r1 - mid.astype(F32)).astype(BF16)
    return hi, mid, lo


def _dot_sel_rhs(a, sel):
    hi, mid, lo = _split3(a)
    return (jnp.dot(hi, sel, preferred_element_type=F32) + jnp.dot(mid, sel, preferred_element_type=F32)
            + jnp.dot(lo, sel, preferred_element_type=F32))


def _dot_sel_lhs(sel, a):
    hi, mid, lo = _split3(a)
    return (jnp.dot(sel, hi, preferred_element_type=F32) + jnp.dot(sel, mid, preferred_element_type=F32)
            + jnp.dot(sel, lo, preferred_element_type=F32))


def _sigmoid(x):
    return 1.0 / (1.0 + jnp.exp(-x))


def _silu(x):
    return x * _sigmoid(x)


def _gelu_tanh(x):
    c = math.sqrt(2.0 / math.pi)
    return 0.5 * x * (1.0 + jnp.tanh(c * (x + 0.044715 * (x * x * x))))


def _softplus(x):
    return jnp.maximum(x, 0.0) + jnp.log1p(jnp.exp(-jnp.abs(x)))


def _shift_rows(x, tail, k, row8):
    sh = pltpu.roll(x, k, 0)
    tl = pltpu.roll(tail, k, 0)
    top = jnp.where(row8 < k, tl, sh[0:SUBLANES])
    return jnp.concatenate([top, sh[SUBLANES:]], axis=0)


def _mod_kernel(c_ref, w_ref, b_ref, o_ref):
    o_ref[0] = _bdot(_silu(c_ref[...]), w_ref[0]) + b_ref[0]


def _modulation(c, ada_w, ada_b):
    L, D, D6 = ada_w.shape
    B = c.shape[0]
    tn = D6 // 4
    return pl.pallas_call(
        _mod_kernel,
        grid=(L, D6 // tn),
        in_specs=[pl.BlockSpec((B, D), lambda l, n: (0, 0)),
                  pl.BlockSpec((1, D, tn), lambda l, n: (l, 0, n)),
                  pl.BlockSpec((1, 1, tn), lambda l, n: (l, 0, n))],
        out_specs=pl.BlockSpec((1, B, tn), lambda l, n: (l, 0, n)),
        out_shape=jax.ShapeDtypeStruct((L, B, D6), F32),
        compiler_params=_cparams("parallel", "parallel"),
        name="adaln_modulation",
    )(c, ada_w, ada_b.reshape(L, 1, D6))


def _in_kernel(x_ref, mod_ref, g_ref, w_ref, u_ref, h_ref):
    @pl.when(pl.program_id(1) == 0)
    def _():
        x = x_ref[...]
        y = x * lax.rsqrt(jnp.mean(x * x, axis=-1, keepdims=True) + RMS_EPS) * g_ref[...]
        h = y * (1.0 + mod_ref[0, 1:2, :]) + mod_ref[0, 0:1, :]
        h_ref[...] = h.astype(BF16)

    u_ref[...] = jnp.dot(h_ref[...], w_ref[...], preferred_element_type=F32)


def _in_proj(x2d, mod_l, norm_g, w_cat, T):
    M, D = x2d.shape
    tm = min(2048, T)
    per_b = T // tm
    return pl.pallas_call(
        _in_kernel,
        grid=(M // tm, U_WIDTH // U_TILE_N),
        in_specs=[pl.BlockSpec((tm, D), lambda m, n: (m, 0)),
                  pl.BlockSpec((1, 6, D), lambda m, n: (m // per_b, 0, 0)),
                  pl.BlockSpec((1, D), lambda m, n: (0, 0)),
                  pl.BlockSpec((D, U_TILE_N), lambda m, n: (0, n))],
        out_specs=[pl.BlockSpec((tm, U_TILE_N), lambda m, n: (m, n)),
                   pl.BlockSpec((tm, D), lambda m, n: (m, 0))],
        out_shape=[jax.ShapeDtypeStruct((M, U_WIDTH), F32), jax.ShapeDtypeStruct((M, D), BF16)],
        compiler_params=_cparams("parallel", "arbitrary"),
        name="norm_in_proj",
    )(x2d, mod_l, norm_g.reshape(1, D), w_cat)


def _ssd_kernel(z_ref, xa_ref, xb_ref, misc_ref, cw_ref, cb_ref, dtb_ref, alog_ref, dsk_ref, ng_ref,
                o_ref, tail_ref, st_ref):
    Q, P, N, H, G = SSD_CHUNK, SSD_HEAD_DIM, SSD_STATE, SSD_HEADS, SSD_GROUPS
    R = H // G

    @pl.when(pl.program_id(1) == 0)
    def _():
        tail_ref[...] = jnp.zeros_like(tail_ref)
        st_ref[...] = jnp.zeros_like(st_ref)

    xin = jnp.concatenate([xa_ref[...], xb_ref[...]], axis=1)
    tail = tail_ref[...]
    row8 = lax.broadcasted_iota(jnp.int32, (SUBLANES, SSD_XBC), 0)
    acc = xin * cw_ref[SSD_CONV - 1:SSD_CONV, :] + cb_ref[...]
    for k in range(1, SSD_CONV):
        acc = acc + _shift_rows(xin, tail, k, row8) * cw_ref[SSD_CONV - 1 - k:SSD_CONV - k, :]
    tail_ref[...] = xin[Q - SUBLANES:Q, :]
    xbc = _silu(acc)
    xs = xbc[:, :MIX_W]
    bm = xbc[:, MIX_W:MIX_W + G * N]
    cm = xbc[:, MIX_W + G * N:]

    lane = lax.broadcasted_iota(jnp.int32, (Q, LANES), 1)
    rowi = lax.broadcasted_iota(jnp.int32, (Q, LANES), 0)
    is_head = lane < H
    dt = jnp.where(is_head, _softplus(misc_ref[...] + dtb_ref[...]), 0.0)
    a = dt * (-jnp.exp(alog_ref[...]))
    tri = (lane <= rowi).astype(BF16)
    a_cs = _dot_sel_lhs(tri, a)
    a_cs_t = _dot_sel_rhs(a.T, (rowi <= lane).astype(BF16))
    a_last = a_cs[Q - 1:Q, :]
    ea = jnp.exp(a_cs)
    dec = jnp.exp(a_last - a_cs)
    e_row = lax.broadcasted_iota(jnp.int32, (LANES, MIX_W), 0)
    e_col = lax.broadcasted_iota(jnp.int32, (LANES, MIX_W), 1)
    expand = (jnp.right_shift(e_col, 6) == e_row).astype(BF16)
    xdt = xs * _dot_sel_rhs(dt, expand)
    ea_e = _dot_sel_rhs(ea, expand)
    xdec = xdt * _dot_sel_rhs(dec, expand)
    causal = lane <= rowi

    ys = []
    for g in range(G):
        bg = bm[:, g * N:(g + 1) * N]
        cg = cm[:, g * N:(g + 1) * N].astype(BF16)
        cb = _bdot_nt(cg, bg)
        bg_t = bg.T.astype(BF16)
        for r in range(R):
            h = g * R + r
            seg = jnp.where(causal, a_cs[:, h:h + 1] - a_cs_t[h:h + 1, :], NEG_INF)
            y_diag = _bdot(cb * jnp.exp(seg), xdt[:, h * P:(h + 1) * P])
            state = st_ref[h]
            y_off = _bdot(cg, state) * ea_e[:, h * P:(h + 1) * P]
            st_ref[h] = state * jnp.exp(a_last[:, h:h + 1]) + _bdot(bg_t, xdec[:, h * P:(h + 1) * P])
            ys.append(y_diag + y_off)
    y = jnp.concatenate(ys, axis=1) + xs * dsk_ref[...]
    y = y * _silu(z_ref[...])
    gw = MIX_W // G
    outs = []
    for g in range(G):
        yg = y[:, g * gw:(g + 1) * gw]
        outs.append(yg * lax.rsqrt(jnp.mean(yg * yg, axis=-1, keepdims=True) + RMS_EPS))
    o_ref[...] = (jnp.concatenate(outs, axis=1) * ng_ref[...]).astype(BF16)


def _ssd_mixer(u, B, T, conv_w, conv_b, dt_bias, a_log, d_skip, norm_g):
    nc = T // SSD_CHUNK
    Q = SSD_CHUNK

    def pad_lane(v):
        return jnp.pad(v, (0, LANES - v.shape[0])).reshape(1, LANES)

    row = lambda b, c: b * nc + c
    full = lambda shape: pl.BlockSpec(shape, lambda b, c: (0,) * len(shape))
    return pl.pallas_call(
        _ssd_kernel,
        grid=(B, nc),
        in_specs=[pl.BlockSpec((Q, MIX_W), lambda b, c: (row(b, c), COL_Z // MIX_W)),
                  pl.BlockSpec((Q, MIX_W), lambda b, c: (row(b, c), COL_XBC // MIX_W)),
                  pl.BlockSpec((Q, MIX_W), lambda b, c: (row(b, c), COL_XBC // MIX_W + 1)),
                  pl.BlockSpec((Q, LANES), lambda b, c: (row(b, c), COL_MISC // LANES)),
                  full((SSD_CONV, SSD_XBC)), full((1, SSD_XBC)), full((1, LANES)), full((1, LANES)),
                  full((1, MIX_W)), full((1, MIX_W))],
        out_specs=pl.BlockSpec((Q, MIX_W), lambda b, c: (row(b, c), 0)),
        out_shape=jax.ShapeDtypeStruct((B * T, MIX_W), BF16),
        scratch_shapes=[pltpu.VMEM((SUBLANES, SSD_XBC), F32),
                        pltpu.VMEM((SSD_HEADS, SSD_STATE, SSD_HEAD_DIM), F32)],
        compiler_params=_cparams("parallel", "arbitrary"),
        name="ssd_mixer",
    )(u, u, u, u, conv_w, conv_b.reshape(1, SSD_XBC), pad_lane(dt_bias), pad_lane(a_log),
      jnp.repeat(d_skip, SSD_HEAD_DIM).reshape(1, MIX_W), norm_g.reshape(1, MIX_W))


def _sc_kernel(b_ref, c_ref, h_ref, w_ref, o_ref, tail_ref):
    @pl.when(pl.program_id(1) == 0)
    def _():
        tail_ref[...] = jnp.zeros_like(tail_ref)

    cx = c_ref[...] * h_ref[...]
    tt = cx.shape[0]
    tail = tail_ref[...]
    row8 = lax.broadcasted_iota(jnp.int32, (SUBLANES, MIX_W), 0)
    acc = cx * w_ref[SC_CONV - 1:SC_CONV, :]
    for k in range(1, SC_CONV):
        acc = acc + _shift_rows(cx, tail, k, row8) * w_ref[SC_CONV - 1 - k:SC_CONV - k, :]
    tail_ref[...] = cx[tt - SUBLANES:tt, :]
    o_ref[...] = (b_ref[...] * acc).astype(BF16)


def _short_conv_mixer(u, B, T, conv_w):
    tt = min(512, T)
    nt = T // tt
    c0 = COL_SC // MIX_W
    spec = lambda j: pl.BlockSpec((tt, MIX_W), lambda b, i: (b * nt + i, c0 + j))
    return pl.pallas_call(
        _sc_kernel,
        grid=(B, nt),
        in_specs=[spec(0), spec(1), spec(2), pl.BlockSpec((SC_CONV, MIX_W), lambda b, i: (0, 0))],
        out_specs=pl.BlockSpec((tt, MIX_W), lambda b, i: (b * nt + i, 0)),
        out_shape=jax.ShapeDtypeStruct((B * T, MIX_W), BF16),
        scratch_shapes=[pltpu.VMEM((SUBLANES, MIX_W), F32)],
        compiler_params=_cparams("parallel", "arbitrary"),
        name="short_conv_mixer",
    )(u, u, u, conv_w)


def _sg_kernel(u_ref, v_ref, ng_ref, w_ref, bias_ref, o_ref):
    Q = SG_CHUNK
    rowi = lax.broadcasted_iota(jnp.int32, (Q, Q), 0)
    coli = lax.broadcasted_iota(jnp.int32, (Q, Q), 1)
    gd = MIX_W // SG_GROUPS
    ws = [jnp.where(coli <= rowi, w_ref[g], 0.0).astype(BF16) for g in range(SG_GROUPS)]
    for c in range(u_ref.shape[0] // Q):
        rows = slice(c * Q, (c + 1) * Q)
        v = _gelu_tanh(v_ref[rows, :])
        v = (v * lax.rsqrt(jnp.mean(v * v, axis=-1, keepdims=True) + RMS_EPS) * ng_ref[...]).astype(BF16)
        mixed = jnp.concatenate([jnp.dot(ws[g], v[:, g * gd:(g + 1) * gd], preferred_element_type=F32)
                                 for g in range(SG_GROUPS)], axis=1)
        o_ref[rows, :] = (_gelu_tanh(u_ref[rows, :]) * (mixed + bias_ref[...])).astype(BF16)


def _spatial_gating_mixer(u, B, T, norm_g, w_s, b_s):
    Q = SG_CHUNK
    tt = min(4 * Q, T)
    nt = T // tt
    c0 = COL_SG // MIX_W
    bias = jnp.repeat(b_s.T, MIX_W // SG_GROUPS, axis=1)
    return pl.pallas_call(
        _sg_kernel,
        grid=(B, nt),
        in_specs=[pl.BlockSpec((tt, MIX_W), lambda b, c: (b * nt + c, c0)),
                  pl.BlockSpec((tt, MIX_W), lambda b, c: (b * nt + c, c0 + 1)),
                  pl.BlockSpec((1, MIX_W), lambda b, c: (0, 0)),
                  pl.BlockSpec((SG_GROUPS, Q, Q), lambda b, c: (0, 0, 0)),
                  pl.BlockSpec((Q, MIX_W), lambda b, c: (0, 0))],
        out_specs=pl.BlockSpec((tt, MIX_W), lambda b, c: (b * nt + c, 0)),
        out_shape=jax.ShapeDtypeStruct((B * T, MIX_W), BF16),
        compiler_params=_cparams("parallel", "parallel"),
        name="spatial_gating_mixer",
    )(u, u, norm_g.reshape(1, MIX_W), w_s, bias)


def _group_mean_sq(x, width):
    n = x.shape[1]
    r = lax.broadcasted_iota(jnp.int32, (n, n), 0)
    c = lax.broadcasted_iota(jnp.int32, (n, n), 1)
    sh = width.bit_length() - 1
    same = (jnp.right_shift(r, sh) == jnp.right_shift(c, sh)).astype(BF16)
    return _dot_sel_rhs(x * x, same) * (1.0 / width)


def _nsa_prep_kernel(q_ref, kc_ref, vc_ref, ks_ref, vs_ref, kw_ref, vw_ref, misc_ref, qg_ref, ksg_ref, kwg_ref,
                     qn_ref, kvc_ref, ksn_ref, vst_ref, kwn_ref, vwt_ref, gate_ref):
    dh = NSA_HEAD_DIM
    q = q_ref[...]
    qn = (q * lax.rsqrt(_group_mean_sq(q, dh) + RMS_EPS) * qg_ref[...] * (dh ** -0.5)).astype(BF16)
    for h in range(NSA_HEADS):
        qn_ref[0, h] = qn[:, h * dh:(h + 1) * dh]
    kvc_ref[...] = jnp.concatenate([kc_ref[...], vc_ref[...]], axis=1).astype(BF16)
    ks = ks_ref[...]
    ksn = (ks * lax.rsqrt(_group_mean_sq(ks, dh) + RMS_EPS) * ksg_ref[...]).astype(BF16)
    kw = kw_ref[...]
    kwn = (kw * lax.rsqrt(_group_mean_sq(kw, dh) + RMS_EPS) * kwg_ref[...]).astype(BF16)
    vs_t = vs_ref[...].T.astype(BF16)
    vw_t = vw_ref[...].T.astype(BF16)
    for g in range(NSA_KV_HEADS):
        sl = slice(g * dh, (g + 1) * dh)
        ksn_ref[0, g] = ksn[:, sl]
        kwn_ref[0, g] = kwn[:, sl]
        vst_ref[0, g] = vs_t[sl, :]
        vwt_ref[0, g] = vw_t[sl, :]
    gate_ref[0] = _sigmoid(misc_ref[...]).T[0:GATE_ROWS, :]


def _nsa_prep(u, B, T, q_norm_g, k_norm_g):
    tt = min(512, T)
    nt = T // tt
    G, H, dh = NSA_KV_HEADS, NSA_HEADS, NSA_HEAD_DIM
    kv0 = COL_KV // LANES
    kvspec = lambda j: pl.BlockSpec((tt, LANES), lambda b, i: (b * nt + i, kv0 + j))
    vec = lambda n: pl.BlockSpec((1, n), lambda b, i: (0, 0))
    kspec = pl.BlockSpec((1, G, tt, dh), lambda b, i: (b, 0, i, 0))
    kshape = jax.ShapeDtypeStruct((B, G, T, dh), BF16)
    vspec = pl.BlockSpec((1, G, dh, tt), lambda b, i: (b, 0, 0, i))
    vshape = jax.ShapeDtypeStruct((B, G, dh, T), BF16)
    return pl.pallas_call(
        _nsa_prep_kernel,
        grid=(B, nt),
        in_specs=[pl.BlockSpec((tt, NSA_Q_W), lambda b, i: (b * nt + i, COL_Q // NSA_Q_W)),
                  kvspec(0), kvspec(1), kvspec(2), kvspec(3), kvspec(4), kvspec(5),
                  pl.BlockSpec((tt, LANES), lambda b, i: (b * nt + i, COL_MISC // LANES)),
                  vec(NSA_Q_W), vec(LANES), vec(LANES)],
        out_specs=[pl.BlockSpec((1, H, tt, dh), lambda b, i: (b, 0, i, 0)),
                   pl.BlockSpec((tt, 2 * LANES), lambda b, i: (b * nt + i, 0)),
                   kspec, vspec, kspec, vspec,
                   pl.BlockSpec((1, GATE_ROWS, tt), lambda b, i: (b, 0, i))],
        out_shape=[jax.ShapeDtypeStruct((B, H, T, dh), BF16),
                   jax.ShapeDtypeStruct((B * T, 2 * LANES), BF16),
                   kshape, vshape, kshape, vshape,
                   jax.ShapeDtypeStruct((B, GATE_ROWS, T), F32)],
        compiler_params=_cparams("parallel", "parallel"),
        name="nsa_prep",
    )(u, u, u, u, u, u, u, u, jnp.tile(q_norm_g, NSA_HEADS).reshape(1, NSA_Q_W),
      jnp.tile(k_norm_g[1], G).reshape(1, LANES), jnp.tile(k_norm_g[2], G).reshape(1, LANES))


def _nsa_cmp_kernel(x_ref, w1_ref, pe_ref, w2k_ref, w2vt_ref, kg_ref, kc_ref, vct_ref):
    G = NSA_KV_HEADS
    half = CMP_STRIDE * NSA_HEAD_DIM
    for j in range(2):
        w1 = w1_ref[j]
        pe_term = jnp.dot(pe_ref[j], w1, preferred_element_type=F32)[0:1, :]
        for g in range(G):
            x = x_ref[0, j * G + g]
            lo = jnp.dot(x, w1[:half], preferred_element_type=F32)
            hi = jnp.dot(x, w1[half:], preferred_element_type=F32)
            nseg = x.shape[0]
            y = _gelu_tanh(lo + pltpu.roll(hi, nseg - 1, 0) + pe_term)
            if j == 0:
                r = _bdot(y, w2k_ref[...])
                r = r * lax.rsqrt(jnp.mean(r * r, axis=-1, keepdims=True) + RMS_EPS) * kg_ref[...]
                kc_ref[0, g] = r.astype(BF16)
            else:
                vct_ref[0, g] = _bdot_nt(w2vt_ref[...], y).astype(BF16)


def _nsa_compress(xseg, cmp_pe, cmp_w1, cmp_w2, k_norm_g0):
    B, _, nseg, width = xseg.shape
    G, dh = NSA_KV_HEADS, NSA_HEAD_DIM
    pe = jnp.broadcast_to(cmp_pe.reshape(2, 1, CMP_BLOCK * dh), (2, SUBLANES, CMP_BLOCK * dh)).astype(BF16)
    full = lambda shape: pl.BlockSpec(shape, lambda b: (0,) * len(shape))
    return pl.pallas_call(
        _nsa_cmp_kernel,
        grid=(B,),
        in_specs=[pl.BlockSpec((1, 2 * G, nseg, width), lambda b: (b, 0, 0, 0)),
                  full((2, CMP_BLOCK * dh, dh)), full((2, SUBLANES, CMP_BLOCK * dh)), full((dh, dh)),
                  full((dh, dh)), full((1, dh))],
        out_specs=[pl.BlockSpec((1, G, nseg, dh), lambda b: (b, 0, 0, 0)),
                   pl.BlockSpec((1, G, dh, nseg), lambda b: (b, 0, 0, 0))],
        out_shape=[jax.ShapeDtypeStruct((B, G, nseg, dh), BF16), jax.ShapeDtypeStruct((B, G, dh, nseg), BF16)],
        compiler_params=_cparams("parallel"),
        name="nsa_compress",
    )(xseg, cmp_w1.astype(BF16), pe, cmp_w2[0].astype(BF16), cmp_w2[1].T.astype(BF16), k_norm_g0.reshape(1, dh))


def _softmax_keys(s, ok):
    s = jnp.where(ok, s, NEG_INF)
    m = jnp.max(s, axis=0, keepdims=True)
    m = jnp.where(m > NEG_INF, m, 0.0)
    p = jnp.exp(s - m)
    return p * (1.0 / jnp.maximum(jnp.sum(p, axis=0, keepdims=True), 1e-30))


def _nsa_attn_kernel(q_ref, gt_ref, kc_ref, vct_ref, ks_ref, vst_ref, kw_ref, vwt_ref, o_ref, selt_ref, *, seq_len):
    G, R, dh = NSA_KV_HEADS, NSA_REP, NSA_HEAD_DIM
    TQ = LANES
    NB = seq_len // SLC_BLOCK
    i = pl.program_id(1)
    t0 = i * TQ
    key_r = lax.broadcasted_iota(jnp.int32, (TQ, TQ), 0)
    q_l = lax.broadcasted_iota(jnp.int32, (TQ, TQ), 1)

    dist_c = (t0 + q_l) - (key_r * CMP_STRIDE + (CMP_BLOCK - 1))
    ok_c = dist_c >= 0
    dist_cf = dist_c.astype(F32)
    jb = lax.broadcasted_iota(jnp.int32, (NB, TQ), 0)
    nl = lax.broadcasted_iota(jnp.int32, (NB, TQ), 1)
    overlap_t = ((nl * CMP_STRIDE < jb * SLC_BLOCK + SLC_BLOCK)
                 & (nl * CMP_STRIDE + CMP_BLOCK > jb * SLC_BLOCK)).astype(BF16)
    cur = jnp.right_shift(t0 + nl, SLC_SHIFT)
    future = jb > cur
    forced = (jb == 0) | (jb == cur) | (jb == cur - 1)

    qs_l, nslope_l, o_cmp_l = [], [], []
    for g in range(G):
        qs = q_ref[0, g * R:(g + 1) * R].reshape(R * TQ, dh)
        nslope = [-(2.0 ** -(g * R + r + 1)) for r in range(R)]
        qs_l.append(qs)
        nslope_l.append(nslope)

        s_c = _bdot_nt(kc_ref[0, g], qs)
        p_c = [_softmax_keys(s_c[:, r * TQ:(r + 1) * TQ] + nslope[r] * dist_cf, ok_c) for r in range(R)]
        o_cmp_l.append(jnp.dot(vct_ref[0, g], jnp.concatenate(p_c, axis=1).astype(BF16),
                               preferred_element_type=F32))

        importance = _dot_sel_lhs(overlap_t, p_c[0] + p_c[1] + p_c[2] + p_c[3])
        score = jnp.where(future, NEG_INF, jnp.where(forced, FORCE_SCORE, importance))
        rank = jnp.zeros((NB, TQ), F32)
        for k in range(NB):
            sk = score[k:k + 1, :]
            rank = rank + jnp.where((sk > score) | ((sk == score) & (jb > k)), 1.0, 0.0)
        selt_ref[g] = jnp.where((rank < SLC_TOPN) & (score > NEG_INF), 1.0, 0.0)

    bpt = SLC_TILE // SLC_BLOCK
    tkey = lax.broadcasted_iota(jnp.int32, (SLC_TILE, TQ), 0)
    trel = lax.broadcasted_iota(jnp.int32, (SLC_TILE, TQ), 1) - tkey
    tblk = jnp.right_shift(tkey, SLC_SHIFT)

    def slc_step(j, carry):
        off = pl.multiple_of(j * SLC_TILE, SLC_TILE)
        dist = trel + (t0 - off)
        causal = dist >= 0
        distf = dist.astype(F32)
        out = []
        for g in range(G):
            m, l, acc = carry[g]
            s = _bdot_nt(ks_ref[0, g, pl.ds(off, SLC_TILE), :], qs_l[g])
            selx = selt_ref[g, pl.ds(bpt * j, 1), :]
            for b in range(1, bpt):
                selx = jnp.where(tblk >= b, selt_ref[g, pl.ds(bpt * j + b, 1), :], selx)
            ok = (selx > 0.5) & causal
            s = jnp.concatenate([jnp.where(ok, s[:, r * TQ:(r + 1) * TQ] + nslope_l[g][r] * distf, NEG_INF)
                                 for r in range(R)], axis=1)
            m_new = jnp.maximum(m, jnp.max(s, axis=0, keepdims=True))
            m_safe = jnp.where(m_new > NEG_INF, m_new, 0.0)
            alpha = jnp.exp(m - m_safe)
            p = jnp.exp(s - m_safe)
            l = alpha * l + jnp.sum(p, axis=0, keepdims=True)
            acc = alpha * acc + jnp.dot(vst_ref[0, g, :, pl.ds(off, SLC_TILE)], p.astype(BF16),
                                        preferred_element_type=F32)
            out.append((m_new, l, acc))
        return tuple(out)

    init = tuple((jnp.full((1, R * TQ), NEG_INF, F32), jnp.zeros((1, R * TQ), F32), jnp.zeros((dh, R * TQ), F32))
                 for _ in range(G))
    slc = lax.fori_loop(0, (t0 + TQ + SLC_TILE - 1) // SLC_TILE, slc_step, init)

    wstart = pl.multiple_of(jnp.maximum(i - WIN // TQ, 0) * TQ, TQ)
    wlen = (WIN // TQ + 1) * TQ
    wkey = lax.broadcasted_iota(jnp.int32, (wlen, TQ), 0)
    wq = lax.broadcasted_iota(jnp.int32, (wlen, TQ), 1)
    dist_w = (t0 + wq) - (wstart + wkey)
    ok_w = (dist_w >= 0) & (dist_w < WIN)
    dist_wf = dist_w.astype(F32)
    gt = gt_ref[0]
    for g in range(G):
        s_w = _bdot_nt(kw_ref[0, g, pl.ds(wstart, wlen), :], qs_l[g])
        p_w = [_softmax_keys(s_w[:, r * TQ:(r + 1) * TQ] + nslope_l[g][r] * dist_wf, ok_w) for r in range(R)]
        o_win = jnp.dot(vwt_ref[0, g, :, pl.ds(wstart, wlen)], jnp.concatenate(p_w, axis=1).astype(BF16),
                        preferred_element_type=F32)
        _, l, acc = slc[g]
        o_slc = acc * (1.0 / jnp.maximum(l, 1e-30))
        for a in range(R // 2):
            pair = []
            for r in (2 * a, 2 * a + 1):
                c = MISC_GATE0 + 3 * (g * R + r)
                cs = slice(r * TQ, (r + 1) * TQ)
                pair.append(gt[c:c + 1, :] * o_cmp_l[g][:, cs] + gt[c + 1:c + 2, :] * o_slc[:, cs]
                            + gt[c + 2:c + 3, :] * o_win[:, cs])
            lo = (g * R + 2 * a) * dh
            o_ref[:, lo:lo + 2 * dh] = jnp.concatenate(pair, axis=0).T.astype(BF16)


def _nsa_attention(qn, gates_t, kc, vct, ksn, vst, kwn, vwt, B, T):
    G, H, dh = NSA_KV_HEADS, NSA_HEADS, NSA_HEAD_DIM
    TQ = LANES
    nq = T // TQ
    nseg = kc.shape[2]
    per_b = lambda shape: pl.BlockSpec((1,) + shape, lambda b, i: (b, 0, 0, 0))
    return pl.pallas_call(
        functools.partial(_nsa_attn_kernel, seq_len=T),
        grid=(B, nq),
        in_specs=[pl.BlockSpec((1, H, TQ, dh), lambda b, i: (b, 0, i, 0)),
                  pl.BlockSpec((1, GATE_ROWS, TQ), lambda b, i: (b, 0, i)),
                  per_b((G, nseg, dh)), per_b((G, dh, nseg)),
                  per_b((G, T, dh)), per_b((G, dh, T)), per_b((G, T, dh)), per_b((G, dh, T))],
        out_specs=pl.BlockSpec((TQ, NSA_Q_W), lambda b, i: (b * nq + i, 0)),
        out_shape=jax.ShapeDtypeStruct((B * T, NSA_Q_W), BF16),
        scratch_shapes=[pltpu.VMEM((G, T // SLC_BLOCK, TQ), F32)],
        compiler_params=_cparams("parallel", "arbitrary"),
        name="nsa_attention",
    )(qn, gates_t, kc, vct, ksn, vst, kwn, vwt)


def _nsa_mixer(u, B, T, q_norm_g, k_norm_g, cmp_pe, cmp_w1, cmp_w2):
    G, dh = NSA_KV_HEADS, NSA_HEAD_DIM
    assert T % LANES == 0 and T >= (WIN // LANES + 1) * LANES and T // CMP_STRIDE == LANES
    assert SLC_BLOCK == 1 << SLC_SHIFT and LANES == 2 * SLC_BLOCK
    qn, kvc, ksn, vst, kwn, vwt, gates_t = _nsa_prep(u, B, T, q_norm_g, k_norm_g)
    nseg = T // CMP_STRIDE
    xseg = kvc.reshape(B, nseg, CMP_STRIDE, 2 * G, dh).transpose(0, 3, 1, 2, 4).reshape(
        B, 2 * G, nseg, CMP_STRIDE * dh)
    kc, vct = _nsa_compress(xseg, cmp_pe, cmp_w1, cmp_w2, k_norm_g[0])
    return _nsa_attention(qn, gates_t, kc, vct, ksn, vst, kwn, vwt, B, T)


def _merge_kernel(x_ref, h_ref, oa_ref, ob_ref, oc_ref, od_ref, mod_ref, wg_ref, wb_ref, wo_ref, y_ref):
    h = h_ref[...]
    merged = None
    for i, o_ref in enumerate((oa_ref, ob_ref, oc_ref, od_ref)):
        gate = _sigmoid(jnp.dot(h, wg_ref[i], preferred_element_type=F32))
        term = gate * jnp.dot(o_ref[...], wb_ref[i], preferred_element_type=F32)
        merged = term if merged is None else merged + term
    y_ref[...] = x_ref[...] + mod_ref[0, 2:3, :] * _bdot(merged, wo_ref[...])


def _merge(x2d, h, outs, mod_l, wg, wb, wo, T):
    M, D = x2d.shape
    tm = min(512, T)
    per_b = T // tm
    row = lambda w: pl.BlockSpec((tm, w), lambda m: (m, 0))
    const = lambda shape: pl.BlockSpec(shape, lambda m: (0,) * len(shape), pipeline_mode=pl.Buffered(1))
    return pl.pallas_call(
        _merge_kernel,
        grid=(M // tm,),
        in_specs=[row(D), row(D), row(MIX_W), row(MIX_W), row(MIX_W), row(MIX_W),
                  pl.BlockSpec((1, 6, D), lambda m: (m // per_b, 0, 0)),
                  const((N_BRANCH, D, D)), const((N_BRANCH, MIX_W, D)), const((D, D))],
        out_specs=row(D),
        out_shape=jax.ShapeDtypeStruct((M, D), F32),
        compiler_params=_cparams("parallel"),
        name="gated_merge_out_proj",
    )(x2d, h, *outs, mod_l, wg, wb, wo)


def _ffn_kernel(x_ref, mod_ref, g_ref, wa_ref, wb_ref, wo_ref, y_ref, h_ref, acc_ref):
    f = pl.program_id(1)

    @pl.when(f == 0)
    def _():
        x = x_ref[...]
        y = x * lax.rsqrt(jnp.mean(x * x, axis=-1, keepdims=True) + RMS_EPS) * g_ref[...]
        h_ref[...] = (y * (1.0 + mod_ref[0, 4:5, :]) + mod_ref[0, 3:4, :]).astype(BF16)
        acc_ref[...] = jnp.zeros_like(acc_ref)

    h = h_ref[...]
    a = jnp.dot(h, wa_ref[...], preferred_element_type=F32)
    b = jnp.dot(h, wb_ref[...], preferred_element_type=F32)
    acc_ref[...] += _bdot(_silu(a) * b, wo_ref[...])

    @pl.when(f == pl.num_programs(1) - 1)
    def _():
        y_ref[...] = x_ref[...] + mod_ref[0, 5:6, :] * acc_ref[...]


def _ffn(x2d, mod_l, norm_g, w_in, w_out, T):
    M, D = x2d.shape
    d_ff = w_out.shape[0]
    tm = min(512, T)
    per_b = T // tm
    nf = 2
    tf = d_ff // nf
    return pl.pallas_call(
        _ffn_kernel,
        grid=(M // tm, nf),
        in_specs=[pl.BlockSpec((tm, D), lambda m, f: (m, 0)),
                  pl.BlockSpec((1, 6, D), lambda m, f: (m // per_b, 0, 0)),
                  pl.BlockSpec((1, D), lambda m, f: (0, 0)),
                  pl.BlockSpec((D, tf), lambda m, f: (0, f)),
                  pl.BlockSpec((D, tf), lambda m, f: (0, f + nf)),
                  pl.BlockSpec((tf, D), lambda m, f: (f, 0))],
        out_specs=pl.BlockSpec((tm, D), lambda m, f: (m, 0)),
        out_shape=jax.ShapeDtypeStruct((M, D), F32),
        scratch_shapes=[pltpu.VMEM((tm, D), BF16), pltpu.VMEM((tm, D), F32)],
        compiler_params=_cparams("parallel", "arbitrary"),
        name="swiglu_ffn",
    )(x2d, mod_l, norm_g.reshape(1, D), w_in, w_in, w_out)


def _pack_w_in(w_in):
    D = w_in.shape[0]
    o1 = SSD_IN
    o2 = o1 + SC_IN
    o3 = o2 + SG_IN
    q0 = o3
    kv0 = q0 + NSA_Q_W
    gt0 = kv0 + 6 * NSA_KV_W
    parts = [w_in[:, :MIX_W + SSD_XBC],
             w_in[:, o1:o3],
             w_in[:, q0:gt0],
             w_in[:, MIX_W + SSD_XBC:o1],
             w_in[:, gt0:gt0 + 3 * NSA_HEADS]]
    w = jnp.concatenate(parts, axis=1)
    return jnp.pad(w, ((0, 0), (0, U_WIDTH - w.shape[1]))).astype(BF16)


def kernel(x, c, ada_w, ada_b, norm_mix_g, norm_ffn_g, w_in, ssd_conv_w, ssd_conv_b, ssd_dt_bias, ssd_a_log, ssd_d,
           ssd_norm_g, sc_conv_w, sg_norm_g, sg_w, sg_b, nsa_q_norm_g, nsa_k_norm_g, nsa_cmp_pe, nsa_cmp_w1,
           nsa_cmp_w2, w_branch, w_branch_gate, w_out, w_ffn_in, w_ffn_out):
    B, T, D = x.shape
    L = w_in.shape[0]
    mod = _modulation(c, ada_w, ada_b).reshape(L, B, 6, D)
    x2d = x.reshape(B * T, D)
    for l in range(L):
        u, h = _in_proj(x2d, mod[l], norm_mix_g[l], _pack_w_in(w_in[l]), T)
        outs = (
            _ssd_mixer(u, B, T, ssd_conv_w[l], ssd_conv_b[l], ssd_dt_bias[l], ssd_a_log[l], ssd_d[l],
                       ssd_norm_g[l]),
            _short_conv_mixer(u, B, T, sc_conv_w[l]),
            _spatial_gating_mixer(u, B, T, sg_norm_g[l], sg_w[l], sg_b[l]),
            _nsa_mixer(u, B, T, nsa_q_norm_g[l], nsa_k_norm_g[l], nsa_cmp_pe[l], nsa_cmp_w1[l], nsa_cmp_w2[l]),
        )
        x2d = _merge(x2d, h, outs, mod[l], w_branch_gate[l].astype(BF16), w_branch[l].astype(BF16),
                     w_out[l].astype(BF16), T)
        x2d = _ffn(x2d, mod[l], norm_ffn_g[l], w_ffn_in[l].astype(BF16), w_ffn_out[l].astype(BF16), T)
    return x2d.reshape(B, T, D)
```

```python
import functools
import math

import jax
import jax.numpy as jnp
from jax import lax
from jax.experimental import pallas as pl
from jax.experimental.pallas import tpu as pltpu

F32 = jnp.float32
BF16 = jnp.bfloat16
RMS_EPS = 1e-6
NEG_INF = float("-inf")

MIX_W = 512
N_BRANCH = 4

SSD_HEAD_DIM = 64
SSD_HEADS = 8
SSD_GROUPS = 2
SSD_STATE = 128
SSD_CONV = 4
SSD_CHUNK = 128
SSD_XBC = MIX_W + 2 * SSD_GROUPS * SSD_STATE
SSD_IN = MIX_W + SSD_XBC + SSD_HEADS
SC_CONV = 3
SC_IN = 3 * MIX_W
SG_GROUPS = 4
SG_CHUNK = 128
SG_IN = 2 * MIX_W
NSA_HEADS = 8
NSA_KV_HEADS = 2
NSA_REP = NSA_HEADS // NSA_KV_HEADS
NSA_HEAD_DIM = 64
CMP_BLOCK = 32
CMP_STRIDE = 16
SLC_BLOCK = 64
SLC_TOPN = 8
WIN = 256
FORCE_SCORE = 1e9
NSA_KV_W = NSA_KV_HEADS * NSA_HEAD_DIM
NSA_Q_W = NSA_HEADS * NSA_HEAD_DIM
NSA_IN = NSA_Q_W + 6 * NSA_KV_W + 3 * NSA_HEADS

LANES = 128
SUBLANES = 8
VMEM_LIMIT_BYTES = 56 * 1024 * 1024

COL_Z = 0
COL_XBC = COL_Z + MIX_W
COL_SC = COL_XBC + SSD_XBC
COL_SG = COL_SC + SC_IN
COL_Q = COL_SG + SG_IN
COL_KV = COL_Q + NSA_Q_W
COL_MISC = COL_KV + 6 * NSA_KV_W
MISC_GATE0 = SSD_HEADS
GATE_ROWS = 32
SLC_SHIFT = 6
SLC_TILE = 512
AUG_POS = NSA_HEAD_DIM
AUG_SEL = AUG_POS + SUBLANES
MASK_SCORE = -(2.0 ** 100)
U_TILE_N = 512
U_WIDTH = ((COL_MISC + LANES + U_TILE_N - 1) // U_TILE_N) * U_TILE_N


def _cparams(*sem):
    return pltpu.CompilerParams(dimension_semantics=sem, vmem_limit_bytes=VMEM_LIMIT_BYTES)


def _bdot(a, b):
    return jnp.dot(a.astype(BF16), b.astype(BF16), preferred_element_type=F32)


def _bdot_nt(a, b):
    return lax.dot_general(a.astype(BF16), b.astype(BF16), (((1,), (1,)), ((), ())),
                           preferred_element_type=F32)


def _split3(a):
    hi = a.astype(BF16)
    r1 = a - hi.astype(F32)
    mid = r1.astype(BF16)
    lo = (r1 - mid.astype(F32)).astype(BF16)
    return hi, mid, lo


def _dot_sel_rhs(a, sel):
    hi, mid, lo = _split3(a)
    return (jnp.dot(hi, sel, preferred_element_type=F32) + jnp.dot(mid, sel, preferred_element_type=F32)
            + jnp.dot(lo, sel, preferred_element_type=F32))


def _dot_sel_lhs(sel, a):
    hi, mid, lo = _split3(a)
    return (jnp.dot(sel, hi, preferred_element_type=F32) + jnp.dot(sel, mid, preferred_element_type=F32)
            + jnp.dot(sel, lo, preferred_element_type=F32))


def _sigmoid(x):
    return 1.0 / (1.0 + jnp.exp(-x))


def _silu(x):
    return x * _sigmoid(x)


def _gelu_tanh(x):
    c = math.sqrt(2.0 / math.pi)
    return 0.5 * x * (1.0 + jnp.tanh(c * (x + 0.044715 * (x * x * x))))


def _softplus(x):
    return jnp.maximum(x, 0.0) + jnp.log1p(jnp.exp(-jnp.abs(x)))


def _shift_rows(x, tail, k, row8):
    sh = pltpu.roll(x, k, 0)
    tl = pltpu.roll(tail, k, 0)
    top = jnp.where(row8 < k, tl, sh[0:SUBLANES])
    return jnp.concatenate([top, sh[SUBLANES:]], axis=0)


def _mod_kernel(c_ref, w_ref, b_ref, o_ref):
    o_ref[0] = _bdot(_silu(c_ref[...]), w_ref[0]) + b_ref[0]


def _modulation(c, ada_w, ada_b):
    L, D, D6 = ada_w.shape
    B = c.shape[0]
    tn = D6 // 4
    return pl.pallas_call(
        _mod_kernel,
        grid=(L, D6 // tn),
        in_specs=[pl.BlockSpec((B, D), lambda l, n: (0, 0)),
                  pl.BlockSpec((1, D, tn), lambda l, n: (l, 0, n)),
                  pl.BlockSpec((1, 1, tn), lambda l, n: (l, 0, n))],
        out_specs=pl.BlockSpec((1, B, tn), lambda l, n: (l, 0, n)),
        out_shape=jax.ShapeDtypeStruct((L, B, D6), F32),
        compiler_params=_cparams("parallel", "parallel"),
        name="adaln_modulation",
    )(c, ada_w, ada_b.reshape(L, 1, D6))


def _in_kernel(x_ref, mod_ref, g_ref, w_ref, u_ref, h_ref):
    @pl.when(pl.program_id(1) == 0)
    def _():
        x = x_ref[...]
        y = x * lax.rsqrt(jnp.mean(x * x, axis=-1, keepdims=True) + RMS_EPS) * g_ref[...]
        h = y * (1.0 + mod_ref[0, 1:2, :]) + mod_ref[0, 0:1, :]
        h_ref[...] = h.astype(BF16)

    u_ref[...] = jnp.dot(h_ref[...], w_ref[...], preferred_element_type=F32)


def _in_proj(x2d, mod_l, norm_g, w_cat, T):
    M, D = x2d.shape
    tm = min(2048, T)
    per_b = T // tm
    return pl.pallas_call(
        _in_kernel,
        grid=(M // tm, U_WIDTH // U_TILE_N),
        in_specs=[pl.BlockSpec((tm, D), lambda m, n: (m, 0)),
                  pl.BlockSpec((1, 6, D), lambda m, n: (m // per_b, 0, 0)),
                  pl.BlockSpec((1, D), lambda m, n: (0, 0)),
                  pl.BlockSpec((D, U_TILE_N), lambda m, n: (0, n))],
        out_specs=[pl.BlockSpec((tm, U_TILE_N), lambda m, n: (m, n)),
                   pl.BlockSpec((tm, D), lambda m, n: (m, 0))],
        out_shape=[jax.ShapeDtypeStruct((M, U_WIDTH), F32), jax.ShapeDtypeStruct((M, D), BF16)],
        compiler_params=_cparams("parallel", "arbitrary"),
        name="norm_in_proj",
    )(x2d, mod_l, norm_g.reshape(1, D), w_cat)


def _ssd_kernel(z_ref, xa_ref, xb_ref, misc_ref, cw_ref, cb_ref, dtb_ref, alog_ref, dsk_ref, ng_ref,
                o_ref, tail_ref, st_ref):
    Q, P, N, H, G = SSD_CHUNK, SSD_HEAD_DIM, SSD_STATE, SSD_HEADS, SSD_GROUPS
    R = H // G

    @pl.when(pl.program_id(1) == 0)
    def _():
        tail_ref[...] = jnp.zeros_like(tail_ref)
        st_ref[...] = jnp.zeros_like(st_ref)

    xin = jnp.concatenate([xa_ref[...], xb_ref[...]], axis=1)
    tail = tail_ref[...]
    row8 = lax.broadcasted_iota(jnp.int32, (SUBLANES, SSD_XBC), 0)
    acc = xin * cw_ref[SSD_CONV - 1:SSD_CONV, :] + cb_ref[...]
    for k in range(1, SSD_CONV):
        acc = acc + _shift_rows(xin, tail, k, row8) * cw_ref[SSD_CONV - 1 - k:SSD_CONV - k, :]
    tail_ref[...] = xin[Q - SUBLANES:Q, :]
    xbc = _silu(acc)
    xs = xbc[:, :MIX_W]
    bm = xbc[:, MIX_W:MIX_W + G * N]
    cm = xbc[:, MIX_W + G * N:]

    lane = lax.broadcasted_iota(jnp.int32, (Q, LANES), 1)
    rowi = lax.broadcasted_iota(jnp.int32, (Q, LANES), 0)
    is_head = lane < H
    dt = jnp.where(is_head, _softplus(misc_ref[...] + dtb_ref[...]), 0.0)
    a = dt * (-jnp.exp(alog_ref[...]))
    tri = (lane <= rowi).astype(BF16)
    a_cs = _dot_sel_lhs(tri, a)
    a_cs_t = _dot_sel_rhs(a.T, (rowi <= lane).astype(BF16))
    a_last = a_cs[Q - 1:Q, :]
    ea = jnp.exp(a_cs)
    dec = jnp.exp(a_last - a_cs)
    e_row = lax.broadcasted_iota(jnp.int32, (LANES, MIX_W), 0)
    e_col = lax.broadcasted_iota(jnp.int32, (LANES, MIX_W), 1)
    expand = (jnp.right_shift(e_col, 6) == e_row).astype(BF16)
    xdt = xs * _dot_sel_rhs(dt, expand)
    ea_e = _dot_sel_rhs(ea, expand)
    xdec = xdt * _dot_sel_rhs(dec, expand)
    causal = lane <= rowi

    ys = []
    for g in range(G):
        bg = bm[:, g * N:(g + 1) * N]
        cg = cm[:, g * N:(g + 1) * N].astype(BF16)
        cb = _bdot_nt(cg, bg)
        bg_t = bg.T.astype(BF16)
        for r in range(R):
            h = g * R + r
            seg = jnp.where(causal, a_cs[:, h:h + 1] - a_cs_t[h:h + 1, :], NEG_INF)
            y_diag = _bdot(cb * jnp.exp(seg), xdt[:, h * P:(h + 1) * P])
            state = st_ref[h]
            y_off = _bdot(cg, state) * ea_e[:, h * P:(h + 1) * P]
            st_ref[h] = state * jnp.exp(a_last[:, h:h + 1]) + _bdot(bg_t, xdec[:, h * P:(h + 1) * P])
            ys.append(y_diag + y_off)
    y = jnp.concatenate(ys, axis=1) + xs * dsk_ref[...]
    y = y * _silu(z_ref[...])
    gw = MIX_W // G
    outs = []
    for g in range(G):
        yg = y[:, g * gw:(g + 1) * gw]
        outs.append(yg * lax.rsqrt(jnp.mean(yg * yg, axis=-1, keepdims=True) + RMS_EPS))
    o_ref[...] = (jnp.concatenate(outs, axis=1) * ng_ref[...]).astype(BF16)


def _ssd_mixer(u, B, T, conv_w, conv_b, dt_bias, a_log, d_skip, norm_g):
    nc = T // SSD_CHUNK
    Q = SSD_CHUNK

    def pad_lane(v):
        return jnp.pad(v, (0, LANES - v.shape[0])).reshape(1, LANES)

    row = lambda b, c: b * nc + c
    full = lambda shape: pl.BlockSpec(shape, lambda b, c: (0,) * len(shape))
    return pl.pallas_call(
        _ssd_kernel,
        grid=(B, nc),
        in_specs=[pl.BlockSpec((Q, MIX_W), lambda b, c: (row(b, c), COL_Z // MIX_W)),
                  pl.BlockSpec((Q, MIX_W), lambda b, c: (row(b, c), COL_XBC // MIX_W)),
                  pl.BlockSpec((Q, MIX_W), lambda b, c: (row(b, c), COL_XBC // MIX_W + 1)),
                  pl.BlockSpec((Q, LANES), lambda b, c: (row(b, c), COL_MISC // LANES)),
                  full((SSD_CONV, SSD_XBC)), full((1, SSD_XBC)), full((1, LANES)), full((1, LANES)),
                  full((1, MIX_W)), full((1, MIX_W))],
        out_specs=pl.BlockSpec((Q, MIX_W), lambda b, c: (row(b, c), 0)),
        out_shape=jax.ShapeDtypeStruct((B * T, MIX_W), BF16),
        scratch_shapes=[pltpu.VMEM((SUBLANES, SSD_XBC), F32),
                        pltpu.VMEM((SSD_HEADS, SSD_STATE, SSD_HEAD_DIM), F32)],
        compiler_params=_cparams("parallel", "arbitrary"),
        name="ssd_mixer",
    )(u, u, u, u, conv_w, conv_b.reshape(1, SSD_XBC), pad_lane(dt_bias), pad_lane(a_log),
      jnp.repeat(d_skip, SSD_HEAD_DIM).reshape(1, MIX_W), norm_g.reshape(1, MIX_W))


def _sc_kernel(b_ref, c_ref, h_ref, w_ref, o_ref, tail_ref):
    @pl.when(pl.program_id(1) == 0)
    def _():
        tail_ref[...] = jnp.zeros_like(tail_ref)

    cx = c_ref[...] * h_ref[...]
    tt = cx.shape[0]
    tail = tail_ref[...]
    row8 = lax.broadcasted_iota(jnp.int32, (SUBLANES, MIX_W), 0)
    acc = cx * w_ref[SC_CONV - 1:SC_CONV, :]
    for k in range(1, SC_CONV):
        acc = acc + _shift_rows(cx, tail, k, row8) * w_ref[SC_CONV - 1 - k:SC_CONV - k, :]
    tail_ref[...] = cx[tt - SUBLANES:tt, :]
    o_ref[...] = (b_ref[...] * acc).astype(BF16)


def _short_conv_mixer(u, B, T, conv_w):
    tt = min(512, T)
    nt = T // tt
    c0 = COL_SC // MIX_W
    spec = lambda j: pl.BlockSpec((tt, MIX_W), lambda b, i: (b * nt + i, c0 + j))
    return pl.pallas_call(
        _sc_kernel,
        grid=(B, nt),
        in_specs=[spec(0), spec(1), spec(2), pl.BlockSpec((SC_CONV, MIX_W), lambda b, i: (0, 0))],
        out_specs=pl.BlockSpec((tt, MIX_W), lambda b, i: (b * nt + i, 0)),
        out_shape=jax.ShapeDtypeStruct((B * T, MIX_W), BF16),
        scratch_shapes=[pltpu.VMEM((SUBLANES, MIX_W), F32)],
        compiler_params=_cparams("parallel", "arbitrary"),
        name="short_conv_mixer",
    )(u, u, u, conv_w)


def _sg_kernel(u_ref, v_ref, ng_ref, w_ref, bias_ref, o_ref):
    Q = SG_CHUNK
    rowi = lax.broadcasted_iota(jnp.int32, (Q, Q), 0)
    coli = lax.broadcasted_iota(jnp.int32, (Q, Q), 1)
    gd = MIX_W // SG_GROUPS
    ws = [jnp.where(coli <= rowi, w_ref[g], 0.0).astype(BF16) for g in range(SG_GROUPS)]
    for c in range(u_ref.shape[0] // Q):
        rows = slice(c * Q, (c + 1) * Q)
        v = _gelu_tanh(v_ref[rows, :])
        v = (v * lax.rsqrt(jnp.mean(v * v, axis=-1, keepdims=True) + RMS_EPS) * ng_ref[...]).astype(BF16)
        mixed = jnp.concatenate([jnp.dot(ws[g], v[:, g * gd:(g + 1) * gd], preferred_element_type=F32)
                                 for g in range(SG_GROUPS)], axis=1)
        o_ref[rows, :] = (_gelu_tanh(u_ref[rows, :]) * (mixed + bias_ref[...])).astype(BF16)


def _spatial_gating_mixer(u, B, T, norm_g, w_s, b_s):
    Q = SG_CHUNK
    tt = min(4 * Q, T)
    nt = T // tt
    c0 = COL_SG // MIX_W
    bias = jnp.repeat(b_s.T, MIX_W // SG_GROUPS, axis=1)
    return pl.pallas_call(
        _sg_kernel,
        grid=(B, nt),
        in_specs=[pl.BlockSpec((tt, MIX_W), lambda b, c: (b * nt + c, c0)),
                  pl.BlockSpec((tt, MIX_W), lambda b, c: (b * nt + c, c0 + 1)),
                  pl.BlockSpec((1, MIX_W), lambda b, c: (0, 0)),
                  pl.BlockSpec((SG_GROUPS, Q, Q), lambda b, c: (0, 0, 0)),
                  pl.BlockSpec((Q, MIX_W), lambda b, c: (0, 0))],
        out_specs=pl.BlockSpec((tt, MIX_W), lambda b, c: (b * nt + c, 0)),
        out_shape=jax.ShapeDtypeStruct((B * T, MIX_W), BF16),
        compiler_params=_cparams("parallel", "parallel"),
        name="spatial_gating_mixer",
    )(u, u, norm_g.reshape(1, MIX_W), w_s, bias)


def _group_mean_sq(x, width):
    n = x.shape[1]
    r = lax.broadcasted_iota(jnp.int32, (n, n), 0)
    c = lax.broadcasted_iota(jnp.int32, (n, n), 1)
    sh = width.bit_length() - 1
    same = (jnp.right_shift(r, sh) == jnp.right_shift(c, sh)).astype(BF16)
    return _dot_sel_rhs(x * x, same) * (1.0 / width)


def _key_aug(lane, pos):
    return jnp.where((lane == AUG_POS) | (lane == AUG_POS + 1), 1.0,
                     jnp.where(lane == AUG_POS + 2, -(pos & ~(LANES - 1)).astype(F32),
                               jnp.where(lane == AUG_POS + 3, -(pos & (LANES - 1)).astype(F32), 0.0)))


def _nsa_prep_kernel(q_ref, kc_ref, vc_ref, ks_ref, vs_ref, kw_ref, vw_ref, misc_ref, qg_ref, ksg_ref, kwg_ref,
                     qa_ref, kvc_ref, ksa_ref, vst_ref, kwa_ref, vwt_ref, gate_ref):
    dh = NSA_HEAD_DIM
    tt = q_ref.shape[0]
    lane = lax.broadcasted_iota(jnp.int32, (tt, LANES), 1)
    pos = pl.program_id(1) * tt + lax.broadcasted_iota(jnp.int32, (tt, LANES), 0)
    is_feat = lane < dh

    q = q_ref[...]
    qn = q * lax.rsqrt(_group_mean_sq(q, dh) + RMS_EPS) * qg_ref[...] * (dh ** -0.5)
    q_pos = jnp.where(lane == AUG_POS, (pos & ~(LANES - 1)).astype(F32),
                      jnp.where(lane == AUG_POS + 1, (pos & (LANES - 1)).astype(F32),
                                jnp.where((lane == AUG_POS + 2) | (lane == AUG_POS + 3), 1.0, 0.0)))
    for h in range(NSA_HEADS):
        pair = qn[:, (h // 2) * LANES:(h // 2 + 1) * LANES]
        feat = pair if h % 2 == 0 else pltpu.roll(pair, dh, 1)
        qa_ref[0, h] = jnp.where(is_feat, feat, -(2.0 ** -(h + 1)) * q_pos).astype(BF16)

    kvc_ref[...] = jnp.concatenate([kc_ref[...], vc_ref[...]], axis=1).astype(BF16)
    ks = ks_ref[...]
    ksn = ks * lax.rsqrt(_group_mean_sq(ks, dh) + RMS_EPS) * ksg_ref[...]
    kw = kw_ref[...]
    kwn = kw * lax.rsqrt(_group_mean_sq(kw, dh) + RMS_EPS) * kwg_ref[...]
    k_pos = _key_aug(lane, pos)
    k_pos_sel = jnp.where(lane == AUG_SEL + jnp.right_shift(pos, SLC_SHIFT), MASK_SCORE, k_pos)
    vs_t = vs_ref[...].T.astype(BF16)
    vw_t = vw_ref[...].T.astype(BF16)
    for g in range(NSA_KV_HEADS):
        sl = slice(g * dh, (g + 1) * dh)
        ksa_ref[0, g] = jnp.where(is_feat, ksn if g == 0 else pltpu.roll(ksn, dh, 1), k_pos_sel).astype(BF16)
        kwa_ref[0, g] = jnp.where(is_feat, kwn if g == 0 else pltpu.roll(kwn, dh, 1), k_pos).astype(BF16)
        vst_ref[0, g] = vs_t[sl, :]
        vwt_ref[0, g] = vw_t[sl, :]
    gate_ref[0] = _sigmoid(misc_ref[...]).T[0:GATE_ROWS, :]


def _nsa_prep(u, B, T, q_norm_g, k_norm_g):
    tt = min(512, T)
    nt = T // tt
    G, H, dh = NSA_KV_HEADS, NSA_HEADS, NSA_HEAD_DIM
    kv0 = COL_KV // LANES
    kvspec = lambda j: pl.BlockSpec((tt, LANES), lambda b, i: (b * nt + i, kv0 + j))
    vec = lambda n: pl.BlockSpec((1, n), lambda b, i: (0, 0))
    kspec = pl.BlockSpec((1, G, tt, LANES), lambda b, i: (b, 0, i, 0))
    kshape = jax.ShapeDtypeStruct((B, G, T, LANES), BF16)
    vspec = pl.BlockSpec((1, G, dh, tt), lambda b, i: (b, 0, 0, i))
    vshape = jax.ShapeDtypeStruct((B, G, dh, T), BF16)
    return pl.pallas_call(
        _nsa_prep_kernel,
        grid=(B, nt),
        in_specs=[pl.BlockSpec((tt, NSA_Q_W), lambda b, i: (b * nt + i, COL_Q // NSA_Q_W)),
                  kvspec(0), kvspec(1), kvspec(2), kvspec(3), kvspec(4), kvspec(5),
                  pl.BlockSpec((tt, LANES), lambda b, i: (b * nt + i, COL_MISC // LANES)),
                  vec(NSA_Q_W), vec(LANES), vec(LANES)],
        out_specs=[pl.BlockSpec((1, H, tt, LANES), lambda b, i: (b, 0, i, 0)),
                   pl.BlockSpec((tt, 2 * LANES), lambda b, i: (b * nt + i, 0)),
                   kspec, vspec, kspec, vspec,
                   pl.BlockSpec((1, GATE_ROWS, tt), lambda b, i: (b, 0, i))],
        out_shape=[jax.ShapeDtypeStruct((B, H, T, LANES), BF16),
                   jax.ShapeDtypeStruct((B * T, 2 * LANES), BF16),
                   kshape, vshape, kshape, vshape,
                   jax.ShapeDtypeStruct((B, GATE_ROWS, T), F32)],
        compiler_params=_cparams("parallel", "parallel"),
        name="nsa_prep",
    )(u, u, u, u, u, u, u, u, jnp.tile(q_norm_g, NSA_HEADS).reshape(1, NSA_Q_W),
      jnp.tile(k_norm_g[1], G).reshape(1, LANES), jnp.tile(k_norm_g[2], G).reshape(1, LANES))


def _nsa_cmp_kernel(x_ref, w1_ref, pe_ref, w2k_ref, w2vt_ref, kg_ref, kc_ref, vct_ref):
    G = NSA_KV_HEADS
    half = CMP_STRIDE * NSA_HEAD_DIM
    for j in range(2):
        w1 = w1_ref[j]
        pe_term = jnp.dot(pe_ref[j], w1, preferred_element_type=F32)[0:1, :]
        for g in range(G):
            x = x_ref[0, j * G + g]
            lo = jnp.dot(x, w1[:half], preferred_element_type=F32)
            hi = jnp.dot(x, w1[half:], preferred_element_type=F32)
            nseg = x.shape[0]
            y = _gelu_tanh(lo + pltpu.roll(hi, nseg - 1, 0) + pe_term)
            if j == 0:
                r = _bdot(y, w2k_ref[...])
                ms = jnp.sum(r * r, axis=-1, keepdims=True) * (1.0 / NSA_HEAD_DIM)
                lane = lax.broadcasted_iota(jnp.int32, r.shape, 1)
                last = lax.broadcasted_iota(jnp.int32, r.shape, 0) * CMP_STRIDE + (CMP_BLOCK - 1)
                kc_ref[0, g] = (r * lax.rsqrt(ms + RMS_EPS) * kg_ref[...] + _key_aug(lane, last)).astype(BF16)
            else:
                vct_ref[0, g] = _bdot_nt(w2vt_ref[...], y).astype(BF16)


def _nsa_compress(xseg, cmp_pe, cmp_w1, cmp_w2, k_norm_g0):
    B, _, nseg, width = xseg.shape
    G, dh = NSA_KV_HEADS, NSA_HEAD_DIM
    pe = jnp.broadcast_to(cmp_pe.reshape(2, 1, CMP_BLOCK * dh), (2, SUBLANES, CMP_BLOCK * dh)).astype(BF16)
    full = lambda shape: pl.BlockSpec(shape, lambda b: (0,) * len(shape))
    return pl.pallas_call(
        _nsa_cmp_kernel,
        grid=(B,),
        in_specs=[pl.BlockSpec((1, 2 * G, nseg, width), lambda b: (b, 0, 0, 0)),
                  full((2, CMP_BLOCK * dh, dh)), full((2, SUBLANES, CMP_BLOCK * dh)), full((dh, LANES)),
                  full((dh, dh)), full((1, LANES))],
        out_specs=[pl.BlockSpec((1, G, nseg, LANES), lambda b: (b, 0, 0, 0)),
                   pl.BlockSpec((1, G, dh, nseg), lambda b: (b, 0, 0, 0))],
        out_shape=[jax.ShapeDtypeStruct((B, G, nseg, LANES), BF16), jax.ShapeDtypeStruct((B, G, dh, nseg), BF16)],
        compiler_params=_cparams("parallel"),
        name="nsa_compress",
    )(xseg, cmp_w1.astype(BF16), pe, jnp.pad(cmp_w2[0], ((0, 0), (0, LANES - dh))).astype(BF16),
      cmp_w2[1].T.astype(BF16), jnp.pad(k_norm_g0, (0, LANES - dh)).reshape(1, LANES))


def _softmax_keys(s, ok):
    s = jnp.where(ok, s, NEG_INF)
    m = jnp.max(s, axis=0, keepdims=True)
    m = jnp.where(m > NEG_INF, m, 0.0)
    p = jnp.exp(s - m)
    return p * (1.0 / jnp.maximum(jnp.sum(p, axis=0, keepdims=True), 1e-30))


def _nsa_attn_kernel(q_ref, gt_ref, kc_ref, vct_ref, ks_ref, vst_ref, kw_ref, vwt_ref, o_ref, selt_ref, s_ref, *,
                     seq_len):
    G, R, dh = NSA_KV_HEADS, NSA_REP, NSA_HEAD_DIM
    TQ = LANES
    NB = seq_len // SLC_BLOCK
    i = pl.program_id(1)
    t0 = i * TQ
    key_r = lax.broadcasted_iota(jnp.int32, (TQ, TQ), 0)
    q_l = lax.broadcasted_iota(jnp.int32, (TQ, TQ), 1)

    ok_c = (t0 + q_l) - (key_r * CMP_STRIDE + (CMP_BLOCK - 1)) >= 0
    ok_d = q_l >= key_r
    jb = lax.broadcasted_iota(jnp.int32, (NB, TQ), 0)
    nl = lax.broadcasted_iota(jnp.int32, (NB, TQ), 1)
    overlap_t = ((nl * CMP_STRIDE < jb * SLC_BLOCK + SLC_BLOCK)
                 & (nl * CMP_STRIDE + CMP_BLOCK > jb * SLC_BLOCK)).astype(BF16)
    cur = jnp.right_shift(t0 + nl, SLC_SHIFT)
    future = jb > cur
    forced = (jb == 0) | (jb == cur) | (jb == cur - 1)

    wstart = pl.multiple_of(jnp.maximum(i - WIN // TQ, 0) * TQ, TQ)
    wlen = (WIN // TQ + 1) * TQ
    wkey = lax.broadcasted_iota(jnp.int32, (wlen, TQ), 0)
    wq = lax.broadcasted_iota(jnp.int32, (wlen, TQ), 1)
    dist_w = (t0 + wq) - (wstart + wkey)
    ok_w = (dist_w >= 0) & (dist_w < WIN)
    gt = gt_ref[0]
    d0 = pl.multiple_of(t0, TQ)

    def gate_rows(g, branch):
        return jnp.concatenate([gt[MISC_GATE0 + 3 * (g * R + r) + branch:MISC_GATE0 + 3 * (g * R + r) + branch + 1, :]
                                for r in range(R)], axis=1)

    def heads(s, ok, fn):
        return [fn(s[:, r * TQ:(r + 1) * TQ], ok) for r in range(R)]

    qa_l, o_fix_l, s_d_l = [], [], []
    for g in range(G):
        qs = q_ref[0, g * R:(g + 1) * R].reshape(R * TQ, LANES)

        p_c = heads(_bdot_nt(kc_ref[0, g], qs), ok_c, _softmax_keys)
        o_cmp = jnp.dot(vct_ref[0, g], jnp.concatenate(p_c, axis=1).astype(BF16),
                        preferred_element_type=F32)

        p_w = heads(_bdot_nt(kw_ref[0, g, pl.ds(wstart, wlen), :], qs), ok_w, _softmax_keys)
        o_win = jnp.dot(vwt_ref[0, g, :, pl.ds(wstart, wlen)], jnp.concatenate(p_w, axis=1).astype(BF16),
                        preferred_element_type=F32)
        o_fix_l.append(gate_rows(g, 0) * o_cmp + gate_rows(g, 2) * o_win)

        importance = _dot_sel_lhs(overlap_t, p_c[0] + p_c[1] + p_c[2] + p_c[3])
        score = jnp.where(future, NEG_INF, jnp.where(forced, FORCE_SCORE, importance))
        rank = jnp.zeros((NB, TQ), F32)
        for k in range(NB):
            sk = score[k:k + 1, :]
            rank = rank + jnp.where((sk > score) | ((sk == score) & (jb > k)), 1.0, 0.0)
        sel_t = (rank < SLC_TOPN) & (score > NEG_INF)
        selt_ref[g] = jnp.where(sel_t, 1.0, 0.0)

        sel_d = jnp.where(key_r < SLC_BLOCK, selt_ref[g, pl.ds(2 * i, 1), :], selt_ref[g, pl.ds(2 * i + 1, 1), :])
        s_d = jnp.concatenate(heads(_bdot_nt(ks_ref[0, g, pl.ds(d0, TQ), :], qs), (sel_d > 0.5) & ok_d,
                                    lambda s, ok: jnp.where(ok, s, NEG_INF)), axis=1)
        s_d_l.append(s_d)

        masked_t = jnp.where(sel_t & (jb < 2 * i), 0.0, 1.0)
        cols_t = jnp.concatenate([jnp.zeros((AUG_SEL, TQ), F32), masked_t,
                                  jnp.zeros((LANES - AUG_SEL - NB, TQ), F32)], axis=0)
        cols = cols_t.T.astype(BF16)
        qa_l.append(jnp.concatenate([qs[r * TQ:(r + 1) * TQ] + cols for r in range(R)], axis=0))

    n_tiles = (t0 + SLC_TILE - 1) // SLC_TILE

    def tile_scores(j, g):
        off = pl.multiple_of(j * SLC_TILE, SLC_TILE)
        return _bdot_nt(ks_ref[0, g, pl.ds(off, SLC_TILE), :], qa_l[g])

    def slc_step(j, carry):
        off = pl.multiple_of(j * SLC_TILE, SLC_TILE)
        nxt = jnp.minimum(j + 1, n_tiles - 1)
        out = []
        for g in range(G):
            m, l, acc = carry[g]
            s = s_ref[g]
            s_ref[g] = tile_scores(nxt, g)
            m_new = jnp.maximum(m, jnp.max(s, axis=0, keepdims=True))
            alpha = jnp.exp(m - m_new)
            p = jnp.exp(s - m_new)
            l = alpha * l + jnp.sum(p, axis=0, keepdims=True)
            acc = alpha * acc + jnp.dot(vst_ref[0, g, :, pl.ds(off, SLC_TILE)], p.astype(BF16),
                                        preferred_element_type=F32)
            out.append((m_new, l, acc))
        return tuple(out)

    init = []
    for g in range(G):
        s_ref[g] = tile_scores(0, g)
        m = jnp.max(s_d_l[g], axis=0, keepdims=True)
        p = jnp.exp(s_d_l[g] - m)
        init.append((m, jnp.sum(p, axis=0, keepdims=True),
                     jnp.dot(vst_ref[0, g, :, pl.ds(d0, TQ)], p.astype(BF16), preferred_element_type=F32)))
    slc = lax.fori_loop(0, n_tiles, slc_step, tuple(init))

    for g in range(G):
        _, l, acc = slc[g]
        o = o_fix_l[g] + (gate_rows(g, 1) * (1.0 / jnp.maximum(l, 1e-30))) * acc
        for a in range(R // 2):
            pair = jnp.concatenate([o[:, (2 * a) * TQ:(2 * a + 1) * TQ], o[:, (2 * a + 1) * TQ:(2 * a + 2) * TQ]],
                                   axis=0)
            lo = (g * R + 2 * a) * dh
            o_ref[:, lo:lo + 2 * dh] = pair.T.astype(BF16)


def _nsa_attention(qn, gates_t, kc, vct, ksn, vst, kwn, vwt, B, T):
    G, H, dh = NSA_KV_HEADS, NSA_HEADS, NSA_HEAD_DIM
    TQ = LANES
    nq = T // TQ
    nseg = kc.shape[2]
    per_b = lambda shape: pl.BlockSpec((1,) + shape, lambda b, i: (b, 0, 0, 0))
    return pl.pallas_call(
        functools.partial(_nsa_attn_kernel, seq_len=T),
        grid=(B, nq),
        in_specs=[pl.BlockSpec((1, H, TQ, LANES), lambda b, i: (b, 0, i, 0)),
                  pl.BlockSpec((1, GATE_ROWS, TQ), lambda b, i: (b, 0, i)),
                  per_b((G, nseg, LANES)), per_b((G, dh, nseg)),
                  per_b((G, T, LANES)), per_b((G, dh, T)), per_b((G, T, LANES)), per_b((G, dh, T))],
        out_specs=pl.BlockSpec((TQ, NSA_Q_W), lambda b, i: (b * nq + i, 0)),
        out_shape=jax.ShapeDtypeStruct((B * T, NSA_Q_W), BF16),
        scratch_shapes=[pltpu.VMEM((G, T // SLC_BLOCK, TQ), F32),
                        pltpu.VMEM((G, SLC_TILE, NSA_REP * TQ), F32)],
        compiler_params=_cparams("parallel", "arbitrary"),
        name="nsa_attention",
    )(qn, gates_t, kc, vct, ksn, vst, kwn, vwt)


def _nsa_mixer(u, B, T, q_norm_g, k_norm_g, cmp_pe, cmp_w1, cmp_w2):
    G, dh = NSA_KV_HEADS, NSA_HEAD_DIM
    assert T % LANES == 0 and T >= (WIN // LANES + 1) * LANES and T // CMP_STRIDE == LANES
    assert SLC_BLOCK == 1 << SLC_SHIFT and LANES == 2 * SLC_BLOCK
    qn, kvc, ksn, vst, kwn, vwt, gates_t = _nsa_prep(u, B, T, q_norm_g, k_norm_g)
    nseg = T // CMP_STRIDE
    xseg = kvc.reshape(B, nseg, CMP_STRIDE, 2 * G, dh).transpose(0, 3, 1, 2, 4).reshape(
        B, 2 * G, nseg, CMP_STRIDE * dh)
    kc, vct = _nsa_compress(xseg, cmp_pe, cmp_w1, cmp_w2, k_norm_g[0])
    return _nsa_attention(qn, gates_t, kc, vct, ksn, vst, kwn, vwt, B, T)


def _merge_kernel(x_ref, h_ref, oa_ref, ob_ref, oc_ref, od_ref, mod_ref, wg_ref, wb_ref, wo_ref, y_ref):
    h = h_ref[...]
    merged = None
    for i, o_ref in enumerate((oa_ref, ob_ref, oc_ref, od_ref)):
        gate = _sigmoid(jnp.dot(h, wg_ref[i], preferred_element_type=F32))
        term = gate * jnp.dot(o_ref[...], wb_ref[i], preferred_element_type=F32)
        merged = term if merged is None else merged + term
    y_ref[...] = x_ref[...] + mod_ref[0, 2:3, :] * _bdot(merged, wo_ref[...])


def _merge(x2d, h, outs, mod_l, wg, wb, wo, T):
    M, D = x2d.shape
    tm = min(512, T)
    per_b = T // tm
    row = lambda w: pl.BlockSpec((tm, w), lambda m: (m, 0))
    const = lambda shape: pl.BlockSpec(shape, lambda m: (0,) * len(shape), pipeline_mode=pl.Buffered(1))
    return pl.pallas_call(
        _merge_kernel,
        grid=(M // tm,),
        in_specs=[row(D), row(D), row(MIX_W), row(MIX_W), row(MIX_W), row(MIX_W),
                  pl.BlockSpec((1, 6, D), lambda m: (m // per_b, 0, 0)),
                  const((N_BRANCH, D, D)), const((N_BRANCH, MIX_W, D)), const((D, D))],
        out_specs=row(D),
        out_shape=jax.ShapeDtypeStruct((M, D), F32),
        compiler_params=_cparams("parallel"),
        name="gated_merge_out_proj",
    )(x2d, h, *outs, mod_l, wg, wb, wo)


FFN_CHUNK = 256


def _ffn_kernel(x_ref, mod_ref, g_ref, wa_ref, wb_ref, wo_ref, y_ref, acc_ref):
    x = x_ref[...]
    y = x * lax.rsqrt(jnp.mean(x * x, axis=-1, keepdims=True) + RMS_EPS) * g_ref[...]
    h = (y * (1.0 + mod_ref[0, 4:5, :]) + mod_ref[0, 3:4, :]).astype(BF16)
    d_ff = wo_ref.shape[0]
    for c in range(d_ff // FFN_CHUNK):
        cols = slice(c * FFN_CHUNK, (c + 1) * FFN_CHUNK)
        a = jnp.dot(h, wa_ref[:, cols], preferred_element_type=F32)
        b = jnp.dot(h, wb_ref[:, cols], preferred_element_type=F32)
        part = _bdot(_silu(a) * b, wo_ref[cols, :])
        if c == 0:
            acc_ref[...] = part
        else:
            acc_ref[...] += part
    y_ref[...] = x + mod_ref[0, 5:6, :] * acc_ref[...]


def _ffn(x2d, mod_l, norm_g, w_in, w_out, T):
    M, D = x2d.shape
    d_ff = w_out.shape[0]
    assert d_ff % FFN_CHUNK == 0
    tm = min(1024, T)
    per_b = T // tm
    const = lambda shape, idx: pl.BlockSpec(shape, lambda m: idx, pipeline_mode=pl.Buffered(1))
    return pl.pallas_call(
        _ffn_kernel,
        grid=(M // tm,),
        in_specs=[pl.BlockSpec((tm, D), lambda m: (m, 0)),
                  pl.BlockSpec((1, 6, D), lambda m: (m // per_b, 0, 0)),
                  pl.BlockSpec((1, D), lambda m: (0, 0)),
                  const((D, d_ff), (0, 0)), const((D, d_ff), (0, 1)), const((d_ff, D), (0, 0))],
        out_specs=pl.BlockSpec((tm, D), lambda m: (m, 0)),
        out_shape=jax.ShapeDtypeStruct((M, D), F32),
        scratch_shapes=[pltpu.VMEM((tm, D), F32)],
        compiler_params=_cparams("parallel"),
        name="swiglu_ffn",
    )(x2d, mod_l, norm_g.reshape(1, D), w_in, w_in, w_out)


def _pack_w_in(w_in):
    D = w_in.shape[0]
    o1 = SSD_IN
    o2 = o1 + SC_IN
    o3 = o2 + SG_IN
    q0 = o3
    kv0 = q0 + NSA_Q_W
    gt0 = kv0 + 6 * NSA_KV_W
    parts = [w_in[:, :MIX_W + SSD_XBC],
             w_in[:, o1:o3],
             w_in[:, q0:gt0],
             w_in[:, MIX_W + SSD_XBC:o1],
             w_in[:, gt0:gt0 + 3 * NSA_HEADS]]
    w = jnp.concatenate(parts, axis=1)
    return jnp.pad(w, ((0, 0), (0, U_WIDTH - w.shape[1]))).astype(BF16)


def kernel(x, c, ada_w, ada_b, norm_mix_g, norm_ffn_g, w_in, ssd_conv_w, ssd_conv_b, ssd_dt_bias, ssd_a_log, ssd_d,
           ssd_norm_g, sc_conv_w, sg_norm_g, sg_w, sg_b, nsa_q_norm_g, nsa_k_norm_g, nsa_cmp_pe, nsa_cmp_w1,
           nsa_cmp_w2, w_branch, w_branch_gate, w_out, w_ffn_in, w_ffn_out):
    B, T, D = x.shape
    L = w_in.shape[0]
    mod = _modulation(c, ada_w, ada_b).reshape(L, B, 6, D)
    x2d = x.reshape(B * T, D)
    for l in range(L):
        u, h = _in_proj(x2d, mod[l], norm_mix_g[l], _pack_w_in(w_in[l]), T)
        outs = (
            _ssd_mixer(u, B, T, ssd_conv_w[l], ssd_conv_b[l], ssd_dt_bias[l], ssd_a_log[l], ssd_d[l],
                       ssd_norm_g[l]),
            _short_conv_mixer(u, B, T, sc_conv_w[l]),
            _spatial_gating_mixer(u, B, T, sg_norm_g[l], sg_w[l], sg_b[l]),
            _nsa_mixer(u, B, T, nsa_q_norm_g[l], nsa_k_norm_g[l], nsa_cmp_pe[l], nsa_cmp_w1[l], nsa_cmp_w2[l]),
        )
        x2d = _merge(x2d, h, outs, mod[l], w_branch_gate[l].astype(BF16), w_branch[l].astype(BF16),
                     w_out[l].astype(BF16), T)
        x2d = _ffn(x2d, mod[l], norm_ffn_g[l], w_ffn_in[l].astype(BF16), w_ffn_out[l].astype(BF16), T)
    return x2d.reshape(B, T, D)
```

```python
import functools
import math

import jax
import jax.numpy as jnp
from jax import lax
from jax.experimental import pallas as pl
from jax.experimental.pallas import tpu as pltpu

F32 = jnp.float32
BF16 = jnp.bfloat16
RMS_EPS = 1e-6
NEG_INF = float("-inf")

MIX_W = 512
N_BRANCH = 4

SSD_HEAD_DIM = 64
SSD_HEADS = 8
SSD_GROUPS = 2
SSD_STATE = 128
SSD_CONV = 4
SSD_CHUNK = 128
SSD_XBC = MIX_W + 2 * SSD_GROUPS * SSD_STATE
SSD_IN = MIX_W + SSD_XBC + SSD_HEADS
SC_CONV = 3
SC_IN = 3 * MIX_W
SG_GROUPS = 4
SG_CHUNK = 128
SG_IN = 2 * MIX_W
NSA_HEADS = 8
NSA_KV_HEADS = 2
NSA_REP = NSA_HEADS // NSA_KV_HEADS
NSA_HEAD_DIM = 64
CMP_BLOCK = 32
CMP_STRIDE = 16
SLC_BLOCK = 64
SLC_TOPN = 8
WIN = 256
FORCE_SCORE = 1e9
NSA_KV_W = NSA_KV_HEADS * NSA_HEAD_DIM
NSA_Q_W = NSA_HEADS * NSA_HEAD_DIM
NSA_IN = NSA_Q_W + 6 * NSA_KV_W + 3 * NSA_HEADS

LANES = 128
SUBLANES = 8
VMEM_LIMIT_BYTES = 56 * 1024 * 1024

COL_Z = 0
COL_XBC = COL_Z + MIX_W
COL_SC = COL_XBC + SSD_XBC
COL_SG = COL_SC + SC_IN
COL_Q = COL_SG + SG_IN
COL_KV = COL_Q + NSA_Q_W
COL_MISC = COL_KV + 6 * NSA_KV_W
MISC_GATE0 = SSD_HEADS
GATE_ROWS = 32
SLC_SHIFT = 6
SLC_TILE = 512
AUG_POS = NSA_HEAD_DIM
AUG_SEL = AUG_POS + SUBLANES
MASK_SCORE = -(2.0 ** 100)
U_TILE_N = 512
U_WIDTH = ((COL_MISC + LANES + U_TILE_N - 1) // U_TILE_N) * U_TILE_N


def _cparams(*sem):
    return pltpu.CompilerParams(dimension_semantics=sem, vmem_limit_bytes=VMEM_LIMIT_BYTES)


def _bdot(a, b):
    return jnp.dot(a.astype(BF16), b.astype(BF16), preferred_element_type=F32)


def _bdot_nt(a, b):
    return lax.dot_general(a.astype(BF16), b.astype(BF16), (((1,), (1,)), ((), ())),
                           preferred_element_type=F32)


def _split3(a):
    hi = a.astype(BF16)
    r1 = a - hi.astype(F32)
    mid = r1.astype(BF16)
    lo = (r1 - mid.astype(F32)).astype(BF16)
    return hi, mid, lo


def _dot_sel_rhs(a, sel):
    hi, mid, lo = _split3(a)
    return (jnp.dot(hi, sel, preferred_element_type=F32) + jnp.dot(mid, sel, preferred_element_type=F32)
            + jnp.dot(lo, sel, preferred_element_type=F32))


def _dot_sel_lhs(sel, a):
    hi, mid, lo = _split3(a)
    return (jnp.dot(sel, hi, preferred_element_type=F32) + jnp.dot(sel, mid, preferred_element_type=F32)
            + jnp.dot(sel, lo, preferred_element_type=F32))


def _sigmoid(x):
    return 1.0 / (1.0 + jnp.exp(-x))


def _silu(x):
    return x * _sigmoid(x)


def _gelu_tanh(x):
    c = math.sqrt(2.0 / math.pi)
    return 0.5 * x * (1.0 + jnp.tanh(c * (x + 0.044715 * (x * x * x))))


def _softplus(x):
    return jnp.maximum(x, 0.0) + jnp.log1p(jnp.exp(-jnp.abs(x)))


def _shift_rows(x, tail, k, row8):
    sh = pltpu.roll(x, k, 0)
    tl = pltpu.roll(tail, k, 0)
    top = jnp.where(row8 < k, tl, sh[0:SUBLANES])
    return jnp.concatenate([top, sh[SUBLANES:]], axis=0)


def _mod_kernel(c_ref, w_ref, b_ref, o_ref):
    o_ref[0] = _bdot(_silu(c_ref[...]), w_ref[0]) + b_ref[0]


def _modulation(c, ada_w, ada_b):
    L, D, D6 = ada_w.shape
    B = c.shape[0]
    tn = D6 // 4
    return pl.pallas_call(
        _mod_kernel,
        grid=(L, D6 // tn),
        in_specs=[pl.BlockSpec((B, D), lambda l, n: (0, 0)),
                  pl.BlockSpec((1, D, tn), lambda l, n: (l, 0, n)),
                  pl.BlockSpec((1, 1, tn), lambda l, n: (l, 0, n))],
        out_specs=pl.BlockSpec((1, B, tn), lambda l, n: (l, 0, n)),
        out_shape=jax.ShapeDtypeStruct((L, B, D6), F32),
        compiler_params=_cparams("parallel", "parallel"),
        name="adaln_modulation",
    )(c, ada_w, ada_b.reshape(L, 1, D6))


def _in_kernel(x_ref, mod_ref, g_ref, w_ref, u_ref, h_ref, misc_ref):
    n = pl.program_id(1)

    @pl.when(n == 0)
    def _():
        x = x_ref[...]
        y = x * lax.rsqrt(jnp.mean(x * x, axis=-1, keepdims=True) + RMS_EPS) * g_ref[...]
        h = y * (1.0 + mod_ref[0, 1:2, :]) + mod_ref[0, 0:1, :]
        h_ref[...] = h.astype(BF16)

    u = jnp.dot(h_ref[...], w_ref[...], preferred_element_type=F32)
    u_ref[...] = u.astype(BF16)

    @pl.when(n == COL_MISC // U_TILE_N)
    def _():
        misc_ref[...] = u[:, COL_MISC % U_TILE_N:COL_MISC % U_TILE_N + LANES]


def _in_proj(x2d, mod_l, norm_g, w_cat, T):
    M, D = x2d.shape
    tm = min(2048, T)
    per_b = T // tm
    return pl.pallas_call(
        _in_kernel,
        grid=(M // tm, U_WIDTH // U_TILE_N),
        in_specs=[pl.BlockSpec((tm, D), lambda m, n: (m, 0)),
                  pl.BlockSpec((1, 6, D), lambda m, n: (m // per_b, 0, 0)),
                  pl.BlockSpec((1, D), lambda m, n: (0, 0)),
                  pl.BlockSpec((D, U_TILE_N), lambda m, n: (0, n))],
        out_specs=[pl.BlockSpec((tm, U_TILE_N), lambda m, n: (m, n)),
                   pl.BlockSpec((tm, D), lambda m, n: (m, 0)),
                   pl.BlockSpec((tm, LANES), lambda m, n: (m, 0))],
        out_shape=[jax.ShapeDtypeStruct((M, U_WIDTH), BF16), jax.ShapeDtypeStruct((M, D), BF16),
                   jax.ShapeDtypeStruct((M, LANES), F32)],
        compiler_params=_cparams("parallel", "arbitrary"),
        name="norm_in_proj",
    )(x2d, mod_l, norm_g.reshape(1, D), w_cat)


def _ssd_kernel(z_ref, xa_ref, xb_ref, misc_ref, cw_ref, cb_ref, dtb_ref, alog_ref, dsk_ref, ng_ref,
                o_ref, tail_ref, st_ref):
    Q, P, N, H, G = SSD_CHUNK, SSD_HEAD_DIM, SSD_STATE, SSD_HEADS, SSD_GROUPS
    R = H // G

    @pl.when(pl.program_id(1) == 0)
    def _():
        tail_ref[...] = jnp.zeros_like(tail_ref)
        st_ref[...] = jnp.zeros_like(st_ref)

    xin = jnp.concatenate([xa_ref[...], xb_ref[...]], axis=1).astype(F32)
    tail = tail_ref[...]
    row8 = lax.broadcasted_iota(jnp.int32, (SUBLANES, SSD_XBC), 0)
    acc = xin * cw_ref[SSD_CONV - 1:SSD_CONV, :] + cb_ref[...]
    for k in range(1, SSD_CONV):
        acc = acc + _shift_rows(xin, tail, k, row8) * cw_ref[SSD_CONV - 1 - k:SSD_CONV - k, :]
    tail_ref[...] = xin[Q - SUBLANES:Q, :]
    xbc = _silu(acc)
    xs = xbc[:, :MIX_W]
    bm = xbc[:, MIX_W:MIX_W + G * N]
    cm = xbc[:, MIX_W + G * N:]

    lane = lax.broadcasted_iota(jnp.int32, (Q, LANES), 1)
    rowi = lax.broadcasted_iota(jnp.int32, (Q, LANES), 0)
    is_head = lane < H
    dt = jnp.where(is_head, _softplus(misc_ref[...] + dtb_ref[...]), 0.0)
    a = dt * (-jnp.exp(alog_ref[...]))
    tri = (lane <= rowi).astype(BF16)
    a_cs = _dot_sel_lhs(tri, a)
    a_cs_t = _dot_sel_rhs(a.T, (rowi <= lane).astype(BF16))
    a_last = a_cs[Q - 1:Q, :]
    ea = jnp.exp(a_cs)
    dec = jnp.exp(a_last - a_cs)
    e_row = lax.broadcasted_iota(jnp.int32, (LANES, MIX_W), 0)
    e_col = lax.broadcasted_iota(jnp.int32, (LANES, MIX_W), 1)
    expand = (jnp.right_shift(e_col, 6) == e_row).astype(BF16)
    xdt = xs * _dot_sel_rhs(dt, expand)
    ea_e = _dot_sel_rhs(ea, expand)
    xdec = xdt * _dot_sel_rhs(dec, expand)
    causal = lane <= rowi

    ys = []
    for g in range(G):
        bg = bm[:, g * N:(g + 1) * N]
        cg = cm[:, g * N:(g + 1) * N].astype(BF16)
        cb = _bdot_nt(cg, bg)
        bg_t = bg.T.astype(BF16)
        for r in range(R):
            h = g * R + r
            seg = jnp.where(causal, a_cs[:, h:h + 1] - a_cs_t[h:h + 1, :], NEG_INF)
            y_diag = _bdot(cb * jnp.exp(seg), xdt[:, h * P:(h + 1) * P])
            state = st_ref[h]
            y_off = _bdot(cg, state) * ea_e[:, h * P:(h + 1) * P]
            st_ref[h] = state * jnp.exp(a_last[:, h:h + 1]) + _bdot(bg_t, xdec[:, h * P:(h + 1) * P])
            ys.append(y_diag + y_off)
    y = jnp.concatenate(ys, axis=1) + xs * dsk_ref[...]
    y = y * _silu(z_ref[...].astype(F32))
    gw = MIX_W // G
    outs = []
    for g in range(G):
        yg = y[:, g * gw:(g + 1) * gw]
        outs.append(yg * lax.rsqrt(jnp.mean(yg * yg, axis=-1, keepdims=True) + RMS_EPS))
    o_ref[...] = (jnp.concatenate(outs, axis=1) * ng_ref[...]).astype(BF16)


def _ssd_mixer(u, misc, B, T, conv_w, conv_b, dt_bias, a_log, d_skip, norm_g):
    nc = T // SSD_CHUNK
    Q = SSD_CHUNK

    def pad_lane(v):
        return jnp.pad(v, (0, LANES - v.shape[0])).reshape(1, LANES)

    row = lambda b, c: b * nc + c
    full = lambda shape: pl.BlockSpec(shape, lambda b, c: (0,) * len(shape))
    return pl.pallas_call(
        _ssd_kernel,
        grid=(B, nc),
        in_specs=[pl.BlockSpec((Q, MIX_W), lambda b, c: (row(b, c), COL_Z // MIX_W)),
                  pl.BlockSpec((Q, MIX_W), lambda b, c: (row(b, c), COL_XBC // MIX_W)),
                  pl.BlockSpec((Q, MIX_W), lambda b, c: (row(b, c), COL_XBC // MIX_W + 1)),
                  pl.BlockSpec((Q, LANES), lambda b, c: (row(b, c), 0)),
                  full((SSD_CONV, SSD_XBC)), full((1, SSD_XBC)), full((1, LANES)), full((1, LANES)),
                  full((1, MIX_W)), full((1, MIX_W))],
        out_specs=pl.BlockSpec((Q, MIX_W), lambda b, c: (row(b, c), 0)),
        out_shape=jax.ShapeDtypeStruct((B * T, MIX_W), BF16),
        scratch_shapes=[pltpu.VMEM((SUBLANES, SSD_XBC), F32),
                        pltpu.VMEM((SSD_HEADS, SSD_STATE, SSD_HEAD_DIM), F32)],
        compiler_params=_cparams("parallel", "arbitrary"),
        name="ssd_mixer",
    )(u, u, u, misc, conv_w, conv_b.reshape(1, SSD_XBC), pad_lane(dt_bias), pad_lane(a_log),
      jnp.repeat(d_skip, SSD_HEAD_DIM).reshape(1, MIX_W), norm_g.reshape(1, MIX_W))


def _sc_kernel(b_ref, c_ref, h_ref, w_ref, o_ref, tail_ref):
    @pl.when(pl.program_id(1) == 0)
    def _():
        tail_ref[...] = jnp.zeros_like(tail_ref)

    cx = c_ref[...].astype(F32) * h_ref[...].astype(F32)
    tt = cx.shape[0]
    tail = tail_ref[...]
    row8 = lax.broadcasted_iota(jnp.int32, (SUBLANES, MIX_W), 0)
    acc = cx * w_ref[SC_CONV - 1:SC_CONV, :]
    for k in range(1, SC_CONV):
        acc = acc + _shift_rows(cx, tail, k, row8) * w_ref[SC_CONV - 1 - k:SC_CONV - k, :]
    tail_ref[...] = cx[tt - SUBLANES:tt, :]
    o_ref[...] = (b_ref[...].astype(F32) * acc).astype(BF16)


def _short_conv_mixer(u, B, T, conv_w):
    tt = min(512, T)
    nt = T // tt
    c0 = COL_SC // MIX_W
    spec = lambda j: pl.BlockSpec((tt, MIX_W), lambda b, i: (b * nt + i, c0 + j))
    return pl.pallas_call(
        _sc_kernel,
        grid=(B, nt),
        in_specs=[spec(0), spec(1), spec(2), pl.BlockSpec((SC_CONV, MIX_W), lambda b, i: (0, 0))],
        out_specs=pl.BlockSpec((tt, MIX_W), lambda b, i: (b * nt + i, 0)),
        out_shape=jax.ShapeDtypeStruct((B * T, MIX_W), BF16),
        scratch_shapes=[pltpu.VMEM((SUBLANES, MIX_W), F32)],
        compiler_params=_cparams("parallel", "arbitrary"),
        name="short_conv_mixer",
    )(u, u, u, conv_w)


def _sg_kernel(u_ref, v_ref, ng_ref, w_ref, bias_ref, o_ref):
    Q = SG_CHUNK
    rowi = lax.broadcasted_iota(jnp.int32, (Q, Q), 0)
    coli = lax.broadcasted_iota(jnp.int32, (Q, Q), 1)
    gd = MIX_W // SG_GROUPS
    ws = [jnp.where(coli <= rowi, w_ref[g], 0.0).astype(BF16) for g in range(SG_GROUPS)]
    for c in range(u_ref.shape[0] // Q):
        rows = slice(c * Q, (c + 1) * Q)
        v = _gelu_tanh(v_ref[rows, :].astype(F32))
        v = (v * lax.rsqrt(jnp.mean(v * v, axis=-1, keepdims=True) + RMS_EPS) * ng_ref[...]).astype(BF16)
        mixed = jnp.concatenate([jnp.dot(ws[g], v[:, g * gd:(g + 1) * gd], preferred_element_type=F32)
                                 for g in range(SG_GROUPS)], axis=1)
        o_ref[rows, :] = (_gelu_tanh(u_ref[rows, :].astype(F32)) * (mixed + bias_ref[...])).astype(BF16)


def _spatial_gating_mixer(u, B, T, norm_g, w_s, b_s):
    Q = SG_CHUNK
    tt = min(4 * Q, T)
    nt = T // tt
    c0 = COL_SG // MIX_W
    bias = jnp.repeat(b_s.T, MIX_W // SG_GROUPS, axis=1)
    return pl.pallas_call(
        _sg_kernel,
        grid=(B, nt),
        in_specs=[pl.BlockSpec((tt, MIX_W), lambda b, c: (b * nt + c, c0)),
                  pl.BlockSpec((tt, MIX_W), lambda b, c: (b * nt + c, c0 + 1)),
                  pl.BlockSpec((1, MIX_W), lambda b, c: (0, 0)),
                  pl.BlockSpec((SG_GROUPS, Q, Q), lambda b, c: (0, 0, 0)),
                  pl.BlockSpec((Q, MIX_W), lambda b, c: (0, 0))],
        out_specs=pl.BlockSpec((tt, MIX_W), lambda b, c: (b * nt + c, 0)),
        out_shape=jax.ShapeDtypeStruct((B * T, MIX_W), BF16),
        compiler_params=_cparams("parallel", "parallel"),
        name="spatial_gating_mixer",
    )(u, u, norm_g.reshape(1, MIX_W), w_s, bias)


def _group_mean_sq(x, width):
    n = x.shape[1]
    r = lax.broadcasted_iota(jnp.int32, (n, n), 0)
    c = lax.broadcasted_iota(jnp.int32, (n, n), 1)
    sh = width.bit_length() - 1
    same = (jnp.right_shift(r, sh) == jnp.right_shift(c, sh)).astype(BF16)
    return _dot_sel_rhs(x * x, same) * (1.0 / width)


def _key_aug(lane, pos):
    return jnp.where((lane == AUG_POS) | (lane == AUG_POS + 1), 1.0,
                     jnp.where(lane == AUG_POS + 2, -(pos & ~(LANES - 1)).astype(F32),
                               jnp.where(lane == AUG_POS + 3, -(pos & (LANES - 1)).astype(F32), 0.0)))


def _nsa_prep_kernel(q_ref, kc_ref, vc_ref, ks_ref, vs_ref, kw_ref, vw_ref, misc_ref, qg_ref, ksg_ref, kwg_ref,
                     qa_ref, kvc_ref, ksa_ref, vst_ref, kwa_ref, vwt_ref, gate_ref):
    dh = NSA_HEAD_DIM
    tt = q_ref.shape[0]
    lane = lax.broadcasted_iota(jnp.int32, (tt, LANES), 1)
    pos = pl.program_id(1) * tt + lax.broadcasted_iota(jnp.int32, (tt, LANES), 0)
    is_feat = lane < dh

    q = q_ref[...].astype(F32)
    qn = q * lax.rsqrt(_group_mean_sq(q, dh) + RMS_EPS) * qg_ref[...] * (dh ** -0.5)
    q_pos = jnp.where(lane == AUG_POS, (pos & ~(LANES - 1)).astype(F32),
                      jnp.where(lane == AUG_POS + 1, (pos & (LANES - 1)).astype(F32),
                                jnp.where((lane == AUG_POS + 2) | (lane == AUG_POS + 3), 1.0, 0.0)))
    for h in range(NSA_HEADS):
        pair = qn[:, (h // 2) * LANES:(h // 2 + 1) * LANES]
        feat = pair if h % 2 == 0 else pltpu.roll(pair, dh, 1)
        qa_ref[0, h] = jnp.where(is_feat, feat, -(2.0 ** -(h + 1)) * q_pos).astype(BF16)

    kvc_ref[...] = jnp.concatenate([kc_ref[...], vc_ref[...]], axis=1)
    ks = ks_ref[...].astype(F32)
    ksn = ks * lax.rsqrt(_group_mean_sq(ks, dh) + RMS_EPS) * ksg_ref[...]
    kw = kw_ref[...].astype(F32)
    kwn = kw * lax.rsqrt(_group_mean_sq(kw, dh) + RMS_EPS) * kwg_ref[...]
    k_pos = _key_aug(lane, pos)
    k_pos_sel = jnp.where(lane == AUG_SEL + jnp.right_shift(pos, SLC_SHIFT), MASK_SCORE, k_pos)
    vs_t = vs_ref[...].astype(F32).T.astype(BF16)
    vw_t = vw_ref[...].astype(F32).T.astype(BF16)
    for g in range(NSA_KV_HEADS):
        sl = slice(g * dh, (g + 1) * dh)
        ksa_ref[0, g] = jnp.where(is_feat, ksn if g == 0 else pltpu.roll(ksn, dh, 1), k_pos_sel).astype(BF16)
        kwa_ref[0, g] = jnp.where(is_feat, kwn if g == 0 else pltpu.roll(kwn, dh, 1), k_pos).astype(BF16)
        vst_ref[0, g] = vs_t[sl, :]
        vwt_ref[0, g] = vw_t[sl, :]
    gate_ref[0] = _sigmoid(misc_ref[...]).T[0:GATE_ROWS, :]


def _nsa_prep(u, misc, B, T, q_norm_g, k_norm_g):
    tt = min(512, T)
    nt = T // tt
    G, H, dh = NSA_KV_HEADS, NSA_HEADS, NSA_HEAD_DIM
    kv0 = COL_KV // LANES
    kvspec = lambda j: pl.BlockSpec((tt, LANES), lambda b, i: (b * nt + i, kv0 + j))
    vec = lambda n: pl.BlockSpec((1, n), lambda b, i: (0, 0))
    kspec = pl.BlockSpec((1, G, tt, LANES), lambda b, i: (b, 0, i, 0))
    kshape = jax.ShapeDtypeStruct((B, G, T, LANES), BF16)
    vspec = pl.BlockSpec((1, G, dh, tt), lambda b, i: (b, 0, 0, i))
    vshape = jax.ShapeDtypeStruct((B, G, dh, T), BF16)
    return pl.pallas_call(
        _nsa_prep_kernel,
        grid=(B, nt),
        in_specs=[pl.BlockSpec((tt, NSA_Q_W), lambda b, i: (b * nt + i, COL_Q // NSA_Q_W)),
                  kvspec(0), kvspec(1), kvspec(2), kvspec(3), kvspec(4), kvspec(5),
                  pl.BlockSpec((tt, LANES), lambda b, i: (b * nt + i, 0)),
                  vec(NSA_Q_W), vec(LANES), vec(LANES)],
        out_specs=[pl.BlockSpec((1, H, tt, LANES), lambda b, i: (b, 0, i, 0)),
                   pl.BlockSpec((tt, 2 * LANES), lambda b, i: (b * nt + i, 0)),
                   kspec, vspec, kspec, vspec,
                   pl.BlockSpec((1, GATE_ROWS, tt), lambda b, i: (b, 0, i))],
        out_shape=[jax.ShapeDtypeStruct((B, H, T, LANES), BF16),
                   jax.ShapeDtypeStruct((B * T, 2 * LANES), BF16),
                   kshape, vshape, kshape, vshape,
                   jax.ShapeDtypeStruct((B, GATE_ROWS, T), F32)],
        compiler_params=_cparams("parallel", "parallel"),
        name="nsa_prep",
    )(u, u, u, u, u, u, u, misc, jnp.tile(q_norm_g, NSA_HEADS).reshape(1, NSA_Q_W),
      jnp.tile(k_norm_g[1], G).reshape(1, LANES), jnp.tile(k_norm_g[2], G).reshape(1, LANES))


def _nsa_cmp_kernel(x_ref, w1_ref, pe_ref, w2k_ref, w2vt_ref, kg_ref, kc_ref, vct_ref):
    G = NSA_KV_HEADS
    half = CMP_STRIDE * NSA_HEAD_DIM
    for j in range(2):
        w1 = w1_ref[j]
        pe_term = jnp.dot(pe_ref[j], w1, preferred_element_type=F32)[0:1, :]
        for g in range(G):
            x = x_ref[0, j * G + g]
            lo = jnp.dot(x, w1[:half], preferred_element_type=F32)
            hi = jnp.dot(x, w1[half:], preferred_element_type=F32)
            nseg = x.shape[0]
            y = _gelu_tanh(lo + pltpu.roll(hi, nseg - 1, 0) + pe_term)
            if j == 0:
                r = _bdot(y, w2k_ref[...])
                ms = jnp.sum(r * r, axis=-1, keepdims=True) * (1.0 / NSA_HEAD_DIM)
                lane = lax.broadcasted_iota(jnp.int32, r.shape, 1)
                last = lax.broadcasted_iota(jnp.int32, r.shape, 0) * CMP_STRIDE + (CMP_BLOCK - 1)
                kc_ref[0, g] = (r * lax.rsqrt(ms + RMS_EPS) * kg_ref[...] + _key_aug(lane, last)).astype(BF16)
            else:
                vct_ref[0, g] = _bdot_nt(w2vt_ref[...], y).astype(BF16)


def _nsa_compress(xseg, cmp_pe, cmp_w1, cmp_w2, k_norm_g0):
    B, _, nseg, width = xseg.shape
    G, dh = NSA_KV_HEADS, NSA_HEAD_DIM
    pe = jnp.broadcast_to(cmp_pe.reshape(2, 1, CMP_BLOCK * dh), (2, SUBLANES, CMP_BLOCK * dh)).astype(BF16)
    full = lambda shape: pl.BlockSpec(shape, lambda b: (0,) * len(shape))
    return pl.pallas_call(
        _nsa_cmp_kernel,
        grid=(B,),
        in_specs=[pl.BlockSpec((1, 2 * G, nseg, width), lambda b: (b, 0, 0, 0)),
                  full((2, CMP_BLOCK * dh, dh)), full((2, SUBLANES, CMP_BLOCK * dh)), full((dh, LANES)),
                  full((dh, dh)), full((1, LANES))],
        out_specs=[pl.BlockSpec((1, G, nseg, LANES), lambda b: (b, 0, 0, 0)),
                   pl.BlockSpec((1, G, dh, nseg), lambda b: (b, 0, 0, 0))],
        out_shape=[jax.ShapeDtypeStruct((B, G, nseg, LANES), BF16), jax.ShapeDtypeStruct((B, G, dh, nseg), BF16)],
        compiler_params=_cparams("parallel"),
        name="nsa_compress",
    )(xseg, cmp_w1.astype(BF16), pe, jnp.pad(cmp_w2[0], ((0, 0), (0, LANES - dh))).astype(BF16),
      cmp_w2[1].T.astype(BF16), jnp.pad(k_norm_g0, (0, LANES - dh)).reshape(1, LANES))


def _softmax_keys(s, ok):
    s = jnp.where(ok, s, NEG_INF)
    m = jnp.max(s, axis=0, keepdims=True)
    m = jnp.where(m > NEG_INF, m, 0.0)
    p = jnp.exp(s - m)
    return p * (1.0 / jnp.maximum(jnp.sum(p, axis=0, keepdims=True), 1e-30))


def _exp_keys(s, ok):
    s = jnp.where(ok, s, NEG_INF)
    m = jnp.max(s, axis=0, keepdims=True)
    m = jnp.where(m > NEG_INF, m, 0.0)
    p = jnp.exp(s - m)
    return p, 1.0 / jnp.maximum(jnp.sum(p, axis=0, keepdims=True), 1e-30)


def _nsa_attn_kernel(q_ref, gt_ref, kc_ref, vct_ref, ks_ref, vst_ref, kw_ref, vwt_ref, o_ref, selt_ref, s_ref, *,
                     seq_len):
    G, R, dh = NSA_KV_HEADS, NSA_REP, NSA_HEAD_DIM
    TQ = LANES
    NB = seq_len // SLC_BLOCK
    i = pl.program_id(1)
    t0 = i * TQ
    key_r = lax.broadcasted_iota(jnp.int32, (TQ, TQ), 0)
    q_l = lax.broadcasted_iota(jnp.int32, (TQ, TQ), 1)

    ok_c = (t0 + q_l) - (key_r * CMP_STRIDE + (CMP_BLOCK - 1)) >= 0
    ok_d = q_l >= key_r
    jb = lax.broadcasted_iota(jnp.int32, (NB, TQ), 0)
    row8 = lax.broadcasted_iota(jnp.int32, (SUBLANES, TQ), 0)
    nl = lax.broadcasted_iota(jnp.int32, (NB, TQ), 1)
    overlap_t = ((nl * CMP_STRIDE < jb * SLC_BLOCK + SLC_BLOCK)
                 & (nl * CMP_STRIDE + CMP_BLOCK > jb * SLC_BLOCK)).astype(BF16)
    cur = jnp.right_shift(t0 + nl, SLC_SHIFT)
    future = jb > cur
    forced = (jb == 0) | (jb == cur) | (jb == cur - 1)

    wstart = pl.multiple_of(jnp.maximum(i - WIN // TQ, 0) * TQ, TQ)
    wlen = (WIN // TQ + 1) * TQ
    wkey = lax.broadcasted_iota(jnp.int32, (wlen, TQ), 0)
    wq = lax.broadcasted_iota(jnp.int32, (wlen, TQ), 1)
    dist_w = (t0 + wq) - (wstart + wkey)
    ok_w = (dist_w >= 0) & (dist_w < WIN)
    gt = gt_ref[0]
    d0 = pl.multiple_of(t0, TQ)

    def gate_rows(g, branch):
        return jnp.concatenate([gt[MISC_GATE0 + 3 * (g * R + r) + branch:MISC_GATE0 + 3 * (g * R + r) + branch + 1, :]
                                for r in range(R)], axis=1)

    def heads(s, ok, fn):
        return [fn(s[:, r * TQ:(r + 1) * TQ], ok) for r in range(R)]

    qa_l, o_fix_l, s_d_l = [], [], []
    for g in range(G):
        qs = q_ref[0, g * R:(g + 1) * R].reshape(R * TQ, LANES)

        p_c = heads(_bdot_nt(kc_ref[0, g], qs), ok_c, _softmax_keys)
        o_cmp = jnp.dot(vct_ref[0, g], jnp.concatenate(p_c, axis=1).astype(BF16),
                        preferred_element_type=F32)

        pw = heads(_bdot_nt(kw_ref[0, g, pl.ds(wstart, wlen), :], qs), ok_w, _exp_keys)
        o_win = jnp.dot(vwt_ref[0, g, :, pl.ds(wstart, wlen)],
                        jnp.concatenate([p for p, _ in pw], axis=1).astype(BF16), preferred_element_type=F32)
        inv_w = jnp.concatenate([inv for _, inv in pw], axis=1)
        o_fix_l.append(gate_rows(g, 0) * o_cmp + (gate_rows(g, 2) * inv_w) * o_win)

        importance = _dot_sel_lhs(overlap_t, p_c[0] + p_c[1] + p_c[2] + p_c[3])
        score = jnp.where(future, NEG_INF, jnp.where(forced, FORCE_SCORE, importance))
        groups = [score[v * SUBLANES:(v + 1) * SUBLANES] for v in range(NB // SUBLANES)]
        ranks = [jnp.zeros((SUBLANES, TQ), F32) for _ in groups]
        for k in range(NB):
            sk = score[k:k + 1, :]
            for v, sv in enumerate(groups):
                if v < k // SUBLANES:
                    ahead = sk > sv
                elif v > k // SUBLANES:
                    ahead = sk >= sv
                else:
                    ahead = (sk > sv) | ((sk == sv) & (row8 > k % SUBLANES))
                ranks[v] = ranks[v] + jnp.where(ahead, 1.0, 0.0)
        rank = jnp.concatenate(ranks, axis=0)
        sel_t = (rank < SLC_TOPN) & (score > NEG_INF)
        selt_ref[g] = jnp.where(sel_t, 1.0, 0.0)

        sel_d = jnp.where(key_r < SLC_BLOCK, selt_ref[g, pl.ds(2 * i, 1), :], selt_ref[g, pl.ds(2 * i + 1, 1), :])
        s_d = jnp.concatenate(heads(_bdot_nt(ks_ref[0, g, pl.ds(d0, TQ), :], qs), (sel_d > 0.5) & ok_d,
                                    lambda s, ok: jnp.where(ok, s, NEG_INF)), axis=1)
        s_d_l.append(s_d)

        masked_t = jnp.where(sel_t & (jb < 2 * i), 0.0, 1.0)
        cols_t = jnp.concatenate([jnp.zeros((AUG_SEL, TQ), F32), masked_t,
                                  jnp.zeros((LANES - AUG_SEL - NB, TQ), F32)], axis=0)
        cols = cols_t.T.astype(BF16)
        qa_l.append(jnp.concatenate([qs[r * TQ:(r + 1) * TQ] + cols for r in range(R)], axis=0))

    n_tiles = (t0 + SLC_TILE - 1) // SLC_TILE

    def tile_scores(j, g):
        off = pl.multiple_of(j * SLC_TILE, SLC_TILE)
        return _bdot_nt(ks_ref[0, g, pl.ds(off, SLC_TILE), :], qa_l[g])

    def slc_step(j, carry):
        off = pl.multiple_of(j * SLC_TILE, SLC_TILE)
        nxt = jnp.minimum(j + 1, n_tiles - 1)
        out = []
        for g in range(G):
            m, l, acc = carry[g]
            s = s_ref[g]
            s_ref[g] = tile_scores(nxt, g)
            m_new = jnp.maximum(m, jnp.max(s, axis=0, keepdims=True))
            alpha = jnp.exp(m - m_new)
            p = jnp.exp(s - m_new)
            l = alpha * l + jnp.sum(p, axis=0, keepdims=True)
            acc = alpha * acc + jnp.dot(vst_ref[0, g, :, pl.ds(off, SLC_TILE)], p.astype(BF16),
                                        preferred_element_type=F32)
            out.append((m_new, l, acc))
        return tuple(out)

    init = []
    for g in range(G):
        s_ref[g] = tile_scores(0, g)
        m = jnp.max(s_d_l[g], axis=0, keepdims=True)
        p = jnp.exp(s_d_l[g] - m)
        init.append((m, jnp.sum(p, axis=0, keepdims=True),
                     jnp.dot(vst_ref[0, g, :, pl.ds(d0, TQ)], p.astype(BF16), preferred_element_type=F32)))
    slc = lax.fori_loop(0, n_tiles, slc_step, tuple(init))

    for g in range(G):
        _, l, acc = slc[g]
        o = o_fix_l[g] + (gate_rows(g, 1) * (1.0 / jnp.maximum(l, 1e-30))) * acc
        for a in range(R // 2):
            pair = jnp.concatenate([o[:, (2 * a) * TQ:(2 * a + 1) * TQ], o[:, (2 * a + 1) * TQ:(2 * a + 2) * TQ]],
                                   axis=0)
            lo = (g * R + 2 * a) * dh
            o_ref[:, lo:lo + 2 * dh] = pair.T.astype(BF16)


def _nsa_attention(qn, gates_t, kc, vct, ksn, vst, kwn, vwt, B, T):
    G, H, dh = NSA_KV_HEADS, NSA_HEADS, NSA_HEAD_DIM
    TQ = LANES
    nq = T // TQ
    nseg = kc.shape[2]
    per_b = lambda shape: pl.BlockSpec((1,) + shape, lambda b, i: (b, 0, 0, 0))
    return pl.pallas_call(
        functools.partial(_nsa_attn_kernel, seq_len=T),
        grid=(B, nq),
        in_specs=[pl.BlockSpec((1, H, TQ, LANES), lambda b, i: (b, 0, i, 0)),
                  pl.BlockSpec((1, GATE_ROWS, TQ), lambda b, i: (b, 0, i)),
                  per_b((G, nseg, LANES)), per_b((G, dh, nseg)),
                  per_b((G, T, LANES)), per_b((G, dh, T)), per_b((G, T, LANES)), per_b((G, dh, T))],
        out_specs=pl.BlockSpec((TQ, NSA_Q_W), lambda b, i: (b * nq + i, 0)),
        out_shape=jax.ShapeDtypeStruct((B * T, NSA_Q_W), BF16),
        scratch_shapes=[pltpu.VMEM((G, T // SLC_BLOCK, TQ), F32),
                        pltpu.VMEM((G, SLC_TILE, NSA_REP * TQ), F32)],
        compiler_params=_cparams("parallel", "arbitrary"),
        name="nsa_attention",
    )(qn, gates_t, kc, vct, ksn, vst, kwn, vwt)


def _nsa_mixer(u, misc, B, T, q_norm_g, k_norm_g, cmp_pe, cmp_w1, cmp_w2):
    G, dh = NSA_KV_HEADS, NSA_HEAD_DIM
    assert T % LANES == 0 and T >= (WIN // LANES + 1) * LANES and T // CMP_STRIDE == LANES
    assert SLC_BLOCK == 1 << SLC_SHIFT and LANES == 2 * SLC_BLOCK
    qn, kvc, ksn, vst, kwn, vwt, gates_t = _nsa_prep(u, misc, B, T, q_norm_g, k_norm_g)
    nseg = T // CMP_STRIDE
    xseg = kvc.reshape(B, nseg, CMP_STRIDE, 2 * G, dh).transpose(0, 3, 1, 2, 4).reshape(
        B, 2 * G, nseg, CMP_STRIDE * dh)
    kc, vct = _nsa_compress(xseg, cmp_pe, cmp_w1, cmp_w2, k_norm_g[0])
    return _nsa_attention(qn, gates_t, kc, vct, ksn, vst, kwn, vwt, B, T)


def _merge_kernel(x_ref, h_ref, oa_ref, ob_ref, oc_ref, od_ref, mod_ref, wg_ref, wb_ref, wo_ref, y_ref):
    h = h_ref[...]
    merged = None
    for i, o_ref in enumerate((oa_ref, ob_ref, oc_ref, od_ref)):
        gate = _sigmoid(jnp.dot(h, wg_ref[i], preferred_element_type=F32))
        term = gate * jnp.dot(o_ref[...], wb_ref[i], preferred_element_type=F32)
        merged = term if merged is None else merged + term
    y_ref[...] = x_ref[...] + mod_ref[0, 2:3, :] * _bdot(merged, wo_ref[...])


def _merge(x2d, h, outs, mod_l, wg, wb, wo, T):
    M, D = x2d.shape
    tm = min(512, T)
    per_b = T // tm
    row = lambda w: pl.BlockSpec((tm, w), lambda m: (m, 0))
    const = lambda shape: pl.BlockSpec(shape, lambda m: (0,) * len(shape), pipeline_mode=pl.Buffered(1))
    return pl.pallas_call(
        _merge_kernel,
        grid=(M // tm,),
        in_specs=[row(D), row(D), row(MIX_W), row(MIX_W), row(MIX_W), row(MIX_W),
                  pl.BlockSpec((1, 6, D), lambda m: (m // per_b, 0, 0)),
                  const((N_BRANCH, D, D)), const((N_BRANCH, MIX_W, D)), const((D, D))],
        out_specs=row(D),
        out_shape=jax.ShapeDtypeStruct((M, D), F32),
        compiler_params=_cparams("parallel"),
        name="gated_merge_out_proj",
    )(x2d, h, *outs, mod_l, wg, wb, wo)


FFN_CHUNK = 256


def _ffn_kernel(x_ref, mod_ref, g_ref, wa_ref, wb_ref, wo_ref, y_ref, acc_ref):
    x = x_ref[...]
    y = x * lax.rsqrt(jnp.mean(x * x, axis=-1, keepdims=True) + RMS_EPS) * g_ref[...]
    h = (y * (1.0 + mod_ref[0, 4:5, :]) + mod_ref[0, 3:4, :]).astype(BF16)
    d_ff = wo_ref.shape[0]
    for c in range(d_ff // FFN_CHUNK):
        cols = slice(c * FFN_CHUNK, (c + 1) * FFN_CHUNK)
        a = jnp.dot(h, wa_ref[:, cols], preferred_element_type=F32)
        b = jnp.dot(h, wb_ref[:, cols], preferred_element_type=F32)
        part = _bdot(_silu(a) * b, wo_ref[cols, :])
        if c == 0:
            acc_ref[...] = part
        else:
            acc_ref[...] += part
    y_ref[...] = x + mod_ref[0, 5:6, :] * acc_ref[...]


def _ffn(x2d, mod_l, norm_g, w_in, w_out, T):
    M, D = x2d.shape
    d_ff = w_out.shape[0]
    assert d_ff % FFN_CHUNK == 0
    tm = min(1024, T)
    per_b = T // tm
    const = lambda shape, idx: pl.BlockSpec(shape, lambda m: idx, pipeline_mode=pl.Buffered(1))
    return pl.pallas_call(
        _ffn_kernel,
        grid=(M // tm,),
        in_specs=[pl.BlockSpec((tm, D), lambda m: (m, 0)),
                  pl.BlockSpec((1, 6, D), lambda m: (m // per_b, 0, 0)),
                  pl.BlockSpec((1, D), lambda m: (0, 0)),
                  const((D, d_ff), (0, 0)), const((D, d_ff), (0, 1)), const((d_ff, D), (0, 0))],
        out_specs=pl.BlockSpec((tm, D), lambda m: (m, 0)),
        out_shape=jax.ShapeDtypeStruct((M, D), F32),
        scratch_shapes=[pltpu.VMEM((tm, D), F32)],
        compiler_params=_cparams("parallel"),
        name="swiglu_ffn",
    )(x2d, mod_l, norm_g.reshape(1, D), w_in, w_in, w_out)


def _pack_w_in(w_in):
    D = w_in.shape[0]
    o1 = SSD_IN
    o2 = o1 + SC_IN
    o3 = o2 + SG_IN
    q0 = o3
    kv0 = q0 + NSA_Q_W
    gt0 = kv0 + 6 * NSA_KV_W
    parts = [w_in[:, :MIX_W + SSD_XBC],
             w_in[:, o1:o3],
             w_in[:, q0:gt0],
             w_in[:, MIX_W + SSD_XBC:o1],
             w_in[:, gt0:gt0 + 3 * NSA_HEADS]]
    w = jnp.concatenate(parts, axis=1)
    return jnp.pad(w, ((0, 0), (0, U_WIDTH - w.shape[1]))).astype(BF16)


def kernel(x, c, ada_w, ada_b, norm_mix_g, norm_ffn_g, w_in, ssd_conv_w, ssd_conv_b, ssd_dt_bias, ssd_a_log, ssd_d,
           ssd_norm_g, sc_conv_w, sg_norm_g, sg_w, sg_b, nsa_q_norm_g, nsa_k_norm_g, nsa_cmp_pe, nsa_cmp_w1,
           nsa_cmp_w2, w_branch, w_branch_gate, w_out, w_ffn_in, w_ffn_out):
    B, T, D = x.shape
    L = w_in.shape[0]
    mod = _modulation(c, ada_w, ada_b).reshape(L, B, 6, D)
    x2d = x.reshape(B * T, D)
    for l in range(L):
        u, h, misc = _in_proj(x2d, mod[l], norm_mix_g[l], _pack_w_in(w_in[l]), T)
        outs = (
            _ssd_mixer(u, misc, B, T, ssd_conv_w[l], ssd_conv_b[l], ssd_dt_bias[l], ssd_a_log[l], ssd_d[l],
                       ssd_norm_g[l]),
            _short_conv_mixer(u, B, T, sc_conv_w[l]),
            _spatial_gating_mixer(u, B, T, sg_norm_g[l], sg_w[l], sg_b[l]),
            _nsa_mixer(u, misc, B, T, nsa_q_norm_g[l], nsa_k_norm_g[l], nsa_cmp_pe[l], nsa_cmp_w1[l], nsa_cmp_w2[l]),
        )
        x2d = _merge(x2d, h, outs, mod[l], w_branch_gate[l].astype(BF16), w_branch[l].astype(BF16),
                     w_out[l].astype(BF16), T)
        x2d = _ffn(x2d, mod[l], norm_ffn_g[l], w_ffn_in[l].astype(BF16), w_ffn_out[l].astype(BF16), T)
    return x2d.reshape(B, T, D)
```

```python
import functools
import math

import jax
import jax.numpy as jnp
from jax import lax
from jax.experimental import pallas as pl
from jax.experimental.pallas import tpu as pltpu

F32 = jnp.float32
BF16 = jnp.bfloat16
RMS_EPS = 1e-6
NEG_INF = float("-inf")

MIX_W = 512
N_BRANCH = 4

SSD_HEAD_DIM = 64
SSD_HEADS = 8
SSD_GROUPS = 2
SSD_STATE = 128
SSD_CONV = 4
SSD_CHUNK = 128
SSD_XBC = MIX_W + 2 * SSD_GROUPS * SSD_STATE
SSD_IN = MIX_W + SSD_XBC + SSD_HEADS
SC_CONV = 3
SC_IN = 3 * MIX_W
SG_GROUPS = 4
SG_CHUNK = 128
SG_IN = 2 * MIX_W
NSA_HEADS = 8
NSA_KV_HEADS = 2
NSA_REP = NSA_HEADS // NSA_KV_HEADS
NSA_HEAD_DIM = 64
CMP_BLOCK = 32
CMP_STRIDE = 16
SLC_BLOCK = 64
SLC_TOPN = 8
WIN = 256
FORCE_SCORE = 1e9
NSA_KV_W = NSA_KV_HEADS * NSA_HEAD_DIM
NSA_Q_W = NSA_HEADS * NSA_HEAD_DIM
NSA_IN = NSA_Q_W + 6 * NSA_KV_W + 3 * NSA_HEADS

LANES = 128
SUBLANES = 8
VMEM_LIMIT_BYTES = 56 * 1024 * 1024

COL_Z = 0
COL_XBC = COL_Z + MIX_W
COL_SC = COL_XBC + SSD_XBC
COL_SG = COL_SC + SC_IN
COL_Q = COL_SG + SG_IN
COL_KV = COL_Q + NSA_Q_W
COL_MISC = COL_KV + 6 * NSA_KV_W
MISC_GATE0 = SSD_HEADS
GATE_ROWS = 32
SLC_SHIFT = 6
SLC_TILE = 512
ATTN_TILES = 2
AUG_POS = NSA_HEAD_DIM
AUG_SEL = AUG_POS + SUBLANES
MASK_SCORE = -(2.0 ** 100)
U_TILE_N = 512
U_WIDTH = ((COL_MISC + LANES + U_TILE_N - 1) // U_TILE_N) * U_TILE_N


def _cparams(*sem):
    return pltpu.CompilerParams(dimension_semantics=sem, vmem_limit_bytes=VMEM_LIMIT_BYTES)


def _bdot(a, b):
    return jnp.dot(a.astype(BF16), b.astype(BF16), preferred_element_type=F32)


def _bdot_nt(a, b):
    return lax.dot_general(a.astype(BF16), b.astype(BF16), (((1,), (1,)), ((), ())),
                           preferred_element_type=F32)


def _split3(a):
    hi = a.astype(BF16)
    r1 = a - hi.astype(F32)
    mid = r1.astype(BF16)
    lo = (r1 - mid.astype(F32)).astype(BF16)
    return hi, mid, lo


def _dot_sel_rhs(a, sel):
    hi, mid, lo = _split3(a)
    return (jnp.dot(hi, sel, preferred_element_type=F32) + jnp.dot(mid, sel, preferred_element_type=F32)
            + jnp.dot(lo, sel, preferred_element_type=F32))


def _dot_sel_lhs(sel, a):
    hi, mid, lo = _split3(a)
    return (jnp.dot(sel, hi, preferred_element_type=F32) + jnp.dot(sel, mid, preferred_element_type=F32)
            + jnp.dot(sel, lo, preferred_element_type=F32))


def _sigmoid(x):
    return 1.0 / (1.0 + jnp.exp(-x))


def _silu(x):
    return x * _sigmoid(x)


def _gelu_tanh(x):
    c = math.sqrt(2.0 / math.pi)
    return 0.5 * x * (1.0 + jnp.tanh(c * (x + 0.044715 * (x * x * x))))


def _softplus(x):
    return jnp.maximum(x, 0.0) + jnp.log1p(jnp.exp(-jnp.abs(x)))


def _shift_rows(x, tail, k, row8):
    sh = pltpu.roll(x, k, 0)
    tl = pltpu.roll(tail, k, 0)
    top = jnp.where(row8 < k, tl, sh[0:SUBLANES])
    return jnp.concatenate([top, sh[SUBLANES:]], axis=0)


def _mod_kernel(c_ref, w_ref, b_ref, o_ref):
    o_ref[0] = _bdot(_silu(c_ref[...]), w_ref[0]) + b_ref[0]


def _modulation(c, ada_w, ada_b):
    L, D, D6 = ada_w.shape
    B = c.shape[0]
    tn = D6 // 4
    return pl.pallas_call(
        _mod_kernel,
        grid=(L, D6 // tn),
        in_specs=[pl.BlockSpec((B, D), lambda l, n: (0, 0)),
                  pl.BlockSpec((1, D, tn), lambda l, n: (l, 0, n)),
                  pl.BlockSpec((1, 1, tn), lambda l, n: (l, 0, n))],
        out_specs=pl.BlockSpec((1, B, tn), lambda l, n: (l, 0, n)),
        out_shape=jax.ShapeDtypeStruct((L, B, D6), F32),
        compiler_params=_cparams("parallel", "parallel"),
        name="adaln_modulation",
    )(c, ada_w, ada_b.reshape(L, 1, D6))


def _in_kernel(x_ref, mod_ref, g_ref, w_ref, u_ref, h_ref, misc_ref):
    x = x_ref[...]
    y = x * lax.rsqrt(jnp.mean(x * x, axis=-1, keepdims=True) + RMS_EPS) * g_ref[...]
    h = (y * (1.0 + mod_ref[0, 1:2, :]) + mod_ref[0, 0:1, :]).astype(BF16)
    h_ref[...] = h
    for n in range(U_WIDTH // U_TILE_N):
        u = jnp.dot(h, w_ref[:, n * U_TILE_N:(n + 1) * U_TILE_N], preferred_element_type=F32)
        u_ref[:, n * U_TILE_N:(n + 1) * U_TILE_N] = u.astype(BF16)
        if n == COL_MISC // U_TILE_N:
            misc_ref[...] = u[:, COL_MISC % U_TILE_N:COL_MISC % U_TILE_N + LANES]


def _in_proj(x2d, mod_l, norm_g, w_cat, T):
    M, D = x2d.shape
    tm = min(512, T)
    per_b = T // tm
    return pl.pallas_call(
        _in_kernel,
        grid=(M // tm,),
        in_specs=[pl.BlockSpec((tm, D), lambda m: (m, 0)),
                  pl.BlockSpec((1, 6, D), lambda m: (m // per_b, 0, 0)),
                  pl.BlockSpec((1, D), lambda m: (0, 0)),
                  pl.BlockSpec((D, U_WIDTH), lambda m: (0, 0), pipeline_mode=pl.Buffered(1))],
        out_specs=[pl.BlockSpec((tm, U_WIDTH), lambda m: (m, 0)),
                   pl.BlockSpec((tm, D), lambda m: (m, 0)),
                   pl.BlockSpec((tm, LANES), lambda m: (m, 0))],
        out_shape=[jax.ShapeDtypeStruct((M, U_WIDTH), BF16), jax.ShapeDtypeStruct((M, D), BF16),
                   jax.ShapeDtypeStruct((M, LANES), F32)],
        compiler_params=_cparams("parallel"),
        name="norm_in_proj",
    )(x2d, mod_l, norm_g.reshape(1, D), w_cat)


def _ssd_kernel(z_ref, xa_ref, xb_ref, misc_ref, cw_ref, cb_ref, dtb_ref, alog_ref, dsk_ref, ng_ref,
                o_ref, tail_ref, st_ref):
    Q, P, N, H, G = SSD_CHUNK, SSD_HEAD_DIM, SSD_STATE, SSD_HEADS, SSD_GROUPS
    R = H // G

    @pl.when(pl.program_id(1) == 0)
    def _():
        tail_ref[...] = jnp.zeros_like(tail_ref)
        st_ref[...] = jnp.zeros_like(st_ref)

    xin = jnp.concatenate([xa_ref[...], xb_ref[...]], axis=1).astype(F32)
    tail = tail_ref[...]
    row8 = lax.broadcasted_iota(jnp.int32, (SUBLANES, SSD_XBC), 0)
    acc = xin * cw_ref[SSD_CONV - 1:SSD_CONV, :] + cb_ref[...]
    for k in range(1, SSD_CONV):
        acc = acc + _shift_rows(xin, tail, k, row8) * cw_ref[SSD_CONV - 1 - k:SSD_CONV - k, :]
    tail_ref[...] = xin[Q - SUBLANES:Q, :]
    xbc = _silu(acc)
    xs = xbc[:, :MIX_W]
    bm = xbc[:, MIX_W:MIX_W + G * N]
    cm = xbc[:, MIX_W + G * N:]

    lane = lax.broadcasted_iota(jnp.int32, (Q, LANES), 1)
    rowi = lax.broadcasted_iota(jnp.int32, (Q, LANES), 0)
    is_head = lane < H
    dt = jnp.where(is_head, _softplus(misc_ref[...] + dtb_ref[...]), 0.0)
    a = dt * (-jnp.exp(alog_ref[...]))
    tri = (lane <= rowi).astype(BF16)
    a_cs = _dot_sel_lhs(tri, a)
    a_cs_t = _dot_sel_rhs(a.T, (rowi <= lane).astype(BF16))
    a_last = a_cs[Q - 1:Q, :]
    ea = jnp.exp(a_cs)
    dec = jnp.exp(a_last - a_cs)
    e_row = lax.broadcasted_iota(jnp.int32, (LANES, MIX_W), 0)
    e_col = lax.broadcasted_iota(jnp.int32, (LANES, MIX_W), 1)
    expand = (jnp.right_shift(e_col, 6) == e_row).astype(BF16)
    xdt = xs * _dot_sel_rhs(dt, expand)
    ea_e = _dot_sel_rhs(ea, expand)
    xdec = xdt * _dot_sel_rhs(dec, expand)
    causal = lane <= rowi

    ys = []
    for g in range(G):
        bg = bm[:, g * N:(g + 1) * N]
        cg = cm[:, g * N:(g + 1) * N].astype(BF16)
        cb = _bdot_nt(cg, bg)
        bg_t = bg.T.astype(BF16)
        for r in range(R):
            h = g * R + r
            seg = jnp.where(causal, a_cs[:, h:h + 1] - a_cs_t[h:h + 1, :], NEG_INF)
            y_diag = _bdot(cb * jnp.exp(seg), xdt[:, h * P:(h + 1) * P])
            state = st_ref[h]
            y_off = _bdot(cg, state) * ea_e[:, h * P:(h + 1) * P]
            st_ref[h] = state * jnp.exp(a_last[:, h:h + 1]) + _bdot(bg_t, xdec[:, h * P:(h + 1) * P])
            ys.append(y_diag + y_off)
    y = jnp.concatenate(ys, axis=1) + xs * dsk_ref[...]
    y = y * _silu(z_ref[...].astype(F32))
    gw = MIX_W // G
    outs = []
    for g in range(G):
        yg = y[:, g * gw:(g + 1) * gw]
        outs.append(yg * lax.rsqrt(jnp.mean(yg * yg, axis=-1, keepdims=True) + RMS_EPS))
    o_ref[...] = (jnp.concatenate(outs, axis=1) * ng_ref[...]).astype(BF16)


def _ssd_mixer(u, misc, B, T, conv_w, conv_b, dt_bias, a_log, d_skip, norm_g):
    nc = T // SSD_CHUNK
    Q = SSD_CHUNK

    def pad_lane(v):
        return jnp.pad(v, (0, LANES - v.shape[0])).reshape(1, LANES)

    row = lambda b, c: b * nc + c
    full = lambda shape: pl.BlockSpec(shape, lambda b, c: (0,) * len(shape))
    return pl.pallas_call(
        _ssd_kernel,
        grid=(B, nc),
        in_specs=[pl.BlockSpec((Q, MIX_W), lambda b, c: (row(b, c), COL_Z // MIX_W)),
                  pl.BlockSpec((Q, MIX_W), lambda b, c: (row(b, c), COL_XBC // MIX_W)),
                  pl.BlockSpec((Q, MIX_W), lambda b, c: (row(b, c), COL_XBC // MIX_W + 1)),
                  pl.BlockSpec((Q, LANES), lambda b, c: (row(b, c), 0)),
                  full((SSD_CONV, SSD_XBC)), full((1, SSD_XBC)), full((1, LANES)), full((1, LANES)),
                  full((1, MIX_W)), full((1, MIX_W))],
        out_specs=pl.BlockSpec((Q, MIX_W), lambda b, c: (row(b, c), 0)),
        out_shape=jax.ShapeDtypeStruct((B * T, MIX_W), BF16),
        scratch_shapes=[pltpu.VMEM((SUBLANES, SSD_XBC), F32),
                        pltpu.VMEM((SSD_HEADS, SSD_STATE, SSD_HEAD_DIM), F32)],
        compiler_params=_cparams("parallel", "arbitrary"),
        name="ssd_mixer",
    )(u, u, u, misc, conv_w, conv_b.reshape(1, SSD_XBC), pad_lane(dt_bias), pad_lane(a_log),
      jnp.repeat(d_skip, SSD_HEAD_DIM).reshape(1, MIX_W), norm_g.reshape(1, MIX_W))


def _sc_kernel(b_ref, c_ref, h_ref, w_ref, o_ref, tail_ref):
    @pl.when(pl.program_id(1) == 0)
    def _():
        tail_ref[...] = jnp.zeros_like(tail_ref)

    cx = c_ref[...].astype(F32) * h_ref[...].astype(F32)
    tt = cx.shape[0]
    tail = tail_ref[...]
    row8 = lax.broadcasted_iota(jnp.int32, (SUBLANES, MIX_W), 0)
    acc = cx * w_ref[SC_CONV - 1:SC_CONV, :]
    for k in range(1, SC_CONV):
        acc = acc + _shift_rows(cx, tail, k, row8) * w_ref[SC_CONV - 1 - k:SC_CONV - k, :]
    tail_ref[...] = cx[tt - SUBLANES:tt, :]
    o_ref[...] = (b_ref[...].astype(F32) * acc).astype(BF16)


def _short_conv_mixer(u, B, T, conv_w):
    tt = min(512, T)
    nt = T // tt
    c0 = COL_SC // MIX_W
    spec = lambda j: pl.BlockSpec((tt, MIX_W), lambda b, i: (b * nt + i, c0 + j))
    return pl.pallas_call(
        _sc_kernel,
        grid=(B, nt),
        in_specs=[spec(0), spec(1), spec(2), pl.BlockSpec((SC_CONV, MIX_W), lambda b, i: (0, 0))],
        out_specs=pl.BlockSpec((tt, MIX_W), lambda b, i: (b * nt + i, 0)),
        out_shape=jax.ShapeDtypeStruct((B * T, MIX_W), BF16),
        scratch_shapes=[pltpu.VMEM((SUBLANES, MIX_W), F32)],
        compiler_params=_cparams("parallel", "arbitrary"),
        name="short_conv_mixer",
    )(u, u, u, conv_w)


def _sg_kernel(u_ref, v_ref, ng_ref, w_ref, bias_ref, o_ref):
    Q = SG_CHUNK
    rowi = lax.broadcasted_iota(jnp.int32, (Q, Q), 0)
    coli = lax.broadcasted_iota(jnp.int32, (Q, Q), 1)
    gd = MIX_W // SG_GROUPS
    ws = [jnp.where(coli <= rowi, w_ref[g], 0.0).astype(BF16) for g in range(SG_GROUPS)]
    for c in range(u_ref.shape[0] // Q):
        rows = slice(c * Q, (c + 1) * Q)
        v = _gelu_tanh(v_ref[rows, :].astype(F32))
        v = (v * lax.rsqrt(jnp.mean(v * v, axis=-1, keepdims=True) + RMS_EPS) * ng_ref[...]).astype(BF16)
        mixed = jnp.concatenate([jnp.dot(ws[g], v[:, g * gd:(g + 1) * gd], preferred_element_type=F32)
                                 for g in range(SG_GROUPS)], axis=1)
        o_ref[rows, :] = (_gelu_tanh(u_ref[rows, :].astype(F32)) * (mixed + bias_ref[...])).astype(BF16)


def _spatial_gating_mixer(u, B, T, norm_g, w_s, b_s):
    Q = SG_CHUNK
    tt = min(4 * Q, T)
    nt = T // tt
    c0 = COL_SG // MIX_W
    bias = jnp.repeat(b_s.T, MIX_W // SG_GROUPS, axis=1)
    return pl.pallas_call(
        _sg_kernel,
        grid=(B, nt),
        in_specs=[pl.BlockSpec((tt, MIX_W), lambda b, c: (b * nt + c, c0)),
                  pl.BlockSpec((tt, MIX_W), lambda b, c: (b * nt + c, c0 + 1)),
                  pl.BlockSpec((1, MIX_W), lambda b, c: (0, 0)),
                  pl.BlockSpec((SG_GROUPS, Q, Q), lambda b, c: (0, 0, 0)),
                  pl.BlockSpec((Q, MIX_W), lambda b, c: (0, 0))],
        out_specs=pl.BlockSpec((tt, MIX_W), lambda b, c: (b * nt + c, 0)),
        out_shape=jax.ShapeDtypeStruct((B * T, MIX_W), BF16),
        compiler_params=_cparams("parallel", "parallel"),
        name="spatial_gating_mixer",
    )(u, u, norm_g.reshape(1, MIX_W), w_s, bias)


def _group_mean_sq(x, width):
    n = x.shape[1]
    r = lax.broadcasted_iota(jnp.int32, (n, n), 0)
    c = lax.broadcasted_iota(jnp.int32, (n, n), 1)
    sh = width.bit_length() - 1
    same = (jnp.right_shift(r, sh) == jnp.right_shift(c, sh)).astype(BF16)
    return _dot_sel_rhs(x * x, same) * (1.0 / width)


def _key_aug(lane, pos):
    return jnp.where((lane == AUG_POS) | (lane == AUG_POS + 1), 1.0,
                     jnp.where(lane == AUG_POS + 2, -(pos & ~(LANES - 1)).astype(F32),
                               jnp.where(lane == AUG_POS + 3, -(pos & (LANES - 1)).astype(F32), 0.0)))


def _nsa_prep_kernel(q_ref, kc_ref, vc_ref, ks_ref, vs_ref, kw_ref, vw_ref, misc_ref, qg_ref, ksg_ref, kwg_ref,
                     qa_ref, kvc_ref, ksa_ref, vst_ref, kwa_ref, vwt_ref, gate_ref):
    dh = NSA_HEAD_DIM
    tt = q_ref.shape[0]
    lane = lax.broadcasted_iota(jnp.int32, (tt, LANES), 1)
    pos = pl.program_id(1) * tt + lax.broadcasted_iota(jnp.int32, (tt, LANES), 0)
    is_feat = lane < dh

    q = q_ref[...].astype(F32)
    qn = q * lax.rsqrt(_group_mean_sq(q, dh) + RMS_EPS) * qg_ref[...] * (dh ** -0.5)
    q_pos = jnp.where(lane == AUG_POS, (pos & ~(LANES - 1)).astype(F32),
                      jnp.where(lane == AUG_POS + 1, (pos & (LANES - 1)).astype(F32),
                                jnp.where((lane == AUG_POS + 2) | (lane == AUG_POS + 3), 1.0, 0.0)))
    for h in range(NSA_HEADS):
        pair = qn[:, (h // 2) * LANES:(h // 2 + 1) * LANES]
        feat = pair if h % 2 == 0 else pltpu.roll(pair, dh, 1)
        qa_ref[0, h] = jnp.where(is_feat, feat, -(2.0 ** -(h + 1)) * q_pos).astype(BF16)

    kvc_ref[...] = jnp.concatenate([kc_ref[...], vc_ref[...]], axis=1)
    ks = ks_ref[...].astype(F32)
    ksn = ks * lax.rsqrt(_group_mean_sq(ks, dh) + RMS_EPS) * ksg_ref[...]
    kw = kw_ref[...].astype(F32)
    kwn = kw * lax.rsqrt(_group_mean_sq(kw, dh) + RMS_EPS) * kwg_ref[...]
    k_pos = _key_aug(lane, pos)
    k_pos_sel = jnp.where(lane == AUG_SEL + jnp.right_shift(pos, SLC_SHIFT), MASK_SCORE, k_pos)
    vs_t = vs_ref[...].astype(F32).T.astype(BF16)
    vw_t = vw_ref[...].astype(F32).T.astype(BF16)
    for g in range(NSA_KV_HEADS):
        sl = slice(g * dh, (g + 1) * dh)
        ksa_ref[0, g] = jnp.where(is_feat, ksn if g == 0 else pltpu.roll(ksn, dh, 1), k_pos_sel).astype(BF16)
        kwa_ref[0, g] = jnp.where(is_feat, kwn if g == 0 else pltpu.roll(kwn, dh, 1), k_pos).astype(BF16)
        vst_ref[0, g] = vs_t[sl, :]
        vwt_ref[0, g] = vw_t[sl, :]
    gate_ref[0] = _sigmoid(misc_ref[...]).T[0:GATE_ROWS, :]


def _nsa_prep(u, misc, B, T, q_norm_g, k_norm_g):
    tt = min(512, T)
    nt = T // tt
    G, H, dh = NSA_KV_HEADS, NSA_HEADS, NSA_HEAD_DIM
    kv0 = COL_KV // LANES
    kvspec = lambda j: pl.BlockSpec((tt, LANES), lambda b, i: (b * nt + i, kv0 + j))
    vec = lambda n: pl.BlockSpec((1, n), lambda b, i: (0, 0))
    kspec = pl.BlockSpec((1, G, tt, LANES), lambda b, i: (b, 0, i, 0))
    kshape = jax.ShapeDtypeStruct((B, G, T, LANES), BF16)
    vspec = pl.BlockSpec((1, G, dh, tt), lambda b, i: (b, 0, 0, i))
    vshape = jax.ShapeDtypeStruct((B, G, dh, T), BF16)
    return pl.pallas_call(
        _nsa_prep_kernel,
        grid=(B, nt),
        in_specs=[pl.BlockSpec((tt, NSA_Q_W), lambda b, i: (b * nt + i, COL_Q // NSA_Q_W)),
                  kvspec(0), kvspec(1), kvspec(2), kvspec(3), kvspec(4), kvspec(5),
                  pl.BlockSpec((tt, LANES), lambda b, i: (b * nt + i, 0)),
                  vec(NSA_Q_W), vec(LANES), vec(LANES)],
        out_specs=[pl.BlockSpec((1, H, tt, LANES), lambda b, i: (b, 0, i, 0)),
                   pl.BlockSpec((tt, 2 * LANES), lambda b, i: (b * nt + i, 0)),
                   kspec, vspec, kspec, vspec,
                   pl.BlockSpec((1, GATE_ROWS, tt), lambda b, i: (b, 0, i))],
        out_shape=[jax.ShapeDtypeStruct((B, H, T, LANES), BF16),
                   jax.ShapeDtypeStruct((B * T, 2 * LANES), BF16),
                   kshape, vshape, kshape, vshape,
                   jax.ShapeDtypeStruct((B, GATE_ROWS, T), F32)],
        compiler_params=_cparams("parallel", "parallel"),
        name="nsa_prep",
    )(u, u, u, u, u, u, u, misc, jnp.tile(q_norm_g, NSA_HEADS).reshape(1, NSA_Q_W),
      jnp.tile(k_norm_g[1], G).reshape(1, LANES), jnp.tile(k_norm_g[2], G).reshape(1, LANES))


def _nsa_cmp_kernel(x_ref, w1_ref, pe_ref, w2k_ref, w2vt_ref, kg_ref, kc_ref, vct_ref):
    G = NSA_KV_HEADS
    half = CMP_STRIDE * NSA_HEAD_DIM
    for j in range(2):
        w1 = w1_ref[j]
        pe_term = jnp.dot(pe_ref[j], w1, preferred_element_type=F32)[0:1, :]
        for g in range(G):
            x = x_ref[0, j * G + g]
            lo = jnp.dot(x, w1[:half], preferred_element_type=F32)
            hi = jnp.dot(x, w1[half:], preferred_element_type=F32)
            nseg = x.shape[0]
            y = _gelu_tanh(lo + pltpu.roll(hi, nseg - 1, 0) + pe_term)
            if j == 0:
                r = _bdot(y, w2k_ref[...])
                ms = jnp.sum(r * r, axis=-1, keepdims=True) * (1.0 / NSA_HEAD_DIM)
                lane = lax.broadcasted_iota(jnp.int32, r.shape, 1)
                last = lax.broadcasted_iota(jnp.int32, r.shape, 0) * CMP_STRIDE + (CMP_BLOCK - 1)
                kc_ref[0, g] = (r * lax.rsqrt(ms + RMS_EPS) * kg_ref[...] + _key_aug(lane, last)).astype(BF16)
            else:
                vct_ref[0, g] = _bdot_nt(w2vt_ref[...], y).astype(BF16)


def _nsa_compress(xseg, cmp_pe, cmp_w1, cmp_w2, k_norm_g0):
    B, _, nseg, width = xseg.shape
    G, dh = NSA_KV_HEADS, NSA_HEAD_DIM
    pe = jnp.broadcast_to(cmp_pe.reshape(2, 1, CMP_BLOCK * dh), (2, SUBLANES, CMP_BLOCK * dh)).astype(BF16)
    full = lambda shape: pl.BlockSpec(shape, lambda b: (0,) * len(shape))
    return pl.pallas_call(
        _nsa_cmp_kernel,
        grid=(B,),
        in_specs=[pl.BlockSpec((1, 2 * G, nseg, width), lambda b: (b, 0, 0, 0)),
                  full((2, CMP_BLOCK * dh, dh)), full((2, SUBLANES, CMP_BLOCK * dh)), full((dh, LANES)),
                  full((dh, dh)), full((1, LANES))],
        out_specs=[pl.BlockSpec((1, G, nseg, LANES), lambda b: (b, 0, 0, 0)),
                   pl.BlockSpec((1, G, dh, nseg), lambda b: (b, 0, 0, 0))],
        out_shape=[jax.ShapeDtypeStruct((B, G, nseg, LANES), BF16), jax.ShapeDtypeStruct((B, G, dh, nseg), BF16)],
        compiler_params=_cparams("parallel"),
        name="nsa_compress",
    )(xseg, cmp_w1.astype(BF16), pe, jnp.pad(cmp_w2[0], ((0, 0), (0, LANES - dh))).astype(BF16),
      cmp_w2[1].T.astype(BF16), jnp.pad(k_norm_g0, (0, LANES - dh)).reshape(1, LANES))


def _softmax_keys(s, ok):
    s = jnp.where(ok, s, NEG_INF)
    m = jnp.max(s, axis=0, keepdims=True)
    m = jnp.where(m > NEG_INF, m, 0.0)
    p = jnp.exp(s - m)
    return p * (1.0 / jnp.maximum(jnp.sum(p, axis=0, keepdims=True), 1e-30))


def _exp_keys(s, ok):
    s = jnp.where(ok, s, NEG_INF)
    m = jnp.max(s, axis=0, keepdims=True)
    m = jnp.where(m > NEG_INF, m, 0.0)
    p = jnp.exp(s - m)
    return p, 1.0 / jnp.maximum(jnp.sum(p, axis=0, keepdims=True), 1e-30)


def _nsa_attn_kernel(q_ref, gt_ref, kc_ref, vct_ref, ks_ref, vst_ref, kw_ref, vwt_ref, o_ref, selt_ref, s_ref, *,
                     seq_len):
    G, R, dh = NSA_KV_HEADS, NSA_REP, NSA_HEAD_DIM
    TQ = LANES
    NB = seq_len // SLC_BLOCK
    key_r = lax.broadcasted_iota(jnp.int32, (TQ, TQ), 0)
    q_l = lax.broadcasted_iota(jnp.int32, (TQ, TQ), 1)
    ok_d = q_l >= key_r
    jb = lax.broadcasted_iota(jnp.int32, (NB, TQ), 0)
    row8 = lax.broadcasted_iota(jnp.int32, (SUBLANES, TQ), 0)
    nl = lax.broadcasted_iota(jnp.int32, (NB, TQ), 1)
    overlap_t = ((nl * CMP_STRIDE < jb * SLC_BLOCK + SLC_BLOCK)
                 & (nl * CMP_STRIDE + CMP_BLOCK > jb * SLC_BLOCK)).astype(BF16)
    wlen = (WIN // TQ + 1) * TQ
    wkey = lax.broadcasted_iota(jnp.int32, (wlen, TQ), 0)
    wq = lax.broadcasted_iota(jnp.int32, (wlen, TQ), 1)

    def heads(s, ok, fn):
        return [fn(s[:, r * TQ:(r + 1) * TQ], ok) for r in range(R)]

    def gate_rows(sub, g, branch):
        rows = [MISC_GATE0 + 3 * (g * R + r) + branch for r in range(R)]
        return jnp.concatenate([gt_ref[0, c:c + 1, sub * TQ:(sub + 1) * TQ] for c in rows], axis=1)

    def tile_scores(j, g, qa):
        off = pl.multiple_of(j * SLC_TILE, SLC_TILE)
        return _bdot_nt(ks_ref[0, g, pl.ds(off, SLC_TILE), :], qa)

    def front(sub):
        i = pl.program_id(1) * ATTN_TILES + sub
        t0 = i * TQ
        d0 = pl.multiple_of(t0, TQ)
        ok_c = (t0 + q_l) - (key_r * CMP_STRIDE + (CMP_BLOCK - 1)) >= 0
        cur = jnp.right_shift(t0 + nl, SLC_SHIFT)
        future = jb > cur
        forced = (jb == 0) | (jb == cur) | (jb == cur - 1)
        wstart = pl.multiple_of(jnp.maximum(i - WIN // TQ, 0) * TQ, TQ)
        dist_w = (t0 + wq) - (wstart + wkey)
        ok_w = (dist_w >= 0) & (dist_w < WIN)

        qa_l, o_fix_l, s_d_l, init = [], [], [], []
        for g in range(G):
            qs = q_ref[0, g * R:(g + 1) * R, sub * TQ:(sub + 1) * TQ, :].reshape(R * TQ, LANES)

            p_c = heads(_bdot_nt(kc_ref[0, g], qs), ok_c, _softmax_keys)
            o_cmp = jnp.dot(vct_ref[0, g], jnp.concatenate(p_c, axis=1).astype(BF16),
                            preferred_element_type=F32)

            pw = heads(_bdot_nt(kw_ref[0, g, pl.ds(wstart, wlen), :], qs), ok_w, _exp_keys)
            o_win = jnp.dot(vwt_ref[0, g, :, pl.ds(wstart, wlen)],
                            jnp.concatenate([p for p, _ in pw], axis=1).astype(BF16), preferred_element_type=F32)
            inv_w = jnp.concatenate([inv for _, inv in pw], axis=1)
            o_fix_l.append(gate_rows(sub, g, 0) * o_cmp + (gate_rows(sub, g, 2) * inv_w) * o_win)

            importance = _dot_sel_lhs(overlap_t, p_c[0] + p_c[1] + p_c[2] + p_c[3])
            score = jnp.where(future, NEG_INF, jnp.where(forced, FORCE_SCORE, importance))
            groups = [score[v * SUBLANES:(v + 1) * SUBLANES] for v in range(NB // SUBLANES)]
            ranks = [jnp.zeros((SUBLANES, TQ), F32) for _ in groups]
            for k in range(NB):
                sk = score[k:k + 1, :]
                for v, sv in enumerate(groups):
                    if v < k // SUBLANES:
                        ahead = sk > sv
                    elif v > k // SUBLANES:
                        ahead = sk >= sv
                    else:
                        ahead = (sk > sv) | ((sk == sv) & (row8 > k % SUBLANES))
                    ranks[v] = ranks[v] + jnp.where(ahead, 1.0, 0.0)
            rank = jnp.concatenate(ranks, axis=0)
            sel_t = (rank < SLC_TOPN) & (score > NEG_INF)
            selt_ref[sub, g] = jnp.where(sel_t, 1.0, 0.0)

            sel_d = jnp.where(key_r < SLC_BLOCK, selt_ref[sub, g, pl.ds(2 * i, 1), :],
                              selt_ref[sub, g, pl.ds(2 * i + 1, 1), :])
            s_d = jnp.concatenate(heads(_bdot_nt(ks_ref[0, g, pl.ds(d0, TQ), :], qs), (sel_d > 0.5) & ok_d,
                                        lambda s, ok: jnp.where(ok, s, NEG_INF)), axis=1)
            s_d_l.append(s_d)

            masked_t = jnp.where(sel_t & (jb < 2 * i), 0.0, 1.0)
            cols_t = jnp.concatenate([jnp.zeros((AUG_SEL, TQ), F32), masked_t,
                                      jnp.zeros((LANES - AUG_SEL - NB, TQ), F32)], axis=0)
            cols = cols_t.T.astype(BF16)
            qa_l.append(jnp.concatenate([qs[r * TQ:(r + 1) * TQ] + cols for r in range(R)], axis=0))

        for g in range(G):
            s_ref[sub, g] = tile_scores(0, g, qa_l[g])
            m = jnp.max(s_d_l[g], axis=0, keepdims=True)
            p = jnp.exp(s_d_l[g] - m)
            init.append((m, jnp.sum(p, axis=0, keepdims=True),
                         jnp.dot(vst_ref[0, g, :, pl.ds(d0, TQ)], p.astype(BF16), preferred_element_type=F32)))
        return (t0 + SLC_TILE - 1) // SLC_TILE, qa_l, o_fix_l, tuple(init)

    def key_loop(sub, n_tiles, qa_l, init):
        def slc_step(j, carry):
            off = pl.multiple_of(j * SLC_TILE, SLC_TILE)
            nxt = jnp.minimum(j + 1, n_tiles - 1)
            out = []
            for g in range(G):
                m, l, acc = carry[g]
                s = s_ref[sub, g]
                s_ref[sub, g] = tile_scores(nxt, g, qa_l[g])
                m_new = jnp.maximum(m, jnp.max(s, axis=0, keepdims=True))
                alpha = jnp.exp(m - m_new)
                p = jnp.exp(s - m_new)
                l = alpha * l + jnp.sum(p, axis=0, keepdims=True)
                acc = alpha * acc + jnp.dot(vst_ref[0, g, :, pl.ds(off, SLC_TILE)], p.astype(BF16),
                                            preferred_element_type=F32)
                out.append((m_new, l, acc))
            return tuple(out)

        return lax.fori_loop(0, n_tiles, slc_step, init)

    def back(sub, o_fix_l, slc):
        for g in range(G):
            _, l, acc = slc[g]
            o = o_fix_l[g] + (gate_rows(sub, g, 1) * (1.0 / jnp.maximum(l, 1e-30))) * acc
            for a in range(R // 2):
                pair = jnp.concatenate([o[:, (2 * a) * TQ:(2 * a + 1) * TQ],
                                        o[:, (2 * a + 1) * TQ:(2 * a + 2) * TQ]], axis=0)
                lo = (g * R + 2 * a) * dh
                o_ref[sub * TQ:(sub + 1) * TQ, lo:lo + 2 * dh] = pair.T.astype(BF16)

    fronts = [front(sub) for sub in range(ATTN_TILES)]
    loops = [key_loop(sub, n_tiles, qa_l, init) for sub, (n_tiles, qa_l, _, init) in enumerate(fronts)]
    for sub in range(ATTN_TILES):
        back(sub, fronts[sub][2], loops[sub])


def _nsa_attention(qn, gates_t, kc, vct, ksn, vst, kwn, vwt, B, T):
    G, H, dh = NSA_KV_HEADS, NSA_HEADS, NSA_HEAD_DIM
    TQ = ATTN_TILES * LANES
    nq = T // TQ
    nseg = kc.shape[2]
    per_b = lambda shape: pl.BlockSpec((1,) + shape, lambda b, i: (b, 0, 0, 0))
    return pl.pallas_call(
        functools.partial(_nsa_attn_kernel, seq_len=T),
        grid=(B, nq),
        in_specs=[pl.BlockSpec((1, H, TQ, LANES), lambda b, i: (b, 0, i, 0)),
                  pl.BlockSpec((1, GATE_ROWS, TQ), lambda b, i: (b, 0, i)),
                  per_b((G, nseg, LANES)), per_b((G, dh, nseg)),
                  per_b((G, T, LANES)), per_b((G, dh, T)), per_b((G, T, LANES)), per_b((G, dh, T))],
        out_specs=pl.BlockSpec((TQ, NSA_Q_W), lambda b, i: (b * nq + i, 0)),
        out_shape=jax.ShapeDtypeStruct((B * T, NSA_Q_W), BF16),
        scratch_shapes=[pltpu.VMEM((ATTN_TILES, G, T // SLC_BLOCK, LANES), F32),
                        pltpu.VMEM((ATTN_TILES, G, SLC_TILE, NSA_REP * LANES), F32)],
        compiler_params=_cparams("parallel", "arbitrary"),
        name="nsa_attention",
    )(qn, gates_t, kc, vct, ksn, vst, kwn, vwt)


def _nsa_mixer(u, misc, B, T, q_norm_g, k_norm_g, cmp_pe, cmp_w1, cmp_w2):
    G, dh = NSA_KV_HEADS, NSA_HEAD_DIM
    assert T % LANES == 0 and T >= (WIN // LANES + 1) * LANES and T // CMP_STRIDE == LANES
    assert SLC_BLOCK == 1 << SLC_SHIFT and LANES == 2 * SLC_BLOCK
    qn, kvc, ksn, vst, kwn, vwt, gates_t = _nsa_prep(u, misc, B, T, q_norm_g, k_norm_g)
    nseg = T // CMP_STRIDE
    xseg = kvc.reshape(B, nseg, CMP_STRIDE, 2 * G, dh).transpose(0, 3, 1, 2, 4).reshape(
        B, 2 * G, nseg, CMP_STRIDE * dh)
    kc, vct = _nsa_compress(xseg, cmp_pe, cmp_w1, cmp_w2, k_norm_g[0])
    return _nsa_attention(qn, gates_t, kc, vct, ksn, vst, kwn, vwt, B, T)


def _merge_kernel(x_ref, h_ref, oa_ref, ob_ref, oc_ref, od_ref, mod_ref, wg_ref, wb_ref, wo_ref, y_ref):
    h = h_ref[...]
    merged = None
    for i, o_ref in enumerate((oa_ref, ob_ref, oc_ref, od_ref)):
        gate = _sigmoid(jnp.dot(h, wg_ref[i], preferred_element_type=F32))
        term = gate * jnp.dot(o_ref[...], wb_ref[i], preferred_element_type=F32)
        merged = term if merged is None else merged + term
    y_ref[...] = x_ref[...] + mod_ref[0, 2:3, :] * _bdot(merged, wo_ref[...])


def _merge(x2d, h, outs, mod_l, wg, wb, wo, T):
    M, D = x2d.shape
    tm = min(512, T)
    per_b = T // tm
    row = lambda w: pl.BlockSpec((tm, w), lambda m: (m, 0))
    const = lambda shape: pl.BlockSpec(shape, lambda m: (0,) * len(shape), pipeline_mode=pl.Buffered(1))
    return pl.pallas_call(
        _merge_kernel,
        grid=(M // tm,),
        in_specs=[row(D), row(D), row(MIX_W), row(MIX_W), row(MIX_W), row(MIX_W),
                  pl.BlockSpec((1, 6, D), lambda m: (m // per_b, 0, 0)),
                  const((N_BRANCH, D, D)), const((N_BRANCH, MIX_W, D)), const((D, D))],
        out_specs=row(D),
        out_shape=jax.ShapeDtypeStruct((M, D), F32),
        compiler_params=_cparams("parallel"),
        name="gated_merge_out_proj",
    )(x2d, h, *outs, mod_l, wg, wb, wo)


FFN_CHUNK = 256


def _ffn_kernel(x_ref, mod_ref, g_ref, wa_ref, wb_ref, wo_ref, y_ref, acc_ref):
    x = x_ref[...]
    y = x * lax.rsqrt(jnp.mean(x * x, axis=-1, keepdims=True) + RMS_EPS) * g_ref[...]
    h = (y * (1.0 + mod_ref[0, 4:5, :]) + mod_ref[0, 3:4, :]).astype(BF16)
    d_ff = wo_ref.shape[0]
    for c in range(d_ff // FFN_CHUNK):
        cols = slice(c * FFN_CHUNK, (c + 1) * FFN_CHUNK)
        a = jnp.dot(h, wa_ref[:, cols], preferred_element_type=F32)
        b = jnp.dot(h, wb_ref[:, cols], preferred_element_type=F32)
        part = _bdot(_silu(a) * b, wo_ref[cols, :])
        if c == 0:
            acc_ref[...] = part
        else:
            acc_ref[...] += part
    y_ref[...] = x + mod_ref[0, 5:6, :] * acc_ref[...]


def _ffn(x2d, mod_l, norm_g, w_in, w_out, T):
    M, D = x2d.shape
    d_ff = w_out.shape[0]
    assert d_ff % FFN_CHUNK == 0
    tm = min(1024, T)
    per_b = T // tm
    const = lambda shape, idx: pl.BlockSpec(shape, lambda m: idx, pipeline_mode=pl.Buffered(1))
    return pl.pallas_call(
        _ffn_kernel,
        grid=(M // tm,),
        in_specs=[pl.BlockSpec((tm, D), lambda m: (m, 0)),
                  pl.BlockSpec((1, 6, D), lambda m: (m // per_b, 0, 0)),
                  pl.BlockSpec((1, D), lambda m: (0, 0)),
                  const((D, d_ff), (0, 0)), const((D, d_ff), (0, 1)), const((d_ff, D), (0, 0))],
        out_specs=pl.BlockSpec((tm, D), lambda m: (m, 0)),
        out_shape=jax.ShapeDtypeStruct((M, D), F32),
        scratch_shapes=[pltpu.VMEM((tm, D), F32)],
        compiler_params=_cparams("parallel"),
        name="swiglu_ffn",
    )(x2d, mod_l, norm_g.reshape(1, D), w_in, w_in, w_out)


def _pack_w_in(w_in):
    D = w_in.shape[0]
    o1 = SSD_IN
    o2 = o1 + SC_IN
    o3 = o2 + SG_IN
    q0 = o3
    kv0 = q0 + NSA_Q_W
    gt0 = kv0 + 6 * NSA_KV_W
    parts = [w_in[:, :MIX_W + SSD_XBC],
             w_in[:, o1:o3],
             w_in[:, q0:gt0],
             w_in[:, MIX_W + SSD_XBC:o1],
             w_in[:, gt0:gt0 + 3 * NSA_HEADS]]
    w = jnp.concatenate(parts, axis=1)
    return jnp.pad(w, ((0, 0), (0, U_WIDTH - w.shape[1]))).astype(BF16)


def kernel(x, c, ada_w, ada_b, norm_mix_g, norm_ffn_g, w_in, ssd_conv_w, ssd_conv_b, ssd_dt_bias, ssd_a_log, ssd_d,
           ssd_norm_g, sc_conv_w, sg_norm_g, sg_w, sg_b, nsa_q_norm_g, nsa_k_norm_g, nsa_cmp_pe, nsa_cmp_w1,
           nsa_cmp_w2, w_branch, w_branch_gate, w_out, w_ffn_in, w_ffn_out):
    B, T, D = x.shape
    L = w_in.shape[0]
    mod = _modulation(c, ada_w, ada_b).reshape(L, B, 6, D)
    x2d = x.reshape(B * T, D)
    for l in range(L):
        u, h, misc = _in_proj(x2d, mod[l], norm_mix_g[l], _pack_w_in(w_in[l]), T)
        outs = (
            _ssd_mixer(u, misc, B, T, ssd_conv_w[l], ssd_conv_b[l], ssd_dt_bias[l], ssd_a_log[l], ssd_d[l],
                       ssd_norm_g[l]),
            _short_conv_mixer(u, B, T, sc_conv_w[l]),
            _spatial_gating_mixer(u, B, T, sg_norm_g[l], sg_w[l], sg_b[l]),
            _nsa_mixer(u, misc, B, T, nsa_q_norm_g[l], nsa_k_norm_g[l], nsa_cmp_pe[l], nsa_cmp_w1[l], nsa_cmp_w2[l]),
        )
        x2d = _merge(x2d, h, outs, mod[l], w_branch_gate[l].astype(BF16), w_branch[l].astype(BF16),
                     w_out[l].astype(BF16), T)
        x2d = _ffn(x2d, mod[l], norm_ffn_g[l], w_ffn_in[l].astype(BF16), w_ffn_out[l].astype(BF16), T)
    return x2d.reshape(B, T, D)
```

```python
import functools
import math

import jax
import jax.numpy as jnp
from jax import lax
from jax.experimental import pallas as pl
from jax.experimental.pallas import tpu as pltpu

F32 = jnp.float32
BF16 = jnp.bfloat16
RMS_EPS = 1e-6
NEG_INF = float("-inf")

MIX_W = 512
N_BRANCH = 4

SSD_HEAD_DIM = 64
SSD_HEADS = 8
SSD_GROUPS = 2
SSD_STATE = 128
SSD_CONV = 4
SSD_CHUNK = 128
SSD_STEP_CHUNKS = 4
SSD_XBC = MIX_W + 2 * SSD_GROUPS * SSD_STATE
SSD_IN = MIX_W + SSD_XBC + SSD_HEADS
SC_CONV = 3
SC_IN = 3 * MIX_W
SG_GROUPS = 4
SG_CHUNK = 128
SG_IN = 2 * MIX_W
NSA_HEADS = 8
NSA_KV_HEADS = 2
NSA_REP = NSA_HEADS // NSA_KV_HEADS
NSA_HEAD_DIM = 64
CMP_BLOCK = 32
CMP_STRIDE = 16
SLC_BLOCK = 64
SLC_TOPN = 8
WIN = 256
FORCE_SCORE = 1e9
NSA_KV_W = NSA_KV_HEADS * NSA_HEAD_DIM
NSA_Q_W = NSA_HEADS * NSA_HEAD_DIM
NSA_IN = NSA_Q_W + 6 * NSA_KV_W + 3 * NSA_HEADS

LANES = 128
SUBLANES = 8
VMEM_LIMIT_BYTES = 56 * 1024 * 1024

COL_Z = 0
COL_XBC = COL_Z + MIX_W
COL_SC = COL_XBC + SSD_XBC
COL_SG = COL_SC + SC_IN
COL_Q = COL_SG + SG_IN
COL_KV = COL_Q + NSA_Q_W
COL_MISC = COL_KV + 6 * NSA_KV_W
MISC_GATE0 = SSD_HEADS
GATE_ROWS = 32
SLC_SHIFT = 6
SLC_TILE = 512
ATTN_TILES = 2
AUG_POS = NSA_HEAD_DIM
AUG_SEL = AUG_POS + SUBLANES
MASK_SCORE = -(2.0 ** 100)
U_TILE_N = 512
U_WIDTH = ((COL_MISC + LANES + U_TILE_N - 1) // U_TILE_N) * U_TILE_N


def _cparams(*sem):
    return pltpu.CompilerParams(dimension_semantics=sem, vmem_limit_bytes=VMEM_LIMIT_BYTES)


def _bdot(a, b):
    return jnp.dot(a.astype(BF16), b.astype(BF16), preferred_element_type=F32)


def _bdot_nt(a, b):
    return lax.dot_general(a.astype(BF16), b.astype(BF16), (((1,), (1,)), ((), ())),
                           preferred_element_type=F32)


def _split3(a):
    hi = a.astype(BF16)
    r1 = a - hi.astype(F32)
    mid = r1.astype(BF16)
    lo = (r1 - mid.astype(F32)).astype(BF16)
    return hi, mid, lo


def _dot_sel_rhs(a, sel):
    hi, mid, lo = _split3(a)
    return (jnp.dot(hi, sel, preferred_element_type=F32) + jnp.dot(mid, sel, preferred_element_type=F32)
            + jnp.dot(lo, sel, preferred_element_type=F32))


def _dot_sel_lhs(sel, a):
    hi, mid, lo = _split3(a)
    return (jnp.dot(sel, hi, preferred_element_type=F32) + jnp.dot(sel, mid, preferred_element_type=F32)
            + jnp.dot(sel, lo, preferred_element_type=F32))


def _sigmoid(x):
    return 1.0 / (1.0 + jnp.exp(-x))


def _silu(x):
    return x * _sigmoid(x)


def _gelu_tanh(x):
    c = math.sqrt(2.0 / math.pi)
    return 0.5 * x * (1.0 + jnp.tanh(c * (x + 0.044715 * (x * x * x))))


def _softplus(x):
    return jnp.maximum(x, 0.0) + jnp.log1p(jnp.exp(-jnp.abs(x)))


def _shift_rows(x, tail, k, row8):
    sh = pltpu.roll(x, k, 0)
    tl = pltpu.roll(tail, k, 0)
    top = jnp.where(row8 < k, tl, sh[0:SUBLANES])
    return jnp.concatenate([top, sh[SUBLANES:]], axis=0)


def _mod_kernel(c_ref, w_ref, b_ref, o_ref):
    o_ref[0] = _bdot(_silu(c_ref[...]), w_ref[0]) + b_ref[0]


def _modulation(c, ada_w, ada_b):
    L, D, D6 = ada_w.shape
    B = c.shape[0]
    tn = D6 // 4
    return pl.pallas_call(
        _mod_kernel,
        grid=(L, D6 // tn),
        in_specs=[pl.BlockSpec((B, D), lambda l, n: (0, 0)),
                  pl.BlockSpec((1, D, tn), lambda l, n: (l, 0, n)),
                  pl.BlockSpec((1, 1, tn), lambda l, n: (l, 0, n))],
        out_specs=pl.BlockSpec((1, B, tn), lambda l, n: (l, 0, n)),
        out_shape=jax.ShapeDtypeStruct((L, B, D6), F32),
        compiler_params=_cparams("parallel", "parallel"),
        name="adaln_modulation",
    )(c, ada_w, ada_b.reshape(L, 1, D6))


def _in_kernel(x_ref, mod_ref, g_ref, w_ref, u_ref, h_ref, misc_ref):
    x = x_ref[...]
    y = x * lax.rsqrt(jnp.mean(x * x, axis=-1, keepdims=True) + RMS_EPS) * g_ref[...]
    h = (y * (1.0 + mod_ref[0, 1:2, :]) + mod_ref[0, 0:1, :]).astype(BF16)
    h_ref[...] = h
    for n in range(U_WIDTH // U_TILE_N):
        u = jnp.dot(h, w_ref[:, n * U_TILE_N:(n + 1) * U_TILE_N], preferred_element_type=F32)
        u_ref[:, n * U_TILE_N:(n + 1) * U_TILE_N] = u.astype(BF16)
        if n == COL_MISC // U_TILE_N:
            misc_ref[...] = u[:, COL_MISC % U_TILE_N:COL_MISC % U_TILE_N + LANES]


def _in_proj(x2d, mod_l, norm_g, w_cat, T):
    M, D = x2d.shape
    tm = min(512, T)
    per_b = T // tm
    return pl.pallas_call(
        _in_kernel,
        grid=(M // tm,),
        in_specs=[pl.BlockSpec((tm, D), lambda m: (m, 0)),
                  pl.BlockSpec((1, 6, D), lambda m: (m // per_b, 0, 0)),
                  pl.BlockSpec((1, D), lambda m: (0, 0)),
                  pl.BlockSpec((D, U_WIDTH), lambda m: (0, 0), pipeline_mode=pl.Buffered(1))],
        out_specs=[pl.BlockSpec((tm, U_WIDTH), lambda m: (m, 0)),
                   pl.BlockSpec((tm, D), lambda m: (m, 0)),
                   pl.BlockSpec((tm, LANES), lambda m: (m, 0))],
        out_shape=[jax.ShapeDtypeStruct((M, U_WIDTH), BF16), jax.ShapeDtypeStruct((M, D), BF16),
                   jax.ShapeDtypeStruct((M, LANES), F32)],
        compiler_params=_cparams("parallel"),
        name="norm_in_proj",
    )(x2d, mod_l, norm_g.reshape(1, D), w_cat)


def _ssd_kernel(z_ref, xa_ref, xb_ref, misc_ref, cw_ref, cb_ref, dtb_ref, alog_ref, dsk_ref, ng_ref,
                o_ref, tail_ref, st_ref):
    Q, P, N, H, G = SSD_CHUNK, SSD_HEAD_DIM, SSD_STATE, SSD_HEADS, SSD_GROUPS
    R = H // G

    @pl.when(pl.program_id(1) == 0)
    def _():
        tail_ref[...] = jnp.zeros_like(tail_ref)
        st_ref[...] = jnp.zeros_like(st_ref)

    xin = jnp.concatenate([xa_ref[...], xb_ref[...]], axis=1).astype(F32)
    tail = tail_ref[...]
    row8 = lax.broadcasted_iota(jnp.int32, (SUBLANES, SSD_XBC), 0)
    acc = xin * cw_ref[SSD_CONV - 1:SSD_CONV, :] + cb_ref[...]
    for k in range(1, SSD_CONV):
        acc = acc + _shift_rows(xin, tail, k, row8) * cw_ref[SSD_CONV - 1 - k:SSD_CONV - k, :]
    rows_in = xin.shape[0]
    tail_ref[...] = xin[rows_in - SUBLANES:rows_in, :]
    xbc = _silu(acc)

    lane = lax.broadcasted_iota(jnp.int32, (Q, LANES), 1)
    rowi = lax.broadcasted_iota(jnp.int32, (Q, LANES), 0)
    is_head = lane < H
    tri = (lane <= rowi).astype(BF16)
    tri_t = (rowi <= lane).astype(BF16)
    e_row = lax.broadcasted_iota(jnp.int32, (LANES, MIX_W), 0)
    e_col = lax.broadcasted_iota(jnp.int32, (LANES, MIX_W), 1)
    expand = (jnp.right_shift(e_col, 6) == e_row).astype(BF16)
    causal = lane <= rowi
    neg_a = -jnp.exp(alog_ref[...])

    for c in range(rows_in // Q):
        rows = slice(c * Q, (c + 1) * Q)
        xs = xbc[rows, :MIX_W]
        bm = xbc[rows, MIX_W:MIX_W + G * N]
        cm = xbc[rows, MIX_W + G * N:]
        dt = jnp.where(is_head, _softplus(misc_ref[rows, :] + dtb_ref[...]), 0.0)
        a = dt * neg_a
        a_cs = _dot_sel_lhs(tri, a)
        a_cs_t = _dot_sel_rhs(a.T, tri_t)
        a_last = a_cs[Q - 1:Q, :]
        ea = jnp.exp(a_cs)
        dec = jnp.exp(a_last - a_cs)
        xdt = xs * _dot_sel_rhs(dt, expand)
        ea_e = _dot_sel_rhs(ea, expand)
        xdec = xdt * _dot_sel_rhs(dec, expand)

        ys = []
        for g in range(G):
            bg = bm[:, g * N:(g + 1) * N]
            cg = cm[:, g * N:(g + 1) * N].astype(BF16)
            cb = _bdot_nt(cg, bg)
            bg_t = bg.T.astype(BF16)
            for r in range(R):
                h = g * R + r
                seg = jnp.where(causal, a_cs[:, h:h + 1] - a_cs_t[h:h + 1, :], NEG_INF)
                y_diag = _bdot(cb * jnp.exp(seg), xdt[:, h * P:(h + 1) * P])
                state = st_ref[h]
                y_off = _bdot(cg, state) * ea_e[:, h * P:(h + 1) * P]
                st_ref[h] = state * jnp.exp(a_last[:, h:h + 1]) + _bdot(bg_t, xdec[:, h * P:(h + 1) * P])
                ys.append(y_diag + y_off)
        y = jnp.concatenate(ys, axis=1) + xs * dsk_ref[...]
        y = y * _silu(z_ref[rows, :].astype(F32))
        gw = MIX_W // G
        outs = []
        for g in range(G):
            yg = y[:, g * gw:(g + 1) * gw]
            outs.append(yg * lax.rsqrt(jnp.mean(yg * yg, axis=-1, keepdims=True) + RMS_EPS))
        o_ref[rows, :] = (jnp.concatenate(outs, axis=1) * ng_ref[...]).astype(BF16)


def _ssd_mixer(u, misc, B, T, conv_w, conv_b, dt_bias, a_log, d_skip, norm_g):
    Q = min(SSD_STEP_CHUNKS * SSD_CHUNK, T)
    nc = T // Q

    def pad_lane(v):
        return jnp.pad(v, (0, LANES - v.shape[0])).reshape(1, LANES)

    row = lambda b, c: b * nc + c
    full = lambda shape: pl.BlockSpec(shape, lambda b, c: (0,) * len(shape))
    return pl.pallas_call(
        _ssd_kernel,
        grid=(B, nc),
        in_specs=[pl.BlockSpec((Q, MIX_W), lambda b, c: (row(b, c), COL_Z // MIX_W)),
                  pl.BlockSpec((Q, MIX_W), lambda b, c: (row(b, c), COL_XBC // MIX_W)),
                  pl.BlockSpec((Q, MIX_W), lambda b, c: (row(b, c), COL_XBC // MIX_W + 1)),
                  pl.BlockSpec((Q, LANES), lambda b, c: (row(b, c), 0)),
                  full((SSD_CONV, SSD_XBC)), full((1, SSD_XBC)), full((1, LANES)), full((1, LANES)),
                  full((1, MIX_W)), full((1, MIX_W))],
        out_specs=pl.BlockSpec((Q, MIX_W), lambda b, c: (row(b, c), 0)),
        out_shape=jax.ShapeDtypeStruct((B * T, MIX_W), BF16),
        scratch_shapes=[pltpu.VMEM((SUBLANES, SSD_XBC), F32),
                        pltpu.VMEM((SSD_HEADS, SSD_STATE, SSD_HEAD_DIM), F32)],
        compiler_params=_cparams("parallel", "arbitrary"),
        name="ssd_mixer",
    )(u, u, u, misc, conv_w, conv_b.reshape(1, SSD_XBC), pad_lane(dt_bias), pad_lane(a_log),
      jnp.repeat(d_skip, SSD_HEAD_DIM).reshape(1, MIX_W), norm_g.reshape(1, MIX_W))


def _sc_kernel(b_ref, c_ref, h_ref, w_ref, o_ref, tail_ref):
    @pl.when(pl.program_id(1) == 0)
    def _():
        tail_ref[...] = jnp.zeros_like(tail_ref)

    cx = c_ref[...].astype(F32) * h_ref[...].astype(F32)
    tt = cx.shape[0]
    tail = tail_ref[...]
    row8 = lax.broadcasted_iota(jnp.int32, (SUBLANES, MIX_W), 0)
    acc = cx * w_ref[SC_CONV - 1:SC_CONV, :]
    for k in range(1, SC_CONV):
        acc = acc + _shift_rows(cx, tail, k, row8) * w_ref[SC_CONV - 1 - k:SC_CONV - k, :]
    tail_ref[...] = cx[tt - SUBLANES:tt, :]
    o_ref[...] = (b_ref[...].astype(F32) * acc).astype(BF16)


def _short_conv_mixer(u, B, T, conv_w):
    tt = min(512, T)
    nt = T // tt
    c0 = COL_SC // MIX_W
    spec = lambda j: pl.BlockSpec((tt, MIX_W), lambda b, i: (b * nt + i, c0 + j))
    return pl.pallas_call(
        _sc_kernel,
        grid=(B, nt),
        in_specs=[spec(0), spec(1), spec(2), pl.BlockSpec((SC_CONV, MIX_W), lambda b, i: (0, 0))],
        out_specs=pl.BlockSpec((tt, MIX_W), lambda b, i: (b * nt + i, 0)),
        out_shape=jax.ShapeDtypeStruct((B * T, MIX_W), BF16),
        scratch_shapes=[pltpu.VMEM((SUBLANES, MIX_W), F32)],
        compiler_params=_cparams("parallel", "arbitrary"),
        name="short_conv_mixer",
    )(u, u, u, conv_w)


def _sg_kernel(u_ref, v_ref, ng_ref, w_ref, bias_ref, o_ref):
    Q = SG_CHUNK
    rowi = lax.broadcasted_iota(jnp.int32, (Q, Q), 0)
    coli = lax.broadcasted_iota(jnp.int32, (Q, Q), 1)
    gd = MIX_W // SG_GROUPS
    ws = [jnp.where(coli <= rowi, w_ref[g], 0.0).astype(BF16) for g in range(SG_GROUPS)]
    for c in range(u_ref.shape[0] // Q):
        rows = slice(c * Q, (c + 1) * Q)
        v = _gelu_tanh(v_ref[rows, :].astype(F32))
        v = (v * lax.rsqrt(jnp.mean(v * v, axis=-1, keepdims=True) + RMS_EPS) * ng_ref[...]).astype(BF16)
        mixed = jnp.concatenate([jnp.dot(ws[g], v[:, g * gd:(g + 1) * gd], preferred_element_type=F32)
                                 for g in range(SG_GROUPS)], axis=1)
        o_ref[rows, :] = (_gelu_tanh(u_ref[rows, :].astype(F32)) * (mixed + bias_ref[...])).astype(BF16)


def _spatial_gating_mixer(u, B, T, norm_g, w_s, b_s):
    Q = SG_CHUNK
    tt = min(4 * Q, T)
    nt = T // tt
    c0 = COL_SG // MIX_W
    bias = jnp.repeat(b_s.T, MIX_W // SG_GROUPS, axis=1)
    return pl.pallas_call(
        _sg_kernel,
        grid=(B, nt),
        in_specs=[pl.BlockSpec((tt, MIX_W), lambda b, c: (b * nt + c, c0)),
                  pl.BlockSpec((tt, MIX_W), lambda b, c: (b * nt + c, c0 + 1)),
                  pl.BlockSpec((1, MIX_W), lambda b, c: (0, 0)),
                  pl.BlockSpec((SG_GROUPS, Q, Q), lambda b, c: (0, 0, 0)),
                  pl.BlockSpec((Q, MIX_W), lambda b, c: (0, 0))],
        out_specs=pl.BlockSpec((tt, MIX_W), lambda b, c: (b * nt + c, 0)),
        out_shape=jax.ShapeDtypeStruct((B * T, MIX_W), BF16),
        compiler_params=_cparams("parallel", "parallel"),
        name="spatial_gating_mixer",
    )(u, u, norm_g.reshape(1, MIX_W), w_s, bias)


def _group_mean_sq(x, width):
    n = x.shape[1]
    r = lax.broadcasted_iota(jnp.int32, (n, n), 0)
    c = lax.broadcasted_iota(jnp.int32, (n, n), 1)
    sh = width.bit_length() - 1
    same = (jnp.right_shift(r, sh) == jnp.right_shift(c, sh)).astype(BF16)
    return _dot_sel_rhs(x * x, same) * (1.0 / width)


def _key_aug(lane, pos):
    return jnp.where((lane == AUG_POS) | (lane == AUG_POS + 1), 1.0,
                     jnp.where(lane == AUG_POS + 2, -(pos & ~(LANES - 1)).astype(F32),
                               jnp.where(lane == AUG_POS + 3, -(pos & (LANES - 1)).astype(F32), 0.0)))


def _nsa_prep_kernel(q_ref, ks_ref, vs_ref, kw_ref, vw_ref, misc_ref, qg_ref, ksg_ref, kwg_ref,
                     qa_ref, ksa_ref, vst_ref, kwa_ref, vwt_ref, gate_ref):
    dh = NSA_HEAD_DIM
    tt = q_ref.shape[0]
    lane = lax.broadcasted_iota(jnp.int32, (tt, LANES), 1)
    pos = pl.program_id(1) * tt + lax.broadcasted_iota(jnp.int32, (tt, LANES), 0)
    is_feat = lane < dh

    q = q_ref[...].astype(F32)
    qn = q * lax.rsqrt(_group_mean_sq(q, dh) + RMS_EPS) * qg_ref[...] * (dh ** -0.5)
    q_pos = jnp.where(lane == AUG_POS, (pos & ~(LANES - 1)).astype(F32),
                      jnp.where(lane == AUG_POS + 1, (pos & (LANES - 1)).astype(F32),
                                jnp.where((lane == AUG_POS + 2) | (lane == AUG_POS + 3), 1.0, 0.0)))
    for h in range(NSA_HEADS):
        pair = qn[:, (h // 2) * LANES:(h // 2 + 1) * LANES]
        feat = pair if h % 2 == 0 else pltpu.roll(pair, dh, 1)
        qa_ref[0, h] = jnp.where(is_feat, feat, -(2.0 ** -(h + 1)) * q_pos).astype(BF16)

    ks = ks_ref[...].astype(F32)
    ksn = ks * lax.rsqrt(_group_mean_sq(ks, dh) + RMS_EPS) * ksg_ref[...]
    kw = kw_ref[...].astype(F32)
    kwn = kw * lax.rsqrt(_group_mean_sq(kw, dh) + RMS_EPS) * kwg_ref[...]
    k_pos = _key_aug(lane, pos)
    k_pos_sel = jnp.where(lane == AUG_SEL + jnp.right_shift(pos, SLC_SHIFT), MASK_SCORE, k_pos)
    vs_t = vs_ref[...].astype(F32).T.astype(BF16)
    vw_t = vw_ref[...].astype(F32).T.astype(BF16)
    for g in range(NSA_KV_HEADS):
        sl = slice(g * dh, (g + 1) * dh)
        ksa_ref[0, g] = jnp.where(is_feat, ksn if g == 0 else pltpu.roll(ksn, dh, 1), k_pos_sel).astype(BF16)
        kwa_ref[0, g] = jnp.where(is_feat, kwn if g == 0 else pltpu.roll(kwn, dh, 1), k_pos).astype(BF16)
        vst_ref[0, g] = vs_t[sl, :]
        vwt_ref[0, g] = vw_t[sl, :]
    gate_ref[0] = _sigmoid(misc_ref[...]).T[0:GATE_ROWS, :]


def _nsa_prep(u, misc, B, T, q_norm_g, k_norm_g):
    tt = min(512, T)
    nt = T // tt
    G, H, dh = NSA_KV_HEADS, NSA_HEADS, NSA_HEAD_DIM
    kv0 = COL_KV // LANES
    kvspec = lambda j: pl.BlockSpec((tt, LANES), lambda b, i: (b * nt + i, kv0 + j))
    vec = lambda n: pl.BlockSpec((1, n), lambda b, i: (0, 0))
    kspec = pl.BlockSpec((1, G, tt, LANES), lambda b, i: (b, 0, i, 0))
    kshape = jax.ShapeDtypeStruct((B, G, T, LANES), BF16)
    vspec = pl.BlockSpec((1, G, dh, tt), lambda b, i: (b, 0, 0, i))
    vshape = jax.ShapeDtypeStruct((B, G, dh, T), BF16)
    return pl.pallas_call(
        _nsa_prep_kernel,
        grid=(B, nt),
        in_specs=[pl.BlockSpec((tt, NSA_Q_W), lambda b, i: (b * nt + i, COL_Q // NSA_Q_W)),
                  kvspec(2), kvspec(3), kvspec(4), kvspec(5),
                  pl.BlockSpec((tt, LANES), lambda b, i: (b * nt + i, 0)),
                  vec(NSA_Q_W), vec(LANES), vec(LANES)],
        out_specs=[pl.BlockSpec((1, H, tt, LANES), lambda b, i: (b, 0, i, 0)),
                   kspec, vspec, kspec, vspec,
                   pl.BlockSpec((1, GATE_ROWS, tt), lambda b, i: (b, 0, i))],
        out_shape=[jax.ShapeDtypeStruct((B, H, T, LANES), BF16),
                   kshape, vshape, kshape, vshape,
                   jax.ShapeDtypeStruct((B, GATE_ROWS, T), F32)],
        compiler_params=_cparams("parallel", "parallel"),
        name="nsa_prep",
    )(u, u, u, u, u, misc, jnp.tile(q_norm_g, NSA_HEADS).reshape(1, NSA_Q_W),
      jnp.tile(k_norm_g[1], G).reshape(1, LANES), jnp.tile(k_norm_g[2], G).reshape(1, LANES))


def _nsa_cmp_kernel(kin_ref, vin_ref, w1_ref, pe_ref, w2k_ref, w2vt_ref, kg_ref, kc_ref, vct_ref, x_ref):
    G = NSA_KV_HEADS
    nseg = kin_ref.shape[0] // CMP_STRIDE
    x_ref[0] = kin_ref[...].astype(F32)
    x_ref[1] = vin_ref[...].astype(F32)
    for j in range(2):
        a = jnp.concatenate([x_ref[j, pl.ds(l, nseg, stride=CMP_STRIDE), :] for l in range(CMP_STRIDE)],
                            axis=1).astype(BF16)
        pe_term = (jnp.dot(pe_ref[j, 0], w1_ref[j, 0], preferred_element_type=F32)
                   + jnp.dot(pe_ref[j, 1], w1_ref[j, 1], preferred_element_type=F32))[0:1, :]
        lo = jnp.dot(a, w1_ref[j, 0], preferred_element_type=F32)
        hi = jnp.dot(a, w1_ref[j, 1], preferred_element_type=F32)
        y = _gelu_tanh(lo + pltpu.roll(hi, nseg - 1, 0) + pe_term)
        for g in range(G):
            if j == 0:
                r = _bdot(y, w2k_ref[g])
                ms = jnp.sum(r * r, axis=-1, keepdims=True) * (1.0 / NSA_HEAD_DIM)
                lane = lax.broadcasted_iota(jnp.int32, r.shape, 1)
                last = lax.broadcasted_iota(jnp.int32, r.shape, 0) * CMP_STRIDE + (CMP_BLOCK - 1)
                kc_ref[0, g] = (r * lax.rsqrt(ms + RMS_EPS) * kg_ref[...] + _key_aug(lane, last)).astype(BF16)
            else:
                vct_ref[0, g] = _bdot_nt(w2vt_ref[g], y).astype(BF16)


def _nsa_compress(u, B, T, cmp_pe, cmp_w1, cmp_w2, k_norm_g0):
    G, dh = NSA_KV_HEADS, NSA_HEAD_DIM
    nseg = T // CMP_STRIDE
    width = CMP_STRIDE * G * dh
    eye = jnp.eye(G, dtype=F32)
    w1 = jnp.einsum('jhlde,gk->jhlgdke', cmp_w1.reshape(2, 2, CMP_STRIDE, dh, dh), eye).reshape(
        2, 2, width, G * dh).astype(BF16)
    pe = jnp.broadcast_to(cmp_pe.reshape(2, 2, CMP_STRIDE, 1, dh), (2, 2, CMP_STRIDE, G, dh)).reshape(2, 2, 1, width)
    pe = jnp.broadcast_to(pe, (2, 2, SUBLANES, width)).astype(BF16)
    w2k = jnp.stack([jnp.zeros((G * dh, LANES), F32).at[g * dh:(g + 1) * dh, :dh].set(cmp_w2[0])
                     for g in range(G)]).astype(BF16)
    w2vt = jnp.stack([jnp.zeros((dh, G * dh), F32).at[:, g * dh:(g + 1) * dh].set(cmp_w2[1].T)
                      for g in range(G)]).astype(BF16)
    full = lambda shape: pl.BlockSpec(shape, lambda b: (0,) * len(shape))
    kv0 = COL_KV // LANES
    return pl.pallas_call(
        _nsa_cmp_kernel,
        grid=(B,),
        in_specs=[pl.BlockSpec((T, LANES), lambda b: (b, kv0)), pl.BlockSpec((T, LANES), lambda b: (b, kv0 + 1)),
                  full((2, 2, width, G * dh)), full((2, 2, SUBLANES, width)), full((G, G * dh, LANES)),
                  full((G, dh, G * dh)), full((1, LANES))],
        out_specs=[pl.BlockSpec((1, G, nseg, LANES), lambda b: (b, 0, 0, 0)),
                   pl.BlockSpec((1, G, dh, nseg), lambda b: (b, 0, 0, 0))],
        out_shape=[jax.ShapeDtypeStruct((B, G, nseg, LANES), BF16), jax.ShapeDtypeStruct((B, G, dh, nseg), BF16)],
        scratch_shapes=[pltpu.VMEM((2, T, LANES), F32)],
        compiler_params=_cparams("parallel"),
        name="nsa_compress",
    )(u, u, w1, pe, w2k, w2vt, jnp.pad(k_norm_g0, (0, LANES - dh)).reshape(1, LANES))


def _softmax_keys(s, ok):
    s = jnp.where(ok, s, NEG_INF)
    m = jnp.max(s, axis=0, keepdims=True)
    m = jnp.where(m > NEG_INF, m, 0.0)
    p = jnp.exp(s - m)
    return p * (1.0 / jnp.maximum(jnp.sum(p, axis=0, keepdims=True), 1e-30))


def _exp_keys(s, ok):
    s = jnp.where(ok, s, NEG_INF)
    m = jnp.max(s, axis=0, keepdims=True)
    m = jnp.where(m > NEG_INF, m, 0.0)
    p = jnp.exp(s - m)
    return p, 1.0 / jnp.maximum(jnp.sum(p, axis=0, keepdims=True), 1e-30)


def _nsa_attn_kernel(q_ref, gt_ref, kc_ref, vct_ref, ks_ref, vst_ref, kw_ref, vwt_ref, o_ref, selt_ref, s_ref, *,
                     seq_len):
    G, R, dh = NSA_KV_HEADS, NSA_REP, NSA_HEAD_DIM
    TQ = LANES
    NB = seq_len // SLC_BLOCK
    key_r = lax.broadcasted_iota(jnp.int32, (TQ, TQ), 0)
    q_l = lax.broadcasted_iota(jnp.int32, (TQ, TQ), 1)
    ok_d = q_l >= key_r
    jb = lax.broadcasted_iota(jnp.int32, (NB, TQ), 0)
    row8 = lax.broadcasted_iota(jnp.int32, (SUBLANES, TQ), 0)
    nl = lax.broadcasted_iota(jnp.int32, (NB, TQ), 1)
    overlap_t = ((nl * CMP_STRIDE < jb * SLC_BLOCK + SLC_BLOCK)
                 & (nl * CMP_STRIDE + CMP_BLOCK > jb * SLC_BLOCK)).astype(BF16)
    wlen = (WIN // TQ + 1) * TQ
    wkey = lax.broadcasted_iota(jnp.int32, (wlen, TQ), 0)
    wq = lax.broadcasted_iota(jnp.int32, (wlen, TQ), 1)

    def heads(s, ok, fn):
        return [fn(s[:, r * TQ:(r + 1) * TQ], ok) for r in range(R)]

    def gate_rows(sub, g, branch):
        rows = [MISC_GATE0 + 3 * (g * R + r) + branch for r in range(R)]
        return jnp.concatenate([gt_ref[0, c:c + 1, sub * TQ:(sub + 1) * TQ] for c in rows], axis=1)

    def tile_scores(j, g, qa):
        off = pl.multiple_of(j * SLC_TILE, SLC_TILE)
        return _bdot_nt(ks_ref[0, g, pl.ds(off, SLC_TILE), :], qa)

    def front(sub):
        i = pl.program_id(1) * ATTN_TILES + sub
        t0 = i * TQ
        d0 = pl.multiple_of(t0, TQ)
        ok_c = (t0 + q_l) - (key_r * CMP_STRIDE + (CMP_BLOCK - 1)) >= 0
        cur = jnp.right_shift(t0 + nl, SLC_SHIFT)
        future = jb > cur
        forced = (jb == 0) | (jb == cur) | (jb == cur - 1)
        wstart = pl.multiple_of(jnp.maximum(i - WIN // TQ, 0) * TQ, TQ)
        dist_w = (t0 + wq) - (wstart + wkey)
        ok_w = (dist_w >= 0) & (dist_w < WIN)

        qa_l, o_fix_l, s_d_l, init = [], [], [], []
        for g in range(G):
            qs = q_ref[0, g * R:(g + 1) * R, sub * TQ:(sub + 1) * TQ, :].reshape(R * TQ, LANES)

            p_c = heads(_bdot_nt(kc_ref[0, g], qs), ok_c, _softmax_keys)
            o_cmp = jnp.dot(vct_ref[0, g], jnp.concatenate(p_c, axis=1).astype(BF16),
                            preferred_element_type=F32)

            pw = heads(_bdot_nt(kw_ref[0, g, pl.ds(wstart, wlen), :], qs), ok_w, _exp_keys)
            o_win = jnp.dot(vwt_ref[0, g, :, pl.ds(wstart, wlen)],
                            jnp.concatenate([p for p, _ in pw], axis=1).astype(BF16), preferred_element_type=F32)
            inv_w = jnp.concatenate([inv for _, inv in pw], axis=1)
            o_fix_l.append(gate_rows(sub, g, 0) * o_cmp + (gate_rows(sub, g, 2) * inv_w) * o_win)

            importance = _dot_sel_lhs(overlap_t, p_c[0] + p_c[1] + p_c[2] + p_c[3])
            score = jnp.where(future, NEG_INF, jnp.where(forced, FORCE_SCORE, importance))
            groups = [score[v * SUBLANES:(v + 1) * SUBLANES] for v in range(NB // SUBLANES)]
            ranks = [jnp.zeros((SUBLANES, TQ), F32) for _ in groups]
            for k in range(NB):
                sk = score[k:k + 1, :]
                for v, sv in enumerate(groups):
                    if v < k // SUBLANES:
                        ahead = sk > sv
                    elif v > k // SUBLANES:
                        ahead = sk >= sv
                    else:
                        ahead = (sk > sv) | ((sk == sv) & (row8 > k % SUBLANES))
                    ranks[v] = ranks[v] + jnp.where(ahead, 1.0, 0.0)
            rank = jnp.concatenate(ranks, axis=0)
            sel_t = (rank < SLC_TOPN) & (score > NEG_INF)
            selt_ref[sub, g] = jnp.where(sel_t, 1.0, 0.0)

            sel_d = jnp.where(key_r < SLC_BLOCK, selt_ref[sub, g, pl.ds(2 * i, 1), :],
                              selt_ref[sub, g, pl.ds(2 * i + 1, 1), :])
            s_d = jnp.concatenate(heads(_bdot_nt(ks_ref[0, g, pl.ds(d0, TQ), :], qs), (sel_d > 0.5) & ok_d,
                                        lambda s, ok: jnp.where(ok, s, NEG_INF)), axis=1)
            s_d_l.append(s_d)

            masked_t = jnp.where(sel_t & (jb < 2 * i), 0.0, 1.0)
            cols_t = jnp.concatenate([jnp.zeros((AUG_SEL, TQ), F32), masked_t,
                                      jnp.zeros((LANES - AUG_SEL - NB, TQ), F32)], axis=0)
            cols = cols_t.T.astype(BF16)
            qa_l.append(jnp.concatenate([qs[r * TQ:(r + 1) * TQ] + cols for r in range(R)], axis=0))

        for g in range(G):
            s_ref[sub, g] = tile_scores(0, g, qa_l[g])
            m = jnp.max(s_d_l[g], axis=0, keepdims=True)
            p = jnp.exp(s_d_l[g] - m)
            init.append((m, jnp.sum(p, axis=0, keepdims=True),
                         jnp.dot(vst_ref[0, g, :, pl.ds(d0, TQ)], p.astype(BF16), preferred_element_type=F32)))
        return (t0 + SLC_TILE - 1) // SLC_TILE, qa_l, o_fix_l, tuple(init)

    def key_loop(sub, n_tiles, qa_l, init):
        def slc_step(j, carry):
            off = pl.multiple_of(j * SLC_TILE, SLC_TILE)
            nxt = jnp.minimum(j + 1, n_tiles - 1)
            out = []
            for g in range(G):
                m, l, acc = carry[g]
                s = s_ref[sub, g]
                s_ref[sub, g] = tile_scores(nxt, g, qa_l[g])
                m_new = jnp.maximum(m, jnp.max(s, axis=0, keepdims=True))
                alpha = jnp.exp(m - m_new)
                p = jnp.exp(s - m_new)
                l = alpha * l + jnp.sum(p, axis=0, keepdims=True)
                acc = alpha * acc + jnp.dot(vst_ref[0, g, :, pl.ds(off, SLC_TILE)], p.astype(BF16),
                                            preferred_element_type=F32)
                out.append((m_new, l, acc))
            return tuple(out)

        return lax.fori_loop(0, n_tiles, slc_step, init)

    def back(sub, o_fix_l, slc):
        for g in range(G):
            _, l, acc = slc[g]
            o = o_fix_l[g] + (gate_rows(sub, g, 1) * (1.0 / jnp.maximum(l, 1e-30))) * acc
            for a in range(R // 2):
                pair = jnp.concatenate([o[:, (2 * a) * TQ:(2 * a + 1) * TQ],
                                        o[:, (2 * a + 1) * TQ:(2 * a + 2) * TQ]], axis=0)
                lo = (g * R + 2 * a) * dh
                o_ref[sub * TQ:(sub + 1) * TQ, lo:lo + 2 * dh] = pair.T.astype(BF16)

    fronts = [front(sub) for sub in range(ATTN_TILES)]
    loops = [key_loop(sub, n_tiles, qa_l, init) for sub, (n_tiles, qa_l, _, init) in enumerate(fronts)]
    for sub in range(ATTN_TILES):
        back(sub, fronts[sub][2], loops[sub])


def _nsa_attention(qn, gates_t, kc, vct, ksn, vst, kwn, vwt, B, T):
    G, H, dh = NSA_KV_HEADS, NSA_HEADS, NSA_HEAD_DIM
    TQ = ATTN_TILES * LANES
    nq = T // TQ
    nseg = kc.shape[2]
    per_b = lambda shape: pl.BlockSpec((1,) + shape, lambda b, i: (b, 0, 0, 0))
    return pl.pallas_call(
        functools.partial(_nsa_attn_kernel, seq_len=T),
        grid=(B, nq),
        in_specs=[pl.BlockSpec((1, H, TQ, LANES), lambda b, i: (b, 0, i, 0)),
                  pl.BlockSpec((1, GATE_ROWS, TQ), lambda b, i: (b, 0, i)),
                  per_b((G, nseg, LANES)), per_b((G, dh, nseg)),
                  per_b((G, T, LANES)), per_b((G, dh, T)), per_b((G, T, LANES)), per_b((G, dh, T))],
        out_specs=pl.BlockSpec((TQ, NSA_Q_W), lambda b, i: (b * nq + i, 0)),
        out_shape=jax.ShapeDtypeStruct((B * T, NSA_Q_W), BF16),
        scratch_shapes=[pltpu.VMEM((ATTN_TILES, G, T // SLC_BLOCK, LANES), F32),
                        pltpu.VMEM((ATTN_TILES, G, SLC_TILE, NSA_REP * LANES), F32)],
        compiler_params=_cparams("parallel", "arbitrary"),
        name="nsa_attention",
    )(qn, gates_t, kc, vct, ksn, vst, kwn, vwt)


def _nsa_mixer(u, misc, B, T, q_norm_g, k_norm_g, cmp_pe, cmp_w1, cmp_w2):
    assert T % LANES == 0 and T >= (WIN // LANES + 1) * LANES and T // CMP_STRIDE == LANES
    assert SLC_BLOCK == 1 << SLC_SHIFT and LANES == 2 * SLC_BLOCK
    qn, ksn, vst, kwn, vwt, gates_t = _nsa_prep(u, misc, B, T, q_norm_g, k_norm_g)
    kc, vct = _nsa_compress(u, B, T, cmp_pe, cmp_w1, cmp_w2, k_norm_g[0])
    return _nsa_attention(qn, gates_t, kc, vct, ksn, vst, kwn, vwt, B, T)


def _merge_kernel(x_ref, h_ref, oa_ref, ob_ref, oc_ref, od_ref, mod_ref, wg_ref, wb_ref, wo_ref, y_ref):
    h = h_ref[...]
    merged = None
    for i, o_ref in enumerate((oa_ref, ob_ref, oc_ref, od_ref)):
        gate = _sigmoid(jnp.dot(h, wg_ref[i], preferred_element_type=F32))
        term = gate * jnp.dot(o_ref[...], wb_ref[i], preferred_element_type=F32)
        merged = term if merged is None else merged + term
    y_ref[...] = x_ref[...] + mod_ref[0, 2:3, :] * _bdot(merged, wo_ref[...])


def _merge(x2d, h, outs, mod_l, wg, wb, wo, T):
    M, D = x2d.shape
    tm = min(512, T)
    per_b = T // tm
    row = lambda w: pl.BlockSpec((tm, w), lambda m: (m, 0))
    const = lambda shape: pl.BlockSpec(shape, lambda m: (0,) * len(shape), pipeline_mode=pl.Buffered(1))
    return pl.pallas_call(
        _merge_kernel,
        grid=(M // tm,),
        in_specs=[row(D), row(D), row(MIX_W), row(MIX_W), row(MIX_W), row(MIX_W),
                  pl.BlockSpec((1, 6, D), lambda m: (m // per_b, 0, 0)),
                  const((N_BRANCH, D, D)), const((N_BRANCH, MIX_W, D)), const((D, D))],
        out_specs=row(D),
        out_shape=jax.ShapeDtypeStruct((M, D), F32),
        compiler_params=_cparams("parallel"),
        name="gated_merge_out_proj",
    )(x2d, h, *outs, mod_l, wg, wb, wo)


FFN_CHUNK = 256


def _ffn_kernel(x_ref, mod_ref, g_ref, wa_ref, wb_ref, wo_ref, y_ref, acc_ref):
    x = x_ref[...]
    y = x * lax.rsqrt(jnp.mean(x * x, axis=-1, keepdims=True) + RMS_EPS) * g_ref[...]
    h = (y * (1.0 + mod_ref[0, 4:5, :]) + mod_ref[0, 3:4, :]).astype(BF16)
    d_ff = wo_ref.shape[0]
    for c in range(d_ff // FFN_CHUNK):
        cols = slice(c * FFN_CHUNK, (c + 1) * FFN_CHUNK)
        a = jnp.dot(h, wa_ref[:, cols], preferred_element_type=F32)
        b = jnp.dot(h, wb_ref[:, cols], preferred_element_type=F32)
        part = _bdot(_silu(a) * b, wo_ref[cols, :])
        if c == 0:
            acc_ref[...] = part
        else:
            acc_ref[...] += part
    y_ref[...] = x + mod_ref[0, 5:6, :] * acc_ref[...]


def _ffn(x2d, mod_l, norm_g, w_in, w_out, T):
    M, D = x2d.shape
    d_ff = w_out.shape[0]
    assert d_ff % FFN_CHUNK == 0
    tm = min(1024, T)
    per_b = T // tm
    const = lambda shape, idx: pl.BlockSpec(shape, lambda m: idx, pipeline_mode=pl.Buffered(1))
    return pl.pallas_call(
        _ffn_kernel,
        grid=(M // tm,),
        in_specs=[pl.BlockSpec((tm, D), lambda m: (m, 0)),
                  pl.BlockSpec((1, 6, D), lambda m: (m // per_b, 0, 0)),
                  pl.BlockSpec((1, D), lambda m: (0, 0)),
                  const((D, d_ff), (0, 0)), const((D, d_ff), (0, 1)), const((d_ff, D), (0, 0))],
        out_specs=pl.BlockSpec((tm, D), lambda m: (m, 0)),
        out_shape=jax.ShapeDtypeStruct((M, D), F32),
        scratch_shapes=[pltpu.VMEM((tm, D), F32)],
        compiler_params=_cparams("parallel"),
        name="swiglu_ffn",
    )(x2d, mod_l, norm_g.reshape(1, D), w_in, w_in, w_out)


def _pack_w_in(w_in):
    D = w_in.shape[0]
    o1 = SSD_IN
    o2 = o1 + SC_IN
    o3 = o2 + SG_IN
    q0 = o3
    kv0 = q0 + NSA_Q_W
    gt0 = kv0 + 6 * NSA_KV_W
    parts = [w_in[:, :MIX_W + SSD_XBC],
             w_in[:, o1:o3],
             w_in[:, q0:gt0],
             w_in[:, MIX_W + SSD_XBC:o1],
             w_in[:, gt0:gt0 + 3 * NSA_HEADS]]
    w = jnp.concatenate(parts, axis=1)
    return jnp.pad(w, ((0, 0), (0, U_WIDTH - w.shape[1]))).astype(BF16)


def kernel(x, c, ada_w, ada_b, norm_mix_g, norm_ffn_g, w_in, ssd_conv_w, ssd_conv_b, ssd_dt_bias, ssd_a_log, ssd_d,
           ssd_norm_g, sc_conv_w, sg_norm_g, sg_w, sg_b, nsa_q_norm_g, nsa_k_norm_g, nsa_cmp_pe, nsa_cmp_w1,
           nsa_cmp_w2, w_branch, w_branch_gate, w_out, w_ffn_in, w_ffn_out):
    B, T, D = x.shape
    L = w_in.shape[0]
    mod = _modulation(c, ada_w, ada_b).reshape(L, B, 6, D)
    x2d = x.reshape(B * T, D)
    for l in range(L):
        u, h, misc = _in_proj(x2d, mod[l], norm_mix_g[l], _pack_w_in(w_in[l]), T)
        outs = (
            _ssd_mixer(u, misc, B, T, ssd_conv_w[l], ssd_conv_b[l], ssd_dt_bias[l], ssd_a_log[l], ssd_d[l],
                       ssd_norm_g[l]),
            _short_conv_mixer(u, B, T, sc_conv_w[l]),
            _spatial_gating_mixer(u, B, T, sg_norm_g[l], sg_w[l], sg_b[l]),
            _nsa_mixer(u, misc, B, T, nsa_q_norm_g[l], nsa_k_norm_g[l], nsa_cmp_pe[l], nsa_cmp_w1[l], nsa_cmp_w2[l]),
        )
        x2d = _merge(x2d, h, outs, mod[l], w_branch_gate[l].astype(BF16), w_branch[l].astype(BF16),
                     w_out[l].astype(BF16), T)
        x2d = _ffn(x2d, mod[l], norm_ffn_g[l], w_ffn_in[l].astype(BF16), w_ffn_out[l].astype(BF16), T)
    return x2d.reshape(B, T, D)
```

```python
import functools
import math

import jax
import jax.numpy as jnp
from jax import lax
from jax.experimental import pallas as pl
from jax.experimental.pallas import tpu as pltpu

F32 = jnp.float32
BF16 = jnp.bfloat16
RMS_EPS = 1e-6
NEG_INF = float("-inf")

MIX_W = 512
N_BRANCH = 4

SSD_HEAD_DIM = 64
SSD_HEADS = 8
SSD_GROUPS = 2
SSD_STATE = 128
SSD_CONV = 4
SSD_CHUNK = 128
SSD_STEP_CHUNKS = 4
SSD_XBC = MIX_W + 2 * SSD_GROUPS * SSD_STATE
SSD_IN = MIX_W + SSD_XBC + SSD_HEADS
SC_CONV = 3
SC_IN = 3 * MIX_W
SG_GROUPS = 4
SG_CHUNK = 128
SG_IN = 2 * MIX_W
NSA_HEADS = 8
NSA_KV_HEADS = 2
NSA_REP = NSA_HEADS // NSA_KV_HEADS
NSA_HEAD_DIM = 64
CMP_BLOCK = 32
CMP_STRIDE = 16
SLC_BLOCK = 64
SLC_TOPN = 8
WIN = 256
FORCE_SCORE = 1e9
NSA_KV_W = NSA_KV_HEADS * NSA_HEAD_DIM
NSA_Q_W = NSA_HEADS * NSA_HEAD_DIM
NSA_IN = NSA_Q_W + 6 * NSA_KV_W + 3 * NSA_HEADS

LANES = 128
SUBLANES = 8
VMEM_LIMIT_BYTES = 56 * 1024 * 1024

COL_Z = 0
COL_XBC = COL_Z + MIX_W
COL_SC = COL_XBC + SSD_XBC
COL_SG = COL_SC + SC_IN
COL_Q = COL_SG + SG_IN
COL_KV = COL_Q + NSA_Q_W
COL_MISC = COL_KV + 6 * NSA_KV_W
MISC_GATE0 = SSD_HEADS
GATE_ROWS = 32
SLC_SHIFT = 6
SLC_TILE = 512
ATTN_TILES = 4
AUG_POS = NSA_HEAD_DIM
AUG_SEL = AUG_POS + SUBLANES
MASK_SCORE = -(2.0 ** 100)
IN_CHUNK = 256
U_WIDTH = COL_MISC + IN_CHUNK


def _cparams(*sem):
    return pltpu.CompilerParams(dimension_semantics=sem, vmem_limit_bytes=VMEM_LIMIT_BYTES)


def _bdot(a, b):
    return jnp.dot(a.astype(BF16), b.astype(BF16), preferred_element_type=F32)


def _bdot_nt(a, b):
    return lax.dot_general(a.astype(BF16), b.astype(BF16), (((1,), (1,)), ((), ())),
                           preferred_element_type=F32)


def _split3(a):
    hi = a.astype(BF16)
    r1 = a - hi.astype(F32)
    mid = r1.astype(BF16)
    lo = (r1 - mid.astype(F32)).astype(BF16)
    return hi, mid, lo


def _dot_sel_rhs(a, sel):
    hi, mid, lo = _split3(a)
    return (jnp.dot(hi, sel, preferred_element_type=F32) + jnp.dot(mid, sel, preferred_element_type=F32)
            + jnp.dot(lo, sel, preferred_element_type=F32))


def _dot_sel_lhs(sel, a):
    hi, mid, lo = _split3(a)
    return (jnp.dot(sel, hi, preferred_element_type=F32) + jnp.dot(sel, mid, preferred_element_type=F32)
            + jnp.dot(sel, lo, preferred_element_type=F32))


def _sigmoid(x):
    return 1.0 / (1.0 + jnp.exp(-x))


def _silu(x):
    return x * _sigmoid(x)


def _gelu_tanh(x):
    c = math.sqrt(2.0 / math.pi)
    return 0.5 * x * (1.0 + jnp.tanh(c * (x + 0.044715 * (x * x * x))))


def _softplus(x):
    return jnp.maximum(x, 0.0) + jnp.log1p(jnp.exp(-jnp.abs(x)))


def _shift_rows(x, tail, k, row8):
    sh = pltpu.roll(x, k, 0)
    tl = pltpu.roll(tail, k, 0)
    top = jnp.where(row8 < k, tl, sh[0:SUBLANES])
    return jnp.concatenate([top, sh[SUBLANES:]], axis=0)


def _mod_kernel(c_ref, w_ref, b_ref, o_ref):
    o_ref[0] = _bdot(_silu(c_ref[...]), w_ref[0]) + b_ref[0]


def _modulation(c, ada_w, ada_b):
    L, D, D6 = ada_w.shape
    B = c.shape[0]
    tn = D6 // 4
    return pl.pallas_call(
        _mod_kernel,
        grid=(L, D6 // tn),
        in_specs=[pl.BlockSpec((B, D), lambda l, n: (0, 0)),
                  pl.BlockSpec((1, D, tn), lambda l, n: (l, 0, n)),
                  pl.BlockSpec((1, 1, tn), lambda l, n: (l, 0, n))],
        out_specs=pl.BlockSpec((1, B, tn), lambda l, n: (l, 0, n)),
        out_shape=jax.ShapeDtypeStruct((L, B, D6), F32),
        compiler_params=_cparams("parallel", "parallel"),
        name="adaln_modulation",
    )(c, ada_w, ada_b.reshape(L, 1, D6))


def _in_kernel(x_ref, mod_ref, g_ref, wa_ref, wb_ref, wm_ref, u_ref, h_ref, misc_ref):
    x = x_ref[...]
    y = x * lax.rsqrt(jnp.mean(x * x, axis=-1, keepdims=True) + RMS_EPS) * g_ref[...]
    h = (y * (1.0 + mod_ref[0, 1:2, :]) + mod_ref[0, 0:1, :]).astype(BF16)
    h_ref[...] = h
    col = 0
    for w_ref in (wa_ref, wb_ref, wm_ref):
        for n in range(w_ref.shape[2] // IN_CHUNK):
            u = jnp.dot(h, w_ref[0, :, n * IN_CHUNK:(n + 1) * IN_CHUNK], preferred_element_type=F32)
            u_ref[:, col:col + IN_CHUNK] = u.astype(BF16)
            if col == COL_MISC:
                misc_ref[...] = u[:, :LANES]
            col += IN_CHUNK


def _in_proj(x2d, mod_l, norm_g, w_parts, layer, T):
    M, D = x2d.shape
    tm = min(512, T)
    per_b = T // tm
    assert sum(w.shape[2] for w in w_parts) == U_WIDTH and all(w.shape[2] % IN_CHUNK == 0 for w in w_parts)
    wspec = lambda w: pl.BlockSpec((1, D, w.shape[2]), lambda m: (layer, 0, 0), pipeline_mode=pl.Buffered(1))
    return pl.pallas_call(
        _in_kernel,
        grid=(M // tm,),
        in_specs=[pl.BlockSpec((tm, D), lambda m: (m, 0)),
                  pl.BlockSpec((1, 6, D), lambda m: (m // per_b, 0, 0)),
                  pl.BlockSpec((1, D), lambda m: (0, 0))] + [wspec(w) for w in w_parts],
        out_specs=[pl.BlockSpec((tm, U_WIDTH), lambda m: (m, 0)),
                   pl.BlockSpec((tm, D), lambda m: (m, 0)),
                   pl.BlockSpec((tm, LANES), lambda m: (m, 0))],
        out_shape=[jax.ShapeDtypeStruct((M, U_WIDTH), BF16), jax.ShapeDtypeStruct((M, D), BF16),
                   jax.ShapeDtypeStruct((M, LANES), F32)],
        compiler_params=_cparams("parallel"),
        name="norm_in_proj",
    )(x2d, mod_l, norm_g.reshape(1, D), *w_parts)


def _ssd_kernel(z_ref, xa_ref, xb_ref, misc_ref, cw_ref, cb_ref, dtb_ref, alog_ref, dsk_ref, ng_ref,
                o_ref, tail_ref, st_ref):
    Q, P, N, H, G = SSD_CHUNK, SSD_HEAD_DIM, SSD_STATE, SSD_HEADS, SSD_GROUPS
    R = H // G

    @pl.when(pl.program_id(1) == 0)
    def _():
        tail_ref[...] = jnp.zeros_like(tail_ref)
        st_ref[...] = jnp.zeros_like(st_ref)

    xin = jnp.concatenate([xa_ref[...], xb_ref[...]], axis=1).astype(F32)
    tail = tail_ref[...]
    row8 = lax.broadcasted_iota(jnp.int32, (SUBLANES, SSD_XBC), 0)
    acc = xin * cw_ref[SSD_CONV - 1:SSD_CONV, :] + cb_ref[...]
    for k in range(1, SSD_CONV):
        acc = acc + _shift_rows(xin, tail, k, row8) * cw_ref[SSD_CONV - 1 - k:SSD_CONV - k, :]
    rows_in = xin.shape[0]
    tail_ref[...] = xin[rows_in - SUBLANES:rows_in, :]
    xbc = _silu(acc)

    lane = lax.broadcasted_iota(jnp.int32, (Q, LANES), 1)
    rowi = lax.broadcasted_iota(jnp.int32, (Q, LANES), 0)
    is_head = lane < H
    tri = (lane <= rowi).astype(BF16)
    tri_t = (rowi <= lane).astype(BF16)
    e_row = lax.broadcasted_iota(jnp.int32, (LANES, MIX_W), 0)
    e_col = lax.broadcasted_iota(jnp.int32, (LANES, MIX_W), 1)
    expand = (jnp.right_shift(e_col, 6) == e_row).astype(BF16)
    causal = lane <= rowi
    neg_a = -jnp.exp(alog_ref[...])

    for c in range(rows_in // Q):
        rows = slice(c * Q, (c + 1) * Q)
        xs = xbc[rows, :MIX_W]
        bm = xbc[rows, MIX_W:MIX_W + G * N]
        cm = xbc[rows, MIX_W + G * N:]
        dt = jnp.where(is_head, _softplus(misc_ref[rows, :] + dtb_ref[...]), 0.0)
        a = dt * neg_a
        a_cs = _dot_sel_lhs(tri, a)
        a_cs_t = _dot_sel_rhs(a.T, tri_t)
        a_last = a_cs[Q - 1:Q, :]
        ea = jnp.exp(a_cs)
        dec = jnp.exp(a_last - a_cs)
        xdt = xs * _dot_sel_rhs(dt, expand)
        ea_e = _dot_sel_rhs(ea, expand)
        xdec = xdt * _dot_sel_rhs(dec, expand)

        ys = []
        for g in range(G):
            bg = bm[:, g * N:(g + 1) * N]
            cg = cm[:, g * N:(g + 1) * N].astype(BF16)
            cb = _bdot_nt(cg, bg)
            bg_t = bg.T.astype(BF16)
            for r in range(R):
                h = g * R + r
                seg = jnp.where(causal, a_cs[:, h:h + 1] - a_cs_t[h:h + 1, :], NEG_INF)
                y_diag = _bdot(cb * jnp.exp(seg), xdt[:, h * P:(h + 1) * P])
                state = st_ref[h]
                y_off = _bdot(cg, state) * ea_e[:, h * P:(h + 1) * P]
                st_ref[h] = state * jnp.exp(a_last[:, h:h + 1]) + _bdot(bg_t, xdec[:, h * P:(h + 1) * P])
                ys.append(y_diag + y_off)
        y = jnp.concatenate(ys, axis=1) + xs * dsk_ref[...]
        y = y * _silu(z_ref[rows, :].astype(F32))
        gw = MIX_W // G
        outs = []
        for g in range(G):
            yg = y[:, g * gw:(g + 1) * gw]
            outs.append(yg * lax.rsqrt(jnp.mean(yg * yg, axis=-1, keepdims=True) + RMS_EPS))
        o_ref[rows, :] = (jnp.concatenate(outs, axis=1) * ng_ref[...]).astype(BF16)


def _ssd_mixer(u, misc, B, T, conv_w, conv_b, dt_bias, a_log, d_skip, norm_g):
    Q = min(SSD_STEP_CHUNKS * SSD_CHUNK, T)
    nc = T // Q

    def pad_lane(v):
        return jnp.pad(v, (0, LANES - v.shape[0])).reshape(1, LANES)

    row = lambda b, c: b * nc + c
    full = lambda shape: pl.BlockSpec(shape, lambda b, c: (0,) * len(shape))
    return pl.pallas_call(
        _ssd_kernel,
        grid=(B, nc),
        in_specs=[pl.BlockSpec((Q, MIX_W), lambda b, c: (row(b, c), COL_Z // MIX_W)),
                  pl.BlockSpec((Q, MIX_W), lambda b, c: (row(b, c), COL_XBC // MIX_W)),
                  pl.BlockSpec((Q, MIX_W), lambda b, c: (row(b, c), COL_XBC // MIX_W + 1)),
                  pl.BlockSpec((Q, LANES), lambda b, c: (row(b, c), 0)),
                  full((SSD_CONV, SSD_XBC)), full((1, SSD_XBC)), full((1, LANES)), full((1, LANES)),
                  full((1, MIX_W)), full((1, MIX_W))],
        out_specs=pl.BlockSpec((Q, MIX_W), lambda b, c: (row(b, c), 0)),
        out_shape=jax.ShapeDtypeStruct((B * T, MIX_W), BF16),
        scratch_shapes=[pltpu.VMEM((SUBLANES, SSD_XBC), F32),
                        pltpu.VMEM((SSD_HEADS, SSD_STATE, SSD_HEAD_DIM), F32)],
        compiler_params=_cparams("parallel", "arbitrary"),
        name="ssd_mixer",
    )(u, u, u, misc, conv_w, conv_b.reshape(1, SSD_XBC), pad_lane(dt_bias), pad_lane(a_log),
      jnp.repeat(d_skip, SSD_HEAD_DIM).reshape(1, MIX_W), norm_g.reshape(1, MIX_W))


def _sc_kernel(b_ref, c_ref, h_ref, w_ref, o_ref, tail_ref):
    @pl.when(pl.program_id(1) == 0)
    def _():
        tail_ref[...] = jnp.zeros_like(tail_ref)

    cx = c_ref[...].astype(F32) * h_ref[...].astype(F32)
    tt = cx.shape[0]
    tail = tail_ref[...]
    row8 = lax.broadcasted_iota(jnp.int32, (SUBLANES, MIX_W), 0)
    acc = cx * w_ref[SC_CONV - 1:SC_CONV, :]
    for k in range(1, SC_CONV):
        acc = acc + _shift_rows(cx, tail, k, row8) * w_ref[SC_CONV - 1 - k:SC_CONV - k, :]
    tail_ref[...] = cx[tt - SUBLANES:tt, :]
    o_ref[...] = (b_ref[...].astype(F32) * acc).astype(BF16)


def _short_conv_mixer(u, B, T, conv_w):
    tt = min(512, T)
    nt = T // tt
    c0 = COL_SC // MIX_W
    spec = lambda j: pl.BlockSpec((tt, MIX_W), lambda b, i: (b * nt + i, c0 + j))
    return pl.pallas_call(
        _sc_kernel,
        grid=(B, nt),
        in_specs=[spec(0), spec(1), spec(2), pl.BlockSpec((SC_CONV, MIX_W), lambda b, i: (0, 0))],
        out_specs=pl.BlockSpec((tt, MIX_W), lambda b, i: (b * nt + i, 0)),
        out_shape=jax.ShapeDtypeStruct((B * T, MIX_W), BF16),
        scratch_shapes=[pltpu.VMEM((SUBLANES, MIX_W), F32)],
        compiler_params=_cparams("parallel", "arbitrary"),
        name="short_conv_mixer",
    )(u, u, u, conv_w)


def _sg_kernel(u_ref, v_ref, ng_ref, w_ref, bias_ref, o_ref):
    Q = SG_CHUNK
    rowi = lax.broadcasted_iota(jnp.int32, (Q, Q), 0)
    coli = lax.broadcasted_iota(jnp.int32, (Q, Q), 1)
    gd = MIX_W // SG_GROUPS
    ws = [jnp.where(coli <= rowi, w_ref[g], 0.0).astype(BF16) for g in range(SG_GROUPS)]
    for c in range(u_ref.shape[0] // Q):
        rows = slice(c * Q, (c + 1) * Q)
        v = _gelu_tanh(v_ref[rows, :].astype(F32))
        v = (v * lax.rsqrt(jnp.mean(v * v, axis=-1, keepdims=True) + RMS_EPS) * ng_ref[...]).astype(BF16)
        mixed = jnp.concatenate([jnp.dot(ws[g], v[:, g * gd:(g + 1) * gd], preferred_element_type=F32)
                                 for g in range(SG_GROUPS)], axis=1)
        o_ref[rows, :] = (_gelu_tanh(u_ref[rows, :].astype(F32)) * (mixed + bias_ref[...])).astype(BF16)


def _spatial_gating_mixer(u, B, T, norm_g, w_s, b_s):
    Q = SG_CHUNK
    tt = min(4 * Q, T)
    nt = T // tt
    c0 = COL_SG // MIX_W
    bias = jnp.repeat(b_s.T, MIX_W // SG_GROUPS, axis=1)
    return pl.pallas_call(
        _sg_kernel,
        grid=(B, nt),
        in_specs=[pl.BlockSpec((tt, MIX_W), lambda b, c: (b * nt + c, c0)),
                  pl.BlockSpec((tt, MIX_W), lambda b, c: (b * nt + c, c0 + 1)),
                  pl.BlockSpec((1, MIX_W), lambda b, c: (0, 0)),
                  pl.BlockSpec((SG_GROUPS, Q, Q), lambda b, c: (0, 0, 0)),
                  pl.BlockSpec((Q, MIX_W), lambda b, c: (0, 0))],
        out_specs=pl.BlockSpec((tt, MIX_W), lambda b, c: (b * nt + c, 0)),
        out_shape=jax.ShapeDtypeStruct((B * T, MIX_W), BF16),
        compiler_params=_cparams("parallel", "parallel"),
        name="spatial_gating_mixer",
    )(u, u, norm_g.reshape(1, MIX_W), w_s, bias)


def _group_mean_sq(x, width):
    n = x.shape[1]
    r = lax.broadcasted_iota(jnp.int32, (n, n), 0)
    c = lax.broadcasted_iota(jnp.int32, (n, n), 1)
    sh = width.bit_length() - 1
    same = (jnp.right_shift(r, sh) == jnp.right_shift(c, sh)).astype(BF16)
    return _dot_sel_rhs(x * x, same) * (1.0 / width)


def _key_aug(lane, pos):
    return jnp.where((lane == AUG_POS) | (lane == AUG_POS + 1), 1.0,
                     jnp.where(lane == AUG_POS + 2, -(pos & ~(LANES - 1)).astype(F32),
                               jnp.where(lane == AUG_POS + 3, -(pos & (LANES - 1)).astype(F32), 0.0)))


def _nsa_prep_kernel(q_ref, ks_ref, vs_ref, kw_ref, vw_ref, misc_ref, qg_ref, ksg_ref, kwg_ref,
                     qa_ref, ksa_ref, vst_ref, kwa_ref, vwt_ref, gate_ref):
    dh = NSA_HEAD_DIM
    tt = q_ref.shape[0]
    lane = lax.broadcasted_iota(jnp.int32, (tt, LANES), 1)
    pos = pl.program_id(1) * tt + lax.broadcasted_iota(jnp.int32, (tt, LANES), 0)
    is_feat = lane < dh

    q = q_ref[...].astype(F32)
    qn = q * lax.rsqrt(_group_mean_sq(q, dh) + RMS_EPS) * qg_ref[...] * (dh ** -0.5)
    q_pos = jnp.where(lane == AUG_POS, (pos & ~(LANES - 1)).astype(F32),
                      jnp.where(lane == AUG_POS + 1, (pos & (LANES - 1)).astype(F32),
                                jnp.where((lane == AUG_POS + 2) | (lane == AUG_POS + 3), 1.0, 0.0)))
    for h in range(NSA_HEADS):
        pair = qn[:, (h // 2) * LANES:(h // 2 + 1) * LANES]
        feat = pair if h % 2 == 0 else pltpu.roll(pair, dh, 1)
        qa_ref[0, h] = jnp.where(is_feat, feat, -(2.0 ** -(h + 1)) * q_pos).astype(BF16)

    ks = ks_ref[...].astype(F32)
    ksn = ks * lax.rsqrt(_group_mean_sq(ks, dh) + RMS_EPS) * ksg_ref[...]
    kw = kw_ref[...].astype(F32)
    kwn = kw * lax.rsqrt(_group_mean_sq(kw, dh) + RMS_EPS) * kwg_ref[...]
    k_pos = _key_aug(lane, pos)
    k_pos_sel = jnp.where(lane == AUG_SEL + jnp.right_shift(pos, SLC_SHIFT), MASK_SCORE, k_pos)
    vs_t = vs_ref[...].astype(F32).T.astype(BF16)
    vw_t = vw_ref[...].astype(F32).T.astype(BF16)
    for g in range(NSA_KV_HEADS):
        sl = slice(g * dh, (g + 1) * dh)
        ksa_ref[0, g] = jnp.where(is_feat, ksn if g == 0 else pltpu.roll(ksn, dh, 1), k_pos_sel).astype(BF16)
        kwa_ref[0, g] = jnp.where(is_feat, kwn if g == 0 else pltpu.roll(kwn, dh, 1), k_pos).astype(BF16)
        vst_ref[0, g] = vs_t[sl, :]
        vwt_ref[0, g] = vw_t[sl, :]
    gate_ref[0] = _sigmoid(misc_ref[...]).T[0:GATE_ROWS, :]


def _nsa_prep(u, misc, B, T, q_norm_g, k_norm_g):
    tt = min(512, T)
    nt = T // tt
    G, H, dh = NSA_KV_HEADS, NSA_HEADS, NSA_HEAD_DIM
    kv0 = COL_KV // LANES
    kvspec = lambda j: pl.BlockSpec((tt, LANES), lambda b, i: (b * nt + i, kv0 + j))
    vec = lambda n: pl.BlockSpec((1, n), lambda b, i: (0, 0))
    kspec = pl.BlockSpec((1, G, tt, LANES), lambda b, i: (b, 0, i, 0))
    kshape = jax.ShapeDtypeStruct((B, G, T, LANES), BF16)
    vspec = pl.BlockSpec((1, G, dh, tt), lambda b, i: (b, 0, 0, i))
    vshape = jax.ShapeDtypeStruct((B, G, dh, T), BF16)
    return pl.pallas_call(
        _nsa_prep_kernel,
        grid=(B, nt),
        in_specs=[pl.BlockSpec((tt, NSA_Q_W), lambda b, i: (b * nt + i, COL_Q // NSA_Q_W)),
                  kvspec(2), kvspec(3), kvspec(4), kvspec(5),
                  pl.BlockSpec((tt, LANES), lambda b, i: (b * nt + i, 0)),
                  vec(NSA_Q_W), vec(LANES), vec(LANES)],
        out_specs=[pl.BlockSpec((1, H, tt, LANES), lambda b, i: (b, 0, i, 0)),
                   kspec, vspec, kspec, vspec,
                   pl.BlockSpec((1, GATE_ROWS, tt), lambda b, i: (b, 0, i))],
        out_shape=[jax.ShapeDtypeStruct((B, H, T, LANES), BF16),
                   kshape, vshape, kshape, vshape,
                   jax.ShapeDtypeStruct((B, GATE_ROWS, T), F32)],
        compiler_params=_cparams("parallel", "parallel"),
        name="nsa_prep",
    )(u, u, u, u, u, misc, jnp.tile(q_norm_g, NSA_HEADS).reshape(1, NSA_Q_W),
      jnp.tile(k_norm_g[1], G).reshape(1, LANES), jnp.tile(k_norm_g[2], G).reshape(1, LANES))


def _nsa_cmp_kernel(kin_ref, vin_ref, w1_ref, pe_ref, w2k_ref, w2vt_ref, kg_ref, kc_ref, vct_ref, x_ref):
    G = NSA_KV_HEADS
    nseg = kin_ref.shape[0] // CMP_STRIDE
    x_ref[0] = kin_ref[...].astype(F32)
    x_ref[1] = vin_ref[...].astype(F32)
    for j in range(2):
        a = jnp.concatenate([x_ref[j, pl.ds(l, nseg, stride=CMP_STRIDE), :] for l in range(CMP_STRIDE)],
                            axis=1).astype(BF16)
        pe_term = (jnp.dot(pe_ref[j, 0], w1_ref[j, 0], preferred_element_type=F32)
                   + jnp.dot(pe_ref[j, 1], w1_ref[j, 1], preferred_element_type=F32))[0:1, :]
        lo = jnp.dot(a, w1_ref[j, 0], preferred_element_type=F32)
        hi = jnp.dot(a, w1_ref[j, 1], preferred_element_type=F32)
        y = _gelu_tanh(lo + pltpu.roll(hi, nseg - 1, 0) + pe_term)
        for g in range(G):
            if j == 0:
                r = _bdot(y, w2k_ref[g])
                ms = jnp.sum(r * r, axis=-1, keepdims=True) * (1.0 / NSA_HEAD_DIM)
                lane = lax.broadcasted_iota(jnp.int32, r.shape, 1)
                last = lax.broadcasted_iota(jnp.int32, r.shape, 0) * CMP_STRIDE + (CMP_BLOCK - 1)
                kc_ref[0, g] = (r * lax.rsqrt(ms + RMS_EPS) * kg_ref[...] + _key_aug(lane, last)).astype(BF16)
            else:
                vct_ref[0, g] = _bdot_nt(w2vt_ref[g], y).astype(BF16)


def _cmp_weights(cmp_pe, cmp_w1, cmp_w2, k_norm_g):
    G, dh = NSA_KV_HEADS, NSA_HEAD_DIM
    L = cmp_w1.shape[0]
    width = CMP_STRIDE * G * dh
    eye = jnp.eye(G, dtype=F32)
    w1 = jnp.einsum('njhlde,gk->njhlgdke', cmp_w1.reshape(L, 2, 2, CMP_STRIDE, dh, dh), eye).reshape(
        L, 2, 2, width, G * dh).astype(BF16)
    pe = jnp.broadcast_to(cmp_pe.reshape(L, 2, 2, CMP_STRIDE, 1, dh), (L, 2, 2, CMP_STRIDE, G, dh))
    pe = jnp.broadcast_to(pe.reshape(L, 2, 2, 1, width), (L, 2, 2, SUBLANES, width)).astype(BF16)
    w2k = jnp.stack([jnp.zeros((L, G * dh, LANES), F32).at[:, g * dh:(g + 1) * dh, :dh].set(cmp_w2[:, 0])
                     for g in range(G)], axis=1).astype(BF16)
    w2vt = jnp.stack([jnp.zeros((L, dh, G * dh), F32).at[:, :, g * dh:(g + 1) * dh].set(
        jnp.swapaxes(cmp_w2[:, 1], 1, 2)) for g in range(G)], axis=1).astype(BF16)
    kg = jnp.pad(k_norm_g[:, 0], ((0, 0), (0, LANES - dh))).reshape(L, 1, LANES)
    return w1, pe, w2k, w2vt, kg


def _nsa_compress(u, B, T, cmp_weights, layer):
    G, dh = NSA_KV_HEADS, NSA_HEAD_DIM
    nseg = T // CMP_STRIDE
    of_layer = lambda w: pl.BlockSpec((None,) + w.shape[1:], lambda b: (layer,) + (0,) * (w.ndim - 1))
    kv0 = COL_KV // LANES
    return pl.pallas_call(
        _nsa_cmp_kernel,
        grid=(B,),
        in_specs=[pl.BlockSpec((T, LANES), lambda b: (b, kv0)), pl.BlockSpec((T, LANES), lambda b: (b, kv0 + 1))]
        + [of_layer(w) for w in cmp_weights],
        out_specs=[pl.BlockSpec((1, G, nseg, LANES), lambda b: (b, 0, 0, 0)),
                   pl.BlockSpec((1, G, dh, nseg), lambda b: (b, 0, 0, 0))],
        out_shape=[jax.ShapeDtypeStruct((B, G, nseg, LANES), BF16), jax.ShapeDtypeStruct((B, G, dh, nseg), BF16)],
        scratch_shapes=[pltpu.VMEM((2, T, LANES), F32)],
        compiler_params=_cparams("parallel"),
        name="nsa_compress",
    )(u, u, *cmp_weights)


def _softmax_keys(s, ok):
    s = jnp.where(ok, s, NEG_INF)
    m = jnp.max(s, axis=0, keepdims=True)
    m = jnp.where(m > NEG_INF, m, 0.0)
    p = jnp.exp(s - m)
    return p * (1.0 / jnp.maximum(jnp.sum(p, axis=0, keepdims=True), 1e-30))


def _exp_keys(s, ok):
    s = jnp.where(ok, s, NEG_INF)
    m = jnp.max(s, axis=0, keepdims=True)
    m = jnp.where(m > NEG_INF, m, 0.0)
    p = jnp.exp(s - m)
    return p, 1.0 / jnp.maximum(jnp.sum(p, axis=0, keepdims=True), 1e-30)


def _nsa_attn_kernel(q_ref, gt_ref, kc_ref, vct_ref, ks_ref, vst_ref, kw_ref, vwt_ref, o_ref, selt_ref, s_ref, *,
                     seq_len):
    G, R, dh = NSA_KV_HEADS, NSA_REP, NSA_HEAD_DIM
    TQ = LANES
    NB = seq_len // SLC_BLOCK
    key_r = lax.broadcasted_iota(jnp.int32, (TQ, TQ), 0)
    q_l = lax.broadcasted_iota(jnp.int32, (TQ, TQ), 1)
    ok_d = q_l >= key_r
    jb = lax.broadcasted_iota(jnp.int32, (NB, TQ), 0)
    row8 = lax.broadcasted_iota(jnp.int32, (SUBLANES, TQ), 0)
    nl = lax.broadcasted_iota(jnp.int32, (NB, TQ), 1)
    overlap_t = ((nl * CMP_STRIDE < jb * SLC_BLOCK + SLC_BLOCK)
                 & (nl * CMP_STRIDE + CMP_BLOCK > jb * SLC_BLOCK)).astype(BF16)
    wlen = (WIN // TQ + 1) * TQ
    wkey = lax.broadcasted_iota(jnp.int32, (wlen, TQ), 0)
    wq = lax.broadcasted_iota(jnp.int32, (wlen, TQ), 1)

    def heads(s, ok, fn):
        return [fn(s[:, r * TQ:(r + 1) * TQ], ok) for r in range(R)]

    def gate_rows(sub, g, branch):
        rows = [MISC_GATE0 + 3 * (g * R + r) + branch for r in range(R)]
        return jnp.concatenate([gt_ref[0, c:c + 1, sub * TQ:(sub + 1) * TQ] for c in rows], axis=1)

    def tile_scores(j, g, qa):
        off = pl.multiple_of(j * SLC_TILE, SLC_TILE)
        return _bdot_nt(ks_ref[0, g, pl.ds(off, SLC_TILE), :], qa)

    def front(sub):
        i = pl.program_id(1) * ATTN_TILES + sub
        t0 = i * TQ
        d0 = pl.multiple_of(t0, TQ)
        ok_c = (t0 + q_l) - (key_r * CMP_STRIDE + (CMP_BLOCK - 1)) >= 0
        cur = jnp.right_shift(t0 + nl, SLC_SHIFT)
        future = jb > cur
        forced = (jb == 0) | (jb == cur) | (jb == cur - 1)
        wstart = pl.multiple_of(jnp.maximum(i - WIN // TQ, 0) * TQ, TQ)
        dist_w = (t0 + wq) - (wstart + wkey)
        ok_w = (dist_w >= 0) & (dist_w < WIN)

        qa_l, o_fix_l, s_d_l, init = [], [], [], []
        for g in range(G):
            qs = q_ref[0, g * R:(g + 1) * R, sub * TQ:(sub + 1) * TQ, :].reshape(R * TQ, LANES)

            p_c = heads(_bdot_nt(kc_ref[0, g], qs), ok_c, _softmax_keys)
            o_cmp = jnp.dot(vct_ref[0, g], jnp.concatenate(p_c, axis=1).astype(BF16),
                            preferred_element_type=F32)

            pw = heads(_bdot_nt(kw_ref[0, g, pl.ds(wstart, wlen), :], qs), ok_w, _exp_keys)
            o_win = jnp.dot(vwt_ref[0, g, :, pl.ds(wstart, wlen)],
                            jnp.concatenate([p for p, _ in pw], axis=1).astype(BF16), preferred_element_type=F32)
            inv_w = jnp.concatenate([inv for _, inv in pw], axis=1)
            o_fix_l.append(gate_rows(sub, g, 0) * o_cmp + (gate_rows(sub, g, 2) * inv_w) * o_win)

            importance = _dot_sel_lhs(overlap_t, p_c[0] + p_c[1] + p_c[2] + p_c[3])
            score = jnp.where(future, NEG_INF, jnp.where(forced, FORCE_SCORE, importance))
            groups = [score[v * SUBLANES:(v + 1) * SUBLANES] for v in range(NB // SUBLANES)]
            ranks = [jnp.zeros((SUBLANES, TQ), F32) for _ in groups]
            for k in range(NB):
                sk = score[k:k + 1, :]
                for v, sv in enumerate(groups):
                    if v < k // SUBLANES:
                        ahead = sk > sv
                    elif v > k // SUBLANES:
                        ahead = sk >= sv
                    else:
                        ahead = (sk > sv) | ((sk == sv) & (row8 > k % SUBLANES))
                    ranks[v] = ranks[v] + jnp.where(ahead, 1.0, 0.0)
            rank = jnp.concatenate(ranks, axis=0)
            sel_t = (rank < SLC_TOPN) & (score > NEG_INF)
            selt_ref[sub, g] = jnp.where(sel_t, 1.0, 0.0)

            sel_d = jnp.where(key_r < SLC_BLOCK, selt_ref[sub, g, pl.ds(2 * i, 1), :],
                              selt_ref[sub, g, pl.ds(2 * i + 1, 1), :])
            s_d = jnp.concatenate(heads(_bdot_nt(ks_ref[0, g, pl.ds(d0, TQ), :], qs), (sel_d > 0.5) & ok_d,
                                        lambda s, ok: jnp.where(ok, s, NEG_INF)), axis=1)
            s_d_l.append(s_d)

            masked_t = jnp.where(sel_t & (jb < 2 * i), 0.0, 1.0)
            cols_t = jnp.concatenate([jnp.zeros((AUG_SEL, TQ), F32), masked_t,
                                      jnp.zeros((LANES - AUG_SEL - NB, TQ), F32)], axis=0)
            cols = cols_t.T.astype(BF16)
            qa_l.append(jnp.concatenate([qs[r * TQ:(r + 1) * TQ] + cols for r in range(R)], axis=0))

        for g in range(G):
            s_ref[sub, g] = tile_scores(0, g, qa_l[g])
            m = jnp.max(s_d_l[g], axis=0, keepdims=True)
            p = jnp.exp(s_d_l[g] - m)
            init.append((m, jnp.sum(p, axis=0, keepdims=True),
                         jnp.dot(vst_ref[0, g, :, pl.ds(d0, TQ)], p.astype(BF16), preferred_element_type=F32)))
        return (t0 + SLC_TILE - 1) // SLC_TILE, qa_l, o_fix_l, tuple(init)

    def key_loop(sub, n_tiles, qa_l, init):
        def slc_step(j, carry):
            off = pl.multiple_of(j * SLC_TILE, SLC_TILE)
            nxt = jnp.minimum(j + 1, n_tiles - 1)
            out = []
            for g in range(G):
                m, l, acc = carry[g]
                s = s_ref[sub, g]
                s_ref[sub, g] = tile_scores(nxt, g, qa_l[g])
                m_new = jnp.maximum(m, jnp.max(s, axis=0, keepdims=True))
                alpha = jnp.exp(m - m_new)
                p = jnp.exp(s - m_new)
                l = alpha * l + jnp.sum(p, axis=0, keepdims=True)
                acc = alpha * acc + jnp.dot(vst_ref[0, g, :, pl.ds(off, SLC_TILE)], p.astype(BF16),
                                            preferred_element_type=F32)
                out.append((m_new, l, acc))
            return tuple(out)

        return lax.fori_loop(0, n_tiles, slc_step, init)

    def back(sub, o_fix_l, slc):
        for g in range(G):
            _, l, acc = slc[g]
            o = o_fix_l[g] + (gate_rows(sub, g, 1) * (1.0 / jnp.maximum(l, 1e-30))) * acc
            for a in range(R // 2):
                pair = jnp.concatenate([o[:, (2 * a) * TQ:(2 * a + 1) * TQ],
                                        o[:, (2 * a + 1) * TQ:(2 * a + 2) * TQ]], axis=0)
                lo = (g * R + 2 * a) * dh
                o_ref[sub * TQ:(sub + 1) * TQ, lo:lo + 2 * dh] = pair.T.astype(BF16)

    fronts = [front(sub) for sub in range(ATTN_TILES)]
    loops = [key_loop(sub, n_tiles, qa_l, init) for sub, (n_tiles, qa_l, _, init) in enumerate(fronts)]
    for sub in range(ATTN_TILES):
        back(sub, fronts[sub][2], loops[sub])


def _nsa_attention(qn, gates_t, kc, vct, ksn, vst, kwn, vwt, B, T):
    G, H, dh = NSA_KV_HEADS, NSA_HEADS, NSA_HEAD_DIM
    TQ = ATTN_TILES * LANES
    nq = T // TQ
    nseg = kc.shape[2]
    per_b = lambda shape: pl.BlockSpec((1,) + shape, lambda b, i: (b, 0, 0, 0))
    return pl.pallas_call(
        functools.partial(_nsa_attn_kernel, seq_len=T),
        grid=(B, nq),
        in_specs=[pl.BlockSpec((1, H, TQ, LANES), lambda b, i: (b, 0, i, 0)),
                  pl.BlockSpec((1, GATE_ROWS, TQ), lambda b, i: (b, 0, i)),
                  per_b((G, nseg, LANES)), per_b((G, dh, nseg)),
                  per_b((G, T, LANES)), per_b((G, dh, T)), per_b((G, T, LANES)), per_b((G, dh, T))],
        out_specs=pl.BlockSpec((TQ, NSA_Q_W), lambda b, i: (b * nq + i, 0)),
        out_shape=jax.ShapeDtypeStruct((B * T, NSA_Q_W), BF16),
        scratch_shapes=[pltpu.VMEM((ATTN_TILES, G, T // SLC_BLOCK, LANES), F32),
                        pltpu.VMEM((ATTN_TILES, G, SLC_TILE, NSA_REP * LANES), F32)],
        compiler_params=_cparams("parallel", "arbitrary"),
        name="nsa_attention",
    )(qn, gates_t, kc, vct, ksn, vst, kwn, vwt)


def _nsa_mixer(u, misc, B, T, q_norm_g, k_norm_g, cmp_weights, layer):
    assert T % LANES == 0 and T >= (WIN // LANES + 1) * LANES and T // CMP_STRIDE == LANES
    assert SLC_BLOCK == 1 << SLC_SHIFT and LANES == 2 * SLC_BLOCK
    qn, ksn, vst, kwn, vwt, gates_t = _nsa_prep(u, misc, B, T, q_norm_g, k_norm_g)
    kc, vct = _nsa_compress(u, B, T, cmp_weights, layer)
    return _nsa_attention(qn, gates_t, kc, vct, ksn, vst, kwn, vwt, B, T)


def _merge_kernel(x_ref, h_ref, oa_ref, ob_ref, oc_ref, od_ref, mod_ref, wg_ref, wb_ref, wo_ref, y_ref):
    h = h_ref[...]
    merged = None
    for i, o_ref in enumerate((oa_ref, ob_ref, oc_ref, od_ref)):
        gate = _sigmoid(jnp.dot(h, wg_ref[i], preferred_element_type=F32))
        term = gate * jnp.dot(o_ref[...], wb_ref[i], preferred_element_type=F32)
        merged = term if merged is None else merged + term
    y_ref[...] = x_ref[...] + mod_ref[0, 2:3, :] * _bdot(merged, wo_ref[...])


def _merge(x2d, h, outs, mod_l, wg, wb, wo, T):
    M, D = x2d.shape
    tm = min(512, T)
    per_b = T // tm
    row = lambda w: pl.BlockSpec((tm, w), lambda m: (m, 0))
    const = lambda shape: pl.BlockSpec(shape, lambda m: (0,) * len(shape), pipeline_mode=pl.Buffered(1))
    return pl.pallas_call(
        _merge_kernel,
        grid=(M // tm,),
        in_specs=[row(D), row(D), row(MIX_W), row(MIX_W), row(MIX_W), row(MIX_W),
                  pl.BlockSpec((1, 6, D), lambda m: (m // per_b, 0, 0)),
                  const((N_BRANCH, D, D)), const((N_BRANCH, MIX_W, D)), const((D, D))],
        out_specs=row(D),
        out_shape=jax.ShapeDtypeStruct((M, D), F32),
        compiler_params=_cparams("parallel"),
        name="gated_merge_out_proj",
    )(x2d, h, *outs, mod_l, wg, wb, wo)


FFN_CHUNK = 256


def _ffn_kernel(x_ref, mod_ref, g_ref, wa_ref, wb_ref, wo_ref, y_ref, acc_ref):
    x = x_ref[...]
    y = x * lax.rsqrt(jnp.mean(x * x, axis=-1, keepdims=True) + RMS_EPS) * g_ref[...]
    h = (y * (1.0 + mod_ref[0, 4:5, :]) + mod_ref[0, 3:4, :]).astype(BF16)
    d_ff = wo_ref.shape[0]
    for c in range(d_ff // FFN_CHUNK):
        cols = slice(c * FFN_CHUNK, (c + 1) * FFN_CHUNK)
        a = jnp.dot(h, wa_ref[:, cols], preferred_element_type=F32)
        b = jnp.dot(h, wb_ref[:, cols], preferred_element_type=F32)
        part = _bdot(_silu(a) * b, wo_ref[cols, :])
        if c == 0:
            acc_ref[...] = part
        else:
            acc_ref[...] += part
    y_ref[...] = x + mod_ref[0, 5:6, :] * acc_ref[...]


def _ffn(x2d, mod_l, norm_g, w_in, w_out, T):
    M, D = x2d.shape
    d_ff = w_out.shape[0]
    assert d_ff % FFN_CHUNK == 0
    tm = min(1024, T)
    per_b = T // tm
    const = lambda shape, idx: pl.BlockSpec(shape, lambda m: idx, pipeline_mode=pl.Buffered(1))
    return pl.pallas_call(
        _ffn_kernel,
        grid=(M // tm,),
        in_specs=[pl.BlockSpec((tm, D), lambda m: (m, 0)),
                  pl.BlockSpec((1, 6, D), lambda m: (m // per_b, 0, 0)),
                  pl.BlockSpec((1, D), lambda m: (0, 0)),
                  const((D, d_ff), (0, 0)), const((D, d_ff), (0, 1)), const((d_ff, D), (0, 0))],
        out_specs=pl.BlockSpec((tm, D), lambda m: (m, 0)),
        out_shape=jax.ShapeDtypeStruct((M, D), F32),
        scratch_shapes=[pltpu.VMEM((tm, D), F32)],
        compiler_params=_cparams("parallel"),
        name="swiglu_ffn",
    )(x2d, mod_l, norm_g.reshape(1, D), w_in, w_in, w_out)


def _split_w_in(w_in):
    gt0 = SSD_IN + SC_IN + SG_IN + NSA_Q_W + 6 * NSA_KV_W
    wa = w_in[:, :, :MIX_W + SSD_XBC]
    wb = w_in[:, :, SSD_IN:gt0]
    wm = jnp.concatenate([w_in[:, :, MIX_W + SSD_XBC:SSD_IN], w_in[:, :, gt0:gt0 + 3 * NSA_HEADS]], axis=2)
    wm = jnp.pad(wm, ((0, 0), (0, 0), (0, U_WIDTH - COL_MISC - wm.shape[2])))
    assert wa.shape[2] == COL_XBC + SSD_XBC and wa.shape[2] + wb.shape[2] == COL_MISC
    return wa.astype(BF16), wb.astype(BF16), wm.astype(BF16)


def kernel(x, c, ada_w, ada_b, norm_mix_g, norm_ffn_g, w_in, ssd_conv_w, ssd_conv_b, ssd_dt_bias, ssd_a_log, ssd_d,
           ssd_norm_g, sc_conv_w, sg_norm_g, sg_w, sg_b, nsa_q_norm_g, nsa_k_norm_g, nsa_cmp_pe, nsa_cmp_w1,
           nsa_cmp_w2, w_branch, w_branch_gate, w_out, w_ffn_in, w_ffn_out):
    B, T, D = x.shape
    L = w_in.shape[0]
    mod = _modulation(c, ada_w, ada_b).reshape(L, B, 6, D)
    x2d = x.reshape(B * T, D)
    w_parts = _split_w_in(w_in)
    cmp_weights = _cmp_weights(nsa_cmp_pe, nsa_cmp_w1, nsa_cmp_w2, nsa_k_norm_g)
    for l in range(L):
        u, h, misc = _in_proj(x2d, mod[l], norm_mix_g[l], w_parts, l, T)
        outs = (
            _ssd_mixer(u, misc, B, T, ssd_conv_w[l], ssd_conv_b[l], ssd_dt_bias[l], ssd_a_log[l], ssd_d[l],
                       ssd_norm_g[l]),
            _short_conv_mixer(u, B, T, sc_conv_w[l]),
            _spatial_gating_mixer(u, B, T, sg_norm_g[l], sg_w[l], sg_b[l]),
            _nsa_mixer(u, misc, B, T, nsa_q_norm_g[l], nsa_k_norm_g[l], cmp_weights, l),
        )
        x2d = _merge(x2d, h, outs, mod[l], w_branch_gate[l].astype(BF16), w_branch[l].astype(BF16),
                     w_out[l].astype(BF16), T)
        x2d = _ffn(x2d, mod[l], norm_ffn_g[l], w_ffn_in[l].astype(BF16), w_ffn_out[l].astype(BF16), T)
    return x2d.reshape(B, T, D)
```

```python
import functools
import math

import jax
import jax.numpy as jnp
from jax import lax
from jax.experimental import pallas as pl
from jax.experimental.pallas import tpu as pltpu

F32 = jnp.float32
BF16 = jnp.bfloat16
RMS_EPS = 1e-6
NEG_INF = float("-inf")

MIX_W = 512
N_BRANCH = 4

SSD_HEAD_DIM = 64
SSD_HEADS = 8
SSD_GROUPS = 2
SSD_STATE = 128
SSD_CONV = 4
SSD_CHUNK = 128
SSD_STEP_CHUNKS = 8
SSD_XBC = MIX_W + 2 * SSD_GROUPS * SSD_STATE
SSD_IN = MIX_W + SSD_XBC + SSD_HEADS
SC_CONV = 3
SC_IN = 3 * MIX_W
SG_GROUPS = 4
SG_CHUNK = 128
SG_IN = 2 * MIX_W
NSA_HEADS = 8
NSA_KV_HEADS = 2
NSA_REP = NSA_HEADS // NSA_KV_HEADS
NSA_HEAD_DIM = 64
CMP_BLOCK = 32
CMP_STRIDE = 16
SLC_BLOCK = 64
SLC_TOPN = 8
WIN = 256
FORCE_SCORE = 1e9
NSA_KV_W = NSA_KV_HEADS * NSA_HEAD_DIM
NSA_Q_W = NSA_HEADS * NSA_HEAD_DIM
NSA_IN = NSA_Q_W + 6 * NSA_KV_W + 3 * NSA_HEADS

LANES = 128
SUBLANES = 8
VMEM_LIMIT_BYTES = 56 * 1024 * 1024

COL_Z = 0
COL_XBC = COL_Z + MIX_W
COL_SC = COL_XBC + SSD_XBC
COL_SG = COL_SC + SC_IN
COL_Q = COL_SG + SG_IN
COL_KV = COL_Q + NSA_Q_W
COL_MISC = COL_KV + 6 * NSA_KV_W
MISC_GATE0 = SSD_HEADS
GATE_ROWS = 32
SLC_SHIFT = 6
SLC_TILE = 512
ATTN_TILES = 4
AUG_POS = NSA_HEAD_DIM
AUG_SEL = AUG_POS + SUBLANES
MASK_SCORE = -(2.0 ** 100)
IN_CHUNK = 256
U_WIDTH = COL_MISC + IN_CHUNK


def _cparams(*sem):
    return pltpu.CompilerParams(dimension_semantics=sem, vmem_limit_bytes=VMEM_LIMIT_BYTES)


def _bdot(a, b):
    return jnp.dot(a.astype(BF16), b.astype(BF16), preferred_element_type=F32)


def _bdot_nt(a, b):
    return lax.dot_general(a.astype(BF16), b.astype(BF16), (((1,), (1,)), ((), ())),
                           preferred_element_type=F32)


def _split3(a):
    hi = a.astype(BF16)
    r1 = a - hi.astype(F32)
    mid = r1.astype(BF16)
    lo = (r1 - mid.astype(F32)).astype(BF16)
    return hi, mid, lo


def _dot_sel_rhs(a, sel):
    hi, mid, lo = _split3(a)
    return (jnp.dot(hi, sel, preferred_element_type=F32) + jnp.dot(mid, sel, preferred_element_type=F32)
            + jnp.dot(lo, sel, preferred_element_type=F32))


def _dot_sel_lhs(sel, a):
    hi, mid, lo = _split3(a)
    return (jnp.dot(sel, hi, preferred_element_type=F32) + jnp.dot(sel, mid, preferred_element_type=F32)
            + jnp.dot(sel, lo, preferred_element_type=F32))


def _sigmoid(x):
    return 1.0 / (1.0 + jnp.exp(-x))


def _silu(x):
    return x * _sigmoid(x)


def _gelu_tanh(x):
    c = math.sqrt(2.0 / math.pi)
    return 0.5 * x * (1.0 + jnp.tanh(c * (x + 0.044715 * (x * x * x))))


def _softplus(x):
    return jnp.maximum(x, 0.0) + jnp.log1p(jnp.exp(-jnp.abs(x)))


def _shift_rows(x, tail, k, row8):
    sh = pltpu.roll(x, k, 0)
    tl = pltpu.roll(tail, k, 0)
    top = jnp.where(row8 < k, tl, sh[0:SUBLANES])
    return jnp.concatenate([top, sh[SUBLANES:]], axis=0)


def _mod_kernel(c_ref, w_ref, b_ref, o_ref):
    o_ref[0] = _bdot(_silu(c_ref[...]), w_ref[0]) + b_ref[0]


def _modulation(c, ada_w, ada_b):
    L, D, D6 = ada_w.shape
    B = c.shape[0]
    tn = D6 // 4
    return pl.pallas_call(
        _mod_kernel,
        grid=(L, D6 // tn),
        in_specs=[pl.BlockSpec((B, D), lambda l, n: (0, 0)),
                  pl.BlockSpec((1, D, tn), lambda l, n: (l, 0, n)),
                  pl.BlockSpec((1, 1, tn), lambda l, n: (l, 0, n))],
        out_specs=pl.BlockSpec((1, B, tn), lambda l, n: (l, 0, n)),
        out_shape=jax.ShapeDtypeStruct((L, B, D6), F32),
        compiler_params=_cparams("parallel", "parallel"),
        name="adaln_modulation",
    )(c, ada_w, ada_b.reshape(L, 1, D6))


def _in_kernel(x_ref, mod_ref, g_ref, wa_ref, wb_ref, wm_ref, u_ref, h_ref, misc_ref):
    x = x_ref[...]
    y = x * lax.rsqrt(jnp.mean(x * x, axis=-1, keepdims=True) + RMS_EPS) * g_ref[...]
    h = (y * (1.0 + mod_ref[0, 1:2, :]) + mod_ref[0, 0:1, :]).astype(BF16)
    h_ref[...] = h
    col = 0
    for w_ref in (wa_ref, wb_ref, wm_ref):
        for n in range(w_ref.shape[2] // IN_CHUNK):
            u = jnp.dot(h, w_ref[0, :, n * IN_CHUNK:(n + 1) * IN_CHUNK], preferred_element_type=F32)
            u_ref[:, col:col + IN_CHUNK] = u.astype(BF16)
            if col == COL_MISC:
                misc_ref[...] = u[:, :LANES]
            col += IN_CHUNK


def _in_proj(x2d, mod_l, norm_g, w_parts, layer, T):
    M, D = x2d.shape
    tm = min(512, T)
    per_b = T // tm
    assert sum(w.shape[2] for w in w_parts) == U_WIDTH and all(w.shape[2] % IN_CHUNK == 0 for w in w_parts)
    wspec = lambda w: pl.BlockSpec((1, D, w.shape[2]), lambda m: (layer, 0, 0), pipeline_mode=pl.Buffered(1))
    return pl.pallas_call(
        _in_kernel,
        grid=(M // tm,),
        in_specs=[pl.BlockSpec((tm, D), lambda m: (m, 0)),
                  pl.BlockSpec((1, 6, D), lambda m: (m // per_b, 0, 0)),
                  pl.BlockSpec((1, D), lambda m: (0, 0))] + [wspec(w) for w in w_parts],
        out_specs=[pl.BlockSpec((tm, U_WIDTH), lambda m: (m, 0)),
                   pl.BlockSpec((tm, D), lambda m: (m, 0)),
                   pl.BlockSpec((tm, LANES), lambda m: (m, 0))],
        out_shape=[jax.ShapeDtypeStruct((M, U_WIDTH), BF16), jax.ShapeDtypeStruct((M, D), BF16),
                   jax.ShapeDtypeStruct((M, LANES), F32)],
        compiler_params=_cparams("parallel"),
        name="norm_in_proj",
    )(x2d, mod_l, norm_g.reshape(1, D), *w_parts)


def _ssd_kernel(z_ref, xa_ref, xb_ref, misc_ref, cw_ref, cb_ref, dtb_ref, alog_ref, dsk_ref, ng_ref,
                o_ref, tail_ref, st_ref):
    Q, P, N, H, G = SSD_CHUNK, SSD_HEAD_DIM, SSD_STATE, SSD_HEADS, SSD_GROUPS
    R = H // G

    @pl.when(pl.program_id(1) == 0)
    def _():
        tail_ref[...] = jnp.zeros_like(tail_ref)
        st_ref[...] = jnp.zeros_like(st_ref)

    xin = jnp.concatenate([xa_ref[...], xb_ref[...]], axis=1).astype(F32)
    tail = tail_ref[...]
    row8 = lax.broadcasted_iota(jnp.int32, (SUBLANES, SSD_XBC), 0)
    acc = xin * cw_ref[SSD_CONV - 1:SSD_CONV, :] + cb_ref[...]
    for k in range(1, SSD_CONV):
        acc = acc + _shift_rows(xin, tail, k, row8) * cw_ref[SSD_CONV - 1 - k:SSD_CONV - k, :]
    rows_in = xin.shape[0]
    tail_ref[...] = xin[rows_in - SUBLANES:rows_in, :]
    xbc = _silu(acc)

    lane = lax.broadcasted_iota(jnp.int32, (Q, LANES), 1)
    rowi = lax.broadcasted_iota(jnp.int32, (Q, LANES), 0)
    is_head = lane < H
    tri = (lane <= rowi).astype(BF16)
    tri_t = (rowi <= lane).astype(BF16)
    e_row = lax.broadcasted_iota(jnp.int32, (LANES, MIX_W), 0)
    e_col = lax.broadcasted_iota(jnp.int32, (LANES, MIX_W), 1)
    expand = (jnp.right_shift(e_col, 6) == e_row).astype(BF16)
    causal = lane <= rowi
    neg_a = -jnp.exp(alog_ref[...])

    for c in range(rows_in // Q):
        rows = slice(c * Q, (c + 1) * Q)
        xs = xbc[rows, :MIX_W]
        bm = xbc[rows, MIX_W:MIX_W + G * N]
        cm = xbc[rows, MIX_W + G * N:]
        dt = jnp.where(is_head, _softplus(misc_ref[rows, :] + dtb_ref[...]), 0.0)
        a = dt * neg_a
        a_cs = _dot_sel_lhs(tri, a)
        a_cs_t = _dot_sel_rhs(a.T, tri_t)
        a_last = a_cs[Q - 1:Q, :]
        ea = jnp.exp(a_cs)
        dec = jnp.exp(a_last - a_cs)
        xdt = xs * _dot_sel_rhs(dt, expand)
        ea_e = _dot_sel_rhs(ea, expand)
        xdec = xdt * _dot_sel_rhs(dec, expand)

        ys = []
        for g in range(G):
            bg = bm[:, g * N:(g + 1) * N]
            cg = cm[:, g * N:(g + 1) * N].astype(BF16)
            cb = _bdot_nt(cg, bg)
            bg_t = bg.T.astype(BF16)
            for r in range(R):
                h = g * R + r
                seg = jnp.where(causal, a_cs[:, h:h + 1] - a_cs_t[h:h + 1, :], NEG_INF)
                y_diag = _bdot(cb * jnp.exp(seg), xdt[:, h * P:(h + 1) * P])
                state = st_ref[h]
                y_off = _bdot(cg, state) * ea_e[:, h * P:(h + 1) * P]
                st_ref[h] = state * jnp.exp(a_last[:, h:h + 1]) + _bdot(bg_t, xdec[:, h * P:(h + 1) * P])
                ys.append(y_diag + y_off)
        y = jnp.concatenate(ys, axis=1) + xs * dsk_ref[...]
        y = y * _silu(z_ref[rows, :].astype(F32))
        gw = MIX_W // G
        outs = []
        for g in range(G):
            yg = y[:, g * gw:(g + 1) * gw]
            outs.append(yg * lax.rsqrt(jnp.mean(yg * yg, axis=-1, keepdims=True) + RMS_EPS))
        o_ref[rows, :] = (jnp.concatenate(outs, axis=1) * ng_ref[...]).astype(BF16)


def _ssd_mixer(u, misc, B, T, conv_w, conv_b, dt_bias, a_log, d_skip, norm_g):
    Q = min(SSD_STEP_CHUNKS * SSD_CHUNK, T)
    nc = T // Q

    def pad_lane(v):
        return jnp.pad(v, (0, LANES - v.shape[0])).reshape(1, LANES)

    row = lambda b, c: b * nc + c
    full = lambda shape: pl.BlockSpec(shape, lambda b, c: (0,) * len(shape))
    return pl.pallas_call(
        _ssd_kernel,
        grid=(B, nc),
        in_specs=[pl.BlockSpec((Q, MIX_W), lambda b, c: (row(b, c), COL_Z // MIX_W)),
                  pl.BlockSpec((Q, MIX_W), lambda b, c: (row(b, c), COL_XBC // MIX_W)),
                  pl.BlockSpec((Q, MIX_W), lambda b, c: (row(b, c), COL_XBC // MIX_W + 1)),
                  pl.BlockSpec((Q, LANES), lambda b, c: (row(b, c), 0)),
                  full((SSD_CONV, SSD_XBC)), full((1, SSD_XBC)), full((1, LANES)), full((1, LANES)),
                  full((1, MIX_W)), full((1, MIX_W))],
        out_specs=pl.BlockSpec((Q, MIX_W), lambda b, c: (row(b, c), 0)),
        out_shape=jax.ShapeDtypeStruct((B * T, MIX_W), BF16),
        scratch_shapes=[pltpu.VMEM((SUBLANES, SSD_XBC), F32),
                        pltpu.VMEM((SSD_HEADS, SSD_STATE, SSD_HEAD_DIM), F32)],
        compiler_params=_cparams("parallel", "arbitrary"),
        name="ssd_mixer",
    )(u, u, u, misc, conv_w, conv_b.reshape(1, SSD_XBC), pad_lane(dt_bias), pad_lane(a_log),
      jnp.repeat(d_skip, SSD_HEAD_DIM).reshape(1, MIX_W), norm_g.reshape(1, MIX_W))


def _sc_kernel(b_ref, c_ref, h_ref, w_ref, o_ref, tail_ref):
    @pl.when(pl.program_id(1) == 0)
    def _():
        tail_ref[...] = jnp.zeros_like(tail_ref)

    cx = c_ref[...].astype(F32) * h_ref[...].astype(F32)
    tt = cx.shape[0]
    tail = tail_ref[...]
    row8 = lax.broadcasted_iota(jnp.int32, (SUBLANES, MIX_W), 0)
    acc = cx * w_ref[SC_CONV - 1:SC_CONV, :]
    for k in range(1, SC_CONV):
        acc = acc + _shift_rows(cx, tail, k, row8) * w_ref[SC_CONV - 1 - k:SC_CONV - k, :]
    tail_ref[...] = cx[tt - SUBLANES:tt, :]
    o_ref[...] = (b_ref[...].astype(F32) * acc).astype(BF16)


def _short_conv_mixer(u, B, T, conv_w):
    tt = min(512, T)
    nt = T // tt
    c0 = COL_SC // MIX_W
    spec = lambda j: pl.BlockSpec((tt, MIX_W), lambda b, i: (b * nt + i, c0 + j))
    return pl.pallas_call(
        _sc_kernel,
        grid=(B, nt),
        in_specs=[spec(0), spec(1), spec(2), pl.BlockSpec((SC_CONV, MIX_W), lambda b, i: (0, 0))],
        out_specs=pl.BlockSpec((tt, MIX_W), lambda b, i: (b * nt + i, 0)),
        out_shape=jax.ShapeDtypeStruct((B * T, MIX_W), BF16),
        scratch_shapes=[pltpu.VMEM((SUBLANES, MIX_W), F32)],
        compiler_params=_cparams("parallel", "arbitrary"),
        name="short_conv_mixer",
    )(u, u, u, conv_w)


def _sg_kernel(u_ref, v_ref, ng_ref, w_ref, bias_ref, o_ref):
    Q = SG_CHUNK
    rowi = lax.broadcasted_iota(jnp.int32, (Q, Q), 0)
    coli = lax.broadcasted_iota(jnp.int32, (Q, Q), 1)
    gd = MIX_W // SG_GROUPS
    ws = [jnp.where(coli <= rowi, w_ref[g], 0.0).astype(BF16) for g in range(SG_GROUPS)]
    for c in range(u_ref.shape[0] // Q):
        rows = slice(c * Q, (c + 1) * Q)
        v = _gelu_tanh(v_ref[rows, :].astype(F32))
        v = (v * lax.rsqrt(jnp.mean(v * v, axis=-1, keepdims=True) + RMS_EPS) * ng_ref[...]).astype(BF16)
        mixed = jnp.concatenate([jnp.dot(ws[g], v[:, g * gd:(g + 1) * gd], preferred_element_type=F32)
                                 for g in range(SG_GROUPS)], axis=1)
        o_ref[rows, :] = (_gelu_tanh(u_ref[rows, :].astype(F32)) * (mixed + bias_ref[...])).astype(BF16)


def _spatial_gating_mixer(u, B, T, norm_g, w_s, b_s):
    Q = SG_CHUNK
    tt = min(4 * Q, T)
    nt = T // tt
    c0 = COL_SG // MIX_W
    bias = jnp.repeat(b_s.T, MIX_W // SG_GROUPS, axis=1)
    return pl.pallas_call(
        _sg_kernel,
        grid=(B, nt),
        in_specs=[pl.BlockSpec((tt, MIX_W), lambda b, c: (b * nt + c, c0)),
                  pl.BlockSpec((tt, MIX_W), lambda b, c: (b * nt + c, c0 + 1)),
                  pl.BlockSpec((1, MIX_W), lambda b, c: (0, 0)),
                  pl.BlockSpec((SG_GROUPS, Q, Q), lambda b, c: (0, 0, 0)),
                  pl.BlockSpec((Q, MIX_W), lambda b, c: (0, 0))],
        out_specs=pl.BlockSpec((tt, MIX_W), lambda b, c: (b * nt + c, 0)),
        out_shape=jax.ShapeDtypeStruct((B * T, MIX_W), BF16),
        compiler_params=_cparams("parallel", "parallel"),
        name="spatial_gating_mixer",
    )(u, u, norm_g.reshape(1, MIX_W), w_s, bias)


def _group_mean_sq(x, width):
    r = lax.broadcasted_iota(jnp.int32, (LANES, LANES), 0)
    c = lax.broadcasted_iota(jnp.int32, (LANES, LANES), 1)
    sh = width.bit_length() - 1
    same = (jnp.right_shift(r, sh) == jnp.right_shift(c, sh)).astype(BF16)
    x2 = x * x
    slabs = [_dot_sel_rhs(x2[:, j * LANES:(j + 1) * LANES], same) for j in range(x.shape[1] // LANES)]
    return jnp.concatenate(slabs, axis=1) * (1.0 / width)


def _key_aug(lane, pos):
    return jnp.where((lane == AUG_POS) | (lane == AUG_POS + 1), 1.0,
                     jnp.where(lane == AUG_POS + 2, -(pos & ~(LANES - 1)).astype(F32),
                               jnp.where(lane == AUG_POS + 3, -(pos & (LANES - 1)).astype(F32), 0.0)))


def _nsa_prep_kernel(q_ref, ks_ref, vs_ref, kw_ref, vw_ref, misc_ref, qg_ref, ksg_ref, kwg_ref,
                     qa_ref, ksa_ref, vst_ref, kwa_ref, vwt_ref, gate_ref):
    dh = NSA_HEAD_DIM
    tt = q_ref.shape[0]
    lane = lax.broadcasted_iota(jnp.int32, (tt, LANES), 1)
    pos = pl.program_id(1) * tt + lax.broadcasted_iota(jnp.int32, (tt, LANES), 0)
    is_feat = lane < dh

    q = q_ref[...].astype(F32)
    qn = q * lax.rsqrt(_group_mean_sq(q, dh) + RMS_EPS) * qg_ref[...] * (dh ** -0.5)
    q_pos = jnp.where(lane == AUG_POS, (pos & ~(LANES - 1)).astype(F32),
                      jnp.where(lane == AUG_POS + 1, (pos & (LANES - 1)).astype(F32),
                                jnp.where((lane == AUG_POS + 2) | (lane == AUG_POS + 3), 1.0, 0.0)))
    for h in range(NSA_HEADS):
        pair = qn[:, (h // 2) * LANES:(h // 2 + 1) * LANES]
        feat = pair if h % 2 == 0 else pltpu.roll(pair, dh, 1)
        qa_ref[0, h] = jnp.where(is_feat, feat, -(2.0 ** -(h + 1)) * q_pos).astype(BF16)

    ks = ks_ref[...].astype(F32)
    ksn = ks * lax.rsqrt(_group_mean_sq(ks, dh) + RMS_EPS) * ksg_ref[...]
    kw = kw_ref[...].astype(F32)
    kwn = kw * lax.rsqrt(_group_mean_sq(kw, dh) + RMS_EPS) * kwg_ref[...]
    k_pos = _key_aug(lane, pos)
    k_pos_sel = jnp.where(lane == AUG_SEL + jnp.right_shift(pos, SLC_SHIFT), MASK_SCORE, k_pos)
    vs_t = vs_ref[...].astype(F32).T.astype(BF16)
    vw_t = vw_ref[...].astype(F32).T.astype(BF16)
    for g in range(NSA_KV_HEADS):
        sl = slice(g * dh, (g + 1) * dh)
        ksa_ref[0, g] = jnp.where(is_feat, ksn if g == 0 else pltpu.roll(ksn, dh, 1), k_pos_sel).astype(BF16)
        kwa_ref[0, g] = jnp.where(is_feat, kwn if g == 0 else pltpu.roll(kwn, dh, 1), k_pos).astype(BF16)
        vst_ref[0, g] = vs_t[sl, :]
        vwt_ref[0, g] = vw_t[sl, :]
    gate_ref[0] = _sigmoid(misc_ref[...]).T[0:GATE_ROWS, :]


def _nsa_prep(u, misc, B, T, q_norm_g, k_norm_g):
    tt = min(512, T)
    nt = T // tt
    G, H, dh = NSA_KV_HEADS, NSA_HEADS, NSA_HEAD_DIM
    kv0 = COL_KV // LANES
    kvspec = lambda j: pl.BlockSpec((tt, LANES), lambda b, i: (b * nt + i, kv0 + j))
    vec = lambda n: pl.BlockSpec((1, n), lambda b, i: (0, 0))
    kspec = pl.BlockSpec((1, G, tt, LANES), lambda b, i: (b, 0, i, 0))
    kshape = jax.ShapeDtypeStruct((B, G, T, LANES), BF16)
    vspec = pl.BlockSpec((1, G, dh, tt), lambda b, i: (b, 0, 0, i))
    vshape = jax.ShapeDtypeStruct((B, G, dh, T), BF16)
    return pl.pallas_call(
        _nsa_prep_kernel,
        grid=(B, nt),
        in_specs=[pl.BlockSpec((tt, NSA_Q_W), lambda b, i: (b * nt + i, COL_Q // NSA_Q_W)),
                  kvspec(2), kvspec(3), kvspec(4), kvspec(5),
                  pl.BlockSpec((tt, LANES), lambda b, i: (b * nt + i, 0)),
                  vec(NSA_Q_W), vec(LANES), vec(LANES)],
        out_specs=[pl.BlockSpec((1, H, tt, LANES), lambda b, i: (b, 0, i, 0)),
                   kspec, vspec, kspec, vspec,
                   pl.BlockSpec((1, GATE_ROWS, tt), lambda b, i: (b, 0, i))],
        out_shape=[jax.ShapeDtypeStruct((B, H, T, LANES), BF16),
                   kshape, vshape, kshape, vshape,
                   jax.ShapeDtypeStruct((B, GATE_ROWS, T), F32)],
        compiler_params=_cparams("parallel", "parallel"),
        name="nsa_prep",
    )(u, u, u, u, u, misc, jnp.tile(q_norm_g, NSA_HEADS).reshape(1, NSA_Q_W),
      jnp.tile(k_norm_g[1], G).reshape(1, LANES), jnp.tile(k_norm_g[2], G).reshape(1, LANES))


def _nsa_cmp_kernel(kin_ref, vin_ref, w1_ref, pe_ref, w2k_ref, w2vt_ref, kg_ref, kc_ref, vct_ref, x_ref):
    G = NSA_KV_HEADS
    nseg = kin_ref.shape[0] // CMP_STRIDE
    x_ref[0] = kin_ref[...].astype(F32)
    x_ref[1] = vin_ref[...].astype(F32)
    for j in range(2):
        a = jnp.concatenate([x_ref[j, pl.ds(l, nseg, stride=CMP_STRIDE), :] for l in range(CMP_STRIDE)],
                            axis=1).astype(BF16)
        pe_term = (jnp.dot(pe_ref[j, 0], w1_ref[j, 0], preferred_element_type=F32)
                   + jnp.dot(pe_ref[j, 1], w1_ref[j, 1], preferred_element_type=F32))[0:1, :]
        lo = jnp.dot(a, w1_ref[j, 0], preferred_element_type=F32)
        hi = jnp.dot(a, w1_ref[j, 1], preferred_element_type=F32)
        y = _gelu_tanh(lo + pltpu.roll(hi, nseg - 1, 0) + pe_term)
        for g in range(G):
            if j == 0:
                r = _bdot(y, w2k_ref[g])
                ms = jnp.sum(r * r, axis=-1, keepdims=True) * (1.0 / NSA_HEAD_DIM)
                lane = lax.broadcasted_iota(jnp.int32, r.shape, 1)
                last = lax.broadcasted_iota(jnp.int32, r.shape, 0) * CMP_STRIDE + (CMP_BLOCK - 1)
                kc_ref[0, g] = (r * lax.rsqrt(ms + RMS_EPS) * kg_ref[...] + _key_aug(lane, last)).astype(BF16)
            else:
                vct_ref[0, g] = _bdot_nt(w2vt_ref[g], y).astype(BF16)


def _cmp_weights(cmp_pe, cmp_w1, cmp_w2, k_norm_g):
    G, dh = NSA_KV_HEADS, NSA_HEAD_DIM
    L = cmp_w1.shape[0]
    width = CMP_STRIDE * G * dh
    w1 = cmp_w1.reshape(L, 2, 2, CMP_STRIDE, dh, dh).astype(BF16)
    w1 = jnp.stack([jnp.pad(w1, ((0, 0),) * 5 + ((g * dh, (G - 1 - g) * dh),)) for g in range(G)], axis=4)
    w1 = w1.reshape(L, 2, 2, width, G * dh)
    pe = jnp.broadcast_to(cmp_pe.reshape(L, 2, 2, CMP_STRIDE, 1, dh), (L, 2, 2, CMP_STRIDE, G, dh))
    pe = jnp.broadcast_to(pe.reshape(L, 2, 2, 1, width), (L, 2, 2, SUBLANES, width)).astype(BF16)
    w2k = jnp.stack([jnp.zeros((L, G * dh, LANES), F32).at[:, g * dh:(g + 1) * dh, :dh].set(cmp_w2[:, 0])
                     for g in range(G)], axis=1).astype(BF16)
    w2vt = jnp.stack([jnp.zeros((L, dh, G * dh), F32).at[:, :, g * dh:(g + 1) * dh].set(
        jnp.swapaxes(cmp_w2[:, 1], 1, 2)) for g in range(G)], axis=1).astype(BF16)
    kg = jnp.pad(k_norm_g[:, 0], ((0, 0), (0, LANES - dh))).reshape(L, 1, LANES)
    return w1, pe, w2k, w2vt, kg


def _nsa_compress(u, B, T, cmp_weights, layer):
    G, dh = NSA_KV_HEADS, NSA_HEAD_DIM
    nseg = T // CMP_STRIDE
    of_layer = lambda w: pl.BlockSpec((None,) + w.shape[1:], lambda b: (layer,) + (0,) * (w.ndim - 1))
    kv0 = COL_KV // LANES
    return pl.pallas_call(
        _nsa_cmp_kernel,
        grid=(B,),
        in_specs=[pl.BlockSpec((T, LANES), lambda b: (b, kv0)), pl.BlockSpec((T, LANES), lambda b: (b, kv0 + 1))]
        + [of_layer(w) for w in cmp_weights],
        out_specs=[pl.BlockSpec((1, G, nseg, LANES), lambda b: (b, 0, 0, 0)),
                   pl.BlockSpec((1, G, dh, nseg), lambda b: (b, 0, 0, 0))],
        out_shape=[jax.ShapeDtypeStruct((B, G, nseg, LANES), BF16), jax.ShapeDtypeStruct((B, G, dh, nseg), BF16)],
        scratch_shapes=[pltpu.VMEM((2, T, LANES), F32)],
        compiler_params=_cparams("parallel"),
        name="nsa_compress",
    )(u, u, *cmp_weights)


def _softmax_keys(s, ok):
    s = jnp.where(ok, s, NEG_INF)
    m = jnp.max(s, axis=0, keepdims=True)
    m = jnp.where(m > NEG_INF, m, 0.0)
    p = jnp.exp(s - m)
    return p * (1.0 / jnp.maximum(jnp.sum(p, axis=0, keepdims=True), 1e-30))


def _exp_keys(s, ok):
    s = jnp.where(ok, s, NEG_INF)
    m = jnp.max(s, axis=0, keepdims=True)
    m = jnp.where(m > NEG_INF, m, 0.0)
    p = jnp.exp(s - m)
    return p, 1.0 / jnp.maximum(jnp.sum(p, axis=0, keepdims=True), 1e-30)


def _nsa_attn_kernel(q_ref, gt_ref, kc_ref, vct_ref, ks_ref, vst_ref, kw_ref, vwt_ref, o_ref, selt_ref, s_ref, *,
                     seq_len):
    G, R, dh = NSA_KV_HEADS, NSA_REP, NSA_HEAD_DIM
    TQ = LANES
    NB = seq_len // SLC_BLOCK
    key_r = lax.broadcasted_iota(jnp.int32, (TQ, TQ), 0)
    q_l = lax.broadcasted_iota(jnp.int32, (TQ, TQ), 1)
    ok_d = q_l >= key_r
    jb = lax.broadcasted_iota(jnp.int32, (NB, TQ), 0)
    row8 = lax.broadcasted_iota(jnp.int32, (SUBLANES, TQ), 0)
    nl = lax.broadcasted_iota(jnp.int32, (NB, TQ), 1)
    overlap_t = ((nl * CMP_STRIDE < jb * SLC_BLOCK + SLC_BLOCK)
                 & (nl * CMP_STRIDE + CMP_BLOCK > jb * SLC_BLOCK)).astype(BF16)
    wlen = (WIN // TQ + 1) * TQ
    wkey = lax.broadcasted_iota(jnp.int32, (wlen, TQ), 0)
    wq = lax.broadcasted_iota(jnp.int32, (wlen, TQ), 1)

    def heads(s, ok, fn):
        return [fn(s[:, r * TQ:(r + 1) * TQ], ok) for r in range(R)]

    def gate_rows(sub, g, branch):
        rows = [MISC_GATE0 + 3 * (g * R + r) + branch for r in range(R)]
        return jnp.concatenate([gt_ref[0, c:c + 1, sub * TQ:(sub + 1) * TQ] for c in rows], axis=1)

    def tile_scores(j, g, qa):
        off = pl.multiple_of(j * SLC_TILE, SLC_TILE)
        return _bdot_nt(ks_ref[0, g, pl.ds(off, SLC_TILE), :], qa)

    def front(sub):
        i = pl.program_id(1) * ATTN_TILES + sub
        t0 = i * TQ
        d0 = pl.multiple_of(t0, TQ)
        ok_c = (t0 + q_l) - (key_r * CMP_STRIDE + (CMP_BLOCK - 1)) >= 0
        cur = jnp.right_shift(t0 + nl, SLC_SHIFT)
        future = jb > cur
        forced = (jb == 0) | (jb == cur) | (jb == cur - 1)
        wstart = pl.multiple_of(jnp.maximum(i - WIN // TQ, 0) * TQ, TQ)
        dist_w = (t0 + wq) - (wstart + wkey)
        ok_w = (dist_w >= 0) & (dist_w < WIN)

        qa_l, o_fix_l, s_d_l, init = [], [], [], []
        for g in range(G):
            qs = q_ref[0, g * R:(g + 1) * R, sub * TQ:(sub + 1) * TQ, :].reshape(R * TQ, LANES)

            p_c = heads(_bdot_nt(kc_ref[0, g], qs), ok_c, _softmax_keys)
            o_cmp = jnp.dot(vct_ref[0, g], jnp.concatenate(p_c, axis=1).astype(BF16),
                            preferred_element_type=F32)

            pw = heads(_bdot_nt(kw_ref[0, g, pl.ds(wstart, wlen), :], qs), ok_w, _exp_keys)
            o_win = jnp.dot(vwt_ref[0, g, :, pl.ds(wstart, wlen)],
                            jnp.concatenate([p for p, _ in pw], axis=1).astype(BF16), preferred_element_type=F32)
            inv_w = jnp.concatenate([inv for _, inv in pw], axis=1)
            o_fix_l.append(gate_rows(sub, g, 0) * o_cmp + (gate_rows(sub, g, 2) * inv_w) * o_win)

            importance = _dot_sel_lhs(overlap_t, p_c[0] + p_c[1] + p_c[2] + p_c[3])
            score = jnp.where(future, NEG_INF, jnp.where(forced, FORCE_SCORE, importance))
            groups = [score[v * SUBLANES:(v + 1) * SUBLANES] for v in range(NB // SUBLANES)]
            ranks = [jnp.zeros((SUBLANES, TQ), F32) for _ in groups]
            for k in range(NB):
                sk = score[k:k + 1, :]
                for v, sv in enumerate(groups):
                    if v < k // SUBLANES:
                        ahead = sk > sv
                    elif v > k // SUBLANES:
                        ahead = sk >= sv
                    else:
                        ahead = (sk > sv) | ((sk == sv) & (row8 > k % SUBLANES))
                    ranks[v] = ranks[v] + jnp.where(ahead, 1.0, 0.0)
            rank = jnp.concatenate(ranks, axis=0)
            sel_t = (rank < SLC_TOPN) & (score > NEG_INF)
            selt_ref[sub, g] = jnp.where(sel_t, 1.0, 0.0)

            sel_d = jnp.where(key_r < SLC_BLOCK, selt_ref[sub, g, pl.ds(2 * i, 1), :],
                              selt_ref[sub, g, pl.ds(2 * i + 1, 1), :])
            s_d = jnp.concatenate(heads(_bdot_nt(ks_ref[0, g, pl.ds(d0, TQ), :], qs), (sel_d > 0.5) & ok_d,
                                        lambda s, ok: jnp.where(ok, s, NEG_INF)), axis=1)
            s_d_l.append(s_d)

            masked_t = jnp.where(sel_t & (jb < 2 * i), 0.0, 1.0)
            cols_t = jnp.concatenate([jnp.zeros((AUG_SEL, TQ), F32), masked_t,
                                      jnp.zeros((LANES - AUG_SEL - NB, TQ), F32)], axis=0)
            cols = cols_t.T.astype(BF16)
            qa_l.append(jnp.concatenate([qs[r * TQ:(r + 1) * TQ] + cols for r in range(R)], axis=0))

        for g in range(G):
            s_ref[sub, g] = tile_scores(0, g, qa_l[g])
            m = jnp.max(s_d_l[g], axis=0, keepdims=True)
            p = jnp.exp(s_d_l[g] - m)
            init.append((m, jnp.sum(p, axis=0, keepdims=True),
                         jnp.dot(vst_ref[0, g, :, pl.ds(d0, TQ)], p.astype(BF16), preferred_element_type=F32)))
        return (t0 + SLC_TILE - 1) // SLC_TILE, qa_l, o_fix_l, tuple(init)

    def key_loop(sub, n_tiles, qa_l, init):
        def slc_step(j, carry):
            off = pl.multiple_of(j * SLC_TILE, SLC_TILE)
            nxt = jnp.minimum(j + 1, n_tiles - 1)
            out = []
            for g in range(G):
                m, l, acc = carry[g]
                s = s_ref[sub, g]
                s_ref[sub, g] = tile_scores(nxt, g, qa_l[g])
                m_new = jnp.maximum(m, jnp.max(s, axis=0, keepdims=True))
                alpha = jnp.exp(m - m_new)
                p = jnp.exp(s - m_new)
                l = alpha * l + jnp.sum(p, axis=0, keepdims=True)
                acc = alpha * acc + jnp.dot(vst_ref[0, g, :, pl.ds(off, SLC_TILE)], p.astype(BF16),
                                            preferred_element_type=F32)
                out.append((m_new, l, acc))
            return tuple(out)

        return lax.fori_loop(0, n_tiles, slc_step, init)

    def back(sub, o_fix_l, slc):
        for g in range(G):
            _, l, acc = slc[g]
            o = o_fix_l[g] + (gate_rows(sub, g, 1) * (1.0 / jnp.maximum(l, 1e-30))) * acc
            for a in range(R // 2):
                pair = jnp.concatenate([o[:, (2 * a) * TQ:(2 * a + 1) * TQ],
                                        o[:, (2 * a + 1) * TQ:(2 * a + 2) * TQ]], axis=0)
                lo = (g * R + 2 * a) * dh
                o_ref[sub * TQ:(sub + 1) * TQ, lo:lo + 2 * dh] = pair.T.astype(BF16)

    fronts = [front(sub) for sub in range(ATTN_TILES)]
    loops = [key_loop(sub, n_tiles, qa_l, init) for sub, (n_tiles, qa_l, _, init) in enumerate(fronts)]
    for sub in range(ATTN_TILES):
        back(sub, fronts[sub][2], loops[sub])


def _nsa_attention(qn, gates_t, kc, vct, ksn, vst, kwn, vwt, B, T):
    G, H, dh = NSA_KV_HEADS, NSA_HEADS, NSA_HEAD_DIM
    TQ = ATTN_TILES * LANES
    nq = T // TQ
    nseg = kc.shape[2]
    per_b = lambda shape: pl.BlockSpec((1,) + shape, lambda b, i: (b, 0, 0, 0))
    return pl.pallas_call(
        functools.partial(_nsa_attn_kernel, seq_len=T),
        grid=(B, nq),
        in_specs=[pl.BlockSpec((1, H, TQ, LANES), lambda b, i: (b, 0, i, 0)),
                  pl.BlockSpec((1, GATE_ROWS, TQ), lambda b, i: (b, 0, i)),
                  per_b((G, nseg, LANES)), per_b((G, dh, nseg)),
                  per_b((G, T, LANES)), per_b((G, dh, T)), per_b((G, T, LANES)), per_b((G, dh, T))],
        out_specs=pl.BlockSpec((TQ, NSA_Q_W), lambda b, i: (b * nq + i, 0)),
        out_shape=jax.ShapeDtypeStruct((B * T, NSA_Q_W), BF16),
        scratch_shapes=[pltpu.VMEM((ATTN_TILES, G, T // SLC_BLOCK, LANES), F32),
                        pltpu.VMEM((ATTN_TILES, G, SLC_TILE, NSA_REP * LANES), F32)],
        compiler_params=_cparams("parallel", "arbitrary"),
        name="nsa_attention",
    )(qn, gates_t, kc, vct, ksn, vst, kwn, vwt)


def _nsa_mixer(u, misc, B, T, q_norm_g, k_norm_g, cmp_weights, layer):
    assert T % LANES == 0 and T >= (WIN // LANES + 1) * LANES and T // CMP_STRIDE == LANES
    assert SLC_BLOCK == 1 << SLC_SHIFT and LANES == 2 * SLC_BLOCK
    qn, ksn, vst, kwn, vwt, gates_t = _nsa_prep(u, misc, B, T, q_norm_g, k_norm_g)
    kc, vct = _nsa_compress(u, B, T, cmp_weights, layer)
    return _nsa_attention(qn, gates_t, kc, vct, ksn, vst, kwn, vwt, B, T)


def _merge_kernel(x_ref, h_ref, oa_ref, ob_ref, oc_ref, od_ref, mod_ref, wg_ref, wb_ref, wo_ref, y_ref):
    h = h_ref[...]
    merged = None
    for i, o_ref in enumerate((oa_ref, ob_ref, oc_ref, od_ref)):
        gate = _sigmoid(jnp.dot(h, wg_ref[i], preferred_element_type=F32))
        term = gate * jnp.dot(o_ref[...], wb_ref[i], preferred_element_type=F32)
        merged = term if merged is None else merged + term
    y_ref[...] = x_ref[...] + mod_ref[0, 2:3, :] * _bdot(merged, wo_ref[...])


def _merge(x2d, h, outs, mod_l, wg, wb, wo, T):
    M, D = x2d.shape
    tm = min(512, T)
    per_b = T // tm
    row = lambda w: pl.BlockSpec((tm, w), lambda m: (m, 0))
    const = lambda shape: pl.BlockSpec(shape, lambda m: (0,) * len(shape), pipeline_mode=pl.Buffered(1))
    return pl.pallas_call(
        _merge_kernel,
        grid=(M // tm,),
        in_specs=[row(D), row(D), row(MIX_W), row(MIX_W), row(MIX_W), row(MIX_W),
                  pl.BlockSpec((1, 6, D), lambda m: (m // per_b, 0, 0)),
                  const((N_BRANCH, D, D)), const((N_BRANCH, MIX_W, D)), const((D, D))],
        out_specs=row(D),
        out_shape=jax.ShapeDtypeStruct((M, D), F32),
        compiler_params=_cparams("parallel"),
        name="gated_merge_out_proj",
    )(x2d, h, *outs, mod_l, wg, wb, wo)


FFN_CHUNK = 256


def _ffn_kernel(x_ref, mod_ref, g_ref, wa_ref, wb_ref, wo_ref, y_ref, acc_ref):
    x = x_ref[...]
    y = x * lax.rsqrt(jnp.mean(x * x, axis=-1, keepdims=True) + RMS_EPS) * g_ref[...]
    h = (y * (1.0 + mod_ref[0, 4:5, :]) + mod_ref[0, 3:4, :]).astype(BF16)
    d_ff = wo_ref.shape[0]
    for c in range(d_ff // FFN_CHUNK):
        cols = slice(c * FFN_CHUNK, (c + 1) * FFN_CHUNK)
        a = jnp.dot(h, wa_ref[:, cols], preferred_element_type=F32)
        b = jnp.dot(h, wb_ref[:, cols], preferred_element_type=F32)
        part = _bdot(_silu(a) * b, wo_ref[cols, :])
        if c == 0:
            acc_ref[...] = part
        else:
            acc_ref[...] += part
    y_ref[...] = x + mod_ref[0, 5:6, :] * acc_ref[...]


def _ffn(x2d, mod_l, norm_g, w_in, w_out, T):
    M, D = x2d.shape
    d_ff = w_out.shape[0]
    assert d_ff % FFN_CHUNK == 0
    tm = min(1024, T)
    per_b = T // tm
    const = lambda shape, idx: pl.BlockSpec(shape, lambda m: idx, pipeline_mode=pl.Buffered(1))
    return pl.pallas_call(
        _ffn_kernel,
        grid=(M // tm,),
        in_specs=[pl.BlockSpec((tm, D), lambda m: (m, 0)),
                  pl.BlockSpec((1, 6, D), lambda m: (m // per_b, 0, 0)),
                  pl.BlockSpec((1, D), lambda m: (0, 0)),
                  const((D, d_ff), (0, 0)), const((D, d_ff), (0, 1)), const((d_ff, D), (0, 0))],
        out_specs=pl.BlockSpec((tm, D), lambda m: (m, 0)),
        out_shape=jax.ShapeDtypeStruct((M, D), F32),
        scratch_shapes=[pltpu.VMEM((tm, D), F32)],
        compiler_params=_cparams("parallel"),
        name="swiglu_ffn",
    )(x2d, mod_l, norm_g.reshape(1, D), w_in, w_in, w_out)


def _split_w_in(w_in):
    gt0 = SSD_IN + SC_IN + SG_IN + NSA_Q_W + 6 * NSA_KV_W
    wa = w_in[:, :, :MIX_W + SSD_XBC]
    wb = w_in[:, :, SSD_IN:gt0]
    wm = jnp.concatenate([w_in[:, :, MIX_W + SSD_XBC:SSD_IN], w_in[:, :, gt0:gt0 + 3 * NSA_HEADS]], axis=2)
    wm = jnp.pad(wm, ((0, 0), (0, 0), (0, U_WIDTH - COL_MISC - wm.shape[2])))
    assert wa.shape[2] == COL_XBC + SSD_XBC and wa.shape[2] + wb.shape[2] == COL_MISC
    return wa.astype(BF16), wb.astype(BF16), wm.astype(BF16)


def kernel(x, c, ada_w, ada_b, norm_mix_g, norm_ffn_g, w_in, ssd_conv_w, ssd_conv_b, ssd_dt_bias, ssd_a_log, ssd_d,
           ssd_norm_g, sc_conv_w, sg_norm_g, sg_w, sg_b, nsa_q_norm_g, nsa_k_norm_g, nsa_cmp_pe, nsa_cmp_w1,
           nsa_cmp_w2, w_branch, w_branch_gate, w_out, w_ffn_in, w_ffn_out):
    B, T, D = x.shape
    L = w_in.shape[0]
    mod = _modulation(c, ada_w, ada_b).reshape(L, B, 6, D)
    x2d = x.reshape(B * T, D)
    w_parts = _split_w_in(w_in)
    cmp_weights = _cmp_weights(nsa_cmp_pe, nsa_cmp_w1, nsa_cmp_w2, nsa_k_norm_g)
    for l in range(L):
        u, h, misc = _in_proj(x2d, mod[l], norm_mix_g[l], w_parts, l, T)
        outs = (
            _ssd_mixer(u, misc, B, T, ssd_conv_w[l], ssd_conv_b[l], ssd_dt_bias[l], ssd_a_log[l], ssd_d[l],
                       ssd_norm_g[l]),
            _short_conv_mixer(u, B, T, sc_conv_w[l]),
            _spatial_gating_mixer(u, B, T, sg_norm_g[l], sg_w[l], sg_b[l]),
            _nsa_mixer(u, misc, B, T, nsa_q_norm_g[l], nsa_k_norm_g[l], cmp_weights, l),
        )
        x2d = _merge(x2d, h, outs, mod[l], w_branch_gate[l].astype(BF16), w_branch[l].astype(BF16),
                     w_out[l].astype(BF16), T)
        x2d = _ffn(x2d, mod[l], norm_ffn_g[l], w_ffn_in[l].astype(BF16), w_ffn_out[l].astype(BF16), T)
    return x2d.reshape(B, T, D)
```

```python
import functools
import math

import jax
import jax.numpy as jnp
from jax import lax
from jax.experimental import pallas as pl
from jax.experimental.pallas import tpu as pltpu

F32 = jnp.float32
BF16 = jnp.bfloat16
RMS_EPS = 1e-6
NEG_INF = float("-inf")

MIX_W = 512
N_BRANCH = 4

SSD_HEAD_DIM = 64
SSD_HEADS = 8
SSD_GROUPS = 2
SSD_STATE = 128
SSD_CONV = 4
SSD_CHUNK = 128
SSD_STEP_CHUNKS = 8
SSD_XBC = MIX_W + 2 * SSD_GROUPS * SSD_STATE
SSD_IN = MIX_W + SSD_XBC + SSD_HEADS
SC_CONV = 3
SC_IN = 3 * MIX_W
SG_GROUPS = 4
SG_CHUNK = 128
SG_IN = 2 * MIX_W
NSA_HEADS = 8
NSA_KV_HEADS = 2
NSA_REP = NSA_HEADS // NSA_KV_HEADS
NSA_HEAD_DIM = 64
CMP_BLOCK = 32
CMP_STRIDE = 16
SLC_BLOCK = 64
SLC_TOPN = 8
WIN = 256
FORCE_SCORE = 1e9
NSA_KV_W = NSA_KV_HEADS * NSA_HEAD_DIM
NSA_Q_W = NSA_HEADS * NSA_HEAD_DIM
NSA_IN = NSA_Q_W + 6 * NSA_KV_W + 3 * NSA_HEADS

LANES = 128
SUBLANES = 8
VMEM_LIMIT_BYTES = 56 * 1024 * 1024

COL_Z = 0
COL_XBC = COL_Z + MIX_W
COL_SC = COL_XBC + SSD_XBC
COL_SG = COL_SC + SC_IN
COL_Q = COL_SG + SG_IN
COL_KV = COL_Q + NSA_Q_W
COL_MISC = COL_KV + 6 * NSA_KV_W
MISC_GATE0 = SSD_HEADS
GATE_ROWS = 32
SLC_SHIFT = 6
SLC_TILE = 512
ATTN_TILES = 4
AUG_POS = NSA_HEAD_DIM
AUG_SEL = AUG_POS + SUBLANES
MASK_SCORE = -(2.0 ** 100)
IN_CHUNK = 256
U_WIDTH = COL_MISC + IN_CHUNK


def _cparams(*sem):
    return pltpu.CompilerParams(dimension_semantics=sem, vmem_limit_bytes=VMEM_LIMIT_BYTES)


def _bdot(a, b):
    return jnp.dot(a.astype(BF16), b.astype(BF16), preferred_element_type=F32)


def _bdot_nt(a, b):
    return lax.dot_general(a.astype(BF16), b.astype(BF16), (((1,), (1,)), ((), ())),
                           preferred_element_type=F32)


def _split3(a):
    hi = a.astype(BF16)
    r1 = a - hi.astype(F32)
    mid = r1.astype(BF16)
    lo = (r1 - mid.astype(F32)).astype(BF16)
    return hi, mid, lo


def _dot_sel_rhs(a, sel):
    hi, mid, lo = _split3(a)
    return (jnp.dot(hi, sel, preferred_element_type=F32) + jnp.dot(mid, sel, preferred_element_type=F32)
            + jnp.dot(lo, sel, preferred_element_type=F32))


def _dot_sel_lhs(sel, a):
    hi, mid, lo = _split3(a)
    return (jnp.dot(sel, hi, preferred_element_type=F32) + jnp.dot(sel, mid, preferred_element_type=F32)
            + jnp.dot(sel, lo, preferred_element_type=F32))


def _sigmoid(x):
    return 1.0 / (1.0 + jnp.exp(-x))


def _silu(x):
    return x * _sigmoid(x)


def _gelu_tanh(x):
    c = math.sqrt(2.0 / math.pi)
    return 0.5 * x * (1.0 + jnp.tanh(c * (x + 0.044715 * (x * x * x))))


def _softplus(x):
    return jnp.maximum(x, 0.0) + jnp.log1p(jnp.exp(-jnp.abs(x)))


def _shift_rows(x, tail, k, row8):
    sh = pltpu.roll(x, k, 0)
    tl = pltpu.roll(tail, k, 0)
    top = jnp.where(row8 < k, tl, sh[0:SUBLANES])
    return jnp.concatenate([top, sh[SUBLANES:]], axis=0)


def _mod_kernel(c_ref, w_ref, b_ref, o_ref):
    o_ref[0] = _bdot(_silu(c_ref[...]), w_ref[0]) + b_ref[0]


def _modulation(c, ada_w, ada_b):
    L, D, D6 = ada_w.shape
    B = c.shape[0]
    tn = D6 // 4
    return pl.pallas_call(
        _mod_kernel,
        grid=(L, D6 // tn),
        in_specs=[pl.BlockSpec((B, D), lambda l, n: (0, 0)),
                  pl.BlockSpec((1, D, tn), lambda l, n: (l, 0, n)),
                  pl.BlockSpec((1, 1, tn), lambda l, n: (l, 0, n))],
        out_specs=pl.BlockSpec((1, B, tn), lambda l, n: (l, 0, n)),
        out_shape=jax.ShapeDtypeStruct((L, B, D6), F32),
        compiler_params=_cparams("parallel", "parallel"),
        name="adaln_modulation",
    )(c, ada_w, ada_b.reshape(L, 1, D6))


def _in_kernel(x_ref, mod_ref, g_ref, wa_ref, wb_ref, wm_ref, u_ref, h_ref, misc_ref):
    x = x_ref[...]
    y = x * lax.rsqrt(jnp.mean(x * x, axis=-1, keepdims=True) + RMS_EPS) * g_ref[...]
    h = (y * (1.0 + mod_ref[0, 1:2, :]) + mod_ref[0, 0:1, :]).astype(BF16)
    h_ref[...] = h
    col = 0
    for w_ref in (wa_ref, wb_ref, wm_ref):
        for n in range(w_ref.shape[2] // IN_CHUNK):
            u = jnp.dot(h, w_ref[0, :, n * IN_CHUNK:(n + 1) * IN_CHUNK], preferred_element_type=F32)
            u_ref[:, col:col + IN_CHUNK] = u.astype(BF16)
            if col == COL_MISC:
                misc_ref[...] = u[:, :LANES]
            col += IN_CHUNK


def _in_proj(x2d, mod_l, norm_g, w_parts, layer, T):
    M, D = x2d.shape
    tm = min(512, T)
    per_b = T // tm
    assert sum(w.shape[2] for w in w_parts) == U_WIDTH and all(w.shape[2] % IN_CHUNK == 0 for w in w_parts)
    wspec = lambda w: pl.BlockSpec((1, D, w.shape[2]), lambda m: (layer, 0, 0), pipeline_mode=pl.Buffered(1))
    return pl.pallas_call(
        _in_kernel,
        grid=(M // tm,),
        in_specs=[pl.BlockSpec((tm, D), lambda m: (m, 0)),
                  pl.BlockSpec((1, 6, D), lambda m: (m // per_b, 0, 0)),
                  pl.BlockSpec((1, D), lambda m: (0, 0))] + [wspec(w) for w in w_parts],
        out_specs=[pl.BlockSpec((tm, U_WIDTH), lambda m: (m, 0)),
                   pl.BlockSpec((tm, D), lambda m: (m, 0)),
                   pl.BlockSpec((tm, LANES), lambda m: (m, 0))],
        out_shape=[jax.ShapeDtypeStruct((M, U_WIDTH), BF16), jax.ShapeDtypeStruct((M, D), BF16),
                   jax.ShapeDtypeStruct((M, LANES), F32)],
        compiler_params=_cparams("parallel"),
        name="norm_in_proj",
    )(x2d, mod_l, norm_g.reshape(1, D), *w_parts)


def _ssd_kernel(z_ref, xa_ref, xb_ref, misc_ref, cw_ref, cb_ref, dtb_ref, alog_ref, dsk_ref, ng_ref,
                o_ref, tail_ref, st_ref):
    Q, P, N, H, G = SSD_CHUNK, SSD_HEAD_DIM, SSD_STATE, SSD_HEADS, SSD_GROUPS
    R = H // G

    @pl.when(pl.program_id(1) == 0)
    def _():
        tail_ref[...] = jnp.zeros_like(tail_ref)
        st_ref[...] = jnp.zeros_like(st_ref)

    xin = jnp.concatenate([xa_ref[...], xb_ref[...]], axis=1).astype(F32)
    tail = tail_ref[...]
    row8 = lax.broadcasted_iota(jnp.int32, (SUBLANES, SSD_XBC), 0)
    acc = xin * cw_ref[SSD_CONV - 1:SSD_CONV, :] + cb_ref[...]
    for k in range(1, SSD_CONV):
        acc = acc + _shift_rows(xin, tail, k, row8) * cw_ref[SSD_CONV - 1 - k:SSD_CONV - k, :]
    rows_in = xin.shape[0]
    tail_ref[...] = xin[rows_in - SUBLANES:rows_in, :]
    xbc = _silu(acc)

    lane = lax.broadcasted_iota(jnp.int32, (Q, LANES), 1)
    rowi = lax.broadcasted_iota(jnp.int32, (Q, LANES), 0)
    is_head = lane < H
    tri = (lane <= rowi).astype(BF16)
    tri_t = (rowi <= lane).astype(BF16)
    e_row = lax.broadcasted_iota(jnp.int32, (LANES, MIX_W), 0)
    e_col = lax.broadcasted_iota(jnp.int32, (LANES, MIX_W), 1)
    expand = (jnp.right_shift(e_col, 6) == e_row).astype(BF16)
    causal = lane <= rowi
    neg_a = -jnp.exp(alog_ref[...])

    for c in range(rows_in // Q):
        rows = slice(c * Q, (c + 1) * Q)
        xs = xbc[rows, :MIX_W]
        bm = xbc[rows, MIX_W:MIX_W + G * N]
        cm = xbc[rows, MIX_W + G * N:]
        dt = jnp.where(is_head, _softplus(misc_ref[rows, :] + dtb_ref[...]), 0.0)
        a = dt * neg_a
        a_cs = _dot_sel_lhs(tri, a)
        a_cs_t = _dot_sel_rhs(a.T, tri_t)
        a_last = a_cs[Q - 1:Q, :]
        ea = jnp.exp(a_cs)
        dec = jnp.exp(a_last - a_cs)
        xdt = xs * _dot_sel_rhs(dt, expand)
        ea_e = _dot_sel_rhs(ea, expand)
        xdec = xdt * _dot_sel_rhs(dec, expand)

        ys = []
        for g in range(G):
            bg = bm[:, g * N:(g + 1) * N]
            cg = cm[:, g * N:(g + 1) * N].astype(BF16)
            cb = _bdot_nt(cg, bg)
            bg_t = bg.T.astype(BF16)
            for r in range(R):
                h = g * R + r
                seg = jnp.where(causal, a_cs[:, h:h + 1] - a_cs_t[h:h + 1, :], NEG_INF)
                y_diag = _bdot(cb * jnp.exp(seg), xdt[:, h * P:(h + 1) * P])
                state = st_ref[h]
                y_off = _bdot(cg, state) * ea_e[:, h * P:(h + 1) * P]
                st_ref[h] = state * jnp.exp(a_last[:, h:h + 1]) + _bdot(bg_t, xdec[:, h * P:(h + 1) * P])
                ys.append(y_diag + y_off)
        y = jnp.concatenate(ys, axis=1) + xs * dsk_ref[...]
        y = y * _silu(z_ref[rows, :].astype(F32))
        gw = MIX_W // G
        outs = []
        for g in range(G):
            yg = y[:, g * gw:(g + 1) * gw]
            outs.append(yg * lax.rsqrt(jnp.mean(yg * yg, axis=-1, keepdims=True) + RMS_EPS))
        o_ref[rows, :] = (jnp.concatenate(outs, axis=1) * ng_ref[...]).astype(BF16)


def _ssd_mixer(u, misc, B, T, conv_w, conv_b, dt_bias, a_log, d_skip, norm_g):
    Q = min(SSD_STEP_CHUNKS * SSD_CHUNK, T)
    nc = T // Q

    def pad_lane(v):
        return jnp.pad(v, (0, LANES - v.shape[0])).reshape(1, LANES)

    row = lambda b, c: b * nc + c
    full = lambda shape: pl.BlockSpec(shape, lambda b, c: (0,) * len(shape))
    return pl.pallas_call(
        _ssd_kernel,
        grid=(B, nc),
        in_specs=[pl.BlockSpec((Q, MIX_W), lambda b, c: (row(b, c), COL_Z // MIX_W)),
                  pl.BlockSpec((Q, MIX_W), lambda b, c: (row(b, c), COL_XBC // MIX_W)),
                  pl.BlockSpec((Q, MIX_W), lambda b, c: (row(b, c), COL_XBC // MIX_W + 1)),
                  pl.BlockSpec((Q, LANES), lambda b, c: (row(b, c), 0)),
                  full((SSD_CONV, SSD_XBC)), full((1, SSD_XBC)), full((1, LANES)), full((1, LANES)),
                  full((1, MIX_W)), full((1, MIX_W))],
        out_specs=pl.BlockSpec((Q, MIX_W), lambda b, c: (row(b, c), 0)),
        out_shape=jax.ShapeDtypeStruct((B * T, MIX_W), BF16),
        scratch_shapes=[pltpu.VMEM((SUBLANES, SSD_XBC), F32),
                        pltpu.VMEM((SSD_HEADS, SSD_STATE, SSD_HEAD_DIM), F32)],
        compiler_params=_cparams("parallel", "arbitrary"),
        name="ssd_mixer",
    )(u, u, u, misc, conv_w, conv_b.reshape(1, SSD_XBC), pad_lane(dt_bias), pad_lane(a_log),
      jnp.repeat(d_skip, SSD_HEAD_DIM).reshape(1, MIX_W), norm_g.reshape(1, MIX_W))


def _sc_kernel(b_ref, c_ref, h_ref, w_ref, o_ref, tail_ref):
    @pl.when(pl.program_id(1) == 0)
    def _():
        tail_ref[...] = jnp.zeros_like(tail_ref)

    cx = c_ref[...].astype(F32) * h_ref[...].astype(F32)
    tt = cx.shape[0]
    tail = tail_ref[...]
    row8 = lax.broadcasted_iota(jnp.int32, (SUBLANES, MIX_W), 0)
    acc = cx * w_ref[SC_CONV - 1:SC_CONV, :]
    for k in range(1, SC_CONV):
        acc = acc + _shift_rows(cx, tail, k, row8) * w_ref[SC_CONV - 1 - k:SC_CONV - k, :]
    tail_ref[...] = cx[tt - SUBLANES:tt, :]
    o_ref[...] = (b_ref[...].astype(F32) * acc).astype(BF16)


def _short_conv_mixer(u, B, T, conv_w):
    tt = min(1024, T)
    nt = T // tt
    c0 = COL_SC // MIX_W
    spec = lambda j: pl.BlockSpec((tt, MIX_W), lambda b, i: (b * nt + i, c0 + j))
    return pl.pallas_call(
        _sc_kernel,
        grid=(B, nt),
        in_specs=[spec(0), spec(1), spec(2), pl.BlockSpec((SC_CONV, MIX_W), lambda b, i: (0, 0))],
        out_specs=pl.BlockSpec((tt, MIX_W), lambda b, i: (b * nt + i, 0)),
        out_shape=jax.ShapeDtypeStruct((B * T, MIX_W), BF16),
        scratch_shapes=[pltpu.VMEM((SUBLANES, MIX_W), F32)],
        compiler_params=_cparams("parallel", "arbitrary"),
        name="short_conv_mixer",
    )(u, u, u, conv_w)


def _sg_kernel(u_ref, v_ref, ng_ref, w_ref, bias_ref, o_ref):
    Q = SG_CHUNK
    rowi = lax.broadcasted_iota(jnp.int32, (Q, Q), 0)
    coli = lax.broadcasted_iota(jnp.int32, (Q, Q), 1)
    gd = MIX_W // SG_GROUPS
    ws = [jnp.where(coli <= rowi, w_ref[g], 0.0).astype(BF16) for g in range(SG_GROUPS)]
    for c in range(u_ref.shape[0] // Q):
        rows = slice(c * Q, (c + 1) * Q)
        v = _gelu_tanh(v_ref[rows, :].astype(F32))
        v = (v * lax.rsqrt(jnp.mean(v * v, axis=-1, keepdims=True) + RMS_EPS) * ng_ref[...]).astype(BF16)
        mixed = jnp.concatenate([jnp.dot(ws[g], v[:, g * gd:(g + 1) * gd], preferred_element_type=F32)
                                 for g in range(SG_GROUPS)], axis=1)
        o_ref[rows, :] = (_gelu_tanh(u_ref[rows, :].astype(F32)) * (mixed + bias_ref[...])).astype(BF16)


def _spatial_gating_mixer(u, B, T, norm_g, w_s, b_s):
    Q = SG_CHUNK
    tt = min(8 * Q, T)
    nt = T // tt
    c0 = COL_SG // MIX_W
    bias = jnp.repeat(b_s.T, MIX_W // SG_GROUPS, axis=1)
    return pl.pallas_call(
        _sg_kernel,
        grid=(B, nt),
        in_specs=[pl.BlockSpec((tt, MIX_W), lambda b, c: (b * nt + c, c0)),
                  pl.BlockSpec((tt, MIX_W), lambda b, c: (b * nt + c, c0 + 1)),
                  pl.BlockSpec((1, MIX_W), lambda b, c: (0, 0)),
                  pl.BlockSpec((SG_GROUPS, Q, Q), lambda b, c: (0, 0, 0)),
                  pl.BlockSpec((Q, MIX_W), lambda b, c: (0, 0))],
        out_specs=pl.BlockSpec((tt, MIX_W), lambda b, c: (b * nt + c, 0)),
        out_shape=jax.ShapeDtypeStruct((B * T, MIX_W), BF16),
        compiler_params=_cparams("parallel", "parallel"),
        name="spatial_gating_mixer",
    )(u, u, norm_g.reshape(1, MIX_W), w_s, bias)


def _group_mean_sq(x, width):
    r = lax.broadcasted_iota(jnp.int32, (LANES, LANES), 0)
    c = lax.broadcasted_iota(jnp.int32, (LANES, LANES), 1)
    sh = width.bit_length() - 1
    same = (jnp.right_shift(r, sh) == jnp.right_shift(c, sh)).astype(BF16)
    x2 = x * x
    slabs = [_dot_sel_rhs(x2[:, j * LANES:(j + 1) * LANES], same) for j in range(x.shape[1] // LANES)]
    return jnp.concatenate(slabs, axis=1) * (1.0 / width)


def _key_aug(lane, pos):
    return jnp.where((lane == AUG_POS) | (lane == AUG_POS + 1), 1.0,
                     jnp.where(lane == AUG_POS + 2, -(pos & ~(LANES - 1)).astype(F32),
                               jnp.where(lane == AUG_POS + 3, -(pos & (LANES - 1)).astype(F32), 0.0)))


def _nsa_prep_kernel(q_ref, ks_ref, vs_ref, kw_ref, vw_ref, misc_ref, qg_ref, ksg_ref, kwg_ref,
                     qa_ref, ksa_ref, vst_ref, kwa_ref, vwt_ref, gate_ref):
    dh = NSA_HEAD_DIM
    tt = q_ref.shape[0]
    lane = lax.broadcasted_iota(jnp.int32, (tt, LANES), 1)
    pos = pl.program_id(1) * tt + lax.broadcasted_iota(jnp.int32, (tt, LANES), 0)
    is_feat = lane < dh

    q = q_ref[...].astype(F32)
    qn = q * lax.rsqrt(_group_mean_sq(q, dh) + RMS_EPS) * qg_ref[...] * (dh ** -0.5)
    q_pos = jnp.where(lane == AUG_POS, (pos & ~(LANES - 1)).astype(F32),
                      jnp.where(lane == AUG_POS + 1, (pos & (LANES - 1)).astype(F32),
                                jnp.where((lane == AUG_POS + 2) | (lane == AUG_POS + 3), 1.0, 0.0)))
    for h in range(NSA_HEADS):
        pair = qn[:, (h // 2) * LANES:(h // 2 + 1) * LANES]
        feat = pair if h % 2 == 0 else pltpu.roll(pair, dh, 1)
        qa_ref[0, h] = jnp.where(is_feat, feat, -(2.0 ** -(h + 1)) * q_pos).astype(BF16)

    ks = ks_ref[...].astype(F32)
    ksn = ks * lax.rsqrt(_group_mean_sq(ks, dh) + RMS_EPS) * ksg_ref[...]
    kw = kw_ref[...].astype(F32)
    kwn = kw * lax.rsqrt(_group_mean_sq(kw, dh) + RMS_EPS) * kwg_ref[...]
    k_pos = _key_aug(lane, pos)
    k_pos_sel = jnp.where(lane == AUG_SEL + jnp.right_shift(pos, SLC_SHIFT), MASK_SCORE, k_pos)
    vs_t = vs_ref[...].astype(F32).T.astype(BF16)
    vw_t = vw_ref[...].astype(F32).T.astype(BF16)
    for g in range(NSA_KV_HEADS):
        sl = slice(g * dh, (g + 1) * dh)
        ksa_ref[0, g] = jnp.where(is_feat, ksn if g == 0 else pltpu.roll(ksn, dh, 1), k_pos_sel).astype(BF16)
        kwa_ref[0, g] = jnp.where(is_feat, kwn if g == 0 else pltpu.roll(kwn, dh, 1), k_pos).astype(BF16)
        vst_ref[0, g] = vs_t[sl, :]
        vwt_ref[0, g] = vw_t[sl, :]
    gate_ref[0] = _sigmoid(misc_ref[...]).T[0:GATE_ROWS, :]


def _nsa_prep(u, misc, B, T, q_norm_g, k_norm_g):
    tt = min(1024, T)
    nt = T // tt
    G, H, dh = NSA_KV_HEADS, NSA_HEADS, NSA_HEAD_DIM
    kv0 = COL_KV // LANES
    kvspec = lambda j: pl.BlockSpec((tt, LANES), lambda b, i: (b * nt + i, kv0 + j))
    vec = lambda n: pl.BlockSpec((1, n), lambda b, i: (0, 0))
    kspec = pl.BlockSpec((1, G, tt, LANES), lambda b, i: (b, 0, i, 0))
    kshape = jax.ShapeDtypeStruct((B, G, T, LANES), BF16)
    vspec = pl.BlockSpec((1, G, dh, tt), lambda b, i: (b, 0, 0, i))
    vshape = jax.ShapeDtypeStruct((B, G, dh, T), BF16)
    return pl.pallas_call(
        _nsa_prep_kernel,
        grid=(B, nt),
        in_specs=[pl.BlockSpec((tt, NSA_Q_W), lambda b, i: (b * nt + i, COL_Q // NSA_Q_W)),
                  kvspec(2), kvspec(3), kvspec(4), kvspec(5),
                  pl.BlockSpec((tt, LANES), lambda b, i: (b * nt + i, 0)),
                  vec(NSA_Q_W), vec(LANES), vec(LANES)],
        out_specs=[pl.BlockSpec((1, H, tt, LANES), lambda b, i: (b, 0, i, 0)),
                   kspec, vspec, kspec, vspec,
                   pl.BlockSpec((1, GATE_ROWS, tt), lambda b, i: (b, 0, i))],
        out_shape=[jax.ShapeDtypeStruct((B, H, T, LANES), BF16),
                   kshape, vshape, kshape, vshape,
                   jax.ShapeDtypeStruct((B, GATE_ROWS, T), F32)],
        compiler_params=_cparams("parallel", "parallel"),
        name="nsa_prep",
    )(u, u, u, u, u, misc, jnp.tile(q_norm_g, NSA_HEADS).reshape(1, NSA_Q_W),
      jnp.tile(k_norm_g[1], G).reshape(1, LANES), jnp.tile(k_norm_g[2], G).reshape(1, LANES))


def _nsa_cmp_kernel(kin_ref, vin_ref, w1_ref, pe_ref, w2k_ref, w2vt_ref, kg_ref, kc_ref, vct_ref, x_ref):
    G = NSA_KV_HEADS
    nseg = kin_ref.shape[0] // CMP_STRIDE
    x_ref[0] = kin_ref[...].astype(F32)
    x_ref[1] = vin_ref[...].astype(F32)
    for j in range(2):
        a = jnp.concatenate([x_ref[j, pl.ds(l, nseg, stride=CMP_STRIDE), :] for l in range(CMP_STRIDE)],
                            axis=1).astype(BF16)
        pe_term = (jnp.dot(pe_ref[j, 0], w1_ref[j, 0], preferred_element_type=F32)
                   + jnp.dot(pe_ref[j, 1], w1_ref[j, 1], preferred_element_type=F32))[0:1, :]
        lo = jnp.dot(a, w1_ref[j, 0], preferred_element_type=F32)
        hi = jnp.dot(a, w1_ref[j, 1], preferred_element_type=F32)
        y = _gelu_tanh(lo + pltpu.roll(hi, nseg - 1, 0) + pe_term)
        for g in range(G):
            if j == 0:
                r = _bdot(y, w2k_ref[g])
                ms = jnp.sum(r * r, axis=-1, keepdims=True) * (1.0 / NSA_HEAD_DIM)
                lane = lax.broadcasted_iota(jnp.int32, r.shape, 1)
                last = lax.broadcasted_iota(jnp.int32, r.shape, 0) * CMP_STRIDE + (CMP_BLOCK - 1)
                kc_ref[0, g] = (r * lax.rsqrt(ms + RMS_EPS) * kg_ref[...] + _key_aug(lane, last)).astype(BF16)
            else:
                vct_ref[0, g] = _bdot_nt(w2vt_ref[g], y).astype(BF16)


def _cmp_weights(cmp_pe, cmp_w1, cmp_w2, k_norm_g):
    G, dh = NSA_KV_HEADS, NSA_HEAD_DIM
    L = cmp_w1.shape[0]
    width = CMP_STRIDE * G * dh
    w1 = cmp_w1.reshape(L, 2, 2, CMP_STRIDE, dh, dh).astype(BF16)
    w1 = jnp.stack([jnp.pad(w1, ((0, 0),) * 5 + ((g * dh, (G - 1 - g) * dh),)) for g in range(G)], axis=4)
    w1 = w1.reshape(L, 2, 2, width, G * dh)
    pe = jnp.broadcast_to(cmp_pe.reshape(L, 2, 2, CMP_STRIDE, 1, dh), (L, 2, 2, CMP_STRIDE, G, dh))
    pe = jnp.broadcast_to(pe.reshape(L, 2, 2, 1, width), (L, 2, 2, SUBLANES, width)).astype(BF16)
    w2k = jnp.stack([jnp.zeros((L, G * dh, LANES), F32).at[:, g * dh:(g + 1) * dh, :dh].set(cmp_w2[:, 0])
                     for g in range(G)], axis=1).astype(BF16)
    w2vt = jnp.stack([jnp.zeros((L, dh, G * dh), F32).at[:, :, g * dh:(g + 1) * dh].set(
        jnp.swapaxes(cmp_w2[:, 1], 1, 2)) for g in range(G)], axis=1).astype(BF16)
    kg = jnp.pad(k_norm_g[:, 0], ((0, 0), (0, LANES - dh))).reshape(L, 1, LANES)
    return w1, pe, w2k, w2vt, kg


def _nsa_compress(u, B, T, cmp_weights, layer):
    G, dh = NSA_KV_HEADS, NSA_HEAD_DIM
    nseg = T // CMP_STRIDE
    of_layer = lambda w: pl.BlockSpec((None,) + w.shape[1:], lambda b: (layer,) + (0,) * (w.ndim - 1))
    kv0 = COL_KV // LANES
    return pl.pallas_call(
        _nsa_cmp_kernel,
        grid=(B,),
        in_specs=[pl.BlockSpec((T, LANES), lambda b: (b, kv0)), pl.BlockSpec((T, LANES), lambda b: (b, kv0 + 1))]
        + [of_layer(w) for w in cmp_weights],
        out_specs=[pl.BlockSpec((1, G, nseg, LANES), lambda b: (b, 0, 0, 0)),
                   pl.BlockSpec((1, G, dh, nseg), lambda b: (b, 0, 0, 0))],
        out_shape=[jax.ShapeDtypeStruct((B, G, nseg, LANES), BF16), jax.ShapeDtypeStruct((B, G, dh, nseg), BF16)],
        scratch_shapes=[pltpu.VMEM((2, T, LANES), F32)],
        compiler_params=_cparams("parallel"),
        name="nsa_compress",
    )(u, u, *cmp_weights)


def _softmax_keys(s, ok):
    s = jnp.where(ok, s, NEG_INF)
    m = jnp.max(s, axis=0, keepdims=True)
    m = jnp.where(m > NEG_INF, m, 0.0)
    p = jnp.exp(s - m)
    return p * (1.0 / jnp.maximum(jnp.sum(p, axis=0, keepdims=True), 1e-30))


def _exp_keys(s, ok):
    s = jnp.where(ok, s, NEG_INF)
    m = jnp.max(s, axis=0, keepdims=True)
    m = jnp.where(m > NEG_INF, m, 0.0)
    p = jnp.exp(s - m)
    return p, 1.0 / jnp.maximum(jnp.sum(p, axis=0, keepdims=True), 1e-30)


def _nsa_attn_kernel(q_ref, gt_ref, kc_ref, vct_ref, ks_ref, vst_ref, kw_ref, vwt_ref, o_ref, selt_ref, s_ref, *,
                     seq_len):
    G, R, dh = NSA_KV_HEADS, NSA_REP, NSA_HEAD_DIM
    TQ = LANES
    NB = seq_len // SLC_BLOCK
    key_r = lax.broadcasted_iota(jnp.int32, (TQ, TQ), 0)
    q_l = lax.broadcasted_iota(jnp.int32, (TQ, TQ), 1)
    ok_d = q_l >= key_r
    jb = lax.broadcasted_iota(jnp.int32, (NB, TQ), 0)
    row8 = lax.broadcasted_iota(jnp.int32, (SUBLANES, TQ), 0)
    nl = lax.broadcasted_iota(jnp.int32, (NB, TQ), 1)
    overlap_t = ((nl * CMP_STRIDE < jb * SLC_BLOCK + SLC_BLOCK)
                 & (nl * CMP_STRIDE + CMP_BLOCK > jb * SLC_BLOCK)).astype(BF16)
    wlen = (WIN // TQ + 1) * TQ
    wkey = lax.broadcasted_iota(jnp.int32, (wlen, TQ), 0)
    wq = lax.broadcasted_iota(jnp.int32, (wlen, TQ), 1)

    def heads(s, ok, fn):
        return [fn(s[:, r * TQ:(r + 1) * TQ], ok) for r in range(R)]

    def gate_rows(sub, g, branch):
        rows = [MISC_GATE0 + 3 * (g * R + r) + branch for r in range(R)]
        return jnp.concatenate([gt_ref[0, c:c + 1, sub * TQ:(sub + 1) * TQ] for c in rows], axis=1)

    def tile_scores(j, g, qa):
        off = pl.multiple_of(j * SLC_TILE, SLC_TILE)
        return _bdot_nt(ks_ref[0, g, pl.ds(off, SLC_TILE), :], qa)

    def front(sub):
        i = pl.program_id(1) * ATTN_TILES + sub
        t0 = i * TQ
        d0 = pl.multiple_of(t0, TQ)
        ok_c = (t0 + q_l) - (key_r * CMP_STRIDE + (CMP_BLOCK - 1)) >= 0
        cur = jnp.right_shift(t0 + nl, SLC_SHIFT)
        future = jb > cur
        forced = (jb == 0) | (jb == cur) | (jb == cur - 1)
        wstart = pl.multiple_of(jnp.maximum(i - WIN // TQ, 0) * TQ, TQ)
        dist_w = (t0 + wq) - (wstart + wkey)
        ok_w = (dist_w >= 0) & (dist_w < WIN)

        qa_l, o_fix_l, s_d_l, init = [], [], [], []
        for g in range(G):
            qs = q_ref[0, g * R:(g + 1) * R, sub * TQ:(sub + 1) * TQ, :].reshape(R * TQ, LANES)

            p_c = heads(_bdot_nt(kc_ref[0, g], qs), ok_c, _softmax_keys)
            o_cmp = jnp.dot(vct_ref[0, g], jnp.concatenate(p_c, axis=1).astype(BF16),
                            preferred_element_type=F32)

            pw = heads(_bdot_nt(kw_ref[0, g, pl.ds(wstart, wlen), :], qs), ok_w, _exp_keys)
            o_win = jnp.dot(vwt_ref[0, g, :, pl.ds(wstart, wlen)],
                            jnp.concatenate([p for p, _ in pw], axis=1).astype(BF16), preferred_element_type=F32)
            inv_w = jnp.concatenate([inv for _, inv in pw], axis=1)
            o_fix_l.append(gate_rows(sub, g, 0) * o_cmp + (gate_rows(sub, g, 2) * inv_w) * o_win)

            importance = _dot_sel_lhs(overlap_t, p_c[0] + p_c[1] + p_c[2] + p_c[3])
            score = jnp.where(future, NEG_INF, jnp.where(forced, FORCE_SCORE, importance))
            groups = [score[v * SUBLANES:(v + 1) * SUBLANES] for v in range(NB // SUBLANES)]
            ranks = [jnp.zeros((SUBLANES, TQ), F32) for _ in groups]
            for k in range(NB):
                sk = score[k:k + 1, :]
                for v, sv in enumerate(groups):
                    if v < k // SUBLANES:
                        ahead = sk > sv
                    elif v > k // SUBLANES:
                        ahead = sk >= sv
                    else:
                        ahead = (sk > sv) | ((sk == sv) & (row8 > k % SUBLANES))
                    ranks[v] = ranks[v] + jnp.where(ahead, 1.0, 0.0)
            rank = jnp.concatenate(ranks, axis=0)
            sel_t = (rank < SLC_TOPN) & (score > NEG_INF)
            selt_ref[sub, g] = jnp.where(sel_t, 1.0, 0.0)

            sel_d = jnp.where(key_r < SLC_BLOCK, selt_ref[sub, g, pl.ds(2 * i, 1), :],
                              selt_ref[sub, g, pl.ds(2 * i + 1, 1), :])
            s_d = jnp.concatenate(heads(_bdot_nt(ks_ref[0, g, pl.ds(d0, TQ), :], qs), (sel_d > 0.5) & ok_d,
                                        lambda s, ok: jnp.where(ok, s, NEG_INF)), axis=1)
            s_d_l.append(s_d)

            masked_t = jnp.where(sel_t & (jb < 2 * i), 0.0, 1.0)
            cols_t = jnp.concatenate([jnp.zeros((AUG_SEL, TQ), F32), masked_t,
                                      jnp.zeros((LANES - AUG_SEL - NB, TQ), F32)], axis=0)
            cols = cols_t.T.astype(BF16)
            qa_l.append(jnp.concatenate([qs[r * TQ:(r + 1) * TQ] + cols for r in range(R)], axis=0))

        for g in range(G):
            s_ref[sub, g] = tile_scores(0, g, qa_l[g])
            m = jnp.max(s_d_l[g], axis=0, keepdims=True)
            p = jnp.exp(s_d_l[g] - m)
            init.append((m, jnp.sum(p, axis=0, keepdims=True),
                         jnp.dot(vst_ref[0, g, :, pl.ds(d0, TQ)], p.astype(BF16), preferred_element_type=F32)))
        return (t0 + SLC_TILE - 1) // SLC_TILE, qa_l, o_fix_l, tuple(init)

    def key_loop(sub, n_tiles, qa_l, init):
        def slc_step(j, carry):
            off = pl.multiple_of(j * SLC_TILE, SLC_TILE)
            nxt = jnp.minimum(j + 1, n_tiles - 1)
            out = []
            for g in range(G):
                m, l, acc = carry[g]
                s = s_ref[sub, g]
                s_ref[sub, g] = tile_scores(nxt, g, qa_l[g])
                m_new = jnp.maximum(m, jnp.max(s, axis=0, keepdims=True))
                alpha = jnp.exp(m - m_new)
                p = jnp.exp(s - m_new)
                l = alpha * l + jnp.sum(p, axis=0, keepdims=True)
                acc = alpha * acc + jnp.dot(vst_ref[0, g, :, pl.ds(off, SLC_TILE)], p.astype(BF16),
                                            preferred_element_type=F32)
                out.append((m_new, l, acc))
            return tuple(out)

        return lax.fori_loop(0, n_tiles, slc_step, init)

    def back(sub, o_fix_l, slc):
        for g in range(G):
            _, l, acc = slc[g]
            o = o_fix_l[g] + (gate_rows(sub, g, 1) * (1.0 / jnp.maximum(l, 1e-30))) * acc
            for a in range(R // 2):
                pair = jnp.concatenate([o[:, (2 * a) * TQ:(2 * a + 1) * TQ],
                                        o[:, (2 * a + 1) * TQ:(2 * a + 2) * TQ]], axis=0)
                lo = (g * R + 2 * a) * dh
                o_ref[sub * TQ:(sub + 1) * TQ, lo:lo + 2 * dh] = pair.T.astype(BF16)

    fronts = [front(sub) for sub in range(ATTN_TILES)]
    loops = [key_loop(sub, n_tiles, qa_l, init) for sub, (n_tiles, qa_l, _, init) in enumerate(fronts)]
    for sub in range(ATTN_TILES):
        back(sub, fronts[sub][2], loops[sub])


def _nsa_attention(qn, gates_t, kc, vct, ksn, vst, kwn, vwt, B, T):
    G, H, dh = NSA_KV_HEADS, NSA_HEADS, NSA_HEAD_DIM
    TQ = ATTN_TILES * LANES
    nq = T // TQ
    nseg = kc.shape[2]
    per_b = lambda shape: pl.BlockSpec((1,) + shape, lambda b, i: (b, 0, 0, 0))
    return pl.pallas_call(
        functools.partial(_nsa_attn_kernel, seq_len=T),
        grid=(B, nq),
        in_specs=[pl.BlockSpec((1, H, TQ, LANES), lambda b, i: (b, 0, i, 0)),
                  pl.BlockSpec((1, GATE_ROWS, TQ), lambda b, i: (b, 0, i)),
                  per_b((G, nseg, LANES)), per_b((G, dh, nseg)),
                  per_b((G, T, LANES)), per_b((G, dh, T)), per_b((G, T, LANES)), per_b((G, dh, T))],
        out_specs=pl.BlockSpec((TQ, NSA_Q_W), lambda b, i: (b * nq + i, 0)),
        out_shape=jax.ShapeDtypeStruct((B * T, NSA_Q_W), BF16),
        scratch_shapes=[pltpu.VMEM((ATTN_TILES, G, T // SLC_BLOCK, LANES), F32),
                        pltpu.VMEM((ATTN_TILES, G, SLC_TILE, NSA_REP * LANES), F32)],
        compiler_params=_cparams("parallel", "arbitrary"),
        name="nsa_attention",
    )(qn, gates_t, kc, vct, ksn, vst, kwn, vwt)


def _nsa_mixer(u, misc, B, T, q_norm_g, k_norm_g, cmp_weights, layer):
    assert T % LANES == 0 and T >= (WIN // LANES + 1) * LANES and T // CMP_STRIDE == LANES
    assert SLC_BLOCK == 1 << SLC_SHIFT and LANES == 2 * SLC_BLOCK
    qn, ksn, vst, kwn, vwt, gates_t = _nsa_prep(u, misc, B, T, q_norm_g, k_norm_g)
    kc, vct = _nsa_compress(u, B, T, cmp_weights, layer)
    return _nsa_attention(qn, gates_t, kc, vct, ksn, vst, kwn, vwt, B, T)


def _merge_kernel(x_ref, h_ref, oa_ref, ob_ref, oc_ref, od_ref, mod_ref, wg_ref, wb_ref, wo_ref, y_ref):
    h = h_ref[...]
    merged = None
    for i, o_ref in enumerate((oa_ref, ob_ref, oc_ref, od_ref)):
        gate = _sigmoid(jnp.dot(h, wg_ref[i], preferred_element_type=F32))
        term = gate * jnp.dot(o_ref[...], wb_ref[i], preferred_element_type=F32)
        merged = term if merged is None else merged + term
    y_ref[...] = x_ref[...] + mod_ref[0, 2:3, :] * _bdot(merged, wo_ref[...])


def _merge(x2d, h, outs, mod_l, wg, wb, wo, T):
    M, D = x2d.shape
    tm = min(512, T)
    per_b = T // tm
    row = lambda w: pl.BlockSpec((tm, w), lambda m: (m, 0))
    const = lambda shape: pl.BlockSpec(shape, lambda m: (0,) * len(shape), pipeline_mode=pl.Buffered(1))
    return pl.pallas_call(
        _merge_kernel,
        grid=(M // tm,),
        in_specs=[row(D), row(D), row(MIX_W), row(MIX_W), row(MIX_W), row(MIX_W),
                  pl.BlockSpec((1, 6, D), lambda m: (m // per_b, 0, 0)),
                  const((N_BRANCH, D, D)), const((N_BRANCH, MIX_W, D)), const((D, D))],
        out_specs=row(D),
        out_shape=jax.ShapeDtypeStruct((M, D), F32),
        compiler_params=_cparams("parallel"),
        name="gated_merge_out_proj",
    )(x2d, h, *outs, mod_l, wg, wb, wo)


FFN_CHUNK = 256


def _ffn_kernel(x_ref, mod_ref, g_ref, wa_ref, wb_ref, wo_ref, y_ref, acc_ref):
    x = x_ref[...]
    y = x * lax.rsqrt(jnp.mean(x * x, axis=-1, keepdims=True) + RMS_EPS) * g_ref[...]
    h = (y * (1.0 + mod_ref[0, 4:5, :]) + mod_ref[0, 3:4, :]).astype(BF16)
    d_ff = wo_ref.shape[0]
    for c in range(d_ff // FFN_CHUNK):
        cols = slice(c * FFN_CHUNK, (c + 1) * FFN_CHUNK)
        a = jnp.dot(h, wa_ref[:, cols], preferred_element_type=F32)
        b = jnp.dot(h, wb_ref[:, cols], preferred_element_type=F32)
        part = _bdot(_silu(a) * b, wo_ref[cols, :])
        if c == 0:
            acc_ref[...] = part
        else:
            acc_ref[...] += part
    y_ref[...] = x + mod_ref[0, 5:6, :] * acc_ref[...]


def _ffn(x2d, mod_l, norm_g, w_in, w_out, T):
    M, D = x2d.shape
    d_ff = w_out.shape[0]
    assert d_ff % FFN_CHUNK == 0
    tm = min(1024, T)
    per_b = T // tm
    const = lambda shape, idx: pl.BlockSpec(shape, lambda m: idx, pipeline_mode=pl.Buffered(1))
    return pl.pallas_call(
        _ffn_kernel,
        grid=(M // tm,),
        in_specs=[pl.BlockSpec((tm, D), lambda m: (m, 0)),
                  pl.BlockSpec((1, 6, D), lambda m: (m // per_b, 0, 0)),
                  pl.BlockSpec((1, D), lambda m: (0, 0)),
                  const((D, d_ff), (0, 0)), const((D, d_ff), (0, 1)), const((d_ff, D), (0, 0))],
        out_specs=pl.BlockSpec((tm, D), lambda m: (m, 0)),
        out_shape=jax.ShapeDtypeStruct((M, D), F32),
        scratch_shapes=[pltpu.VMEM((tm, D), F32)],
        compiler_params=_cparams("parallel"),
        name="swiglu_ffn",
    )(x2d, mod_l, norm_g.reshape(1, D), w_in, w_in, w_out)


def _split_w_in(w_in):
    gt0 = SSD_IN + SC_IN + SG_IN + NSA_Q_W + 6 * NSA_KV_W
    wa = w_in[:, :, :MIX_W + SSD_XBC]
    wb = w_in[:, :, SSD_IN:gt0]
    wm = jnp.concatenate([w_in[:, :, MIX_W + SSD_XBC:SSD_IN], w_in[:, :, gt0:gt0 + 3 * NSA_HEADS]], axis=2)
    wm = jnp.pad(wm, ((0, 0), (0, 0), (0, U_WIDTH - COL_MISC - wm.shape[2])))
    assert wa.shape[2] == COL_XBC + SSD_XBC and wa.shape[2] + wb.shape[2] == COL_MISC
    return wa.astype(BF16), wb.astype(BF16), wm.astype(BF16)


def kernel(x, c, ada_w, ada_b, norm_mix_g, norm_ffn_g, w_in, ssd_conv_w, ssd_conv_b, ssd_dt_bias, ssd_a_log, ssd_d,
           ssd_norm_g, sc_conv_w, sg_norm_g, sg_w, sg_b, nsa_q_norm_g, nsa_k_norm_g, nsa_cmp_pe, nsa_cmp_w1,
           nsa_cmp_w2, w_branch, w_branch_gate, w_out, w_ffn_in, w_ffn_out):
    B, T, D = x.shape
    L = w_in.shape[0]
    mod = _modulation(c, ada_w, ada_b).reshape(L, B, 6, D)
    x2d = x.reshape(B * T, D)
    w_parts = _split_w_in(w_in)
    cmp_weights = _cmp_weights(nsa_cmp_pe, nsa_cmp_w1, nsa_cmp_w2, nsa_k_norm_g)
    for l in range(L):
        u, h, misc = _in_proj(x2d, mod[l], norm_mix_g[l], w_parts, l, T)
        outs = (
            _ssd_mixer(u, misc, B, T, ssd_conv_w[l], ssd_conv_b[l], ssd_dt_bias[l], ssd_a_log[l], ssd_d[l],
                       ssd_norm_g[l]),
            _short_conv_mixer(u, B, T, sc_conv_w[l]),
            _spatial_gating_mixer(u, B, T, sg_norm_g[l], sg_w[l], sg_b[l]),
            _nsa_mixer(u, misc, B, T, nsa_q_norm_g[l], nsa_k_norm_g[l], cmp_weights, l),
        )
        x2d = _merge(x2d, h, outs, mod[l], w_branch_gate[l].astype(BF16), w_branch[l].astype(BF16),
                     w_out[l].astype(BF16), T)
        x2d = _ffn(x2d, mod[l], norm_ffn_g[l], w_ffn_in[l].astype(BF16), w_ffn_out[l].astype(BF16), T)
    return x2d.reshape(B, T, D)
```

```python
import functools
import math

import jax
import jax.numpy as jnp
from jax import lax
from jax.experimental import pallas as pl
from jax.experimental.pallas import tpu as pltpu

F32 = jnp.float32
BF16 = jnp.bfloat16
RMS_EPS = 1e-6
NEG_INF = float("-inf")

MIX_W = 512
N_BRANCH = 4

SSD_HEAD_DIM = 64
SSD_HEADS = 8
SSD_GROUPS = 2
SSD_STATE = 128
SSD_CONV = 4
SSD_CHUNK = 128
SSD_STEP_CHUNKS = 8
SSD_XBC = MIX_W + 2 * SSD_GROUPS * SSD_STATE
SSD_IN = MIX_W + SSD_XBC + SSD_HEADS
SC_CONV = 3
SC_IN = 3 * MIX_W
SG_GROUPS = 4
SG_CHUNK = 128
SG_IN = 2 * MIX_W
NSA_HEADS = 8
NSA_KV_HEADS = 2
NSA_REP = NSA_HEADS // NSA_KV_HEADS
NSA_HEAD_DIM = 64
CMP_BLOCK = 32
CMP_STRIDE = 16
SLC_BLOCK = 64
SLC_TOPN = 8
WIN = 256
FORCE_SCORE = 1e9
NSA_KV_W = NSA_KV_HEADS * NSA_HEAD_DIM
NSA_Q_W = NSA_HEADS * NSA_HEAD_DIM
NSA_IN = NSA_Q_W + 6 * NSA_KV_W + 3 * NSA_HEADS

LANES = 128
SUBLANES = 8
VMEM_LIMIT_BYTES = 56 * 1024 * 1024

COL_Z = 0
COL_XBC = COL_Z + MIX_W
COL_SC = COL_XBC + SSD_XBC
COL_SG = COL_SC + SC_IN
COL_Q = COL_SG + SG_IN
COL_KV = COL_Q + NSA_Q_W
COL_MISC = COL_KV + 6 * NSA_KV_W
MISC_GATE0 = SSD_HEADS
GATE_ROWS = 32
SLC_SHIFT = 6
SLC_TILE = 512
ATTN_TILES = 4
AUG_POS = NSA_HEAD_DIM
AUG_SEL = AUG_POS + SUBLANES
MASK_SCORE = -(2.0 ** 100)
IN_CHUNK = 256
U_WIDTH = COL_MISC + IN_CHUNK


def _cparams(*sem):
    return pltpu.CompilerParams(dimension_semantics=sem, vmem_limit_bytes=VMEM_LIMIT_BYTES)


def _bdot(a, b):
    return jnp.dot(a.astype(BF16), b.astype(BF16), preferred_element_type=F32)


def _bdot_nt(a, b):
    return lax.dot_general(a.astype(BF16), b.astype(BF16), (((1,), (1,)), ((), ())),
                           preferred_element_type=F32)


def _split3(a):
    hi = a.astype(BF16)
    r1 = a - hi.astype(F32)
    mid = r1.astype(BF16)
    lo = (r1 - mid.astype(F32)).astype(BF16)
    return hi, mid, lo


def _dot_sel_rhs(a, sel):
    hi, mid, lo = _split3(a)
    return (jnp.dot(hi, sel, preferred_element_type=F32) + jnp.dot(mid, sel, preferred_element_type=F32)
            + jnp.dot(lo, sel, preferred_element_type=F32))


def _dot_sel_lhs(sel, a):
    hi, mid, lo = _split3(a)
    return (jnp.dot(sel, hi, preferred_element_type=F32) + jnp.dot(sel, mid, preferred_element_type=F32)
            + jnp.dot(sel, lo, preferred_element_type=F32))


def _sigmoid(x):
    return 1.0 / (1.0 + jnp.exp(-x))


def _silu(x):
    return x * _sigmoid(x)


def _gelu_tanh(x):
    c = math.sqrt(2.0 / math.pi)
    return 0.5 * x * (1.0 + jnp.tanh(c * (x + 0.044715 * (x * x * x))))


def _softplus(x):
    return jnp.maximum(x, 0.0) + jnp.log1p(jnp.exp(-jnp.abs(x)))


def _shift_rows(x, tail, k, row8):
    sh = pltpu.roll(x, k, 0)
    tl = pltpu.roll(tail, k, 0)
    top = jnp.where(row8 < k, tl, sh[0:SUBLANES])
    return jnp.concatenate([top, sh[SUBLANES:]], axis=0)


def _mod_kernel(c_ref, w_ref, b_ref, o_ref):
    o_ref[0] = _bdot(_silu(c_ref[...]), w_ref[0]) + b_ref[0]


def _modulation(c, ada_w, ada_b):
    L, D, D6 = ada_w.shape
    B = c.shape[0]
    tn = D6 // 4
    return pl.pallas_call(
        _mod_kernel,
        grid=(L, D6 // tn),
        in_specs=[pl.BlockSpec((B, D), lambda l, n: (0, 0)),
                  pl.BlockSpec((1, D, tn), lambda l, n: (l, 0, n)),
                  pl.BlockSpec((1, 1, tn), lambda l, n: (l, 0, n))],
        out_specs=pl.BlockSpec((1, B, tn), lambda l, n: (l, 0, n)),
        out_shape=jax.ShapeDtypeStruct((L, B, D6), F32),
        compiler_params=_cparams("parallel", "parallel"),
        name="adaln_modulation",
    )(c, ada_w, ada_b.reshape(L, 1, D6))


def _in_kernel(x_ref, mod_ref, g_ref, wa_ref, wb_ref, wm_ref, u_ref, h_ref, misc_ref):
    x = x_ref[...]
    y = x * lax.rsqrt(jnp.mean(x * x, axis=-1, keepdims=True) + RMS_EPS) * g_ref[...]
    h = (y * (1.0 + mod_ref[0, 1:2, :]) + mod_ref[0, 0:1, :]).astype(BF16)
    h_ref[...] = h
    col = 0
    for w_ref in (wa_ref, wb_ref, wm_ref):
        for n in range(w_ref.shape[2] // IN_CHUNK):
            u = jnp.dot(h, w_ref[0, :, n * IN_CHUNK:(n + 1) * IN_CHUNK], preferred_element_type=F32)
            u_ref[:, col:col + IN_CHUNK] = u.astype(BF16)
            if col == COL_MISC:
                misc_ref[...] = u[:, :LANES]
            col += IN_CHUNK


def _in_proj(x2d, mod_l, norm_g, w_parts, layer, T):
    M, D = x2d.shape
    tm = min(1024, T)
    per_b = T // tm
    assert sum(w.shape[2] for w in w_parts) == U_WIDTH and all(w.shape[2] % IN_CHUNK == 0 for w in w_parts)
    wspec = lambda w: pl.BlockSpec((1, D, w.shape[2]), lambda m: (layer, 0, 0), pipeline_mode=pl.Buffered(1))
    return pl.pallas_call(
        _in_kernel,
        grid=(M // tm,),
        in_specs=[pl.BlockSpec((tm, D), lambda m: (m, 0)),
                  pl.BlockSpec((1, 6, D), lambda m: (m // per_b, 0, 0)),
                  pl.BlockSpec((1, D), lambda m: (0, 0))] + [wspec(w) for w in w_parts],
        out_specs=[pl.BlockSpec((tm, U_WIDTH), lambda m: (m, 0)),
                   pl.BlockSpec((tm, D), lambda m: (m, 0)),
                   pl.BlockSpec((tm, LANES), lambda m: (m, 0))],
        out_shape=[jax.ShapeDtypeStruct((M, U_WIDTH), BF16), jax.ShapeDtypeStruct((M, D), BF16),
                   jax.ShapeDtypeStruct((M, LANES), F32)],
        compiler_params=_cparams("parallel"),
        name="norm_in_proj",
    )(x2d, mod_l, norm_g.reshape(1, D), *w_parts)


def _ssd_kernel(z_ref, xa_ref, xb_ref, misc_ref, cw_ref, cb_ref, dtb_ref, alog_ref, dsk_ref, ng_ref,
                o_ref, tail_ref, st_ref):
    Q, P, N, H, G = SSD_CHUNK, SSD_HEAD_DIM, SSD_STATE, SSD_HEADS, SSD_GROUPS
    R = H // G

    @pl.when(pl.program_id(1) == 0)
    def _():
        tail_ref[...] = jnp.zeros_like(tail_ref)
        st_ref[...] = jnp.zeros_like(st_ref)

    xin = jnp.concatenate([xa_ref[...], xb_ref[...]], axis=1).astype(F32)
    tail = tail_ref[...]
    row8 = lax.broadcasted_iota(jnp.int32, (SUBLANES, SSD_XBC), 0)
    acc = xin * cw_ref[SSD_CONV - 1:SSD_CONV, :] + cb_ref[...]
    for k in range(1, SSD_CONV):
        acc = acc + _shift_rows(xin, tail, k, row8) * cw_ref[SSD_CONV - 1 - k:SSD_CONV - k, :]
    rows_in = xin.shape[0]
    tail_ref[...] = xin[rows_in - SUBLANES:rows_in, :]
    xbc = _silu(acc)

    lane = lax.broadcasted_iota(jnp.int32, (Q, LANES), 1)
    rowi = lax.broadcasted_iota(jnp.int32, (Q, LANES), 0)
    is_head = lane < H
    tri = (lane <= rowi).astype(BF16)
    tri_t = (rowi <= lane).astype(BF16)
    e_row = lax.broadcasted_iota(jnp.int32, (LANES, MIX_W), 0)
    e_col = lax.broadcasted_iota(jnp.int32, (LANES, MIX_W), 1)
    expand = (jnp.right_shift(e_col, 6) == e_row).astype(BF16)
    causal = lane <= rowi
    neg_a = -jnp.exp(alog_ref[...])

    for c in range(rows_in // Q):
        rows = slice(c * Q, (c + 1) * Q)
        xs = xbc[rows, :MIX_W]
        bm = xbc[rows, MIX_W:MIX_W + G * N]
        cm = xbc[rows, MIX_W + G * N:]
        dt = jnp.where(is_head, _softplus(misc_ref[rows, :] + dtb_ref[...]), 0.0)
        a = dt * neg_a
        a_cs = _dot_sel_lhs(tri, a)
        a_cs_t = _dot_sel_rhs(a.T, tri_t)
        a_last = a_cs[Q - 1:Q, :]
        ea = jnp.exp(a_cs)
        dec = jnp.exp(a_last - a_cs)
        xdt = xs * _dot_sel_rhs(dt, expand)
        ea_e = _dot_sel_rhs(ea, expand)
        xdec = xdt * _dot_sel_rhs(dec, expand)

        ys = []
        for g in range(G):
            bg = bm[:, g * N:(g + 1) * N]
            cg = cm[:, g * N:(g + 1) * N].astype(BF16)
            cb = _bdot_nt(cg, bg)
            bg_t = bg.T.astype(BF16)
            for r in range(R):
                h = g * R + r
                seg = jnp.where(causal, a_cs[:, h:h + 1] - a_cs_t[h:h + 1, :], NEG_INF)
                y_diag = _bdot(cb * jnp.exp(seg), xdt[:, h * P:(h + 1) * P])
                state = st_ref[h]
                y_off = _bdot(cg, state) * ea_e[:, h * P:(h + 1) * P]
                st_ref[h] = state * jnp.exp(a_last[:, h:h + 1]) + _bdot(bg_t, xdec[:, h * P:(h + 1) * P])
                ys.append(y_diag + y_off)
        y = jnp.concatenate(ys, axis=1) + xs * dsk_ref[...]
        y = y * _silu(z_ref[rows, :].astype(F32))
        gw = MIX_W // G
        outs = []
        for g in range(G):
            yg = y[:, g * gw:(g + 1) * gw]
            outs.append(yg * lax.rsqrt(jnp.mean(yg * yg, axis=-1, keepdims=True) + RMS_EPS))
        o_ref[rows, :] = (jnp.concatenate(outs, axis=1) * ng_ref[...]).astype(BF16)


def _ssd_mixer(u, misc, B, T, conv_w, conv_b, dt_bias, a_log, d_skip, norm_g):
    Q = min(SSD_STEP_CHUNKS * SSD_CHUNK, T)
    nc = T // Q

    def pad_lane(v):
        return jnp.pad(v, (0, LANES - v.shape[0])).reshape(1, LANES)

    row = lambda b, c: b * nc + c
    full = lambda shape: pl.BlockSpec(shape, lambda b, c: (0,) * len(shape))
    return pl.pallas_call(
        _ssd_kernel,
        grid=(B, nc),
        in_specs=[pl.BlockSpec((Q, MIX_W), lambda b, c: (row(b, c), COL_Z // MIX_W)),
                  pl.BlockSpec((Q, MIX_W), lambda b, c: (row(b, c), COL_XBC // MIX_W)),
                  pl.BlockSpec((Q, MIX_W), lambda b, c: (row(b, c), COL_XBC // MIX_W + 1)),
                  pl.BlockSpec((Q, LANES), lambda b, c: (row(b, c), 0)),
                  full((SSD_CONV, SSD_XBC)), full((1, SSD_XBC)), full((1, LANES)), full((1, LANES)),
                  full((1, MIX_W)), full((1, MIX_W))],
        out_specs=pl.BlockSpec((Q, MIX_W), lambda b, c: (row(b, c), 0)),
        out_shape=jax.ShapeDtypeStruct((B * T, MIX_W), BF16),
        scratch_shapes=[pltpu.VMEM((SUBLANES, SSD_XBC), F32),
                        pltpu.VMEM((SSD_HEADS, SSD_STATE, SSD_HEAD_DIM), F32)],
        compiler_params=_cparams("parallel", "arbitrary"),
        name="ssd_mixer",
    )(u, u, u, misc, conv_w, conv_b.reshape(1, SSD_XBC), pad_lane(dt_bias), pad_lane(a_log),
      jnp.repeat(d_skip, SSD_HEAD_DIM).reshape(1, MIX_W), norm_g.reshape(1, MIX_W))


def _sc_kernel(b_ref, c_ref, h_ref, w_ref, o_ref, tail_ref):
    @pl.when(pl.program_id(1) == 0)
    def _():
        tail_ref[...] = jnp.zeros_like(tail_ref)

    cx = c_ref[...].astype(F32) * h_ref[...].astype(F32)
    tt = cx.shape[0]
    tail = tail_ref[...]
    row8 = lax.broadcasted_iota(jnp.int32, (SUBLANES, MIX_W), 0)
    acc = cx * w_ref[SC_CONV - 1:SC_CONV, :]
    for k in range(1, SC_CONV):
        acc = acc + _shift_rows(cx, tail, k, row8) * w_ref[SC_CONV - 1 - k:SC_CONV - k, :]
    tail_ref[...] = cx[tt - SUBLANES:tt, :]
    o_ref[...] = (b_ref[...].astype(F32) * acc).astype(BF16)


def _short_conv_mixer(u, B, T, conv_w):
    tt = min(2048, T)
    nt = T // tt
    c0 = COL_SC // MIX_W
    spec = lambda j: pl.BlockSpec((tt, MIX_W), lambda b, i: (b * nt + i, c0 + j))
    return pl.pallas_call(
        _sc_kernel,
        grid=(B, nt),
        in_specs=[spec(0), spec(1), spec(2), pl.BlockSpec((SC_CONV, MIX_W), lambda b, i: (0, 0))],
        out_specs=pl.BlockSpec((tt, MIX_W), lambda b, i: (b * nt + i, 0)),
        out_shape=jax.ShapeDtypeStruct((B * T, MIX_W), BF16),
        scratch_shapes=[pltpu.VMEM((SUBLANES, MIX_W), F32)],
        compiler_params=_cparams("parallel", "arbitrary"),
        name="short_conv_mixer",
    )(u, u, u, conv_w)


def _sg_kernel(u_ref, v_ref, ng_ref, w_ref, bias_ref, o_ref):
    Q = SG_CHUNK
    rowi = lax.broadcasted_iota(jnp.int32, (Q, Q), 0)
    coli = lax.broadcasted_iota(jnp.int32, (Q, Q), 1)
    gd = MIX_W // SG_GROUPS
    ws = [jnp.where(coli <= rowi, w_ref[g], 0.0).astype(BF16) for g in range(SG_GROUPS)]
    for c in range(u_ref.shape[0] // Q):
        rows = slice(c * Q, (c + 1) * Q)
        v = _gelu_tanh(v_ref[rows, :].astype(F32))
        v = (v * lax.rsqrt(jnp.mean(v * v, axis=-1, keepdims=True) + RMS_EPS) * ng_ref[...]).astype(BF16)
        mixed = jnp.concatenate([jnp.dot(ws[g], v[:, g * gd:(g + 1) * gd], preferred_element_type=F32)
                                 for g in range(SG_GROUPS)], axis=1)
        o_ref[rows, :] = (_gelu_tanh(u_ref[rows, :].astype(F32)) * (mixed + bias_ref[...])).astype(BF16)


def _spatial_gating_mixer(u, B, T, norm_g, w_s, b_s):
    Q = SG_CHUNK
    tt = min(16 * Q, T)
    nt = T // tt
    c0 = COL_SG // MIX_W
    bias = jnp.repeat(b_s.T, MIX_W // SG_GROUPS, axis=1)
    return pl.pallas_call(
        _sg_kernel,
        grid=(B, nt),
        in_specs=[pl.BlockSpec((tt, MIX_W), lambda b, c: (b * nt + c, c0)),
                  pl.BlockSpec((tt, MIX_W), lambda b, c: (b * nt + c, c0 + 1)),
                  pl.BlockSpec((1, MIX_W), lambda b, c: (0, 0)),
                  pl.BlockSpec((SG_GROUPS, Q, Q), lambda b, c: (0, 0, 0)),
                  pl.BlockSpec((Q, MIX_W), lambda b, c: (0, 0))],
        out_specs=pl.BlockSpec((tt, MIX_W), lambda b, c: (b * nt + c, 0)),
        out_shape=jax.ShapeDtypeStruct((B * T, MIX_W), BF16),
        compiler_params=_cparams("parallel", "parallel"),
        name="spatial_gating_mixer",
    )(u, u, norm_g.reshape(1, MIX_W), w_s, bias)


def _group_mean_sq(x, width):
    r = lax.broadcasted_iota(jnp.int32, (LANES, LANES), 0)
    c = lax.broadcasted_iota(jnp.int32, (LANES, LANES), 1)
    sh = width.bit_length() - 1
    same = (jnp.right_shift(r, sh) == jnp.right_shift(c, sh)).astype(BF16)
    x2 = x * x
    slabs = [_dot_sel_rhs(x2[:, j * LANES:(j + 1) * LANES], same) for j in range(x.shape[1] // LANES)]
    return jnp.concatenate(slabs, axis=1) * (1.0 / width)


def _key_aug(lane, pos):
    return jnp.where((lane == AUG_POS) | (lane == AUG_POS + 1), 1.0,
                     jnp.where(lane == AUG_POS + 2, -(pos & ~(LANES - 1)).astype(F32),
                               jnp.where(lane == AUG_POS + 3, -(pos & (LANES - 1)).astype(F32), 0.0)))


def _nsa_prep_kernel(q_ref, ks_ref, vs_ref, kw_ref, vw_ref, misc_ref, qg_ref, ksg_ref, kwg_ref,
                     qa_ref, ksa_ref, vst_ref, kwa_ref, vwt_ref, gate_ref):
    dh = NSA_HEAD_DIM
    tt = q_ref.shape[0]
    lane = lax.broadcasted_iota(jnp.int32, (tt, LANES), 1)
    pos = pl.program_id(1) * tt + lax.broadcasted_iota(jnp.int32, (tt, LANES), 0)
    is_feat = lane < dh

    q = q_ref[...].astype(F32)
    qn = q * lax.rsqrt(_group_mean_sq(q, dh) + RMS_EPS) * qg_ref[...] * (dh ** -0.5)
    q_pos = jnp.where(lane == AUG_POS, (pos & ~(LANES - 1)).astype(F32),
                      jnp.where(lane == AUG_POS + 1, (pos & (LANES - 1)).astype(F32),
                                jnp.where((lane == AUG_POS + 2) | (lane == AUG_POS + 3), 1.0, 0.0)))
    for h in range(NSA_HEADS):
        pair = qn[:, (h // 2) * LANES:(h // 2 + 1) * LANES]
        feat = pair if h % 2 == 0 else pltpu.roll(pair, dh, 1)
        qa_ref[0, h] = jnp.where(is_feat, feat, -(2.0 ** -(h + 1)) * q_pos).astype(BF16)

    ks = ks_ref[...].astype(F32)
    ksn = ks * lax.rsqrt(_group_mean_sq(ks, dh) + RMS_EPS) * ksg_ref[...]
    kw = kw_ref[...].astype(F32)
    kwn = kw * lax.rsqrt(_group_mean_sq(kw, dh) + RMS_EPS) * kwg_ref[...]
    k_pos = _key_aug(lane, pos)
    k_pos_sel = jnp.where(lane == AUG_SEL + jnp.right_shift(pos, SLC_SHIFT), MASK_SCORE, k_pos)
    vs_t = vs_ref[...].astype(F32).T.astype(BF16)
    vw_t = vw_ref[...].astype(F32).T.astype(BF16)
    for g in range(NSA_KV_HEADS):
        sl = slice(g * dh, (g + 1) * dh)
        ksa_ref[0, g] = jnp.where(is_feat, ksn if g == 0 else pltpu.roll(ksn, dh, 1), k_pos_sel).astype(BF16)
        kwa_ref[0, g] = jnp.where(is_feat, kwn if g == 0 else pltpu.roll(kwn, dh, 1), k_pos).astype(BF16)
        vst_ref[0, g] = vs_t[sl, :]
        vwt_ref[0, g] = vw_t[sl, :]
    gate_ref[0] = _sigmoid(misc_ref[...]).T[0:GATE_ROWS, :]


def _nsa_prep(u, misc, B, T, q_norm_g, k_norm_g):
    tt = min(2048, T)
    nt = T // tt
    G, H, dh = NSA_KV_HEADS, NSA_HEADS, NSA_HEAD_DIM
    kv0 = COL_KV // LANES
    kvspec = lambda j: pl.BlockSpec((tt, LANES), lambda b, i: (b * nt + i, kv0 + j))
    vec = lambda n: pl.BlockSpec((1, n), lambda b, i: (0, 0))
    kspec = pl.BlockSpec((1, G, tt, LANES), lambda b, i: (b, 0, i, 0))
    kshape = jax.ShapeDtypeStruct((B, G, T, LANES), BF16)
    vspec = pl.BlockSpec((1, G, dh, tt), lambda b, i: (b, 0, 0, i))
    vshape = jax.ShapeDtypeStruct((B, G, dh, T), BF16)
    return pl.pallas_call(
        _nsa_prep_kernel,
        grid=(B, nt),
        in_specs=[pl.BlockSpec((tt, NSA_Q_W), lambda b, i: (b * nt + i, COL_Q // NSA_Q_W)),
                  kvspec(2), kvspec(3), kvspec(4), kvspec(5),
                  pl.BlockSpec((tt, LANES), lambda b, i: (b * nt + i, 0)),
                  vec(NSA_Q_W), vec(LANES), vec(LANES)],
        out_specs=[pl.BlockSpec((1, H, tt, LANES), lambda b, i: (b, 0, i, 0)),
                   kspec, vspec, kspec, vspec,
                   pl.BlockSpec((1, GATE_ROWS, tt), lambda b, i: (b, 0, i))],
        out_shape=[jax.ShapeDtypeStruct((B, H, T, LANES), BF16),
                   kshape, vshape, kshape, vshape,
                   jax.ShapeDtypeStruct((B, GATE_ROWS, T), F32)],
        compiler_params=_cparams("parallel", "parallel"),
        name="nsa_prep",
    )(u, u, u, u, u, misc, jnp.tile(q_norm_g, NSA_HEADS).reshape(1, NSA_Q_W),
      jnp.tile(k_norm_g[1], G).reshape(1, LANES), jnp.tile(k_norm_g[2], G).reshape(1, LANES))


def _nsa_cmp_kernel(kin_ref, vin_ref, w1_ref, pe_ref, w2k_ref, w2vt_ref, kg_ref, kc_ref, vct_ref, x_ref):
    G = NSA_KV_HEADS
    nseg = kin_ref.shape[0] // CMP_STRIDE
    x_ref[0] = kin_ref[...].astype(F32)
    x_ref[1] = vin_ref[...].astype(F32)
    for j in range(2):
        a = jnp.concatenate([x_ref[j, pl.ds(l, nseg, stride=CMP_STRIDE), :] for l in range(CMP_STRIDE)],
                            axis=1).astype(BF16)
        pe_term = (jnp.dot(pe_ref[j, 0], w1_ref[j, 0], preferred_element_type=F32)
                   + jnp.dot(pe_ref[j, 1], w1_ref[j, 1], preferred_element_type=F32))[0:1, :]
        lo = jnp.dot(a, w1_ref[j, 0], preferred_element_type=F32)
        hi = jnp.dot(a, w1_ref[j, 1], preferred_element_type=F32)
        y = _gelu_tanh(lo + pltpu.roll(hi, nseg - 1, 0) + pe_term)
        for g in range(G):
            if j == 0:
                r = _bdot(y, w2k_ref[g])
                ms = jnp.sum(r * r, axis=-1, keepdims=True) * (1.0 / NSA_HEAD_DIM)
                lane = lax.broadcasted_iota(jnp.int32, r.shape, 1)
                last = lax.broadcasted_iota(jnp.int32, r.shape, 0) * CMP_STRIDE + (CMP_BLOCK - 1)
                kc_ref[0, g] = (r * lax.rsqrt(ms + RMS_EPS) * kg_ref[...] + _key_aug(lane, last)).astype(BF16)
            else:
                vct_ref[0, g] = _bdot_nt(w2vt_ref[g], y).astype(BF16)


def _cmp_weights(cmp_pe, cmp_w1, cmp_w2, k_norm_g):
    G, dh = NSA_KV_HEADS, NSA_HEAD_DIM
    L = cmp_w1.shape[0]
    width = CMP_STRIDE * G * dh
    w1 = cmp_w1.reshape(L, 2, 2, CMP_STRIDE, dh, dh).astype(BF16)
    w1 = jnp.stack([jnp.pad(w1, ((0, 0),) * 5 + ((g * dh, (G - 1 - g) * dh),)) for g in range(G)], axis=4)
    w1 = w1.reshape(L, 2, 2, width, G * dh)
    pe = jnp.broadcast_to(cmp_pe.reshape(L, 2, 2, CMP_STRIDE, 1, dh), (L, 2, 2, CMP_STRIDE, G, dh))
    pe = jnp.broadcast_to(pe.reshape(L, 2, 2, 1, width), (L, 2, 2, SUBLANES, width)).astype(BF16)
    w2k = jnp.stack([jnp.zeros((L, G * dh, LANES), F32).at[:, g * dh:(g + 1) * dh, :dh].set(cmp_w2[:, 0])
                     for g in range(G)], axis=1).astype(BF16)
    w2vt = jnp.stack([jnp.zeros((L, dh, G * dh), F32).at[:, :, g * dh:(g + 1) * dh].set(
        jnp.swapaxes(cmp_w2[:, 1], 1, 2)) for g in range(G)], axis=1).astype(BF16)
    kg = jnp.pad(k_norm_g[:, 0], ((0, 0), (0, LANES - dh))).reshape(L, 1, LANES)
    return w1, pe, w2k, w2vt, kg


def _nsa_compress(u, B, T, cmp_weights, layer):
    G, dh = NSA_KV_HEADS, NSA_HEAD_DIM
    nseg = T // CMP_STRIDE
    of_layer = lambda w: pl.BlockSpec((None,) + w.shape[1:], lambda b: (layer,) + (0,) * (w.ndim - 1))
    kv0 = COL_KV // LANES
    return pl.pallas_call(
        _nsa_cmp_kernel,
        grid=(B,),
        in_specs=[pl.BlockSpec((T, LANES), lambda b: (b, kv0)), pl.BlockSpec((T, LANES), lambda b: (b, kv0 + 1))]
        + [of_layer(w) for w in cmp_weights],
        out_specs=[pl.BlockSpec((1, G, nseg, LANES), lambda b: (b, 0, 0, 0)),
                   pl.BlockSpec((1, G, dh, nseg), lambda b: (b, 0, 0, 0))],
        out_shape=[jax.ShapeDtypeStruct((B, G, nseg, LANES), BF16), jax.ShapeDtypeStruct((B, G, dh, nseg), BF16)],
        scratch_shapes=[pltpu.VMEM((2, T, LANES), F32)],
        compiler_params=_cparams("parallel"),
        name="nsa_compress",
    )(u, u, *cmp_weights)


def _softmax_keys(s, ok):
    s = jnp.where(ok, s, NEG_INF)
    m = jnp.max(s, axis=0, keepdims=True)
    m = jnp.where(m > NEG_INF, m, 0.0)
    p = jnp.exp(s - m)
    return p * (1.0 / jnp.maximum(jnp.sum(p, axis=0, keepdims=True), 1e-30))


def _exp_keys(s, ok):
    s = jnp.where(ok, s, NEG_INF)
    m = jnp.max(s, axis=0, keepdims=True)
    m = jnp.where(m > NEG_INF, m, 0.0)
    p = jnp.exp(s - m)
    return p, 1.0 / jnp.maximum(jnp.sum(p, axis=0, keepdims=True), 1e-30)


def _nsa_attn_kernel(q_ref, gt_ref, kc_ref, vct_ref, ks_ref, vst_ref, kw_ref, vwt_ref, o_ref, selt_ref, s_ref, *,
                     seq_len):
    G, R, dh = NSA_KV_HEADS, NSA_REP, NSA_HEAD_DIM
    TQ = LANES
    NB = seq_len // SLC_BLOCK
    key_r = lax.broadcasted_iota(jnp.int32, (TQ, TQ), 0)
    q_l = lax.broadcasted_iota(jnp.int32, (TQ, TQ), 1)
    ok_d = q_l >= key_r
    jb = lax.broadcasted_iota(jnp.int32, (NB, TQ), 0)
    row8 = lax.broadcasted_iota(jnp.int32, (SUBLANES, TQ), 0)
    nl = lax.broadcasted_iota(jnp.int32, (NB, TQ), 1)
    overlap_t = ((nl * CMP_STRIDE < jb * SLC_BLOCK + SLC_BLOCK)
                 & (nl * CMP_STRIDE + CMP_BLOCK > jb * SLC_BLOCK)).astype(BF16)
    wlen = (WIN // TQ + 1) * TQ
    wkey = lax.broadcasted_iota(jnp.int32, (wlen, TQ), 0)
    wq = lax.broadcasted_iota(jnp.int32, (wlen, TQ), 1)

    def heads(s, ok, fn):
        return [fn(s[:, r * TQ:(r + 1) * TQ], ok) for r in range(R)]

    def gate_rows(sub, g, branch):
        rows = [MISC_GATE0 + 3 * (g * R + r) + branch for r in range(R)]
        return jnp.concatenate([gt_ref[0, c:c + 1, sub * TQ:(sub + 1) * TQ] for c in rows], axis=1)

    def tile_scores(j, g, qa):
        off = pl.multiple_of(j * SLC_TILE, SLC_TILE)
        return _bdot_nt(ks_ref[0, g, pl.ds(off, SLC_TILE), :], qa)

    def front(sub):
        i = pl.program_id(1) * ATTN_TILES + sub
        t0 = i * TQ
        d0 = pl.multiple_of(t0, TQ)
        ok_c = (t0 + q_l) - (key_r * CMP_STRIDE + (CMP_BLOCK - 1)) >= 0
        cur = jnp.right_shift(t0 + nl, SLC_SHIFT)
        future = jb > cur
        forced = (jb == 0) | (jb == cur) | (jb == cur - 1)
        wstart = pl.multiple_of(jnp.maximum(i - WIN // TQ, 0) * TQ, TQ)
        dist_w = (t0 + wq) - (wstart + wkey)
        ok_w = (dist_w >= 0) & (dist_w < WIN)

        qa_l, o_fix_l, s_d_l, init = [], [], [], []
        for g in range(G):
            qs = q_ref[0, g * R:(g + 1) * R, sub * TQ:(sub + 1) * TQ, :].reshape(R * TQ, LANES)

            p_c = heads(_bdot_nt(kc_ref[0, g], qs), ok_c, _softmax_keys)
            o_cmp = jnp.dot(vct_ref[0, g], jnp.concatenate(p_c, axis=1).astype(BF16),
                            preferred_element_type=F32)

            pw = heads(_bdot_nt(kw_ref[0, g, pl.ds(wstart, wlen), :], qs), ok_w, _exp_keys)
            o_win = jnp.dot(vwt_ref[0, g, :, pl.ds(wstart, wlen)],
                            jnp.concatenate([p for p, _ in pw], axis=1).astype(BF16), preferred_element_type=F32)
            inv_w = jnp.concatenate([inv for _, inv in pw], axis=1)
            o_fix_l.append(gate_rows(sub, g, 0) * o_cmp + (gate_rows(sub, g, 2) * inv_w) * o_win)

            importance = _dot_sel_lhs(overlap_t, p_c[0] + p_c[1] + p_c[2] + p_c[3])
            score = jnp.where(future, NEG_INF, jnp.where(forced, FORCE_SCORE, importance))
            groups = [score[v * SUBLANES:(v + 1) * SUBLANES] for v in range(NB // SUBLANES)]
            ranks = [jnp.zeros((SUBLANES, TQ), F32) for _ in groups]
            for k in range(NB):
                sk = score[k:k + 1, :]
                for v, sv in enumerate(groups):
                    if v < k // SUBLANES:
                        ahead = sk > sv
                    elif v > k // SUBLANES:
                        ahead = sk >= sv
                    else:
                        ahead = (sk > sv) | ((sk == sv) & (row8 > k % SUBLANES))
                    ranks[v] = ranks[v] + jnp.where(ahead, 1.0, 0.0)
            rank = jnp.concatenate(ranks, axis=0)
            sel_t = (rank < SLC_TOPN) & (score > NEG_INF)
            selt_ref[sub, g] = jnp.where(sel_t, 1.0, 0.0)

            sel_d = jnp.where(key_r < SLC_BLOCK, selt_ref[sub, g, pl.ds(2 * i, 1), :],
                              selt_ref[sub, g, pl.ds(2 * i + 1, 1), :])
            s_d = jnp.concatenate(heads(_bdot_nt(ks_ref[0, g, pl.ds(d0, TQ), :], qs), (sel_d > 0.5) & ok_d,
                                        lambda s, ok: jnp.where(ok, s, NEG_INF)), axis=1)
            s_d_l.append(s_d)

            masked_t = jnp.where(sel_t & (jb < 2 * i), 0.0, 1.0)
            cols_t = jnp.concatenate([jnp.zeros((AUG_SEL, TQ), F32), masked_t,
                                      jnp.zeros((LANES - AUG_SEL - NB, TQ), F32)], axis=0)
            cols = cols_t.T.astype(BF16)
            qa_l.append(jnp.concatenate([qs[r * TQ:(r + 1) * TQ] + cols for r in range(R)], axis=0))

        for g in range(G):
            s_ref[sub, g] = tile_scores(0, g, qa_l[g])
            m = jnp.max(s_d_l[g], axis=0, keepdims=True)
            p = jnp.exp(s_d_l[g] - m)
            init.append((m, jnp.sum(p, axis=0, keepdims=True),
                         jnp.dot(vst_ref[0, g, :, pl.ds(d0, TQ)], p.astype(BF16), preferred_element_type=F32)))
        return (t0 + SLC_TILE - 1) // SLC_TILE, qa_l, o_fix_l, tuple(init)

    def key_loop(sub, n_tiles, qa_l, init):
        def slc_step(j, carry):
            off = pl.multiple_of(j * SLC_TILE, SLC_TILE)
            nxt = jnp.minimum(j + 1, n_tiles - 1)
            out = []
            for g in range(G):
                m, l, acc = carry[g]
                s = s_ref[sub, g]
                s_ref[sub, g] = tile_scores(nxt, g, qa_l[g])
                m_new = jnp.maximum(m, jnp.max(s, axis=0, keepdims=True))
                alpha = jnp.exp(m - m_new)
                p = jnp.exp(s - m_new)
                l = alpha * l + jnp.sum(p, axis=0, keepdims=True)
                acc = alpha * acc + jnp.dot(vst_ref[0, g, :, pl.ds(off, SLC_TILE)], p.astype(BF16),
                                            preferred_element_type=F32)
                out.append((m_new, l, acc))
            return tuple(out)

        return lax.fori_loop(0, n_tiles, slc_step, init)

    def back(sub, o_fix_l, slc):
        for g in range(G):
            _, l, acc = slc[g]
            o = o_fix_l[g] + (gate_rows(sub, g, 1) * (1.0 / jnp.maximum(l, 1e-30))) * acc
            for a in range(R // 2):
                pair = jnp.concatenate([o[:, (2 * a) * TQ:(2 * a + 1) * TQ],
                                        o[:, (2 * a + 1) * TQ:(2 * a + 2) * TQ]], axis=0)
                lo = (g * R + 2 * a) * dh
                o_ref[sub * TQ:(sub + 1) * TQ, lo:lo + 2 * dh] = pair.T.astype(BF16)

    fronts = [front(sub) for sub in range(ATTN_TILES)]
    loops = [key_loop(sub, n_tiles, qa_l, init) for sub, (n_tiles, qa_l, _, init) in enumerate(fronts)]
    for sub in range(ATTN_TILES):
        back(sub, fronts[sub][2], loops[sub])


def _nsa_attention(qn, gates_t, kc, vct, ksn, vst, kwn, vwt, B, T):
    G, H, dh = NSA_KV_HEADS, NSA_HEADS, NSA_HEAD_DIM
    TQ = ATTN_TILES * LANES
    nq = T // TQ
    nseg = kc.shape[2]
    per_b = lambda shape: pl.BlockSpec((1,) + shape, lambda b, i: (b, 0, 0, 0))
    return pl.pallas_call(
        functools.partial(_nsa_attn_kernel, seq_len=T),
        grid=(B, nq),
        in_specs=[pl.BlockSpec((1, H, TQ, LANES), lambda b, i: (b, 0, i, 0)),
                  pl.BlockSpec((1, GATE_ROWS, TQ), lambda b, i: (b, 0, i)),
                  per_b((G, nseg, LANES)), per_b((G, dh, nseg)),
                  per_b((G, T, LANES)), per_b((G, dh, T)), per_b((G, T, LANES)), per_b((G, dh, T))],
        out_specs=pl.BlockSpec((TQ, NSA_Q_W), lambda b, i: (b * nq + i, 0)),
        out_shape=jax.ShapeDtypeStruct((B * T, NSA_Q_W), BF16),
        scratch_shapes=[pltpu.VMEM((ATTN_TILES, G, T // SLC_BLOCK, LANES), F32),
                        pltpu.VMEM((ATTN_TILES, G, SLC_TILE, NSA_REP * LANES), F32)],
        compiler_params=_cparams("parallel", "arbitrary"),
        name="nsa_attention",
    )(qn, gates_t, kc, vct, ksn, vst, kwn, vwt)


def _nsa_mixer(u, misc, B, T, q_norm_g, k_norm_g, cmp_weights, layer):
    assert T % LANES == 0 and T >= (WIN // LANES + 1) * LANES and T // CMP_STRIDE == LANES
    assert SLC_BLOCK == 1 << SLC_SHIFT and LANES == 2 * SLC_BLOCK
    qn, ksn, vst, kwn, vwt, gates_t = _nsa_prep(u, misc, B, T, q_norm_g, k_norm_g)
    kc, vct = _nsa_compress(u, B, T, cmp_weights, layer)
    return _nsa_attention(qn, gates_t, kc, vct, ksn, vst, kwn, vwt, B, T)


def _merge_kernel(x_ref, h_ref, oa_ref, ob_ref, oc_ref, od_ref, mod_ref, wg_ref, wb_ref, wo_ref, y_ref):
    h = h_ref[...]
    merged = None
    for i, o_ref in enumerate((oa_ref, ob_ref, oc_ref, od_ref)):
        gate = _sigmoid(jnp.dot(h, wg_ref[i], preferred_element_type=F32))
        term = gate * jnp.dot(o_ref[...], wb_ref[i], preferred_element_type=F32)
        merged = term if merged is None else merged + term
    y_ref[...] = x_ref[...] + mod_ref[0, 2:3, :] * _bdot(merged, wo_ref[...])


def _merge(x2d, h, outs, mod_l, wg, wb, wo, T):
    M, D = x2d.shape
    tm = min(512, T)
    per_b = T // tm
    row = lambda w: pl.BlockSpec((tm, w), lambda m: (m, 0))
    const = lambda shape: pl.BlockSpec(shape, lambda m: (0,) * len(shape), pipeline_mode=pl.Buffered(1))
    return pl.pallas_call(
        _merge_kernel,
        grid=(M // tm,),
        in_specs=[row(D), row(D), row(MIX_W), row(MIX_W), row(MIX_W), row(MIX_W),
                  pl.BlockSpec((1, 6, D), lambda m: (m // per_b, 0, 0)),
                  const((N_BRANCH, D, D)), const((N_BRANCH, MIX_W, D)), const((D, D))],
        out_specs=row(D),
        out_shape=jax.ShapeDtypeStruct((M, D), F32),
        compiler_params=_cparams("parallel"),
        name="gated_merge_out_proj",
    )(x2d, h, *outs, mod_l, wg, wb, wo)


FFN_CHUNK = 256


def _ffn_kernel(x_ref, mod_ref, g_ref, wa_ref, wb_ref, wo_ref, y_ref, acc_ref):
    x = x_ref[...]
    y = x * lax.rsqrt(jnp.mean(x * x, axis=-1, keepdims=True) + RMS_EPS) * g_ref[...]
    h = (y * (1.0 + mod_ref[0, 4:5, :]) + mod_ref[0, 3:4, :]).astype(BF16)
    d_ff = wo_ref.shape[0]
    for c in range(d_ff // FFN_CHUNK):
        cols = slice(c * FFN_CHUNK, (c + 1) * FFN_CHUNK)
        a = jnp.dot(h, wa_ref[:, cols], preferred_element_type=F32)
        b = jnp.dot(h, wb_ref[:, cols], preferred_element_type=F32)
        part = _bdot(_silu(a) * b, wo_ref[cols, :])
        if c == 0:
            acc_ref[...] = part
        else:
            acc_ref[...] += part
    y_ref[...] = x + mod_ref[0, 5:6, :] * acc_ref[...]


def _ffn(x2d, mod_l, norm_g, w_in, w_out, T):
    M, D = x2d.shape
    d_ff = w_out.shape[0]
    assert d_ff % FFN_CHUNK == 0
    tm = min(1024, T)
    per_b = T // tm
    const = lambda shape, idx: pl.BlockSpec(shape, lambda m: idx, pipeline_mode=pl.Buffered(1))
    return pl.pallas_call(
        _ffn_kernel,
        grid=(M // tm,),
        in_specs=[pl.BlockSpec((tm, D), lambda m: (m, 0)),
                  pl.BlockSpec((1, 6, D), lambda m: (m // per_b, 0, 0)),
                  pl.BlockSpec((1, D), lambda m: (0, 0)),
                  const((D, d_ff), (0, 0)), const((D, d_ff), (0, 1)), const((d_ff, D), (0, 0))],
        out_specs=pl.BlockSpec((tm, D), lambda m: (m, 0)),
        out_shape=jax.ShapeDtypeStruct((M, D), F32),
        scratch_shapes=[pltpu.VMEM((tm, D), F32)],
        compiler_params=_cparams("parallel"),
        name="swiglu_ffn",
    )(x2d, mod_l, norm_g.reshape(1, D), w_in, w_in, w_out)


def _split_w_in(w_in):
    gt0 = SSD_IN + SC_IN + SG_IN + NSA_Q_W + 6 * NSA_KV_W
    wa = w_in[:, :, :MIX_W + SSD_XBC]
    wb = w_in[:, :, SSD_IN:gt0]
    wm = jnp.concatenate([w_in[:, :, MIX_W + SSD_XBC:SSD_IN], w_in[:, :, gt0:gt0 + 3 * NSA_HEADS]], axis=2)
    wm = jnp.pad(wm, ((0, 0), (0, 0), (0, U_WIDTH - COL_MISC - wm.shape[2])))
    assert wa.shape[2] == COL_XBC + SSD_XBC and wa.shape[2] + wb.shape[2] == COL_MISC
    return wa.astype(BF16), wb.astype(BF16), wm.astype(BF16)


def kernel(x, c, ada_w, ada_b, norm_mix_g, norm_ffn_g, w_in, ssd_conv_w, ssd_conv_b, ssd_dt_bias, ssd_a_log, ssd_d,
           ssd_norm_g, sc_conv_w, sg_norm_g, sg_w, sg_b, nsa_q_norm_g, nsa_k_norm_g, nsa_cmp_pe, nsa_cmp_w1,
           nsa_cmp_w2, w_branch, w_branch_gate, w_out, w_ffn_in, w_ffn_out):
    B, T, D = x.shape
    L = w_in.shape[0]
    mod = _modulation(c, ada_w, ada_b).reshape(L, B, 6, D)
    x2d = x.reshape(B * T, D)
    w_parts = _split_w_in(w_in)
    cmp_weights = _cmp_weights(nsa_cmp_pe, nsa_cmp_w1, nsa_cmp_w2, nsa_k_norm_g)
    for l in range(L):
        u, h, misc = _in_proj(x2d, mod[l], norm_mix_g[l], w_parts, l, T)
        outs = (
            _ssd_mixer(u, misc, B, T, ssd_conv_w[l], ssd_conv_b[l], ssd_dt_bias[l], ssd_a_log[l], ssd_d[l],
                       ssd_norm_g[l]),
            _short_conv_mixer(u, B, T, sc_conv_w[l]),
            _spatial_gating_mixer(u, B, T, sg_norm_g[l], sg_w[l], sg_b[l]),
            _nsa_mixer(u, misc, B, T, nsa_q_norm_g[l], nsa_k_norm_g[l], cmp_weights, l),
        )
        x2d = _merge(x2d, h, outs, mod[l], w_branch_gate[l].astype(BF16), w_branch[l].astype(BF16),
                     w_out[l].astype(BF16), T)
        x2d = _ffn(x2d, mod[l], norm_ffn_g[l], w_ffn_in[l].astype(BF16), w_ffn_out[l].astype(BF16), T)
    return x2d.reshape(B, T, D)
```

```python
import functools
import math

import jax
import jax.numpy as jnp
from jax import lax
from jax.experimental import pallas as pl
from jax.experimental.pallas import tpu as pltpu

F32 = jnp.float32
BF16 = jnp.bfloat16
RMS_EPS = 1e-6
NEG_INF = float("-inf")

MIX_W = 512
N_BRANCH = 4

SSD_HEAD_DIM = 64
SSD_HEADS = 8
SSD_GROUPS = 2
SSD_STATE = 128
SSD_CONV = 4
SSD_CHUNK = 128
SSD_STEP_CHUNKS = 8
SSD_XBC = MIX_W + 2 * SSD_GROUPS * SSD_STATE
SSD_IN = MIX_W + SSD_XBC + SSD_HEADS
SC_CONV = 3
SC_IN = 3 * MIX_W
SG_GROUPS = 4
SG_CHUNK = 128
SG_IN = 2 * MIX_W
NSA_HEADS = 8
NSA_KV_HEADS = 2
NSA_REP = NSA_HEADS // NSA_KV_HEADS
NSA_HEAD_DIM = 64
CMP_BLOCK = 32
CMP_STRIDE = 16
SLC_BLOCK = 64
SLC_TOPN = 8
WIN = 256
FORCE_SCORE = 1e9
NSA_KV_W = NSA_KV_HEADS * NSA_HEAD_DIM
NSA_Q_W = NSA_HEADS * NSA_HEAD_DIM
NSA_IN = NSA_Q_W + 6 * NSA_KV_W + 3 * NSA_HEADS

LANES = 128
SUBLANES = 8
VMEM_LIMIT_BYTES = 56 * 1024 * 1024

COL_Z = 0
COL_XBC = COL_Z + MIX_W
COL_SC = COL_XBC + SSD_XBC
COL_SG = COL_SC + SC_IN
COL_Q = COL_SG + SG_IN
COL_KV = COL_Q + NSA_Q_W
COL_MISC = COL_KV + 6 * NSA_KV_W
MISC_GATE0 = SSD_HEADS
GATE_ROWS = 32
SLC_SHIFT = 6
SLC_TILE = 512
ATTN_TILES = 8
AUG_POS = NSA_HEAD_DIM
AUG_SEL = AUG_POS + SUBLANES
MASK_SCORE = -(2.0 ** 100)
IN_CHUNK = 256
U_WIDTH = COL_MISC + IN_CHUNK


def _cparams(*sem):
    return pltpu.CompilerParams(dimension_semantics=sem, vmem_limit_bytes=VMEM_LIMIT_BYTES)


def _bdot(a, b):
    return jnp.dot(a.astype(BF16), b.astype(BF16), preferred_element_type=F32)


def _bdot_nt(a, b):
    return lax.dot_general(a.astype(BF16), b.astype(BF16), (((1,), (1,)), ((), ())),
                           preferred_element_type=F32)


def _split3(a):
    hi = a.astype(BF16)
    r1 = a - hi.astype(F32)
    mid = r1.astype(BF16)
    lo = (r1 - mid.astype(F32)).astype(BF16)
    return hi, mid, lo


def _dot_sel_rhs(a, sel):
    hi, mid, lo = _split3(a)
    return (jnp.dot(hi, sel, preferred_element_type=F32) + jnp.dot(mid, sel, preferred_element_type=F32)
            + jnp.dot(lo, sel, preferred_element_type=F32))


def _dot_sel_lhs(sel, a):
    hi, mid, lo = _split3(a)
    return (jnp.dot(sel, hi, preferred_element_type=F32) + jnp.dot(sel, mid, preferred_element_type=F32)
            + jnp.dot(sel, lo, preferred_element_type=F32))


def _sigmoid(x):
    return 1.0 / (1.0 + jnp.exp(-x))


def _silu(x):
    return x * _sigmoid(x)


def _gelu_tanh(x):
    c = math.sqrt(2.0 / math.pi)
    return 0.5 * x * (1.0 + jnp.tanh(c * (x + 0.044715 * (x * x * x))))


def _softplus(x):
    return jnp.maximum(x, 0.0) + jnp.log1p(jnp.exp(-jnp.abs(x)))


def _shift_rows(x, tail, k, row8):
    sh = pltpu.roll(x, k, 0)
    tl = pltpu.roll(tail, k, 0)
    top = jnp.where(row8 < k, tl, sh[0:SUBLANES])
    return jnp.concatenate([top, sh[SUBLANES:]], axis=0)


def _mod_kernel(c_ref, w_ref, b_ref, o_ref):
    o_ref[0] = _bdot(_silu(c_ref[...]), w_ref[0]) + b_ref[0]


def _modulation(c, ada_w, ada_b):
    L, D, D6 = ada_w.shape
    B = c.shape[0]
    tn = D6 // 4
    return pl.pallas_call(
        _mod_kernel,
        grid=(L, D6 // tn),
        in_specs=[pl.BlockSpec((B, D), lambda l, n: (0, 0)),
                  pl.BlockSpec((1, D, tn), lambda l, n: (l, 0, n)),
                  pl.BlockSpec((1, 1, tn), lambda l, n: (l, 0, n))],
        out_specs=pl.BlockSpec((1, B, tn), lambda l, n: (l, 0, n)),
        out_shape=jax.ShapeDtypeStruct((L, B, D6), F32),
        compiler_params=_cparams("parallel", "parallel"),
        name="adaln_modulation",
    )(c, ada_w, ada_b.reshape(L, 1, D6))


def _in_kernel(x_ref, mod_ref, g_ref, wa_ref, wb_ref, wm_ref, u_ref, h_ref, misc_ref):
    x = x_ref[...]
    y = x * lax.rsqrt(jnp.mean(x * x, axis=-1, keepdims=True) + RMS_EPS) * g_ref[...]
    h = (y * (1.0 + mod_ref[0, 1:2, :]) + mod_ref[0, 0:1, :]).astype(BF16)
    h_ref[...] = h
    col = 0
    for w_ref in (wa_ref, wb_ref, wm_ref):
        for n in range(w_ref.shape[2] // IN_CHUNK):
            u = jnp.dot(h, w_ref[0, :, n * IN_CHUNK:(n + 1) * IN_CHUNK], preferred_element_type=F32)
            u_ref[:, col:col + IN_CHUNK] = u.astype(BF16)
            if col == COL_MISC:
                misc_ref[...] = u[:, :LANES]
            col += IN_CHUNK


def _in_proj(x2d, mod_l, norm_g, w_parts, layer, T):
    M, D = x2d.shape
    tm = min(1024, T)
    per_b = T // tm
    assert sum(w.shape[2] for w in w_parts) == U_WIDTH and all(w.shape[2] % IN_CHUNK == 0 for w in w_parts)
    wspec = lambda w: pl.BlockSpec((1, D, w.shape[2]), lambda m: (layer, 0, 0), pipeline_mode=pl.Buffered(1))
    return pl.pallas_call(
        _in_kernel,
        grid=(M // tm,),
        in_specs=[pl.BlockSpec((tm, D), lambda m: (m, 0)),
                  pl.BlockSpec((1, 6, D), lambda m: (m // per_b, 0, 0)),
                  pl.BlockSpec((1, D), lambda m: (0, 0))] + [wspec(w) for w in w_parts],
        out_specs=[pl.BlockSpec((tm, U_WIDTH), lambda m: (m, 0)),
                   pl.BlockSpec((tm, D), lambda m: (m, 0)),
                   pl.BlockSpec((tm, LANES), lambda m: (m, 0))],
        out_shape=[jax.ShapeDtypeStruct((M, U_WIDTH), BF16), jax.ShapeDtypeStruct((M, D), BF16),
                   jax.ShapeDtypeStruct((M, LANES), F32)],
        compiler_params=_cparams("parallel"),
        name="norm_in_proj",
    )(x2d, mod_l, norm_g.reshape(1, D), *w_parts)


def _ssd_kernel(z_ref, xa_ref, xb_ref, misc_ref, cw_ref, cb_ref, dtb_ref, alog_ref, dsk_ref, ng_ref,
                o_ref, tail_ref, st_ref):
    Q, P, N, H, G = SSD_CHUNK, SSD_HEAD_DIM, SSD_STATE, SSD_HEADS, SSD_GROUPS
    R = H // G

    @pl.when(pl.program_id(1) == 0)
    def _():
        tail_ref[...] = jnp.zeros_like(tail_ref)
        st_ref[...] = jnp.zeros_like(st_ref)

    xin = jnp.concatenate([xa_ref[...], xb_ref[...]], axis=1).astype(F32)
    tail = tail_ref[...]
    row8 = lax.broadcasted_iota(jnp.int32, (SUBLANES, SSD_XBC), 0)
    acc = xin * cw_ref[SSD_CONV - 1:SSD_CONV, :] + cb_ref[...]
    for k in range(1, SSD_CONV):
        acc = acc + _shift_rows(xin, tail, k, row8) * cw_ref[SSD_CONV - 1 - k:SSD_CONV - k, :]
    rows_in = xin.shape[0]
    tail_ref[...] = xin[rows_in - SUBLANES:rows_in, :]
    xbc = _silu(acc)

    lane = lax.broadcasted_iota(jnp.int32, (Q, LANES), 1)
    rowi = lax.broadcasted_iota(jnp.int32, (Q, LANES), 0)
    is_head = lane < H
    tri = (lane <= rowi).astype(BF16)
    tri_t = (rowi <= lane).astype(BF16)
    e_row = lax.broadcasted_iota(jnp.int32, (LANES, MIX_W), 0)
    e_col = lax.broadcasted_iota(jnp.int32, (LANES, MIX_W), 1)
    expand = (jnp.right_shift(e_col, 6) == e_row).astype(BF16)
    causal = lane <= rowi
    neg_a = -jnp.exp(alog_ref[...])

    for c in range(rows_in // Q):
        rows = slice(c * Q, (c + 1) * Q)
        xs = xbc[rows, :MIX_W]
        bm = xbc[rows, MIX_W:MIX_W + G * N]
        cm = xbc[rows, MIX_W + G * N:]
        dt = jnp.where(is_head, _softplus(misc_ref[rows, :] + dtb_ref[...]), 0.0)
        a = dt * neg_a
        a_cs = _dot_sel_lhs(tri, a)
        a_cs_t = _dot_sel_rhs(a.T, tri_t)
        a_last = a_cs[Q - 1:Q, :]
        ea = jnp.exp(a_cs)
        dec = jnp.exp(a_last - a_cs)
        xdt = xs * _dot_sel_rhs(dt, expand)
        ea_e = _dot_sel_rhs(ea, expand)
        xdec = xdt * _dot_sel_rhs(dec, expand)

        ys = []
        for g in range(G):
            bg = bm[:, g * N:(g + 1) * N]
            cg = cm[:, g * N:(g + 1) * N].astype(BF16)
            cb = _bdot_nt(cg, bg)
            bg_t = bg.T.astype(BF16)
            for r in range(R):
                h = g * R + r
                seg = jnp.where(causal, a_cs[:, h:h + 1] - a_cs_t[h:h + 1, :], NEG_INF)
                y_diag = _bdot(cb * jnp.exp(seg), xdt[:, h * P:(h + 1) * P])
                state = st_ref[h]
                y_off = _bdot(cg, state) * ea_e[:, h * P:(h + 1) * P]
                st_ref[h] = state * jnp.exp(a_last[:, h:h + 1]) + _bdot(bg_t, xdec[:, h * P:(h + 1) * P])
                ys.append(y_diag + y_off)
        y = jnp.concatenate(ys, axis=1) + xs * dsk_ref[...]
        y = y * _silu(z_ref[rows, :].astype(F32))
        gw = MIX_W // G
        outs = []
        for g in range(G):
            yg = y[:, g * gw:(g + 1) * gw]
            outs.append(yg * lax.rsqrt(jnp.mean(yg * yg, axis=-1, keepdims=True) + RMS_EPS))
        o_ref[rows, :] = (jnp.concatenate(outs, axis=1) * ng_ref[...]).astype(BF16)


def _ssd_mixer(u, misc, B, T, conv_w, conv_b, dt_bias, a_log, d_skip, norm_g):
    Q = min(SSD_STEP_CHUNKS * SSD_CHUNK, T)
    nc = T // Q

    def pad_lane(v):
        return jnp.pad(v, (0, LANES - v.shape[0])).reshape(1, LANES)

    row = lambda b, c: b * nc + c
    full = lambda shape: pl.BlockSpec(shape, lambda b, c: (0,) * len(shape))
    return pl.pallas_call(
        _ssd_kernel,
        grid=(B, nc),
        in_specs=[pl.BlockSpec((Q, MIX_W), lambda b, c: (row(b, c), COL_Z // MIX_W)),
                  pl.BlockSpec((Q, MIX_W), lambda b, c: (row(b, c), COL_XBC // MIX_W)),
                  pl.BlockSpec((Q, MIX_W), lambda b, c: (row(b, c), COL_XBC // MIX_W + 1)),
                  pl.BlockSpec((Q, LANES), lambda b, c: (row(b, c), 0)),
                  full((SSD_CONV, SSD_XBC)), full((1, SSD_XBC)), full((1, LANES)), full((1, LANES)),
                  full((1, MIX_W)), full((1, MIX_W))],
        out_specs=pl.BlockSpec((Q, MIX_W), lambda b, c: (row(b, c), 0)),
        out_shape=jax.ShapeDtypeStruct((B * T, MIX_W), BF16),
        scratch_shapes=[pltpu.VMEM((SUBLANES, SSD_XBC), F32),
                        pltpu.VMEM((SSD_HEADS, SSD_STATE, SSD_HEAD_DIM), F32)],
        compiler_params=_cparams("parallel", "arbitrary"),
        name="ssd_mixer",
    )(u, u, u, misc, conv_w, conv_b.reshape(1, SSD_XBC), pad_lane(dt_bias), pad_lane(a_log),
      jnp.repeat(d_skip, SSD_HEAD_DIM).reshape(1, MIX_W), norm_g.reshape(1, MIX_W))


def _sc_kernel(b_ref, c_ref, h_ref, w_ref, o_ref, tail_ref):
    @pl.when(pl.program_id(1) == 0)
    def _():
        tail_ref[...] = jnp.zeros_like(tail_ref)

    cx = c_ref[...].astype(F32) * h_ref[...].astype(F32)
    tt = cx.shape[0]
    tail = tail_ref[...]
    row8 = lax.broadcasted_iota(jnp.int32, (SUBLANES, MIX_W), 0)
    acc = cx * w_ref[SC_CONV - 1:SC_CONV, :]
    for k in range(1, SC_CONV):
        acc = acc + _shift_rows(cx, tail, k, row8) * w_ref[SC_CONV - 1 - k:SC_CONV - k, :]
    tail_ref[...] = cx[tt - SUBLANES:tt, :]
    o_ref[...] = (b_ref[...].astype(F32) * acc).astype(BF16)


def _short_conv_mixer(u, B, T, conv_w):
    tt = min(2048, T)
    nt = T // tt
    c0 = COL_SC // MIX_W
    spec = lambda j: pl.BlockSpec((tt, MIX_W), lambda b, i: (b * nt + i, c0 + j))
    return pl.pallas_call(
        _sc_kernel,
        grid=(B, nt),
        in_specs=[spec(0), spec(1), spec(2), pl.BlockSpec((SC_CONV, MIX_W), lambda b, i: (0, 0))],
        out_specs=pl.BlockSpec((tt, MIX_W), lambda b, i: (b * nt + i, 0)),
        out_shape=jax.ShapeDtypeStruct((B * T, MIX_W), BF16),
        scratch_shapes=[pltpu.VMEM((SUBLANES, MIX_W), F32)],
        compiler_params=_cparams("parallel", "arbitrary"),
        name="short_conv_mixer",
    )(u, u, u, conv_w)


def _sg_kernel(u_ref, v_ref, ng_ref, w_ref, bias_ref, o_ref):
    Q = SG_CHUNK
    rowi = lax.broadcasted_iota(jnp.int32, (Q, Q), 0)
    coli = lax.broadcasted_iota(jnp.int32, (Q, Q), 1)
    gd = MIX_W // SG_GROUPS
    ws = [jnp.where(coli <= rowi, w_ref[g], 0.0).astype(BF16) for g in range(SG_GROUPS)]
    for c in range(u_ref.shape[0] // Q):
        rows = slice(c * Q, (c + 1) * Q)
        v = _gelu_tanh(v_ref[rows, :].astype(F32))
        v = (v * lax.rsqrt(jnp.mean(v * v, axis=-1, keepdims=True) + RMS_EPS) * ng_ref[...]).astype(BF16)
        mixed = jnp.concatenate([jnp.dot(ws[g], v[:, g * gd:(g + 1) * gd], preferred_element_type=F32)
                                 for g in range(SG_GROUPS)], axis=1)
        o_ref[rows, :] = (_gelu_tanh(u_ref[rows, :].astype(F32)) * (mixed + bias_ref[...])).astype(BF16)


def _spatial_gating_mixer(u, B, T, norm_g, w_s, b_s):
    Q = SG_CHUNK
    tt = min(16 * Q, T)
    nt = T // tt
    c0 = COL_SG // MIX_W
    bias = jnp.repeat(b_s.T, MIX_W // SG_GROUPS, axis=1)
    return pl.pallas_call(
        _sg_kernel,
        grid=(B, nt),
        in_specs=[pl.BlockSpec((tt, MIX_W), lambda b, c: (b * nt + c, c0)),
                  pl.BlockSpec((tt, MIX_W), lambda b, c: (b * nt + c, c0 + 1)),
                  pl.BlockSpec((1, MIX_W), lambda b, c: (0, 0)),
                  pl.BlockSpec((SG_GROUPS, Q, Q), lambda b, c: (0, 0, 0)),
                  pl.BlockSpec((Q, MIX_W), lambda b, c: (0, 0))],
        out_specs=pl.BlockSpec((tt, MIX_W), lambda b, c: (b * nt + c, 0)),
        out_shape=jax.ShapeDtypeStruct((B * T, MIX_W), BF16),
        compiler_params=_cparams("parallel", "parallel"),
        name="spatial_gating_mixer",
    )(u, u, norm_g.reshape(1, MIX_W), w_s, bias)


def _group_mean_sq(x, width):
    r = lax.broadcasted_iota(jnp.int32, (LANES, LANES), 0)
    c = lax.broadcasted_iota(jnp.int32, (LANES, LANES), 1)
    sh = width.bit_length() - 1
    same = (jnp.right_shift(r, sh) == jnp.right_shift(c, sh)).astype(BF16)
    x2 = x * x
    slabs = [_dot_sel_rhs(x2[:, j * LANES:(j + 1) * LANES], same) for j in range(x.shape[1] // LANES)]
    return jnp.concatenate(slabs, axis=1) * (1.0 / width)


def _key_aug(lane, pos):
    return jnp.where((lane == AUG_POS) | (lane == AUG_POS + 1), 1.0,
                     jnp.where(lane == AUG_POS + 2, -(pos & ~(LANES - 1)).astype(F32),
                               jnp.where(lane == AUG_POS + 3, -(pos & (LANES - 1)).astype(F32), 0.0)))


def _nsa_prep_kernel(q_ref, ks_ref, vs_ref, kw_ref, vw_ref, misc_ref, qg_ref, ksg_ref, kwg_ref,
                     qa_ref, ksa_ref, vst_ref, kwa_ref, vwt_ref, gate_ref):
    dh = NSA_HEAD_DIM
    tt = q_ref.shape[0]
    lane = lax.broadcasted_iota(jnp.int32, (tt, LANES), 1)
    pos = pl.program_id(1) * tt + lax.broadcasted_iota(jnp.int32, (tt, LANES), 0)
    is_feat = lane < dh

    q = q_ref[...].astype(F32)
    qn = q * lax.rsqrt(_group_mean_sq(q, dh) + RMS_EPS) * qg_ref[...] * (dh ** -0.5)
    q_pos = jnp.where(lane == AUG_POS, (pos & ~(LANES - 1)).astype(F32),
                      jnp.where(lane == AUG_POS + 1, (pos & (LANES - 1)).astype(F32),
                                jnp.where((lane == AUG_POS + 2) | (lane == AUG_POS + 3), 1.0, 0.0)))
    for h in range(NSA_HEADS):
        pair = qn[:, (h // 2) * LANES:(h // 2 + 1) * LANES]
        feat = pair if h % 2 == 0 else pltpu.roll(pair, dh, 1)
        qa_ref[0, h] = jnp.where(is_feat, feat, -(2.0 ** -(h + 1)) * q_pos).astype(BF16)

    ks = ks_ref[...].astype(F32)
    ksn = ks * lax.rsqrt(_group_mean_sq(ks, dh) + RMS_EPS) * ksg_ref[...]
    kw = kw_ref[...].astype(F32)
    kwn = kw * lax.rsqrt(_group_mean_sq(kw, dh) + RMS_EPS) * kwg_ref[...]
    k_pos = _key_aug(lane, pos)
    k_pos_sel = jnp.where(lane == AUG_SEL + jnp.right_shift(pos, SLC_SHIFT), MASK_SCORE, k_pos)
    vs_t = vs_ref[...].astype(F32).T.astype(BF16)
    vw_t = vw_ref[...].astype(F32).T.astype(BF16)
    for g in range(NSA_KV_HEADS):
        sl = slice(g * dh, (g + 1) * dh)
        ksa_ref[0, g] = jnp.where(is_feat, ksn if g == 0 else pltpu.roll(ksn, dh, 1), k_pos_sel).astype(BF16)
        kwa_ref[0, g] = jnp.where(is_feat, kwn if g == 0 else pltpu.roll(kwn, dh, 1), k_pos).astype(BF16)
        vst_ref[0, g] = vs_t[sl, :]
        vwt_ref[0, g] = vw_t[sl, :]
    gate_ref[0] = _sigmoid(misc_ref[...]).T[0:GATE_ROWS, :]


def _nsa_prep(u, misc, B, T, q_norm_g, k_norm_g):
    tt = min(2048, T)
    nt = T // tt
    G, H, dh = NSA_KV_HEADS, NSA_HEADS, NSA_HEAD_DIM
    kv0 = COL_KV // LANES
    kvspec = lambda j: pl.BlockSpec((tt, LANES), lambda b, i: (b * nt + i, kv0 + j))
    vec = lambda n: pl.BlockSpec((1, n), lambda b, i: (0, 0))
    kspec = pl.BlockSpec((1, G, tt, LANES), lambda b, i: (b, 0, i, 0))
    kshape = jax.ShapeDtypeStruct((B, G, T, LANES), BF16)
    vspec = pl.BlockSpec((1, G, dh, tt), lambda b, i: (b, 0, 0, i))
    vshape = jax.ShapeDtypeStruct((B, G, dh, T), BF16)
    return pl.pallas_call(
        _nsa_prep_kernel,
        grid=(B, nt),
        in_specs=[pl.BlockSpec((tt, NSA_Q_W), lambda b, i: (b * nt + i, COL_Q // NSA_Q_W)),
                  kvspec(2), kvspec(3), kvspec(4), kvspec(5),
                  pl.BlockSpec((tt, LANES), lambda b, i: (b * nt + i, 0)),
                  vec(NSA_Q_W), vec(LANES), vec(LANES)],
        out_specs=[pl.BlockSpec((1, H, tt, LANES), lambda b, i: (b, 0, i, 0)),
                   kspec, vspec, kspec, vspec,
                   pl.BlockSpec((1, GATE_ROWS, tt), lambda b, i: (b, 0, i))],
        out_shape=[jax.ShapeDtypeStruct((B, H, T, LANES), BF16),
                   kshape, vshape, kshape, vshape,
                   jax.ShapeDtypeStruct((B, GATE_ROWS, T), F32)],
        compiler_params=_cparams("parallel", "parallel"),
        name="nsa_prep",
    )(u, u, u, u, u, misc, jnp.tile(q_norm_g, NSA_HEADS).reshape(1, NSA_Q_W),
      jnp.tile(k_norm_g[1], G).reshape(1, LANES), jnp.tile(k_norm_g[2], G).reshape(1, LANES))


def _nsa_cmp_kernel(kin_ref, vin_ref, w1_ref, pe_ref, w2k_ref, w2vt_ref, kg_ref, kc_ref, vct_ref, x_ref):
    G = NSA_KV_HEADS
    nseg = kin_ref.shape[0] // CMP_STRIDE
    x_ref[0] = kin_ref[...].astype(F32)
    x_ref[1] = vin_ref[...].astype(F32)
    for j in range(2):
        a = jnp.concatenate([x_ref[j, pl.ds(l, nseg, stride=CMP_STRIDE), :] for l in range(CMP_STRIDE)],
                            axis=1).astype(BF16)
        pe_term = (jnp.dot(pe_ref[j, 0], w1_ref[j, 0], preferred_element_type=F32)
                   + jnp.dot(pe_ref[j, 1], w1_ref[j, 1], preferred_element_type=F32))[0:1, :]
        lo = jnp.dot(a, w1_ref[j, 0], preferred_element_type=F32)
        hi = jnp.dot(a, w1_ref[j, 1], preferred_element_type=F32)
        y = _gelu_tanh(lo + pltpu.roll(hi, nseg - 1, 0) + pe_term)
        for g in range(G):
            if j == 0:
                r = _bdot(y, w2k_ref[g])
                ms = jnp.sum(r * r, axis=-1, keepdims=True) * (1.0 / NSA_HEAD_DIM)
                lane = lax.broadcasted_iota(jnp.int32, r.shape, 1)
                last = lax.broadcasted_iota(jnp.int32, r.shape, 0) * CMP_STRIDE + (CMP_BLOCK - 1)
                kc_ref[0, g] = (r * lax.rsqrt(ms + RMS_EPS) * kg_ref[...] + _key_aug(lane, last)).astype(BF16)
            else:
                vct_ref[0, g] = _bdot_nt(w2vt_ref[g], y).astype(BF16)


def _cmp_weights(cmp_pe, cmp_w1, cmp_w2, k_norm_g):
    G, dh = NSA_KV_HEADS, NSA_HEAD_DIM
    L = cmp_w1.shape[0]
    width = CMP_STRIDE * G * dh
    w1 = cmp_w1.reshape(L, 2, 2, CMP_STRIDE, dh, dh).astype(BF16)
    w1 = jnp.stack([jnp.pad(w1, ((0, 0),) * 5 + ((g * dh, (G - 1 - g) * dh),)) for g in range(G)], axis=4)
    w1 = w1.reshape(L, 2, 2, width, G * dh)
    pe = jnp.broadcast_to(cmp_pe.reshape(L, 2, 2, CMP_STRIDE, 1, dh), (L, 2, 2, CMP_STRIDE, G, dh))
    pe = jnp.broadcast_to(pe.reshape(L, 2, 2, 1, width), (L, 2, 2, SUBLANES, width)).astype(BF16)
    w2k = jnp.stack([jnp.zeros((L, G * dh, LANES), F32).at[:, g * dh:(g + 1) * dh, :dh].set(cmp_w2[:, 0])
                     for g in range(G)], axis=1).astype(BF16)
    w2vt = jnp.stack([jnp.zeros((L, dh, G * dh), F32).at[:, :, g * dh:(g + 1) * dh].set(
        jnp.swapaxes(cmp_w2[:, 1], 1, 2)) for g in range(G)], axis=1).astype(BF16)
    kg = jnp.pad(k_norm_g[:, 0], ((0, 0), (0, LANES - dh))).reshape(L, 1, LANES)
    return w1, pe, w2k, w2vt, kg


def _nsa_compress(u, B, T, cmp_weights, layer):
    G, dh = NSA_KV_HEADS, NSA_HEAD_DIM
    nseg = T // CMP_STRIDE
    of_layer = lambda w: pl.BlockSpec((None,) + w.shape[1:], lambda b: (layer,) + (0,) * (w.ndim - 1))
    kv0 = COL_KV // LANES
    return pl.pallas_call(
        _nsa_cmp_kernel,
        grid=(B,),
        in_specs=[pl.BlockSpec((T, LANES), lambda b: (b, kv0)), pl.BlockSpec((T, LANES), lambda b: (b, kv0 + 1))]
        + [of_layer(w) for w in cmp_weights],
        out_specs=[pl.BlockSpec((1, G, nseg, LANES), lambda b: (b, 0, 0, 0)),
                   pl.BlockSpec((1, G, dh, nseg), lambda b: (b, 0, 0, 0))],
        out_shape=[jax.ShapeDtypeStruct((B, G, nseg, LANES), BF16), jax.ShapeDtypeStruct((B, G, dh, nseg), BF16)],
        scratch_shapes=[pltpu.VMEM((2, T, LANES), F32)],
        compiler_params=_cparams("parallel"),
        name="nsa_compress",
    )(u, u, *cmp_weights)


def _softmax_keys(s, ok):
    s = jnp.where(ok, s, NEG_INF)
    m = jnp.max(s, axis=0, keepdims=True)
    m = jnp.where(m > NEG_INF, m, 0.0)
    p = jnp.exp(s - m)
    return p * (1.0 / jnp.maximum(jnp.sum(p, axis=0, keepdims=True), 1e-30))


def _exp_keys(s, ok):
    s = jnp.where(ok, s, NEG_INF)
    m = jnp.max(s, axis=0, keepdims=True)
    m = jnp.where(m > NEG_INF, m, 0.0)
    p = jnp.exp(s - m)
    return p, 1.0 / jnp.maximum(jnp.sum(p, axis=0, keepdims=True), 1e-30)


def _nsa_attn_kernel(q_ref, gt_ref, kc_ref, vct_ref, ks_ref, vst_ref, kw_ref, vwt_ref, o_ref, selt_ref, s_ref, *,
                     seq_len):
    G, R, dh = NSA_KV_HEADS, NSA_REP, NSA_HEAD_DIM
    TQ = LANES
    NB = seq_len // SLC_BLOCK
    key_r = lax.broadcasted_iota(jnp.int32, (TQ, TQ), 0)
    q_l = lax.broadcasted_iota(jnp.int32, (TQ, TQ), 1)
    ok_d = q_l >= key_r
    jb = lax.broadcasted_iota(jnp.int32, (NB, TQ), 0)
    row8 = lax.broadcasted_iota(jnp.int32, (SUBLANES, TQ), 0)
    nl = lax.broadcasted_iota(jnp.int32, (NB, TQ), 1)
    overlap_t = ((nl * CMP_STRIDE < jb * SLC_BLOCK + SLC_BLOCK)
                 & (nl * CMP_STRIDE + CMP_BLOCK > jb * SLC_BLOCK)).astype(BF16)
    wlen = (WIN // TQ + 1) * TQ
    wkey = lax.broadcasted_iota(jnp.int32, (wlen, TQ), 0)
    wq = lax.broadcasted_iota(jnp.int32, (wlen, TQ), 1)

    def heads(s, ok, fn):
        return [fn(s[:, r * TQ:(r + 1) * TQ], ok) for r in range(R)]

    def gate_rows(sub, g, branch):
        rows = [MISC_GATE0 + 3 * (g * R + r) + branch for r in range(R)]
        return jnp.concatenate([gt_ref[0, c:c + 1, sub * TQ:(sub + 1) * TQ] for c in rows], axis=1)

    def tile_scores(j, g, qa):
        off = pl.multiple_of(j * SLC_TILE, SLC_TILE)
        return _bdot_nt(ks_ref[0, g, pl.ds(off, SLC_TILE), :], qa)

    def front(sub):
        i = pl.program_id(1) * ATTN_TILES + sub
        t0 = i * TQ
        d0 = pl.multiple_of(t0, TQ)
        ok_c = (t0 + q_l) - (key_r * CMP_STRIDE + (CMP_BLOCK - 1)) >= 0
        cur = jnp.right_shift(t0 + nl, SLC_SHIFT)
        future = jb > cur
        forced = (jb == 0) | (jb == cur) | (jb == cur - 1)
        wstart = pl.multiple_of(jnp.maximum(i - WIN // TQ, 0) * TQ, TQ)
        dist_w = (t0 + wq) - (wstart + wkey)
        ok_w = (dist_w >= 0) & (dist_w < WIN)

        qa_l, o_fix_l, s_d_l, init = [], [], [], []
        for g in range(G):
            qs = q_ref[0, g * R:(g + 1) * R, sub * TQ:(sub + 1) * TQ, :].reshape(R * TQ, LANES)

            p_c = heads(_bdot_nt(kc_ref[0, g], qs), ok_c, _softmax_keys)
            o_cmp = jnp.dot(vct_ref[0, g], jnp.concatenate(p_c, axis=1).astype(BF16),
                            preferred_element_type=F32)

            pw = heads(_bdot_nt(kw_ref[0, g, pl.ds(wstart, wlen), :], qs), ok_w, _exp_keys)
            o_win = jnp.dot(vwt_ref[0, g, :, pl.ds(wstart, wlen)],
                            jnp.concatenate([p for p, _ in pw], axis=1).astype(BF16), preferred_element_type=F32)
            inv_w = jnp.concatenate([inv for _, inv in pw], axis=1)
            o_fix_l.append(gate_rows(sub, g, 0) * o_cmp + (gate_rows(sub, g, 2) * inv_w) * o_win)

            importance = _dot_sel_lhs(overlap_t, p_c[0] + p_c[1] + p_c[2] + p_c[3])
            score = jnp.where(future, NEG_INF, jnp.where(forced, FORCE_SCORE, importance))
            groups = [score[v * SUBLANES:(v + 1) * SUBLANES] for v in range(NB // SUBLANES)]
            ranks = [jnp.zeros((SUBLANES, TQ), F32) for _ in groups]
            for k in range(NB):
                sk = score[k:k + 1, :]
                for v, sv in enumerate(groups):
                    if v < k // SUBLANES:
                        ahead = sk > sv
                    elif v > k // SUBLANES:
                        ahead = sk >= sv
                    else:
                        ahead = (sk > sv) | ((sk == sv) & (row8 > k % SUBLANES))
                    ranks[v] = ranks[v] + jnp.where(ahead, 1.0, 0.0)
            rank = jnp.concatenate(ranks, axis=0)
            sel_t = (rank < SLC_TOPN) & (score > NEG_INF)
            selt_ref[sub, g] = jnp.where(sel_t, 1.0, 0.0)

            sel_d = jnp.where(key_r < SLC_BLOCK, selt_ref[sub, g, pl.ds(2 * i, 1), :],
                              selt_ref[sub, g, pl.ds(2 * i + 1, 1), :])
            s_d = jnp.concatenate(heads(_bdot_nt(ks_ref[0, g, pl.ds(d0, TQ), :], qs), (sel_d > 0.5) & ok_d,
                                        lambda s, ok: jnp.where(ok, s, NEG_INF)), axis=1)
            s_d_l.append(s_d)

            masked_t = jnp.where(sel_t & (jb < 2 * i), 0.0, 1.0)
            cols_t = jnp.concatenate([jnp.zeros((AUG_SEL, TQ), F32), masked_t,
                                      jnp.zeros((LANES - AUG_SEL - NB, TQ), F32)], axis=0)
            cols = cols_t.T.astype(BF16)
            qa_l.append(jnp.concatenate([qs[r * TQ:(r + 1) * TQ] + cols for r in range(R)], axis=0))

        for g in range(G):
            s_ref[sub, g] = tile_scores(0, g, qa_l[g])
            m = jnp.max(s_d_l[g], axis=0, keepdims=True)
            p = jnp.exp(s_d_l[g] - m)
            init.append((m, jnp.sum(p, axis=0, keepdims=True),
                         jnp.dot(vst_ref[0, g, :, pl.ds(d0, TQ)], p.astype(BF16), preferred_element_type=F32)))
        return (t0 + SLC_TILE - 1) // SLC_TILE, qa_l, o_fix_l, tuple(init)

    def key_loop(sub, n_tiles, qa_l, init):
        def slc_step(j, carry):
            off = pl.multiple_of(j * SLC_TILE, SLC_TILE)
            nxt = jnp.minimum(j + 1, n_tiles - 1)
            out = []
            for g in range(G):
                m, l, acc = carry[g]
                s = s_ref[sub, g]
                s_ref[sub, g] = tile_scores(nxt, g, qa_l[g])
                m_new = jnp.maximum(m, jnp.max(s, axis=0, keepdims=True))
                alpha = jnp.exp(m - m_new)
                p = jnp.exp(s - m_new)
                l = alpha * l + jnp.sum(p, axis=0, keepdims=True)
                acc = alpha * acc + jnp.dot(vst_ref[0, g, :, pl.ds(off, SLC_TILE)], p.astype(BF16),
                                            preferred_element_type=F32)
                out.append((m_new, l, acc))
            return tuple(out)

        return lax.fori_loop(0, n_tiles, slc_step, init)

    def back(sub, o_fix_l, slc):
        for g in range(G):
            _, l, acc = slc[g]
            o = o_fix_l[g] + (gate_rows(sub, g, 1) * (1.0 / jnp.maximum(l, 1e-30))) * acc
            for a in range(R // 2):
                pair = jnp.concatenate([o[:, (2 * a) * TQ:(2 * a + 1) * TQ],
                                        o[:, (2 * a + 1) * TQ:(2 * a + 2) * TQ]], axis=0)
                lo = (g * R + 2 * a) * dh
                o_ref[sub * TQ:(sub + 1) * TQ, lo:lo + 2 * dh] = pair.T.astype(BF16)

    fronts = [front(sub) for sub in range(ATTN_TILES)]
    loops = [key_loop(sub, n_tiles, qa_l, init) for sub, (n_tiles, qa_l, _, init) in enumerate(fronts)]
    for sub in range(ATTN_TILES):
        back(sub, fronts[sub][2], loops[sub])


def _nsa_attention(qn, gates_t, kc, vct, ksn, vst, kwn, vwt, B, T):
    G, H, dh = NSA_KV_HEADS, NSA_HEADS, NSA_HEAD_DIM
    TQ = ATTN_TILES * LANES
    nq = T // TQ
    nseg = kc.shape[2]
    per_b = lambda shape: pl.BlockSpec((1,) + shape, lambda b, i: (b, 0, 0, 0))
    return pl.pallas_call(
        functools.partial(_nsa_attn_kernel, seq_len=T),
        grid=(B, nq),
        in_specs=[pl.BlockSpec((1, H, TQ, LANES), lambda b, i: (b, 0, i, 0)),
                  pl.BlockSpec((1, GATE_ROWS, TQ), lambda b, i: (b, 0, i)),
                  per_b((G, nseg, LANES)), per_b((G, dh, nseg)),
                  per_b((G, T, LANES)), per_b((G, dh, T)), per_b((G, T, LANES)), per_b((G, dh, T))],
        out_specs=pl.BlockSpec((TQ, NSA_Q_W), lambda b, i: (b * nq + i, 0)),
        out_shape=jax.ShapeDtypeStruct((B * T, NSA_Q_W), BF16),
        scratch_shapes=[pltpu.VMEM((ATTN_TILES, G, T // SLC_BLOCK, LANES), F32),
                        pltpu.VMEM((ATTN_TILES, G, SLC_TILE, NSA_REP * LANES), F32)],
        compiler_params=_cparams("parallel", "arbitrary"),
        name="nsa_attention",
    )(qn, gates_t, kc, vct, ksn, vst, kwn, vwt)


def _nsa_mixer(u, misc, B, T, q_norm_g, k_norm_g, cmp_weights, layer):
    assert T % LANES == 0 and T >= (WIN // LANES + 1) * LANES and T // CMP_STRIDE == LANES
    assert SLC_BLOCK == 1 << SLC_SHIFT and LANES == 2 * SLC_BLOCK
    qn, ksn, vst, kwn, vwt, gates_t = _nsa_prep(u, misc, B, T, q_norm_g, k_norm_g)
    kc, vct = _nsa_compress(u, B, T, cmp_weights, layer)
    return _nsa_attention(qn, gates_t, kc, vct, ksn, vst, kwn, vwt, B, T)


def _merge_kernel(x_ref, h_ref, oa_ref, ob_ref, oc_ref, od_ref, mod_ref, wg_ref, wb_ref, wo_ref, y_ref):
    h = h_ref[...]
    merged = None
    for i, o_ref in enumerate((oa_ref, ob_ref, oc_ref, od_ref)):
        gate = _sigmoid(jnp.dot(h, wg_ref[i], preferred_element_type=F32))
        term = gate * jnp.dot(o_ref[...], wb_ref[i], preferred_element_type=F32)
        merged = term if merged is None else merged + term
    y_ref[...] = x_ref[...] + mod_ref[0, 2:3, :] * _bdot(merged, wo_ref[...])


def _merge(x2d, h, outs, mod_l, wg, wb, wo, T):
    M, D = x2d.shape
    tm = min(512, T)
    per_b = T // tm
    row = lambda w: pl.BlockSpec((tm, w), lambda m: (m, 0))
    const = lambda shape: pl.BlockSpec(shape, lambda m: (0,) * len(shape), pipeline_mode=pl.Buffered(1))
    return pl.pallas_call(
        _merge_kernel,
        grid=(M // tm,),
        in_specs=[row(D), row(D), row(MIX_W), row(MIX_W), row(MIX_W), row(MIX_W),
                  pl.BlockSpec((1, 6, D), lambda m: (m // per_b, 0, 0)),
                  const((N_BRANCH, D, D)), const((N_BRANCH, MIX_W, D)), const((D, D))],
        out_specs=row(D),
        out_shape=jax.ShapeDtypeStruct((M, D), F32),
        compiler_params=_cparams("parallel"),
        name="gated_merge_out_proj",
    )(x2d, h, *outs, mod_l, wg, wb, wo)


FFN_CHUNK = 256


def _ffn_kernel(x_ref, mod_ref, g_ref, wa_ref, wb_ref, wo_ref, y_ref, acc_ref):
    x = x_ref[...]
    y = x * lax.rsqrt(jnp.mean(x * x, axis=-1, keepdims=True) + RMS_EPS) * g_ref[...]
    h = (y * (1.0 + mod_ref[0, 4:5, :]) + mod_ref[0, 3:4, :]).astype(BF16)
    d_ff = wo_ref.shape[0]
    for c in range(d_ff // FFN_CHUNK):
        cols = slice(c * FFN_CHUNK, (c + 1) * FFN_CHUNK)
        a = jnp.dot(h, wa_ref[:, cols], preferred_element_type=F32)
        b = jnp.dot(h, wb_ref[:, cols], preferred_element_type=F32)
        part = _bdot(_silu(a) * b, wo_ref[cols, :])
        if c == 0:
            acc_ref[...] = part
        else:
            acc_ref[...] += part
    y_ref[...] = x + mod_ref[0, 5:6, :] * acc_ref[...]


def _ffn(x2d, mod_l, norm_g, w_in, w_out, T):
    M, D = x2d.shape
    d_ff = w_out.shape[0]
    assert d_ff % FFN_CHUNK == 0
    tm = min(1024, T)
    per_b = T // tm
    const = lambda shape, idx: pl.BlockSpec(shape, lambda m: idx, pipeline_mode=pl.Buffered(1))
    return pl.pallas_call(
        _ffn_kernel,
        grid=(M // tm,),
        in_specs=[pl.BlockSpec((tm, D), lambda m: (m, 0)),
                  pl.BlockSpec((1, 6, D), lambda m: (m // per_b, 0, 0)),
                  pl.BlockSpec((1, D), lambda m: (0, 0)),
                  const((D, d_ff), (0, 0)), const((D, d_ff), (0, 1)), const((d_ff, D), (0, 0))],
        out_specs=pl.BlockSpec((tm, D), lambda m: (m, 0)),
        out_shape=jax.ShapeDtypeStruct((M, D), F32),
        scratch_shapes=[pltpu.VMEM((tm, D), F32)],
        compiler_params=_cparams("parallel"),
        name="swiglu_ffn",
    )(x2d, mod_l, norm_g.reshape(1, D), w_in, w_in, w_out)


def _split_w_in(w_in):
    gt0 = SSD_IN + SC_IN + SG_IN + NSA_Q_W + 6 * NSA_KV_W
    wa = w_in[:, :, :MIX_W + SSD_XBC]
    wb = w_in[:, :, SSD_IN:gt0]
    wm = jnp.concatenate([w_in[:, :, MIX_W + SSD_XBC:SSD_IN], w_in[:, :, gt0:gt0 + 3 * NSA_HEADS]], axis=2)
    wm = jnp.pad(wm, ((0, 0), (0, 0), (0, U_WIDTH - COL_MISC - wm.shape[2])))
    assert wa.shape[2] == COL_XBC + SSD_XBC and wa.shape[2] + wb.shape[2] == COL_MISC
    return wa.astype(BF16), wb.astype(BF16), wm.astype(BF16)


def kernel(x, c, ada_w, ada_b, norm_mix_g, norm_ffn_g, w_in, ssd_conv_w, ssd_conv_b, ssd_dt_bias, ssd_a_log, ssd_d,
           ssd_norm_g, sc_conv_w, sg_norm_g, sg_w, sg_b, nsa_q_norm_g, nsa_k_norm_g, nsa_cmp_pe, nsa_cmp_w1,
           nsa_cmp_w2, w_branch, w_branch_gate, w_out, w_ffn_in, w_ffn_out):
    B, T, D = x.shape
    L = w_in.shape[0]
    mod = _modulation(c, ada_w, ada_b).reshape(L, B, 6, D)
    x2d = x.reshape(B * T, D)
    w_parts = _split_w_in(w_in)
    cmp_weights = _cmp_weights(nsa_cmp_pe, nsa_cmp_w1, nsa_cmp_w2, nsa_k_norm_g)
    for l in range(L):
        u, h, misc = _in_proj(x2d, mod[l], norm_mix_g[l], w_parts, l, T)
        outs = (
            _ssd_mixer(u, misc, B, T, ssd_conv_w[l], ssd_conv_b[l], ssd_dt_bias[l], ssd_a_log[l], ssd_d[l],
                       ssd_norm_g[l]),
            _short_conv_mixer(u, B, T, sc_conv_w[l]),
            _spatial_gating_mixer(u, B, T, sg_norm_g[l], sg_w[l], sg_b[l]),
            _nsa_mixer(u, misc, B, T, nsa_q_norm_g[l], nsa_k_norm_g[l], cmp_weights, l),
        )
        x2d = _merge(x2d, h, outs, mod[l], w_branch_gate[l].astype(BF16), w_branch[l].astype(BF16),
                     w_out[l].astype(BF16), T)
        x2d = _ffn(x2d, mod[l], norm_ffn_g[l], w_ffn_in[l].astype(BF16), w_ffn_out[l].astype(BF16), T)
    return x2d.reshape(B, T, D)
```

```python
import functools
import math

import jax
import jax.numpy as jnp
from jax import lax
from jax.experimental import pallas as pl
from jax.experimental.pallas import tpu as pltpu

F32 = jnp.float32
BF16 = jnp.bfloat16
RMS_EPS = 1e-6
NEG_INF = float("-inf")

MIX_W = 512
N_BRANCH = 4

SSD_HEAD_DIM = 64
SSD_HEADS = 8
SSD_GROUPS = 2
SSD_STATE = 128
SSD_CONV = 4
SSD_CHUNK = 128
SSD_STEP_CHUNKS = 8
SSD_XBC = MIX_W + 2 * SSD_GROUPS * SSD_STATE
SSD_IN = MIX_W + SSD_XBC + SSD_HEADS
SC_CONV = 3
SC_IN = 3 * MIX_W
SG_GROUPS = 4
SG_CHUNK = 128
SG_IN = 2 * MIX_W
NSA_HEADS = 8
NSA_KV_HEADS = 2
NSA_REP = NSA_HEADS // NSA_KV_HEADS
NSA_HEAD_DIM = 64
CMP_BLOCK = 32
CMP_STRIDE = 16
SLC_BLOCK = 64
SLC_TOPN = 8
WIN = 256
FORCE_SCORE = 1e9
NSA_KV_W = NSA_KV_HEADS * NSA_HEAD_DIM
NSA_Q_W = NSA_HEADS * NSA_HEAD_DIM
NSA_IN = NSA_Q_W + 6 * NSA_KV_W + 3 * NSA_HEADS

LANES = 128
SUBLANES = 8
VMEM_LIMIT_BYTES = 56 * 1024 * 1024

COL_Z = 0
COL_XBC = COL_Z + MIX_W
COL_SC = COL_XBC + SSD_XBC
COL_SG = COL_SC + SC_IN
COL_Q = COL_SG + SG_IN
COL_KV = COL_Q + NSA_Q_W
COL_MISC = COL_KV + 6 * NSA_KV_W
MISC_GATE0 = SSD_HEADS
GATE_ROWS = 32
SLC_SHIFT = 6
SLC_TILE = 512
ATTN_TILES = 8
AUG_POS = NSA_HEAD_DIM
AUG_SEL = AUG_POS + SUBLANES
MASK_SCORE = -(2.0 ** 100)
IN_CHUNK = 256
U_WIDTH = COL_MISC + IN_CHUNK


def _cparams(*sem):
    return pltpu.CompilerParams(dimension_semantics=sem, vmem_limit_bytes=VMEM_LIMIT_BYTES)


def _bdot(a, b):
    return jnp.dot(a.astype(BF16), b.astype(BF16), preferred_element_type=F32)


def _bdot_nt(a, b):
    return lax.dot_general(a.astype(BF16), b.astype(BF16), (((1,), (1,)), ((), ())),
                           preferred_element_type=F32)


def _split3(a):
    hi = a.astype(BF16)
    r1 = a - hi.astype(F32)
    mid = r1.astype(BF16)
    lo = (r1 - mid.astype(F32)).astype(BF16)
    return hi, mid, lo


def _dot_sel_rhs(a, sel):
    hi, mid, lo = _split3(a)
    return (jnp.dot(hi, sel, preferred_element_type=F32) + jnp.dot(mid, sel, preferred_element_type=F32)
            + jnp.dot(lo, sel, preferred_element_type=F32))


def _dot_sel_lhs(sel, a):
    hi, mid, lo = _split3(a)
    return (jnp.dot(sel, hi, preferred_element_type=F32) + jnp.dot(sel, mid, preferred_element_type=F32)
            + jnp.dot(sel, lo, preferred_element_type=F32))


def _sigmoid(x):
    return 1.0 / (1.0 + jnp.exp(-x))


def _silu(x):
    return x * _sigmoid(x)


def _gelu_tanh(x):
    c = math.sqrt(2.0 / math.pi)
    return 0.5 * x * (1.0 + jnp.tanh(c * (x + 0.044715 * (x * x * x))))


def _softplus(x):
    return jnp.maximum(x, 0.0) + jnp.log1p(jnp.exp(-jnp.abs(x)))


def _shift_rows(x, tail, k, row8):
    sh = pltpu.roll(x, k, 0)
    tl = pltpu.roll(tail, k, 0)
    top = jnp.where(row8 < k, tl, sh[0:SUBLANES])
    return jnp.concatenate([top, sh[SUBLANES:]], axis=0)


def _mod_kernel(c_ref, w_ref, b_ref, o_ref):
    o_ref[0] = _bdot(_silu(c_ref[...]), w_ref[0]) + b_ref[0]


def _modulation(c, ada_w, ada_b):
    L, D, D6 = ada_w.shape
    B = c.shape[0]
    tn = D6 // 4
    return pl.pallas_call(
        _mod_kernel,
        grid=(L, D6 // tn),
        in_specs=[pl.BlockSpec((B, D), lambda l, n: (0, 0)),
                  pl.BlockSpec((1, D, tn), lambda l, n: (l, 0, n)),
                  pl.BlockSpec((1, 1, tn), lambda l, n: (l, 0, n))],
        out_specs=pl.BlockSpec((1, B, tn), lambda l, n: (l, 0, n)),
        out_shape=jax.ShapeDtypeStruct((L, B, D6), F32),
        compiler_params=_cparams("parallel", "parallel"),
        name="adaln_modulation",
    )(c, ada_w, ada_b.reshape(L, 1, D6))


def _in_kernel(x_ref, mod_ref, g_ref, wa_ref, wb_ref, wm_ref, u_ref, h_ref, misc_ref):
    x = x_ref[...]
    y = x * lax.rsqrt(jnp.mean(x * x, axis=-1, keepdims=True) + RMS_EPS) * g_ref[...]
    h = (y * (1.0 + mod_ref[0, 1:2, :]) + mod_ref[0, 0:1, :]).astype(BF16)
    h_ref[...] = h
    col = 0
    for w_ref in (wa_ref, wb_ref, wm_ref):
        for n in range(w_ref.shape[2] // IN_CHUNK):
            u = jnp.dot(h, w_ref[0, :, n * IN_CHUNK:(n + 1) * IN_CHUNK], preferred_element_type=F32)
            u_ref[:, col:col + IN_CHUNK] = u.astype(BF16)
            if col == COL_MISC:
                misc_ref[...] = u[:, :LANES]
            col += IN_CHUNK


def _in_proj(x2d, mod_l, norm_g, w_parts, layer, T):
    M, D = x2d.shape
    tm = min(1024, T)
    per_b = T // tm
    assert sum(w.shape[2] for w in w_parts) == U_WIDTH and all(w.shape[2] % IN_CHUNK == 0 for w in w_parts)
    wspec = lambda w: pl.BlockSpec((1, D, w.shape[2]), lambda m: (layer, 0, 0), pipeline_mode=pl.Buffered(1))
    return pl.pallas_call(
        _in_kernel,
        grid=(M // tm,),
        in_specs=[pl.BlockSpec((tm, D), lambda m: (m, 0)),
                  pl.BlockSpec((1, 6, D), lambda m: (m // per_b, 0, 0)),
                  pl.BlockSpec((1, D), lambda m: (0, 0))] + [wspec(w) for w in w_parts],
        out_specs=[pl.BlockSpec((tm, U_WIDTH), lambda m: (m, 0)),
                   pl.BlockSpec((tm, D), lambda m: (m, 0)),
                   pl.BlockSpec((tm, LANES), lambda m: (m, 0))],
        out_shape=[jax.ShapeDtypeStruct((M, U_WIDTH), BF16), jax.ShapeDtypeStruct((M, D), BF16),
                   jax.ShapeDtypeStruct((M, LANES), F32)],
        compiler_params=_cparams("parallel"),
        name="norm_in_proj",
    )(x2d, mod_l, norm_g.reshape(1, D), *w_parts)


def _ssd_kernel(z_ref, xa_ref, xb_ref, misc_ref, cw_ref, cb_ref, dtb_ref, alog_ref, dsk_ref, ng_ref,
                o_ref, tail_ref, st_ref):
    Q, P, N, H, G = SSD_CHUNK, SSD_HEAD_DIM, SSD_STATE, SSD_HEADS, SSD_GROUPS
    R = H // G

    @pl.when(pl.program_id(1) == 0)
    def _():
        tail_ref[...] = jnp.zeros_like(tail_ref)
        st_ref[...] = jnp.zeros_like(st_ref)

    xin = jnp.concatenate([xa_ref[...], xb_ref[...]], axis=1).astype(F32)
    tail = tail_ref[...]
    row8 = lax.broadcasted_iota(jnp.int32, (SUBLANES, SSD_XBC), 0)
    acc = xin * cw_ref[SSD_CONV - 1:SSD_CONV, :] + cb_ref[...]
    for k in range(1, SSD_CONV):
        acc = acc + _shift_rows(xin, tail, k, row8) * cw_ref[SSD_CONV - 1 - k:SSD_CONV - k, :]
    rows_in = xin.shape[0]
    tail_ref[...] = xin[rows_in - SUBLANES:rows_in, :]
    xbc = _silu(acc)

    lane = lax.broadcasted_iota(jnp.int32, (Q, LANES), 1)
    rowi = lax.broadcasted_iota(jnp.int32, (Q, LANES), 0)
    is_head = lane < H
    tri = (lane <= rowi).astype(BF16)
    tri_t = (rowi <= lane).astype(BF16)
    e_row = lax.broadcasted_iota(jnp.int32, (LANES, MIX_W), 0)
    e_col = lax.broadcasted_iota(jnp.int32, (LANES, MIX_W), 1)
    expand = (jnp.right_shift(e_col, 6) == e_row).astype(BF16)
    causal = lane <= rowi
    neg_a = -jnp.exp(alog_ref[...])

    for c in range(rows_in // Q):
        rows = slice(c * Q, (c + 1) * Q)
        xs = xbc[rows, :MIX_W]
        bm = xbc[rows, MIX_W:MIX_W + G * N]
        cm = xbc[rows, MIX_W + G * N:]
        dt = jnp.where(is_head, _softplus(misc_ref[rows, :] + dtb_ref[...]), 0.0)
        a = dt * neg_a
        a_cs = _dot_sel_lhs(tri, a)
        a_cs_t = _dot_sel_rhs(a.T, tri_t)
        a_last = a_cs[Q - 1:Q, :]
        ea = jnp.exp(a_cs)
        dec = jnp.exp(a_last - a_cs)
        xdt = xs * _dot_sel_rhs(dt, expand)
        ea_e = _dot_sel_rhs(ea, expand)
        xdec = xdt * _dot_sel_rhs(dec, expand)

        ys = []
        for g in range(G):
            bg = bm[:, g * N:(g + 1) * N]
            cg = cm[:, g * N:(g + 1) * N].astype(BF16)
            cb = _bdot_nt(cg, bg)
            bg_t = bg.T.astype(BF16)
            for r in range(R):
                h = g * R + r
                seg = jnp.where(causal, a_cs[:, h:h + 1] - a_cs_t[h:h + 1, :], NEG_INF)
                y_diag = _bdot(cb * jnp.exp(seg), xdt[:, h * P:(h + 1) * P])
                state = st_ref[h]
                y_off = _bdot(cg, state) * ea_e[:, h * P:(h + 1) * P]
                st_ref[h] = state * jnp.exp(a_last[:, h:h + 1]) + _bdot(bg_t, xdec[:, h * P:(h + 1) * P])
                ys.append(y_diag + y_off)
        y = jnp.concatenate(ys, axis=1) + xs * dsk_ref[...]
        y = y * _silu(z_ref[rows, :].astype(F32))
        gw = MIX_W // G
        outs = []
        for g in range(G):
            yg = y[:, g * gw:(g + 1) * gw]
            outs.append(yg * lax.rsqrt(jnp.mean(yg * yg, axis=-1, keepdims=True) + RMS_EPS))
        o_ref[rows, :] = (jnp.concatenate(outs, axis=1) * ng_ref[...]).astype(BF16)


def _ssd_mixer(u, misc, B, T, conv_w, conv_b, dt_bias, a_log, d_skip, norm_g):
    Q = min(SSD_STEP_CHUNKS * SSD_CHUNK, T)
    nc = T // Q

    def pad_lane(v):
        return jnp.pad(v, (0, LANES - v.shape[0])).reshape(1, LANES)

    row = lambda b, c: b * nc + c
    full = lambda shape: pl.BlockSpec(shape, lambda b, c: (0,) * len(shape))
    return pl.pallas_call(
        _ssd_kernel,
        grid=(B, nc),
        in_specs=[pl.BlockSpec((Q, MIX_W), lambda b, c: (row(b, c), COL_Z // MIX_W)),
                  pl.BlockSpec((Q, MIX_W), lambda b, c: (row(b, c), COL_XBC // MIX_W)),
                  pl.BlockSpec((Q, MIX_W), lambda b, c: (row(b, c), COL_XBC // MIX_W + 1)),
                  pl.BlockSpec((Q, LANES), lambda b, c: (row(b, c), 0)),
                  full((SSD_CONV, SSD_XBC)), full((1, SSD_XBC)), full((1, LANES)), full((1, LANES)),
                  full((1, MIX_W)), full((1, MIX_W))],
        out_specs=pl.BlockSpec((Q, MIX_W), lambda b, c: (row(b, c), 0)),
        out_shape=jax.ShapeDtypeStruct((B * T, MIX_W), BF16),
        scratch_shapes=[pltpu.VMEM((SUBLANES, SSD_XBC), F32),
                        pltpu.VMEM((SSD_HEADS, SSD_STATE, SSD_HEAD_DIM), F32)],
        compiler_params=_cparams("parallel", "arbitrary"),
        name="ssd_mixer",
    )(u, u, u, misc, conv_w, conv_b.reshape(1, SSD_XBC), pad_lane(dt_bias), pad_lane(a_log),
      jnp.repeat(d_skip, SSD_HEAD_DIM).reshape(1, MIX_W), norm_g.reshape(1, MIX_W))


def _sc_kernel(b_ref, c_ref, h_ref, w_ref, o_ref, tail_ref):
    @pl.when(pl.program_id(1) == 0)
    def _():
        tail_ref[...] = jnp.zeros_like(tail_ref)

    cx = c_ref[...].astype(F32) * h_ref[...].astype(F32)
    tt = cx.shape[0]
    tail = tail_ref[...]
    row8 = lax.broadcasted_iota(jnp.int32, (SUBLANES, MIX_W), 0)
    acc = cx * w_ref[SC_CONV - 1:SC_CONV, :]
    for k in range(1, SC_CONV):
        acc = acc + _shift_rows(cx, tail, k, row8) * w_ref[SC_CONV - 1 - k:SC_CONV - k, :]
    tail_ref[...] = cx[tt - SUBLANES:tt, :]
    o_ref[...] = (b_ref[...].astype(F32) * acc).astype(BF16)


def _short_conv_mixer(u, B, T, conv_w):
    tt = min(2048, T)
    nt = T // tt
    c0 = COL_SC // MIX_W
    spec = lambda j: pl.BlockSpec((tt, MIX_W), lambda b, i: (b * nt + i, c0 + j))
    return pl.pallas_call(
        _sc_kernel,
        grid=(B, nt),
        in_specs=[spec(0), spec(1), spec(2), pl.BlockSpec((SC_CONV, MIX_W), lambda b, i: (0, 0))],
        out_specs=pl.BlockSpec((tt, MIX_W), lambda b, i: (b * nt + i, 0)),
        out_shape=jax.ShapeDtypeStruct((B * T, MIX_W), BF16),
        scratch_shapes=[pltpu.VMEM((SUBLANES, MIX_W), F32)],
        compiler_params=_cparams("parallel", "arbitrary"),
        name="short_conv_mixer",
    )(u, u, u, conv_w)


def _sg_kernel(u_ref, v_ref, ng_ref, w_ref, bias_ref, o_ref):
    Q = SG_CHUNK
    rowi = lax.broadcasted_iota(jnp.int32, (Q, Q), 0)
    coli = lax.broadcasted_iota(jnp.int32, (Q, Q), 1)
    gd = MIX_W // SG_GROUPS
    ws = [jnp.where(coli <= rowi, w_ref[g], 0.0).astype(BF16) for g in range(SG_GROUPS)]
    for c in range(u_ref.shape[0] // Q):
        rows = slice(c * Q, (c + 1) * Q)
        v = _gelu_tanh(v_ref[rows, :].astype(F32))
        v = (v * lax.rsqrt(jnp.mean(v * v, axis=-1, keepdims=True) + RMS_EPS) * ng_ref[...]).astype(BF16)
        mixed = jnp.concatenate([jnp.dot(ws[g], v[:, g * gd:(g + 1) * gd], preferred_element_type=F32)
                                 for g in range(SG_GROUPS)], axis=1)
        o_ref[rows, :] = (_gelu_tanh(u_ref[rows, :].astype(F32)) * (mixed + bias_ref[...])).astype(BF16)


def _spatial_gating_mixer(u, B, T, norm_g, w_s, b_s):
    Q = SG_CHUNK
    tt = min(16 * Q, T)
    nt = T // tt
    c0 = COL_SG // MIX_W
    bias = jnp.repeat(b_s.T, MIX_W // SG_GROUPS, axis=1)
    return pl.pallas_call(
        _sg_kernel,
        grid=(B, nt),
        in_specs=[pl.BlockSpec((tt, MIX_W), lambda b, c: (b * nt + c, c0)),
                  pl.BlockSpec((tt, MIX_W), lambda b, c: (b * nt + c, c0 + 1)),
                  pl.BlockSpec((1, MIX_W), lambda b, c: (0, 0)),
                  pl.BlockSpec((SG_GROUPS, Q, Q), lambda b, c: (0, 0, 0)),
                  pl.BlockSpec((Q, MIX_W), lambda b, c: (0, 0))],
        out_specs=pl.BlockSpec((tt, MIX_W), lambda b, c: (b * nt + c, 0)),
        out_shape=jax.ShapeDtypeStruct((B * T, MIX_W), BF16),
        compiler_params=_cparams("parallel", "parallel"),
        name="spatial_gating_mixer",
    )(u, u, norm_g.reshape(1, MIX_W), w_s, bias)


def _group_mean_sq(x, width):
    r = lax.broadcasted_iota(jnp.int32, (LANES, LANES), 0)
    c = lax.broadcasted_iota(jnp.int32, (LANES, LANES), 1)
    sh = width.bit_length() - 1
    same = (jnp.right_shift(r, sh) == jnp.right_shift(c, sh)).astype(BF16)
    x2 = x * x
    slabs = [_dot_sel_rhs(x2[:, j * LANES:(j + 1) * LANES], same) for j in range(x.shape[1] // LANES)]
    return jnp.concatenate(slabs, axis=1) * (1.0 / width)


def _key_aug(lane, pos):
    return jnp.where((lane == AUG_POS) | (lane == AUG_POS + 1), 1.0,
                     jnp.where(lane == AUG_POS + 2, -(pos & ~(LANES - 1)).astype(F32),
                               jnp.where(lane == AUG_POS + 3, -(pos & (LANES - 1)).astype(F32), 0.0)))


def _nsa_prep_kernel(q_ref, ks_ref, vs_ref, kw_ref, vw_ref, misc_ref, qg_ref, ksg_ref, kwg_ref,
                     qa_ref, ksa_ref, vst_ref, kwa_ref, vwt_ref, gate_ref):
    dh = NSA_HEAD_DIM
    tt = q_ref.shape[0]
    lane = lax.broadcasted_iota(jnp.int32, (tt, LANES), 1)
    pos = pl.program_id(1) * tt + lax.broadcasted_iota(jnp.int32, (tt, LANES), 0)
    is_feat = lane < dh

    q = q_ref[...].astype(F32)
    qn = q * lax.rsqrt(_group_mean_sq(q, dh) + RMS_EPS) * qg_ref[...] * (dh ** -0.5)
    q_pos = jnp.where(lane == AUG_POS, (pos & ~(LANES - 1)).astype(F32),
                      jnp.where(lane == AUG_POS + 1, (pos & (LANES - 1)).astype(F32),
                                jnp.where((lane == AUG_POS + 2) | (lane == AUG_POS + 3), 1.0, 0.0)))
    for h in range(NSA_HEADS):
        pair = qn[:, (h // 2) * LANES:(h // 2 + 1) * LANES]
        feat = pair if h % 2 == 0 else pltpu.roll(pair, dh, 1)
        qa_ref[0, h] = jnp.where(is_feat, feat, -(2.0 ** -(h + 1)) * q_pos).astype(BF16)

    ks = ks_ref[...].astype(F32)
    ksn = ks * lax.rsqrt(_group_mean_sq(ks, dh) + RMS_EPS) * ksg_ref[...]
    kw = kw_ref[...].astype(F32)
    kwn = kw * lax.rsqrt(_group_mean_sq(kw, dh) + RMS_EPS) * kwg_ref[...]
    k_pos = _key_aug(lane, pos)
    k_pos_sel = jnp.where(lane == AUG_SEL + jnp.right_shift(pos, SLC_SHIFT), MASK_SCORE, k_pos)
    vs_t = vs_ref[...].astype(F32).T.astype(BF16)
    vw_t = vw_ref[...].astype(F32).T.astype(BF16)
    for g in range(NSA_KV_HEADS):
        sl = slice(g * dh, (g + 1) * dh)
        ksa_ref[0, g] = jnp.where(is_feat, ksn if g == 0 else pltpu.roll(ksn, dh, 1), k_pos_sel).astype(BF16)
        kwa_ref[0, g] = jnp.where(is_feat, kwn if g == 0 else pltpu.roll(kwn, dh, 1), k_pos).astype(BF16)
        vst_ref[0, g] = vs_t[sl, :]
        vwt_ref[0, g] = vw_t[sl, :]
    gate_ref[0] = _sigmoid(misc_ref[...]).T[0:GATE_ROWS, :]


def _nsa_prep(u, misc, B, T, q_norm_g, k_norm_g):
    tt = min(2048, T)
    nt = T // tt
    G, H, dh = NSA_KV_HEADS, NSA_HEADS, NSA_HEAD_DIM
    kv0 = COL_KV // LANES
    kvspec = lambda j: pl.BlockSpec((tt, LANES), lambda b, i: (b * nt + i, kv0 + j))
    vec = lambda n: pl.BlockSpec((1, n), lambda b, i: (0, 0))
    kspec = pl.BlockSpec((1, G, tt, LANES), lambda b, i: (b, 0, i, 0))
    kshape = jax.ShapeDtypeStruct((B, G, T, LANES), BF16)
    vspec = pl.BlockSpec((1, G, dh, tt), lambda b, i: (b, 0, 0, i))
    vshape = jax.ShapeDtypeStruct((B, G, dh, T), BF16)
    return pl.pallas_call(
        _nsa_prep_kernel,
        grid=(B, nt),
        in_specs=[pl.BlockSpec((tt, NSA_Q_W), lambda b, i: (b * nt + i, COL_Q // NSA_Q_W)),
                  kvspec(2), kvspec(3), kvspec(4), kvspec(5),
                  pl.BlockSpec((tt, LANES), lambda b, i: (b * nt + i, 0)),
                  vec(NSA_Q_W), vec(LANES), vec(LANES)],
        out_specs=[pl.BlockSpec((1, H, tt, LANES), lambda b, i: (b, 0, i, 0)),
                   kspec, vspec, kspec, vspec,
                   pl.BlockSpec((1, GATE_ROWS, tt), lambda b, i: (b, 0, i))],
        out_shape=[jax.ShapeDtypeStruct((B, H, T, LANES), BF16),
                   kshape, vshape, kshape, vshape,
                   jax.ShapeDtypeStruct((B, GATE_ROWS, T), F32)],
        compiler_params=_cparams("parallel", "parallel"),
        name="nsa_prep",
    )(u, u, u, u, u, misc, jnp.tile(q_norm_g, NSA_HEADS).reshape(1, NSA_Q_W),
      jnp.tile(k_norm_g[1], G).reshape(1, LANES), jnp.tile(k_norm_g[2], G).reshape(1, LANES))


def _nsa_cmp_kernel(kin_ref, vin_ref, w1_ref, pe_ref, w2k_ref, w2vt_ref, kg_ref, kc_ref, vct_ref, x_ref):
    G = NSA_KV_HEADS
    nseg = kin_ref.shape[0] // CMP_STRIDE
    x_ref[0] = kin_ref[...].astype(F32)
    x_ref[1] = vin_ref[...].astype(F32)
    for j in range(2):
        a = jnp.concatenate([x_ref[j, pl.ds(l, nseg, stride=CMP_STRIDE), :] for l in range(CMP_STRIDE)],
                            axis=1).astype(BF16)
        pe_term = (jnp.dot(pe_ref[j, 0], w1_ref[j, 0], preferred_element_type=F32)
                   + jnp.dot(pe_ref[j, 1], w1_ref[j, 1], preferred_element_type=F32))[0:1, :]
        lo = jnp.dot(a, w1_ref[j, 0], preferred_element_type=F32)
        hi = jnp.dot(a, w1_ref[j, 1], preferred_element_type=F32)
        y = _gelu_tanh(lo + pltpu.roll(hi, nseg - 1, 0) + pe_term)
        for g in range(G):
            if j == 0:
                r = _bdot(y, w2k_ref[g])
                ms = jnp.sum(r * r, axis=-1, keepdims=True) * (1.0 / NSA_HEAD_DIM)
                lane = lax.broadcasted_iota(jnp.int32, r.shape, 1)
                last = lax.broadcasted_iota(jnp.int32, r.shape, 0) * CMP_STRIDE + (CMP_BLOCK - 1)
                kc_ref[0, g] = (r * lax.rsqrt(ms + RMS_EPS) * kg_ref[...] + _key_aug(lane, last)).astype(BF16)
            else:
                vct_ref[0, g] = _bdot_nt(w2vt_ref[g], y).astype(BF16)


def _cmp_weights(cmp_pe, cmp_w1, cmp_w2, k_norm_g):
    G, dh = NSA_KV_HEADS, NSA_HEAD_DIM
    L = cmp_w1.shape[0]
    width = CMP_STRIDE * G * dh
    w1 = cmp_w1.reshape(L, 2, 2, CMP_STRIDE, dh, dh).astype(BF16)
    w1 = jnp.stack([jnp.pad(w1, ((0, 0),) * 5 + ((g * dh, (G - 1 - g) * dh),)) for g in range(G)], axis=4)
    w1 = w1.reshape(L, 2, 2, width, G * dh)
    pe = jnp.broadcast_to(cmp_pe.reshape(L, 2, 2, CMP_STRIDE, 1, dh), (L, 2, 2, CMP_STRIDE, G, dh))
    pe = jnp.broadcast_to(pe.reshape(L, 2, 2, 1, width), (L, 2, 2, SUBLANES, width)).astype(BF16)
    w2k = jnp.stack([jnp.zeros((L, G * dh, LANES), F32).at[:, g * dh:(g + 1) * dh, :dh].set(cmp_w2[:, 0])
                     for g in range(G)], axis=1).astype(BF16)
    w2vt = jnp.stack([jnp.zeros((L, dh, G * dh), F32).at[:, :, g * dh:(g + 1) * dh].set(
        jnp.swapaxes(cmp_w2[:, 1], 1, 2)) for g in range(G)], axis=1).astype(BF16)
    kg = jnp.pad(k_norm_g[:, 0], ((0, 0), (0, LANES - dh))).reshape(L, 1, LANES)
    return w1, pe, w2k, w2vt, kg


def _nsa_compress(u, B, T, cmp_weights, layer):
    G, dh = NSA_KV_HEADS, NSA_HEAD_DIM
    nseg = T // CMP_STRIDE
    of_layer = lambda w: pl.BlockSpec((None,) + w.shape[1:], lambda b: (layer,) + (0,) * (w.ndim - 1))
    kv0 = COL_KV // LANES
    return pl.pallas_call(
        _nsa_cmp_kernel,
        grid=(B,),
        in_specs=[pl.BlockSpec((T, LANES), lambda b: (b, kv0)), pl.BlockSpec((T, LANES), lambda b: (b, kv0 + 1))]
        + [of_layer(w) for w in cmp_weights],
        out_specs=[pl.BlockSpec((1, G, nseg, LANES), lambda b: (b, 0, 0, 0)),
                   pl.BlockSpec((1, G, dh, nseg), lambda b: (b, 0, 0, 0))],
        out_shape=[jax.ShapeDtypeStruct((B, G, nseg, LANES), BF16), jax.ShapeDtypeStruct((B, G, dh, nseg), BF16)],
        scratch_shapes=[pltpu.VMEM((2, T, LANES), F32)],
        compiler_params=_cparams("parallel"),
        name="nsa_compress",
    )(u, u, *cmp_weights)


def _softmax_keys(s, ok):
    s = jnp.where(ok, s, NEG_INF)
    m = jnp.max(s, axis=0, keepdims=True)
    m = jnp.where(m > NEG_INF, m, 0.0)
    p = jnp.exp(s - m)
    return p * (1.0 / jnp.maximum(jnp.sum(p, axis=0, keepdims=True), 1e-30))


def _exp_keys(s, ok):
    s = jnp.where(ok, s, NEG_INF)
    m = jnp.max(s, axis=0, keepdims=True)
    m = jnp.where(m > NEG_INF, m, 0.0)
    p = jnp.exp(s - m)
    return p, 1.0 / jnp.maximum(jnp.sum(p, axis=0, keepdims=True), 1e-30)


def _nsa_attn_kernel(q_ref, gt_ref, kc_ref, vct_ref, ks_ref, vst_ref, kw_ref, vwt_ref, o_ref, selt_ref, s_ref, *,
                     seq_len):
    G, R, dh = NSA_KV_HEADS, NSA_REP, NSA_HEAD_DIM
    TQ = LANES
    NB = seq_len // SLC_BLOCK
    key_r = lax.broadcasted_iota(jnp.int32, (TQ, TQ), 0)
    q_l = lax.broadcasted_iota(jnp.int32, (TQ, TQ), 1)
    ok_d = q_l >= key_r
    jb = lax.broadcasted_iota(jnp.int32, (NB, TQ), 0)
    row8 = lax.broadcasted_iota(jnp.int32, (SUBLANES, TQ), 0)
    nl = lax.broadcasted_iota(jnp.int32, (NB, TQ), 1)
    overlap_t = ((nl * CMP_STRIDE < jb * SLC_BLOCK + SLC_BLOCK)
                 & (nl * CMP_STRIDE + CMP_BLOCK > jb * SLC_BLOCK)).astype(BF16)
    wlen = (WIN // TQ + 1) * TQ
    wkey = lax.broadcasted_iota(jnp.int32, (wlen, TQ), 0)
    wq = lax.broadcasted_iota(jnp.int32, (wlen, TQ), 1)

    def heads(s, ok, fn):
        return [fn(s[:, r * TQ:(r + 1) * TQ], ok) for r in range(R)]

    def gate_rows(sub, g, branch):
        rows = [MISC_GATE0 + 3 * (g * R + r) + branch for r in range(R)]
        return jnp.concatenate([gt_ref[0, c:c + 1, sub * TQ:(sub + 1) * TQ] for c in rows], axis=1)

    def tile_scores(j, g, qa):
        off = pl.multiple_of(j * SLC_TILE, SLC_TILE)
        return _bdot_nt(ks_ref[0, g, pl.ds(off, SLC_TILE), :], qa)

    def front(sub):
        i = pl.program_id(1) * ATTN_TILES + sub
        t0 = i * TQ
        d0 = pl.multiple_of(t0, TQ)
        ok_c = (t0 + q_l) - (key_r * CMP_STRIDE + (CMP_BLOCK - 1)) >= 0
        cur = jnp.right_shift(t0 + nl, SLC_SHIFT)
        future = jb > cur
        forced = (jb == 0) | (jb == cur) | (jb == cur - 1)
        wstart = pl.multiple_of(jnp.maximum(i - WIN // TQ, 0) * TQ, TQ)
        dist_w = (t0 + wq) - (wstart + wkey)
        ok_w = (dist_w >= 0) & (dist_w < WIN)

        qa_l, o_fix_l, s_d_l, init = [], [], [], []
        for g in range(G):
            qs = q_ref[0, g * R:(g + 1) * R, sub * TQ:(sub + 1) * TQ, :].reshape(R * TQ, LANES)

            p_c = heads(_bdot_nt(kc_ref[0, g], qs), ok_c, _softmax_keys)
            o_cmp = jnp.dot(vct_ref[0, g], jnp.concatenate(p_c, axis=1).astype(BF16),
                            preferred_element_type=F32)

            pw = heads(_bdot_nt(kw_ref[0, g, pl.ds(wstart, wlen), :], qs), ok_w, _exp_keys)
            o_win = jnp.dot(vwt_ref[0, g, :, pl.ds(wstart, wlen)],
                            jnp.concatenate([p for p, _ in pw], axis=1).astype(BF16), preferred_element_type=F32)
            inv_w = jnp.concatenate([inv for _, inv in pw], axis=1)
            o_fix_l.append(gate_rows(sub, g, 0) * o_cmp + (gate_rows(sub, g, 2) * inv_w) * o_win)

            importance = _dot_sel_lhs(overlap_t, p_c[0] + p_c[1] + p_c[2] + p_c[3])
            score = jnp.where(future, NEG_INF, jnp.where(forced, FORCE_SCORE, importance))
            groups = [score[v * SUBLANES:(v + 1) * SUBLANES] for v in range(NB // SUBLANES)]
            ranks = [jnp.zeros((SUBLANES, TQ), F32) for _ in groups]
            for k in range(NB):
                sk = score[k:k + 1, :]
                for v, sv in enumerate(groups):
                    if v < k // SUBLANES:
                        ahead = sk > sv
                    elif v > k // SUBLANES:
                        ahead = sk >= sv
                    else:
                        ahead = (sk > sv) | ((sk == sv) & (row8 > k % SUBLANES))
                    ranks[v] = ranks[v] + jnp.where(ahead, 1.0, 0.0)
            rank = jnp.concatenate(ranks, axis=0)
            sel_t = (rank < SLC_TOPN) & (score > NEG_INF)
            selt_ref[sub, g] = jnp.where(sel_t, 1.0, 0.0)

            sel_d = jnp.where(key_r < SLC_BLOCK, selt_ref[sub, g, pl.ds(2 * i, 1), :],
                              selt_ref[sub, g, pl.ds(2 * i + 1, 1), :])
            s_d = jnp.concatenate(heads(_bdot_nt(ks_ref[0, g, pl.ds(d0, TQ), :], qs), (sel_d > 0.5) & ok_d,
                                        lambda s, ok: jnp.where(ok, s, NEG_INF)), axis=1)
            s_d_l.append(s_d)

            masked_t = jnp.where(sel_t & (jb < 2 * i), 0.0, 1.0)
            cols_t = jnp.concatenate([jnp.zeros((AUG_SEL, TQ), F32), masked_t,
                                      jnp.zeros((LANES - AUG_SEL - NB, TQ), F32)], axis=0)
            cols = cols_t.T.astype(BF16)
            qa_l.append(jnp.concatenate([qs[r * TQ:(r + 1) * TQ] + cols for r in range(R)], axis=0))

        for g in range(G):
            s_ref[sub, g] = tile_scores(0, g, qa_l[g])
            m = jnp.max(s_d_l[g], axis=0, keepdims=True)
            p = jnp.exp(s_d_l[g] - m)
            init.append((m, jnp.sum(p, axis=0, keepdims=True),
                         jnp.dot(vst_ref[0, g, :, pl.ds(d0, TQ)], p.astype(BF16), preferred_element_type=F32)))
        return (t0 + SLC_TILE - 1) // SLC_TILE, qa_l, o_fix_l, tuple(init)

    def key_loop(sub, n_tiles, qa_l, init):
        def slc_step(j, carry):
            off = pl.multiple_of(j * SLC_TILE, SLC_TILE)
            nxt = jnp.minimum(j + 1, n_tiles - 1)
            out = []
            for g in range(G):
                m, l, acc = carry[g]
                s = s_ref[sub, g]
                s_ref[sub, g] = tile_scores(nxt, g, qa_l[g])
                m_new = jnp.maximum(m, jnp.max(s, axis=0, keepdims=True))
                alpha = jnp.exp(m - m_new)
                p = jnp.exp(s - m_new)
                l = alpha * l + jnp.sum(p, axis=0, keepdims=True)
                acc = alpha * acc + jnp.dot(vst_ref[0, g, :, pl.ds(off, SLC_TILE)], p.astype(BF16),
                                            preferred_element_type=F32)
                out.append((m_new, l, acc))
            return tuple(out)

        return lax.fori_loop(0, n_tiles, slc_step, init)

    def back(sub, o_fix_l, slc):
        for g in range(G):
            _, l, acc = slc[g]
            o = o_fix_l[g] + (gate_rows(sub, g, 1) * (1.0 / jnp.maximum(l, 1e-30))) * acc
            for a in range(R // 2):
                pair = jnp.concatenate([o[:, (2 * a) * TQ:(2 * a + 1) * TQ],
                                        o[:, (2 * a + 1) * TQ:(2 * a + 2) * TQ]], axis=0)
                lo = (g * R + 2 * a) * dh
                o_ref[sub * TQ:(sub + 1) * TQ, lo:lo + 2 * dh] = pair.T.astype(BF16)

    fronts = [front(sub) for sub in range(ATTN_TILES)]
    loops = [key_loop(sub, n_tiles, qa_l, init) for sub, (n_tiles, qa_l, _, init) in enumerate(fronts)]
    for sub in range(ATTN_TILES):
        back(sub, fronts[sub][2], loops[sub])


def _nsa_attention(qn, gates_t, kc, vct, ksn, vst, kwn, vwt, B, T):
    G, H, dh = NSA_KV_HEADS, NSA_HEADS, NSA_HEAD_DIM
    TQ = ATTN_TILES * LANES
    nq = T // TQ
    nseg = kc.shape[2]
    per_b = lambda shape: pl.BlockSpec((1,) + shape, lambda b, i: (b, 0, 0, 0))
    return pl.pallas_call(
        functools.partial(_nsa_attn_kernel, seq_len=T),
        grid=(B, nq),
        in_specs=[pl.BlockSpec((1, H, TQ, LANES), lambda b, i: (b, 0, i, 0)),
                  pl.BlockSpec((1, GATE_ROWS, TQ), lambda b, i: (b, 0, i)),
                  per_b((G, nseg, LANES)), per_b((G, dh, nseg)),
                  per_b((G, T, LANES)), per_b((G, dh, T)), per_b((G, T, LANES)), per_b((G, dh, T))],
        out_specs=pl.BlockSpec((TQ, NSA_Q_W), lambda b, i: (b * nq + i, 0)),
        out_shape=jax.ShapeDtypeStruct((B * T, NSA_Q_W), BF16),
        scratch_shapes=[pltpu.VMEM((ATTN_TILES, G, T // SLC_BLOCK, LANES), F32),
                        pltpu.VMEM((ATTN_TILES, G, SLC_TILE, NSA_REP * LANES), F32)],
        compiler_params=_cparams("parallel", "arbitrary"),
        name="nsa_attention",
    )(qn, gates_t, kc, vct, ksn, vst, kwn, vwt)


def _nsa_mixer(u, misc, B, T, q_norm_g, k_norm_g, cmp_weights, layer):
    assert T % LANES == 0 and T >= (WIN // LANES + 1) * LANES and T // CMP_STRIDE == LANES
    assert SLC_BLOCK == 1 << SLC_SHIFT and LANES == 2 * SLC_BLOCK
    qn, ksn, vst, kwn, vwt, gates_t = _nsa_prep(u, misc, B, T, q_norm_g, k_norm_g)
    kc, vct = _nsa_compress(u, B, T, cmp_weights, layer)
    return _nsa_attention(qn, gates_t, kc, vct, ksn, vst, kwn, vwt, B, T)


def _merge_kernel(x_ref, h_ref, oa_ref, ob_ref, oc_ref, od_ref, mod_ref, wg_ref, wb_ref, wo_ref, y_ref):
    h = h_ref[...]
    merged = None
    for i, o_ref in enumerate((oa_ref, ob_ref, oc_ref, od_ref)):
        gate = _sigmoid(jnp.dot(h, wg_ref[i], preferred_element_type=F32))
        term = gate * jnp.dot(o_ref[...], wb_ref[i], preferred_element_type=F32)
        merged = term if merged is None else merged + term
    y_ref[...] = x_ref[...] + mod_ref[0, 2:3, :] * _bdot(merged, wo_ref[...])


def _merge(x2d, h, outs, mod_l, wg, wb, wo, T):
    M, D = x2d.shape
    tm = min(1024, T)
    per_b = T // tm
    row = lambda w: pl.BlockSpec((tm, w), lambda m: (m, 0))
    const = lambda shape: pl.BlockSpec(shape, lambda m: (0,) * len(shape), pipeline_mode=pl.Buffered(1))
    return pl.pallas_call(
        _merge_kernel,
        grid=(M // tm,),
        in_specs=[row(D), row(D), row(MIX_W), row(MIX_W), row(MIX_W), row(MIX_W),
                  pl.BlockSpec((1, 6, D), lambda m: (m // per_b, 0, 0)),
                  const((N_BRANCH, D, D)), const((N_BRANCH, MIX_W, D)), const((D, D))],
        out_specs=row(D),
        out_shape=jax.ShapeDtypeStruct((M, D), F32),
        compiler_params=_cparams("parallel"),
        name="gated_merge_out_proj",
    )(x2d, h, *outs, mod_l, wg, wb, wo)


FFN_CHUNK = 256


def _ffn_kernel(x_ref, mod_ref, g_ref, wa_ref, wb_ref, wo_ref, y_ref, acc_ref):
    x = x_ref[...]
    y = x * lax.rsqrt(jnp.mean(x * x, axis=-1, keepdims=True) + RMS_EPS) * g_ref[...]
    h = (y * (1.0 + mod_ref[0, 4:5, :]) + mod_ref[0, 3:4, :]).astype(BF16)
    d_ff = wo_ref.shape[0]
    for c in range(d_ff // FFN_CHUNK):
        cols = slice(c * FFN_CHUNK, (c + 1) * FFN_CHUNK)
        a = jnp.dot(h, wa_ref[:, cols], preferred_element_type=F32)
        b = jnp.dot(h, wb_ref[:, cols], preferred_element_type=F32)
        part = _bdot(_silu(a) * b, wo_ref[cols, :])
        if c == 0:
            acc_ref[...] = part
        else:
            acc_ref[...] += part
    y_ref[...] = x + mod_ref[0, 5:6, :] * acc_ref[...]


def _ffn(x2d, mod_l, norm_g, w_in, w_out, T):
    M, D = x2d.shape
    d_ff = w_out.shape[0]
    assert d_ff % FFN_CHUNK == 0
    tm = min(1024, T)
    per_b = T // tm
    const = lambda shape, idx: pl.BlockSpec(shape, lambda m: idx, pipeline_mode=pl.Buffered(1))
    return pl.pallas_call(
        _ffn_kernel,
        grid=(M // tm,),
        in_specs=[pl.BlockSpec((tm, D), lambda m: (m, 0)),
                  pl.BlockSpec((1, 6, D), lambda m: (m // per_b, 0, 0)),
                  pl.BlockSpec((1, D), lambda m: (0, 0)),
                  const((D, d_ff), (0, 0)), const((D, d_ff), (0, 1)), const((d_ff, D), (0, 0))],
        out_specs=pl.BlockSpec((tm, D), lambda m: (m, 0)),
        out_shape=jax.ShapeDtypeStruct((M, D), F32),
        scratch_shapes=[pltpu.VMEM((tm, D), F32)],
        compiler_params=_cparams("parallel"),
        name="swiglu_ffn",
    )(x2d, mod_l, norm_g.reshape(1, D), w_in, w_in, w_out)


def _split_w_in(w_in):
    gt0 = SSD_IN + SC_IN + SG_IN + NSA_Q_W + 6 * NSA_KV_W
    wa = w_in[:, :, :MIX_W + SSD_XBC]
    wb = w_in[:, :, SSD_IN:gt0]
    wm = jnp.concatenate([w_in[:, :, MIX_W + SSD_XBC:SSD_IN], w_in[:, :, gt0:gt0 + 3 * NSA_HEADS]], axis=2)
    wm = jnp.pad(wm, ((0, 0), (0, 0), (0, U_WIDTH - COL_MISC - wm.shape[2])))
    assert wa.shape[2] == COL_XBC + SSD_XBC and wa.shape[2] + wb.shape[2] == COL_MISC
    return wa.astype(BF16), wb.astype(BF16), wm.astype(BF16)


def kernel(x, c, ada_w, ada_b, norm_mix_g, norm_ffn_g, w_in, ssd_conv_w, ssd_conv_b, ssd_dt_bias, ssd_a_log, ssd_d,
           ssd_norm_g, sc_conv_w, sg_norm_g, sg_w, sg_b, nsa_q_norm_g, nsa_k_norm_g, nsa_cmp_pe, nsa_cmp_w1,
           nsa_cmp_w2, w_branch, w_branch_gate, w_out, w_ffn_in, w_ffn_out):
    B, T, D = x.shape
    L = w_in.shape[0]
    mod = _modulation(c, ada_w, ada_b).reshape(L, B, 6, D)
    x2d = x.reshape(B * T, D)
    w_parts = _split_w_in(w_in)
    cmp_weights = _cmp_weights(nsa_cmp_pe, nsa_cmp_w1, nsa_cmp_w2, nsa_k_norm_g)
    for l in range(L):
        u, h, misc = _in_proj(x2d, mod[l], norm_mix_g[l], w_parts, l, T)
        outs = (
            _ssd_mixer(u, misc, B, T, ssd_conv_w[l], ssd_conv_b[l], ssd_dt_bias[l], ssd_a_log[l], ssd_d[l],
                       ssd_norm_g[l]),
            _short_conv_mixer(u, B, T, sc_conv_w[l]),
            _spatial_gating_mixer(u, B, T, sg_norm_g[l], sg_w[l], sg_b[l]),
            _nsa_mixer(u, misc, B, T, nsa_q_norm_g[l], nsa_k_norm_g[l], cmp_weights, l),
        )
        x2d = _merge(x2d, h, outs, mod[l], w_branch_gate[l].astype(BF16), w_branch[l].astype(BF16),
                     w_out[l].astype(BF16), T)
        x2d = _ffn(x2d, mod[l], norm_ffn_g[l], w_ffn_in[l].astype(BF16), w_ffn_out[l].astype(BF16), T)
    return x2d.reshape(B, T, D)
```

```python
import functools
import math

import jax
import jax.numpy as jnp
from jax import lax
from jax.experimental import pallas as pl
from jax.experimental.pallas import tpu as pltpu

F32 = jnp.float32
BF16 = jnp.bfloat16
RMS_EPS = 1e-6
NEG_INF = float("-inf")

MIX_W = 512
N_BRANCH = 4

SSD_HEAD_DIM = 64
SSD_HEADS = 8
SSD_GROUPS = 2
SSD_STATE = 128
SSD_CONV = 4
SSD_CHUNK = 128
SSD_STEP_CHUNKS = 8
SSD_XBC = MIX_W + 2 * SSD_GROUPS * SSD_STATE
SSD_IN = MIX_W + SSD_XBC + SSD_HEADS
SC_CONV = 3
SC_IN = 3 * MIX_W
SG_GROUPS = 4
SG_CHUNK = 128
SG_IN = 2 * MIX_W
NSA_HEADS = 8
NSA_KV_HEADS = 2
NSA_REP = NSA_HEADS // NSA_KV_HEADS
NSA_HEAD_DIM = 64
CMP_BLOCK = 32
CMP_STRIDE = 16
SLC_BLOCK = 64
SLC_TOPN = 8
WIN = 256
FORCE_SCORE = 1e9
NSA_KV_W = NSA_KV_HEADS * NSA_HEAD_DIM
NSA_Q_W = NSA_HEADS * NSA_HEAD_DIM
NSA_IN = NSA_Q_W + 6 * NSA_KV_W + 3 * NSA_HEADS

LANES = 128
SUBLANES = 8
VMEM_LIMIT_BYTES = 56 * 1024 * 1024

COL_Z = 0
COL_XBC = COL_Z + MIX_W
COL_SC = COL_XBC + SSD_XBC
COL_SG = COL_SC + SC_IN
COL_Q = COL_SG + SG_IN
COL_KV = COL_Q + NSA_Q_W
COL_MISC = COL_KV + 6 * NSA_KV_W
MISC_GATE0 = SSD_HEADS
GATE_ROWS = 32
SLC_SHIFT = 6
SLC_TILE = 512
ATTN_TILES = 8
AUG_POS = NSA_HEAD_DIM
AUG_SEL = AUG_POS + SUBLANES
MASK_SCORE = -(2.0 ** 100)
IN_CHUNK = 256
U_WIDTH = COL_MISC + IN_CHUNK


def _cparams(*sem):
    return pltpu.CompilerParams(dimension_semantics=sem, vmem_limit_bytes=VMEM_LIMIT_BYTES)


def _bdot(a, b):
    return jnp.dot(a.astype(BF16), b.astype(BF16), preferred_element_type=F32)


def _bdot_nt(a, b):
    return lax.dot_general(a.astype(BF16), b.astype(BF16), (((1,), (1,)), ((), ())),
                           preferred_element_type=F32)


def _split3(a):
    hi = a.astype(BF16)
    r1 = a - hi.astype(F32)
    mid = r1.astype(BF16)
    lo = (r1 - mid.astype(F32)).astype(BF16)
    return hi, mid, lo


def _dot_sel_rhs(a, sel):
    hi, mid, lo = _split3(a)
    return (jnp.dot(hi, sel, preferred_element_type=F32) + jnp.dot(mid, sel, preferred_element_type=F32)
            + jnp.dot(lo, sel, preferred_element_type=F32))


def _dot_sel_lhs(sel, a):
    hi, mid, lo = _split3(a)
    return (jnp.dot(sel, hi, preferred_element_type=F32) + jnp.dot(sel, mid, preferred_element_type=F32)
            + jnp.dot(sel, lo, preferred_element_type=F32))


def _sigmoid(x):
    return 1.0 / (1.0 + jnp.exp(-x))


def _silu(x):
    return x * _sigmoid(x)


def _gelu_tanh(x):
    c = math.sqrt(2.0 / math.pi)
    return 0.5 * x * (1.0 + jnp.tanh(c * (x + 0.044715 * (x * x * x))))


def _softplus(x):
    return jnp.maximum(x, 0.0) + jnp.log1p(jnp.exp(-jnp.abs(x)))


def _shift_rows(x, tail, k, row8):
    sh = pltpu.roll(x, k, 0)
    tl = pltpu.roll(tail, k, 0)
    top = jnp.where(row8 < k, tl, sh[0:SUBLANES])
    return jnp.concatenate([top, sh[SUBLANES:]], axis=0)


def _mod_kernel(c_ref, w_ref, b_ref, o_ref):
    o_ref[0] = _bdot(_silu(c_ref[...]), w_ref[0]) + b_ref[0]


def _modulation(c, ada_w, ada_b):
    L, D, D6 = ada_w.shape
    B = c.shape[0]
    tn = D6 // 4
    return pl.pallas_call(
        _mod_kernel,
        grid=(L, D6 // tn),
        in_specs=[pl.BlockSpec((B, D), lambda l, n: (0, 0)),
                  pl.BlockSpec((1, D, tn), lambda l, n: (l, 0, n)),
                  pl.BlockSpec((1, 1, tn), lambda l, n: (l, 0, n))],
        out_specs=pl.BlockSpec((1, B, tn), lambda l, n: (l, 0, n)),
        out_shape=jax.ShapeDtypeStruct((L, B, D6), F32),
        compiler_params=_cparams("parallel", "parallel"),
        name="adaln_modulation",
    )(c, ada_w, ada_b.reshape(L, 1, D6))


def _in_kernel(x_ref, mod_ref, g_ref, wa_ref, wb_ref, wm_ref, u_ref, h_ref, misc_ref):
    x = x_ref[...]
    y = x * lax.rsqrt(jnp.mean(x * x, axis=-1, keepdims=True) + RMS_EPS) * g_ref[...]
    h = (y * (1.0 + mod_ref[0, 1:2, :]) + mod_ref[0, 0:1, :]).astype(BF16)
    h_ref[...] = h
    col = 0
    for w_ref in (wa_ref, wb_ref, wm_ref):
        for n in range(w_ref.shape[2] // IN_CHUNK):
            u = jnp.dot(h, w_ref[0, :, n * IN_CHUNK:(n + 1) * IN_CHUNK], preferred_element_type=F32)
            u_ref[:, col:col + IN_CHUNK] = u.astype(BF16)
            if col == COL_MISC:
                misc_ref[...] = u[:, :LANES]
            col += IN_CHUNK


def _in_proj(x2d, mod_l, norm_g, w_parts, layer, T):
    M, D = x2d.shape
    tm = min(1024, T)
    per_b = T // tm
    assert sum(w.shape[2] for w in w_parts) == U_WIDTH and all(w.shape[2] % IN_CHUNK == 0 for w in w_parts)
    wspec = lambda w: pl.BlockSpec((1, D, w.shape[2]), lambda m: (layer, 0, 0), pipeline_mode=pl.Buffered(1))
    return pl.pallas_call(
        _in_kernel,
        grid=(M // tm,),
        in_specs=[pl.BlockSpec((tm, D), lambda m: (m, 0)),
                  pl.BlockSpec((1, 6, D), lambda m: (m // per_b, 0, 0)),
                  pl.BlockSpec((1, D), lambda m: (0, 0))] + [wspec(w) for w in w_parts],
        out_specs=[pl.BlockSpec((tm, U_WIDTH), lambda m: (m, 0)),
                   pl.BlockSpec((tm, D), lambda m: (m, 0)),
                   pl.BlockSpec((tm, LANES), lambda m: (m, 0))],
        out_shape=[jax.ShapeDtypeStruct((M, U_WIDTH), BF16), jax.ShapeDtypeStruct((M, D), BF16),
                   jax.ShapeDtypeStruct((M, LANES), F32)],
        compiler_params=_cparams("parallel"),
        name="norm_in_proj",
    )(x2d, mod_l, norm_g.reshape(1, D), *w_parts)


def _ssd_kernel(z_ref, xa_ref, xb_ref, misc_ref, cw_ref, cb_ref, dtb_ref, alog_ref, dsk_ref, ng_ref,
                o_ref, tail_ref, st_ref):
    Q, P, N, H, G = SSD_CHUNK, SSD_HEAD_DIM, SSD_STATE, SSD_HEADS, SSD_GROUPS
    R = H // G

    @pl.when(pl.program_id(1) == 0)
    def _():
        tail_ref[...] = jnp.zeros_like(tail_ref)
        st_ref[...] = jnp.zeros_like(st_ref)

    xin = jnp.concatenate([xa_ref[...], xb_ref[...]], axis=1).astype(F32)
    tail = tail_ref[...]
    row8 = lax.broadcasted_iota(jnp.int32, (SUBLANES, SSD_XBC), 0)
    acc = xin * cw_ref[SSD_CONV - 1:SSD_CONV, :] + cb_ref[...]
    for k in range(1, SSD_CONV):
        acc = acc + _shift_rows(xin, tail, k, row8) * cw_ref[SSD_CONV - 1 - k:SSD_CONV - k, :]
    rows_in = xin.shape[0]
    tail_ref[...] = xin[rows_in - SUBLANES:rows_in, :]
    xbc = _silu(acc)

    lane = lax.broadcasted_iota(jnp.int32, (Q, LANES), 1)
    rowi = lax.broadcasted_iota(jnp.int32, (Q, LANES), 0)
    is_head = lane < H
    tri = (lane <= rowi).astype(BF16)
    tri_t = (rowi <= lane).astype(BF16)
    e_row = lax.broadcasted_iota(jnp.int32, (LANES, MIX_W), 0)
    e_col = lax.broadcasted_iota(jnp.int32, (LANES, MIX_W), 1)
    expand = (jnp.right_shift(e_col, 6) == e_row).astype(BF16)
    causal = lane <= rowi
    neg_a = -jnp.exp(alog_ref[...])

    for c in range(rows_in // Q):
        rows = slice(c * Q, (c + 1) * Q)
        xs = xbc[rows, :MIX_W]
        bm = xbc[rows, MIX_W:MIX_W + G * N]
        cm = xbc[rows, MIX_W + G * N:]
        dt = jnp.where(is_head, _softplus(misc_ref[rows, :] + dtb_ref[...]), 0.0)
        a = dt * neg_a
        a_cs = _dot_sel_lhs(tri, a)
        a_cs_t = _dot_sel_rhs(a.T, tri_t)
        a_last = a_cs[Q - 1:Q, :]
        ea = jnp.exp(a_cs)
        dec = jnp.exp(a_last - a_cs)
        xdt = xs * _dot_sel_rhs(dt, expand)
        ea_e = _dot_sel_rhs(ea, expand)
        xdec = xdt * _dot_sel_rhs(dec, expand)

        ys = []
        for g in range(G):
            bg = bm[:, g * N:(g + 1) * N]
            cg = cm[:, g * N:(g + 1) * N].astype(BF16)
            cb = _bdot_nt(cg, bg)
            bg_t = bg.T.astype(BF16)
            for r in range(R):
                h = g * R + r
                seg = jnp.where(causal, a_cs[:, h:h + 1] - a_cs_t[h:h + 1, :], NEG_INF)
                y_diag = _bdot(cb * jnp.exp(seg), xdt[:, h * P:(h + 1) * P])
                state = st_ref[h]
                y_off = _bdot(cg, state) * ea_e[:, h * P:(h + 1) * P]
                st_ref[h] = state * jnp.exp(a_last[:, h:h + 1]) + _bdot(bg_t, xdec[:, h * P:(h + 1) * P])
                ys.append(y_diag + y_off)
        y = jnp.concatenate(ys, axis=1) + xs * dsk_ref[...]
        y = y * _silu(z_ref[rows, :].astype(F32))
        gw = MIX_W // G
        outs = []
        for g in range(G):
            yg = y[:, g * gw:(g + 1) * gw]
            outs.append(yg * lax.rsqrt(jnp.mean(yg * yg, axis=-1, keepdims=True) + RMS_EPS))
        o_ref[rows, :] = (jnp.concatenate(outs, axis=1) * ng_ref[...]).astype(BF16)


def _ssd_mixer(u, misc, B, T, conv_w, conv_b, dt_bias, a_log, d_skip, norm_g):
    Q = min(SSD_STEP_CHUNKS * SSD_CHUNK, T)
    nc = T // Q

    def pad_lane(v):
        return jnp.pad(v, (0, LANES - v.shape[0])).reshape(1, LANES)

    row = lambda b, c: b * nc + c
    full = lambda shape: pl.BlockSpec(shape, lambda b, c: (0,) * len(shape))
    return pl.pallas_call(
        _ssd_kernel,
        grid=(B, nc),
        in_specs=[pl.BlockSpec((Q, MIX_W), lambda b, c: (row(b, c), COL_Z // MIX_W)),
                  pl.BlockSpec((Q, MIX_W), lambda b, c: (row(b, c), COL_XBC // MIX_W)),
                  pl.BlockSpec((Q, MIX_W), lambda b, c: (row(b, c), COL_XBC // MIX_W + 1)),
                  pl.BlockSpec((Q, LANES), lambda b, c: (row(b, c), 0)),
                  full((SSD_CONV, SSD_XBC)), full((1, SSD_XBC)), full((1, LANES)), full((1, LANES)),
                  full((1, MIX_W)), full((1, MIX_W))],
        out_specs=pl.BlockSpec((Q, MIX_W), lambda b, c: (row(b, c), 0)),
        out_shape=jax.ShapeDtypeStruct((B * T, MIX_W), BF16),
        scratch_shapes=[pltpu.VMEM((SUBLANES, SSD_XBC), F32),
                        pltpu.VMEM((SSD_HEADS, SSD_STATE, SSD_HEAD_DIM), F32)],
        compiler_params=_cparams("parallel", "arbitrary"),
        name="ssd_mixer",
    )(u, u, u, misc, conv_w, conv_b.reshape(1, SSD_XBC), pad_lane(dt_bias), pad_lane(a_log),
      jnp.repeat(d_skip, SSD_HEAD_DIM).reshape(1, MIX_W), norm_g.reshape(1, MIX_W))


def _sc_kernel(b_ref, c_ref, h_ref, w_ref, o_ref, tail_ref):
    @pl.when(pl.program_id(1) == 0)
    def _():
        tail_ref[...] = jnp.zeros_like(tail_ref)

    cx = c_ref[...].astype(F32) * h_ref[...].astype(F32)
    tt = cx.shape[0]
    tail = tail_ref[...]
    row8 = lax.broadcasted_iota(jnp.int32, (SUBLANES, MIX_W), 0)
    acc = cx * w_ref[SC_CONV - 1:SC_CONV, :]
    for k in range(1, SC_CONV):
        acc = acc + _shift_rows(cx, tail, k, row8) * w_ref[SC_CONV - 1 - k:SC_CONV - k, :]
    tail_ref[...] = cx[tt - SUBLANES:tt, :]
    o_ref[...] = (b_ref[...].astype(F32) * acc).astype(BF16)


def _short_conv_mixer(u, B, T, conv_w):
    tt = min(2048, T)
    nt = T // tt
    c0 = COL_SC // MIX_W
    spec = lambda j: pl.BlockSpec((tt, MIX_W), lambda b, i: (b * nt + i, c0 + j))
    return pl.pallas_call(
        _sc_kernel,
        grid=(B, nt),
        in_specs=[spec(0), spec(1), spec(2), pl.BlockSpec((SC_CONV, MIX_W), lambda b, i: (0, 0))],
        out_specs=pl.BlockSpec((tt, MIX_W), lambda b, i: (b * nt + i, 0)),
        out_shape=jax.ShapeDtypeStruct((B * T, MIX_W), BF16),
        scratch_shapes=[pltpu.VMEM((SUBLANES, MIX_W), F32)],
        compiler_params=_cparams("parallel", "arbitrary"),
        name="short_conv_mixer",
    )(u, u, u, conv_w)


def _sg_kernel(u_ref, v_ref, ng_ref, w_ref, bias_ref, o_ref):
    Q = SG_CHUNK
    rowi = lax.broadcasted_iota(jnp.int32, (Q, Q), 0)
    coli = lax.broadcasted_iota(jnp.int32, (Q, Q), 1)
    gd = MIX_W // SG_GROUPS
    ws = [jnp.where(coli <= rowi, w_ref[g], 0.0).astype(BF16) for g in range(SG_GROUPS)]
    for c in range(u_ref.shape[0] // Q):
        rows = slice(c * Q, (c + 1) * Q)
        v = _gelu_tanh(v_ref[rows, :].astype(F32))
        v = (v * lax.rsqrt(jnp.mean(v * v, axis=-1, keepdims=True) + RMS_EPS) * ng_ref[...]).astype(BF16)
        mixed = jnp.concatenate([jnp.dot(ws[g], v[:, g * gd:(g + 1) * gd], preferred_element_type=F32)
                                 for g in range(SG_GROUPS)], axis=1)
        o_ref[rows, :] = (_gelu_tanh(u_ref[rows, :].astype(F32)) * (mixed + bias_ref[...])).astype(BF16)


def _spatial_gating_mixer(u, B, T, norm_g, w_s, b_s):
    Q = SG_CHUNK
    tt = min(16 * Q, T)
    nt = T // tt
    c0 = COL_SG // MIX_W
    bias = jnp.repeat(b_s.T, MIX_W // SG_GROUPS, axis=1)
    return pl.pallas_call(
        _sg_kernel,
        grid=(B, nt),
        in_specs=[pl.BlockSpec((tt, MIX_W), lambda b, c: (b * nt + c, c0)),
                  pl.BlockSpec((tt, MIX_W), lambda b, c: (b * nt + c, c0 + 1)),
                  pl.BlockSpec((1, MIX_W), lambda b, c: (0, 0)),
                  pl.BlockSpec((SG_GROUPS, Q, Q), lambda b, c: (0, 0, 0)),
                  pl.BlockSpec((Q, MIX_W), lambda b, c: (0, 0))],
        out_specs=pl.BlockSpec((tt, MIX_W), lambda b, c: (b * nt + c, 0)),
        out_shape=jax.ShapeDtypeStruct((B * T, MIX_W), BF16),
        compiler_params=_cparams("parallel", "parallel"),
        name="spatial_gating_mixer",
    )(u, u, norm_g.reshape(1, MIX_W), w_s, bias)


def _sc_sg_kernel(b_ref, c_ref, h_ref, cw_ref, u_ref, v_ref, ng_ref, ws_ref, bias_ref, ob_ref, oc_ref, tail_ref):
    _sc_kernel(b_ref, c_ref, h_ref, cw_ref, ob_ref, tail_ref)
    _sg_kernel(u_ref, v_ref, ng_ref, ws_ref, bias_ref, oc_ref)


def _conv_and_gating_mixers(u, B, T, conv_w, norm_g, w_s, b_s):
    Q = SG_CHUNK
    tt = min(16 * Q, T)
    nt = T // tt
    c0 = COL_SC // MIX_W
    assert COL_SG == COL_SC + SC_IN
    col = lambda j: pl.BlockSpec((tt, MIX_W), lambda b, i: (b * nt + i, c0 + j))
    const = lambda shape: pl.BlockSpec(shape, lambda b, i: (0,) * len(shape))
    out = pl.BlockSpec((tt, MIX_W), lambda b, i: (b * nt + i, 0))
    bias = jnp.repeat(b_s.T, MIX_W // SG_GROUPS, axis=1)
    return pl.pallas_call(
        _sc_sg_kernel,
        grid=(B, nt),
        in_specs=[col(0), col(1), col(2), const((SC_CONV, MIX_W)), col(3), col(4), const((1, MIX_W)),
                  const((SG_GROUPS, Q, Q)), const((Q, MIX_W))],
        out_specs=[out, out],
        out_shape=[jax.ShapeDtypeStruct((B * T, MIX_W), BF16)] * 2,
        scratch_shapes=[pltpu.VMEM((SUBLANES, MIX_W), F32)],
        compiler_params=_cparams("parallel", "arbitrary"),
        name="conv_and_gating_mixers",
    )(u, u, u, conv_w, u, u, norm_g.reshape(1, MIX_W), w_s, bias)


def _group_mean_sq(x, width):
    r = lax.broadcasted_iota(jnp.int32, (LANES, LANES), 0)
    c = lax.broadcasted_iota(jnp.int32, (LANES, LANES), 1)
    sh = width.bit_length() - 1
    same = (jnp.right_shift(r, sh) == jnp.right_shift(c, sh)).astype(BF16)
    x2 = x * x
    slabs = [_dot_sel_rhs(x2[:, j * LANES:(j + 1) * LANES], same) for j in range(x.shape[1] // LANES)]
    return jnp.concatenate(slabs, axis=1) * (1.0 / width)


def _key_aug(lane, pos):
    return jnp.where((lane == AUG_POS) | (lane == AUG_POS + 1), 1.0,
                     jnp.where(lane == AUG_POS + 2, -(pos & ~(LANES - 1)).astype(F32),
                               jnp.where(lane == AUG_POS + 3, -(pos & (LANES - 1)).astype(F32), 0.0)))


def _nsa_prep_kernel(q_ref, ks_ref, vs_ref, kw_ref, vw_ref, misc_ref, qg_ref, ksg_ref, kwg_ref,
                     qa_ref, ksa_ref, vst_ref, kwa_ref, vwt_ref, gate_ref):
    dh = NSA_HEAD_DIM
    tt = q_ref.shape[0]
    lane = lax.broadcasted_iota(jnp.int32, (tt, LANES), 1)
    pos = pl.program_id(1) * tt + lax.broadcasted_iota(jnp.int32, (tt, LANES), 0)
    is_feat = lane < dh

    q = q_ref[...].astype(F32)
    qn = q * lax.rsqrt(_group_mean_sq(q, dh) + RMS_EPS) * qg_ref[...] * (dh ** -0.5)
    q_pos = jnp.where(lane == AUG_POS, (pos & ~(LANES - 1)).astype(F32),
                      jnp.where(lane == AUG_POS + 1, (pos & (LANES - 1)).astype(F32),
                                jnp.where((lane == AUG_POS + 2) | (lane == AUG_POS + 3), 1.0, 0.0)))
    for h in range(NSA_HEADS):
        pair = qn[:, (h // 2) * LANES:(h // 2 + 1) * LANES]
        feat = pair if h % 2 == 0 else pltpu.roll(pair, dh, 1)
        qa_ref[0, h] = jnp.where(is_feat, feat, -(2.0 ** -(h + 1)) * q_pos).astype(BF16)

    ks = ks_ref[...].astype(F32)
    ksn = ks * lax.rsqrt(_group_mean_sq(ks, dh) + RMS_EPS) * ksg_ref[...]
    kw = kw_ref[...].astype(F32)
    kwn = kw * lax.rsqrt(_group_mean_sq(kw, dh) + RMS_EPS) * kwg_ref[...]
    k_pos = _key_aug(lane, pos)
    k_pos_sel = jnp.where(lane == AUG_SEL + jnp.right_shift(pos, SLC_SHIFT), MASK_SCORE, k_pos)
    vs_t = vs_ref[...].astype(F32).T.astype(BF16)
    vw_t = vw_ref[...].astype(F32).T.astype(BF16)
    for g in range(NSA_KV_HEADS):
        sl = slice(g * dh, (g + 1) * dh)
        ksa_ref[0, g] = jnp.where(is_feat, ksn if g == 0 else pltpu.roll(ksn, dh, 1), k_pos_sel).astype(BF16)
        kwa_ref[0, g] = jnp.where(is_feat, kwn if g == 0 else pltpu.roll(kwn, dh, 1), k_pos).astype(BF16)
        vst_ref[0, g] = vs_t[sl, :]
        vwt_ref[0, g] = vw_t[sl, :]
    gate_ref[0] = _sigmoid(misc_ref[...]).T[0:GATE_ROWS, :]


def _nsa_prep(u, misc, B, T, q_norm_g, k_norm_g):
    tt = min(2048, T)
    nt = T // tt
    G, H, dh = NSA_KV_HEADS, NSA_HEADS, NSA_HEAD_DIM
    kv0 = COL_KV // LANES
    kvspec = lambda j: pl.BlockSpec((tt, LANES), lambda b, i: (b * nt + i, kv0 + j))
    vec = lambda n: pl.BlockSpec((1, n), lambda b, i: (0, 0))
    kspec = pl.BlockSpec((1, G, tt, LANES), lambda b, i: (b, 0, i, 0))
    kshape = jax.ShapeDtypeStruct((B, G, T, LANES), BF16)
    vspec = pl.BlockSpec((1, G, dh, tt), lambda b, i: (b, 0, 0, i))
    vshape = jax.ShapeDtypeStruct((B, G, dh, T), BF16)
    return pl.pallas_call(
        _nsa_prep_kernel,
        grid=(B, nt),
        in_specs=[pl.BlockSpec((tt, NSA_Q_W), lambda b, i: (b * nt + i, COL_Q // NSA_Q_W)),
                  kvspec(2), kvspec(3), kvspec(4), kvspec(5),
                  pl.BlockSpec((tt, LANES), lambda b, i: (b * nt + i, 0)),
                  vec(NSA_Q_W), vec(LANES), vec(LANES)],
        out_specs=[pl.BlockSpec((1, H, tt, LANES), lambda b, i: (b, 0, i, 0)),
                   kspec, vspec, kspec, vspec,
                   pl.BlockSpec((1, GATE_ROWS, tt), lambda b, i: (b, 0, i))],
        out_shape=[jax.ShapeDtypeStruct((B, H, T, LANES), BF16),
                   kshape, vshape, kshape, vshape,
                   jax.ShapeDtypeStruct((B, GATE_ROWS, T), F32)],
        compiler_params=_cparams("parallel", "parallel"),
        name="nsa_prep",
    )(u, u, u, u, u, misc, jnp.tile(q_norm_g, NSA_HEADS).reshape(1, NSA_Q_W),
      jnp.tile(k_norm_g[1], G).reshape(1, LANES), jnp.tile(k_norm_g[2], G).reshape(1, LANES))


def _nsa_cmp_kernel(kin_ref, vin_ref, w1_ref, pe_ref, w2k_ref, w2vt_ref, kg_ref, kc_ref, vct_ref, x_ref):
    G = NSA_KV_HEADS
    nseg = kin_ref.shape[0] // CMP_STRIDE
    x_ref[0] = kin_ref[...].astype(F32)
    x_ref[1] = vin_ref[...].astype(F32)
    for j in range(2):
        a = jnp.concatenate([x_ref[j, pl.ds(l, nseg, stride=CMP_STRIDE), :] for l in range(CMP_STRIDE)],
                            axis=1).astype(BF16)
        pe_term = (jnp.dot(pe_ref[j, 0], w1_ref[j, 0], preferred_element_type=F32)
                   + jnp.dot(pe_ref[j, 1], w1_ref[j, 1], preferred_element_type=F32))[0:1, :]
        lo = jnp.dot(a, w1_ref[j, 0], preferred_element_type=F32)
        hi = jnp.dot(a, w1_ref[j, 1], preferred_element_type=F32)
        y = _gelu_tanh(lo + pltpu.roll(hi, nseg - 1, 0) + pe_term)
        for g in range(G):
            if j == 0:
                r = _bdot(y, w2k_ref[g])
                ms = jnp.sum(r * r, axis=-1, keepdims=True) * (1.0 / NSA_HEAD_DIM)
                lane = lax.broadcasted_iota(jnp.int32, r.shape, 1)
                last = lax.broadcasted_iota(jnp.int32, r.shape, 0) * CMP_STRIDE + (CMP_BLOCK - 1)
                kc_ref[0, g] = (r * lax.rsqrt(ms + RMS_EPS) * kg_ref[...] + _key_aug(lane, last)).astype(BF16)
            else:
                vct_ref[0, g] = _bdot_nt(w2vt_ref[g], y).astype(BF16)


def _cmp_weights(cmp_pe, cmp_w1, cmp_w2, k_norm_g):
    G, dh = NSA_KV_HEADS, NSA_HEAD_DIM
    L = cmp_w1.shape[0]
    width = CMP_STRIDE * G * dh
    w1 = cmp_w1.reshape(L, 2, 2, CMP_STRIDE, dh, dh).astype(BF16)
    w1 = jnp.stack([jnp.pad(w1, ((0, 0),) * 5 + ((g * dh, (G - 1 - g) * dh),)) for g in range(G)], axis=4)
    w1 = w1.reshape(L, 2, 2, width, G * dh)
    pe = jnp.broadcast_to(cmp_pe.reshape(L, 2, 2, CMP_STRIDE, 1, dh), (L, 2, 2, CMP_STRIDE, G, dh))
    pe = jnp.broadcast_to(pe.reshape(L, 2, 2, 1, width), (L, 2, 2, SUBLANES, width)).astype(BF16)
    w2k = jnp.stack([jnp.zeros((L, G * dh, LANES), F32).at[:, g * dh:(g + 1) * dh, :dh].set(cmp_w2[:, 0])
                     for g in range(G)], axis=1).astype(BF16)
    w2vt = jnp.stack([jnp.zeros((L, dh, G * dh), F32).at[:, :, g * dh:(g + 1) * dh].set(
        jnp.swapaxes(cmp_w2[:, 1], 1, 2)) for g in range(G)], axis=1).astype(BF16)
    kg = jnp.pad(k_norm_g[:, 0], ((0, 0), (0, LANES - dh))).reshape(L, 1, LANES)
    return w1, pe, w2k, w2vt, kg


def _nsa_compress(u, B, T, cmp_weights, layer):
    G, dh = NSA_KV_HEADS, NSA_HEAD_DIM
    nseg = T // CMP_STRIDE
    of_layer = lambda w: pl.BlockSpec((None,) + w.shape[1:], lambda b: (layer,) + (0,) * (w.ndim - 1))
    kv0 = COL_KV // LANES
    return pl.pallas_call(
        _nsa_cmp_kernel,
        grid=(B,),
        in_specs=[pl.BlockSpec((T, LANES), lambda b: (b, kv0)), pl.BlockSpec((T, LANES), lambda b: (b, kv0 + 1))]
        + [of_layer(w) for w in cmp_weights],
        out_specs=[pl.BlockSpec((1, G, nseg, LANES), lambda b: (b, 0, 0, 0)),
                   pl.BlockSpec((1, G, dh, nseg), lambda b: (b, 0, 0, 0))],
        out_shape=[jax.ShapeDtypeStruct((B, G, nseg, LANES), BF16), jax.ShapeDtypeStruct((B, G, dh, nseg), BF16)],
        scratch_shapes=[pltpu.VMEM((2, T, LANES), F32)],
        compiler_params=_cparams("parallel"),
        name="nsa_compress",
    )(u, u, *cmp_weights)


def _softmax_keys(s, ok):
    s = jnp.where(ok, s, NEG_INF)
    m = jnp.max(s, axis=0, keepdims=True)
    m = jnp.where(m > NEG_INF, m, 0.0)
    p = jnp.exp(s - m)
    return p * (1.0 / jnp.maximum(jnp.sum(p, axis=0, keepdims=True), 1e-30))


def _exp_keys(s, ok):
    s = jnp.where(ok, s, NEG_INF)
    m = jnp.max(s, axis=0, keepdims=True)
    m = jnp.where(m > NEG_INF, m, 0.0)
    p = jnp.exp(s - m)
    return p, 1.0 / jnp.maximum(jnp.sum(p, axis=0, keepdims=True), 1e-30)


def _nsa_attn_kernel(q_ref, gt_ref, kc_ref, vct_ref, ks_ref, vst_ref, kw_ref, vwt_ref, o_ref, selt_ref, s_ref, *,
                     seq_len):
    G, R, dh = NSA_KV_HEADS, NSA_REP, NSA_HEAD_DIM
    TQ = LANES
    NB = seq_len // SLC_BLOCK
    key_r = lax.broadcasted_iota(jnp.int32, (TQ, TQ), 0)
    q_l = lax.broadcasted_iota(jnp.int32, (TQ, TQ), 1)
    ok_d = q_l >= key_r
    jb = lax.broadcasted_iota(jnp.int32, (NB, TQ), 0)
    row8 = lax.broadcasted_iota(jnp.int32, (SUBLANES, TQ), 0)
    nl = lax.broadcasted_iota(jnp.int32, (NB, TQ), 1)
    overlap_t = ((nl * CMP_STRIDE < jb * SLC_BLOCK + SLC_BLOCK)
                 & (nl * CMP_STRIDE + CMP_BLOCK > jb * SLC_BLOCK)).astype(BF16)
    wlen = (WIN // TQ + 1) * TQ
    wkey = lax.broadcasted_iota(jnp.int32, (wlen, TQ), 0)
    wq = lax.broadcasted_iota(jnp.int32, (wlen, TQ), 1)

    def heads(s, ok, fn):
        return [fn(s[:, r * TQ:(r + 1) * TQ], ok) for r in range(R)]

    def gate_rows(sub, g, branch):
        rows = [MISC_GATE0 + 3 * (g * R + r) + branch for r in range(R)]
        return jnp.concatenate([gt_ref[0, c:c + 1, sub * TQ:(sub + 1) * TQ] for c in rows], axis=1)

    def tile_scores(j, g, qa):
        off = pl.multiple_of(j * SLC_TILE, SLC_TILE)
        return _bdot_nt(ks_ref[0, g, pl.ds(off, SLC_TILE), :], qa)

    def front(sub):
        i = pl.program_id(1) * ATTN_TILES + sub
        t0 = i * TQ
        d0 = pl.multiple_of(t0, TQ)
        ok_c = (t0 + q_l) - (key_r * CMP_STRIDE + (CMP_BLOCK - 1)) >= 0
        cur = jnp.right_shift(t0 + nl, SLC_SHIFT)
        future = jb > cur
        forced = (jb == 0) | (jb == cur) | (jb == cur - 1)
        wstart = pl.multiple_of(jnp.maximum(i - WIN // TQ, 0) * TQ, TQ)
        dist_w = (t0 + wq) - (wstart + wkey)
        ok_w = (dist_w >= 0) & (dist_w < WIN)

        qa_l, o_fix_l, s_d_l, init = [], [], [], []
        for g in range(G):
            qs = q_ref[0, g * R:(g + 1) * R, sub * TQ:(sub + 1) * TQ, :].reshape(R * TQ, LANES)

            p_c = heads(_bdot_nt(kc_ref[0, g], qs), ok_c, _softmax_keys)
            o_cmp = jnp.dot(vct_ref[0, g], jnp.concatenate(p_c, axis=1).astype(BF16),
                            preferred_element_type=F32)

            pw = heads(_bdot_nt(kw_ref[0, g, pl.ds(wstart, wlen), :], qs), ok_w, _exp_keys)
            o_win = jnp.dot(vwt_ref[0, g, :, pl.ds(wstart, wlen)],
                            jnp.concatenate([p for p, _ in pw], axis=1).astype(BF16), preferred_element_type=F32)
            inv_w = jnp.concatenate([inv for _, inv in pw], axis=1)
            o_fix_l.append(gate_rows(sub, g, 0) * o_cmp + (gate_rows(sub, g, 2) * inv_w) * o_win)

            importance = _dot_sel_lhs(overlap_t, p_c[0] + p_c[1] + p_c[2] + p_c[3])
            score = jnp.where(future, NEG_INF, jnp.where(forced, FORCE_SCORE, importance))
            groups = [score[v * SUBLANES:(v + 1) * SUBLANES] for v in range(NB // SUBLANES)]
            ranks = [jnp.zeros((SUBLANES, TQ), F32) for _ in groups]
            for k in range(NB):
                sk = score[k:k + 1, :]
                for v, sv in enumerate(groups):
                    if v < k // SUBLANES:
                        ahead = sk > sv
                    elif v > k // SUBLANES:
                        ahead = sk >= sv
                    else:
                        ahead = (sk > sv) | ((sk == sv) & (row8 > k % SUBLANES))
                    ranks[v] = ranks[v] + jnp.where(ahead, 1.0, 0.0)
            rank = jnp.concatenate(ranks, axis=0)
            sel_t = (rank < SLC_TOPN) & (score > NEG_INF)
            selt_ref[sub, g] = jnp.where(sel_t, 1.0, 0.0)

            sel_d = jnp.where(key_r < SLC_BLOCK, selt_ref[sub, g, pl.ds(2 * i, 1), :],
                              selt_ref[sub, g, pl.ds(2 * i + 1, 1), :])
            s_d = jnp.concatenate(heads(_bdot_nt(ks_ref[0, g, pl.ds(d0, TQ), :], qs), (sel_d > 0.5) & ok_d,
                                        lambda s, ok: jnp.where(ok, s, NEG_INF)), axis=1)
            s_d_l.append(s_d)

            masked_t = jnp.where(sel_t & (jb < 2 * i), 0.0, 1.0)
            cols_t = jnp.concatenate([jnp.zeros((AUG_SEL, TQ), F32), masked_t,
                                      jnp.zeros((LANES - AUG_SEL - NB, TQ), F32)], axis=0)
            cols = cols_t.T.astype(BF16)
            qa_l.append(jnp.concatenate([qs[r * TQ:(r + 1) * TQ] + cols for r in range(R)], axis=0))

        for g in range(G):
            s_ref[sub, g] = tile_scores(0, g, qa_l[g])
            m = jnp.max(s_d_l[g], axis=0, keepdims=True)
            p = jnp.exp(s_d_l[g] - m)
            init.append((m, jnp.sum(p, axis=0, keepdims=True),
                         jnp.dot(vst_ref[0, g, :, pl.ds(d0, TQ)], p.astype(BF16), preferred_element_type=F32)))
        return (t0 + SLC_TILE - 1) // SLC_TILE, qa_l, o_fix_l, tuple(init)

    def key_loop(sub, n_tiles, qa_l, init):
        def slc_step(j, carry):
            off = pl.multiple_of(j * SLC_TILE, SLC_TILE)
            nxt = jnp.minimum(j + 1, n_tiles - 1)
            out = []
            for g in range(G):
                m, l, acc = carry[g]
                s = s_ref[sub, g]
                s_ref[sub, g] = tile_scores(nxt, g, qa_l[g])
                m_new = jnp.maximum(m, jnp.max(s, axis=0, keepdims=True))
                alpha = jnp.exp(m - m_new)
                p = jnp.exp(s - m_new)
                l = alpha * l + jnp.sum(p, axis=0, keepdims=True)
                acc = alpha * acc + jnp.dot(vst_ref[0, g, :, pl.ds(off, SLC_TILE)], p.astype(BF16),
                                            preferred_element_type=F32)
                out.append((m_new, l, acc))
            return tuple(out)

        return lax.fori_loop(0, n_tiles, slc_step, init)

    def back(sub, o_fix_l, slc):
        for g in range(G):
            _, l, acc = slc[g]
            o = o_fix_l[g] + (gate_rows(sub, g, 1) * (1.0 / jnp.maximum(l, 1e-30))) * acc
            for a in range(R // 2):
                pair = jnp.concatenate([o[:, (2 * a) * TQ:(2 * a + 1) * TQ],
                                        o[:, (2 * a + 1) * TQ:(2 * a + 2) * TQ]], axis=0)
                lo = (g * R + 2 * a) * dh
                o_ref[sub * TQ:(sub + 1) * TQ, lo:lo + 2 * dh] = pair.T.astype(BF16)

    fronts = [front(sub) for sub in range(ATTN_TILES)]
    loops = [key_loop(sub, n_tiles, qa_l, init) for sub, (n_tiles, qa_l, _, init) in enumerate(fronts)]
    for sub in range(ATTN_TILES):
        back(sub, fronts[sub][2], loops[sub])


def _nsa_attention(qn, gates_t, kc, vct, ksn, vst, kwn, vwt, B, T):
    G, H, dh = NSA_KV_HEADS, NSA_HEADS, NSA_HEAD_DIM
    TQ = ATTN_TILES * LANES
    nq = T // TQ
    nseg = kc.shape[2]
    per_b = lambda shape: pl.BlockSpec((1,) + shape, lambda b, i: (b, 0, 0, 0))
    return pl.pallas_call(
        functools.partial(_nsa_attn_kernel, seq_len=T),
        grid=(B, nq),
        in_specs=[pl.BlockSpec((1, H, TQ, LANES), lambda b, i: (b, 0, i, 0)),
                  pl.BlockSpec((1, GATE_ROWS, TQ), lambda b, i: (b, 0, i)),
                  per_b((G, nseg, LANES)), per_b((G, dh, nseg)),
                  per_b((G, T, LANES)), per_b((G, dh, T)), per_b((G, T, LANES)), per_b((G, dh, T))],
        out_specs=pl.BlockSpec((TQ, NSA_Q_W), lambda b, i: (b * nq + i, 0)),
        out_shape=jax.ShapeDtypeStruct((B * T, NSA_Q_W), BF16),
        scratch_shapes=[pltpu.VMEM((ATTN_TILES, G, T // SLC_BLOCK, LANES), F32),
                        pltpu.VMEM((ATTN_TILES, G, SLC_TILE, NSA_REP * LANES), F32)],
        compiler_params=_cparams("parallel", "arbitrary"),
        name="nsa_attention",
    )(qn, gates_t, kc, vct, ksn, vst, kwn, vwt)


def _nsa_mixer(u, misc, B, T, q_norm_g, k_norm_g, cmp_weights, layer):
    assert T % LANES == 0 and T >= (WIN // LANES + 1) * LANES and T // CMP_STRIDE == LANES
    assert SLC_BLOCK == 1 << SLC_SHIFT and LANES == 2 * SLC_BLOCK
    qn, ksn, vst, kwn, vwt, gates_t = _nsa_prep(u, misc, B, T, q_norm_g, k_norm_g)
    kc, vct = _nsa_compress(u, B, T, cmp_weights, layer)
    return _nsa_attention(qn, gates_t, kc, vct, ksn, vst, kwn, vwt, B, T)


def _merge_kernel(x_ref, h_ref, oa_ref, ob_ref, oc_ref, od_ref, mod_ref, wg_ref, wb_ref, wo_ref, y_ref):
    h = h_ref[...]
    merged = None
    for i, o_ref in enumerate((oa_ref, ob_ref, oc_ref, od_ref)):
        gate = _sigmoid(jnp.dot(h, wg_ref[i], preferred_element_type=F32))
        term = gate * jnp.dot(o_ref[...], wb_ref[i], preferred_element_type=F32)
        merged = term if merged is None else merged + term
    y_ref[...] = x_ref[...] + mod_ref[0, 2:3, :] * _bdot(merged, wo_ref[...])


def _merge(x2d, h, outs, mod_l, wg, wb, wo, T):
    M, D = x2d.shape
    tm = min(1024, T)
    per_b = T // tm
    row = lambda w: pl.BlockSpec((tm, w), lambda m: (m, 0))
    const = lambda shape: pl.BlockSpec(shape, lambda m: (0,) * len(shape), pipeline_mode=pl.Buffered(1))
    return pl.pallas_call(
        _merge_kernel,
        grid=(M // tm,),
        in_specs=[row(D), row(D), row(MIX_W), row(MIX_W), row(MIX_W), row(MIX_W),
                  pl.BlockSpec((1, 6, D), lambda m: (m // per_b, 0, 0)),
                  const((N_BRANCH, D, D)), const((N_BRANCH, MIX_W, D)), const((D, D))],
        out_specs=row(D),
        out_shape=jax.ShapeDtypeStruct((M, D), F32),
        compiler_params=_cparams("parallel"),
        name="gated_merge_out_proj",
    )(x2d, h, *outs, mod_l, wg, wb, wo)


FFN_CHUNK = 256


def _ffn_kernel(x_ref, mod_ref, g_ref, wa_ref, wb_ref, wo_ref, y_ref, acc_ref):
    x = x_ref[...]
    y = x * lax.rsqrt(jnp.mean(x * x, axis=-1, keepdims=True) + RMS_EPS) * g_ref[...]
    h = (y * (1.0 + mod_ref[0, 4:5, :]) + mod_ref[0, 3:4, :]).astype(BF16)
    d_ff = wo_ref.shape[0]
    for c in range(d_ff // FFN_CHUNK):
        cols = slice(c * FFN_CHUNK, (c + 1) * FFN_CHUNK)
        a = jnp.dot(h, wa_ref[:, cols], preferred_element_type=F32)
        b = jnp.dot(h, wb_ref[:, cols], preferred_element_type=F32)
        part = _bdot(_silu(a) * b, wo_ref[cols, :])
        if c == 0:
            acc_ref[...] = part
        else:
            acc_ref[...] += part
    y_ref[...] = x + mod_ref[0, 5:6, :] * acc_ref[...]


def _ffn(x2d, mod_l, norm_g, w_in, w_out, T):
    M, D = x2d.shape
    d_ff = w_out.shape[0]
    assert d_ff % FFN_CHUNK == 0
    tm = min(1024, T)
    per_b = T // tm
    const = lambda shape, idx: pl.BlockSpec(shape, lambda m: idx, pipeline_mode=pl.Buffered(1))
    return pl.pallas_call(
        _ffn_kernel,
        grid=(M // tm,),
        in_specs=[pl.BlockSpec((tm, D), lambda m: (m, 0)),
                  pl.BlockSpec((1, 6, D), lambda m: (m // per_b, 0, 0)),
                  pl.BlockSpec((1, D), lambda m: (0, 0)),
                  const((D, d_ff), (0, 0)), const((D, d_ff), (0, 1)), const((d_ff, D), (0, 0))],
        out_specs=pl.BlockSpec((tm, D), lambda m: (m, 0)),
        out_shape=jax.ShapeDtypeStruct((M, D), F32),
        scratch_shapes=[pltpu.VMEM((tm, D), F32)],
        compiler_params=_cparams("parallel"),
        name="swiglu_ffn",
    )(x2d, mod_l, norm_g.reshape(1, D), w_in, w_in, w_out)


def _split_w_in(w_in):
    gt0 = SSD_IN + SC_IN + SG_IN + NSA_Q_W + 6 * NSA_KV_W
    wa = w_in[:, :, :MIX_W + SSD_XBC]
    wb = w_in[:, :, SSD_IN:gt0]
    wm = jnp.concatenate([w_in[:, :, MIX_W + SSD_XBC:SSD_IN], w_in[:, :, gt0:gt0 + 3 * NSA_HEADS]], axis=2)
    wm = jnp.pad(wm, ((0, 0), (0, 0), (0, U_WIDTH - COL_MISC - wm.shape[2])))
    assert wa.shape[2] == COL_XBC + SSD_XBC and wa.shape[2] + wb.shape[2] == COL_MISC
    return wa.astype(BF16), wb.astype(BF16), wm.astype(BF16)


def kernel(x, c, ada_w, ada_b, norm_mix_g, norm_ffn_g, w_in, ssd_conv_w, ssd_conv_b, ssd_dt_bias, ssd_a_log, ssd_d,
           ssd_norm_g, sc_conv_w, sg_norm_g, sg_w, sg_b, nsa_q_norm_g, nsa_k_norm_g, nsa_cmp_pe, nsa_cmp_w1,
           nsa_cmp_w2, w_branch, w_branch_gate, w_out, w_ffn_in, w_ffn_out):
    B, T, D = x.shape
    L = w_in.shape[0]
    mod = _modulation(c, ada_w, ada_b).reshape(L, B, 6, D)
    x2d = x.reshape(B * T, D)
    w_parts = _split_w_in(w_in)
    cmp_weights = _cmp_weights(nsa_cmp_pe, nsa_cmp_w1, nsa_cmp_w2, nsa_k_norm_g)
    for l in range(L):
        u, h, misc = _in_proj(x2d, mod[l], norm_mix_g[l], w_parts, l, T)
        o_conv, o_gating = _conv_and_gating_mixers(u, B, T, sc_conv_w[l], sg_norm_g[l], sg_w[l], sg_b[l])
        outs = (
            _ssd_mixer(u, misc, B, T, ssd_conv_w[l], ssd_conv_b[l], ssd_dt_bias[l], ssd_a_log[l], ssd_d[l],
                       ssd_norm_g[l]),
            o_conv,
            o_gating,
            _nsa_mixer(u, misc, B, T, nsa_q_norm_g[l], nsa_k_norm_g[l], cmp_weights, l),
        )
        x2d = _merge(x2d, h, outs, mod[l], w_branch_gate[l].astype(BF16), w_branch[l].astype(BF16),
                     w_out[l].astype(BF16), T)
        x2d = _ffn(x2d, mod[l], norm_ffn_g[l], w_ffn_in[l].astype(BF16), w_ffn_out[l].astype(BF16), T)
    return x2d.reshape(B, T, D)
```
